```python
import jax
import jax.numpy as jnp
from jax import lax
import numpy as np

D_MODEL = 2048
BATCH = 4
SEQ = 2048
DEPTH = 2
DEC_BATCH = 32
DEC_SEQ = 1
PAST_LEN = 8192
PAGE_SIZE = 128

HEAD_DIM = 128
N_MIX_HEADS = 12
N_MEM_HEADS = 4
N_MEM = 256
D_FF = 5632
ROPE_THETA = 10000.0
EPS = 1e-6
N_MIXERS = 2
NSA_KV_HEADS = 3
NSA_GROUP = N_MIX_HEADS // NSA_KV_HEADS
N_BRANCH = 3
CMP_BLOCK = 32
CMP_STRIDE = 16
SEL_BLOCK = 64
SEL_TOPK = 16
NSA_WINDOW = 512
DIL_PAIRS = ((128, 1), (512, 4), (2048, 16))
DIL_HEADS = N_MIX_HEADS // len(DIL_PAIRS)
BAND_BLOCK = 128
NSA_IN = N_MIX_HEADS * HEAD_DIM + N_MIX_HEADS * N_BRANCH + 6 * NSA_KV_HEADS * HEAD_DIM + N_MEM_HEADS * HEAD_DIM
DIL_IN = 3 * N_MIX_HEADS * HEAD_DIM + N_MEM_HEADS * HEAD_DIM
SCALE = HEAD_DIM ** -0.5
NEG = -1e30
FORCE_SCORE = 1e6

kernel_name = 'hybrid_nsa_dilated_macaron_step'


def rmsnorm(x, g):
    xf = x.astype(jnp.float32)
    y = xf * lax.rsqrt(jnp.mean(xf * xf, axis=-1, keepdims=True) + EPS)
    return (y * g.astype(jnp.float32)).astype(x.dtype)


def rope(x, pos):
    half = HEAD_DIM // 2
    inv = ROPE_THETA ** (-jnp.arange(half, dtype=jnp.float32) / half)
    ang = pos.astype(jnp.float32)[:, None, None] * inv
    cos, sin = jnp.cos(ang), jnp.sin(ang)
    xf = x.astype(jnp.float32)
    x1, x2 = xf[..., :half], xf[..., half:]
    return jnp.concatenate([x1 * cos - x2 * sin, x2 * cos + x1 * sin], axis=-1).astype(x.dtype)


def masked_softmax(s, mask):
    s = jnp.where(mask, s.astype(jnp.float32), NEG)
    m = jnp.max(s, axis=-1, keepdims=True)
    e = jnp.where(mask, jnp.exp(s - m), 0.0)
    l = jnp.sum(e, axis=-1, keepdims=True)
    return e / jnp.maximum(l, 1e-30), m[..., 0], l[..., 0]


def swiglu(x, g, w_gate, w_up, w_down):
    h = rmsnorm(x, g)
    return (jax.nn.silu(h @ w_gate) * (h @ w_up)) @ w_down


def memory_kv(mem, g, w_kv, k_g):
    B, M, _ = mem.shape
    kv = (rmsnorm(mem, g) @ w_kv).reshape(B, M, 2, N_MEM_HEADS, HEAD_DIM)
    return jnp.stack([rmsnorm(kv[:, :, 0], k_g), kv[:, :, 1]], axis=2)


def memory_attend(q, kv):
    B, T, Hm, d = q.shape
    s = jnp.einsum('bqhd,bkhd->bhqk', q, kv[:, :, 0]) * SCALE
    p = jax.nn.softmax(s.astype(jnp.float32), axis=-1)
    o = jnp.einsum('bhqk,bkhd->bqhd', p.astype(kv.dtype), kv[:, :, 1])
    return o.reshape(B, T, Hm * d)


def banded_attend(q, k, v, window):
    B, S, G, R, d = q.shape
    nqb = -(-S // BAND_BLOCK)
    sp = nqb * BAND_BLOCK
    front = -(-window // BAND_BLOCK) * BAND_BLOCK
    kw_len = front + BAND_BLOCK
    qb = jnp.pad(q, ((0, 0), (0, sp - S), (0, 0), (0, 0), (0, 0))).reshape(B, nqb, BAND_BLOCK, G, R, d)
    kp = jnp.pad(k, ((0, 0), (front, sp - S), (0, 0), (0, 0)))
    vp = jnp.pad(v, ((0, 0), (front, sp - S), (0, 0), (0, 0)))
    kidx = jnp.arange(nqb)[:, None] * BAND_BLOCK + jnp.arange(kw_len)[None, :]
    kb, vb = kp[:, kidx], vp[:, kidx]
    qpos = jnp.arange(sp).reshape(nqb, BAND_BLOCK)
    kpos = kidx - front
    diff = qpos[:, :, None] - kpos[:, None, :]
    mask = (diff >= 0) & (diff <= window) & (kpos[:, None, :] >= 0)
    s = jnp.einsum('bnqgrd,bnkgd->bngrqk', qb, kb) * SCALE
    p, m, l = masked_softmax(s, mask[None, :, None, None])
    o = jnp.einsum('bngrqk,bnkgd->bnqgrd', p.astype(v.dtype), vb).reshape(B, sp, G, R, d)[:, :S]
    m = m.transpose(0, 1, 4, 2, 3).reshape(B, sp, G, R)[:, :S]
    l = l.transpose(0, 1, 4, 2, 3).reshape(B, sp, G, R)[:, :S]
    return o, m, l


def nsa_compress(rows, w1, b1, w2):
    B, L, G, d = rows.shape
    r = CMP_BLOCK // CMP_STRIDE
    nc = (L - CMP_BLOCK) // CMP_STRIDE + 1
    chunks = rows[:, :(nc + r - 1) * CMP_STRIDE].reshape(B, nc + r - 1, CMP_STRIDE, G, d)
    w1 = w1.reshape(r, CMP_STRIDE, d, d)
    hid = b1
    for j in range(r):
        hid = hid + jnp.einsum('bncgd,cde->bnge', chunks[:, j:j + nc], w1[j])
    return jax.nn.gelu(hid) @ w2


def nsa_cmp_branch(q, kc_rows, vc_rows, q_pos, w1, b1, w2, kc_g):
    kc = rmsnorm(nsa_compress(kc_rows, w1[0], b1[0], w2[0]), kc_g)
    vc = nsa_compress(vc_rows, w1[1], b1[1], w2[1])
    end = jnp.arange(kc.shape[1]) * CMP_STRIDE + (CMP_BLOCK - 1)
    mask = end[None, :] <= q_pos[:, None]
    s = jnp.einsum('bqgrd,bkgd->bgrqk', q, kc) * SCALE
    p, _, _ = masked_softmax(s, mask)
    o = jnp.einsum('bgrqk,bkgd->bqgrd', p.astype(vc.dtype), vc)
    return o, p


def nsa_select(p_cmp, q_pos, ns, n_sel):
    nc = p_cmp.shape[-1]
    imp = jnp.sum(p_cmp, axis=2)
    c0 = jnp.arange(nc)[:, None] * CMP_STRIDE
    s0 = jnp.arange(ns)[None, :] * SEL_BLOCK
    cover = jnp.clip(jnp.minimum(c0 + CMP_BLOCK, s0 + SEL_BLOCK) - jnp.maximum(c0, s0), 0, CMP_BLOCK)
    score = jnp.einsum('bgqc,cs->bgqs', imp, cover.astype(jnp.float32) / CMP_BLOCK)
    blk = jnp.arange(ns)[None, :]
    cur = (q_pos // SEL_BLOCK)[:, None]
    forced = (blk == 0) | (blk == cur) | (blk == cur - 1)
    score = jnp.where(blk <= cur, jnp.where(forced, FORCE_SCORE, score), NEG)
    top_s, top_i = lax.top_k(score, n_sel)
    return top_i, top_s > 0.5 * NEG


def nsa_sel_chunk(q, idx, ok, pos, kb, vb):
    B, c, G, R, d = q.shape
    n = idx.shape[-1]
    bi = jnp.arange(B)[:, None, None, None]
    gi = jnp.arange(G)[None, :, None, None]
    k = kb[bi, gi, idx]
    v = vb[bi, gi, idx].reshape(B, G, c, n * SEL_BLOCK, d)
    s = jnp.einsum('bqgrd,bgqnsd->bgrqns', q, k).reshape(B, G, R, c, n * SEL_BLOCK) * SCALE
    kpos = idx[..., None] * SEL_BLOCK + jnp.arange(SEL_BLOCK)
    mask = ok[..., None] & (kpos <= pos[None, None, :, None, None])
    p, _, _ = masked_softmax(s, mask.reshape(B, G, 1, c, n * SEL_BLOCK))
    return jnp.einsum('bgrqk,bgqkd->bqgrd', p.astype(v.dtype), v)


def nsa_sel_branch(q_rot, ks_rows, vs_rows, q_pos, p_cmp):
    B, T, G, R, d = q_rot.shape
    L = ks_rows.shape[1]
    ns = -(-L // SEL_BLOCK)
    n_sel = min(SEL_TOPK, ns)
    top_i, top_ok = nsa_select(p_cmp, q_pos, ns, n_sel)
    pad = ns * SEL_BLOCK - L

    def blocks(rows):
        rows = jnp.pad(rows, ((0, 0), (0, pad), (0, 0), (0, 0)))
        return rows.reshape(B, ns, SEL_BLOCK, G, d).transpose(0, 3, 1, 2, 4)

    kb, vb = blocks(ks_rows), blocks(vs_rows)
    qc = BAND_BLOCK if T % BAND_BLOCK == 0 else T
    nch = T // qc
    xs = (q_rot.reshape(B, nch, qc, G, R, d).swapaxes(0, 1),
          top_i.reshape(B, G, nch, qc, n_sel).transpose(2, 0, 1, 3, 4),
          top_ok.reshape(B, G, nch, qc, n_sel).transpose(2, 0, 1, 3, 4),
          q_pos.reshape(nch, qc))
    out = lax.map(lambda a: nsa_sel_chunk(a[0], a[1], a[2], a[3], kb, vb), xs)
    return out.swapaxes(0, 1).reshape(B, T, G, R, d)


def window_decode(q, k, v, q_pos, base, window):
    kpos = base + jnp.arange(k.shape[1])
    diff = q_pos[:, None] - kpos[None, :]
    mask = (diff >= 0) & (diff <= window) & (kpos[None, :] >= 0)
    s = jnp.einsum('bqgrd,bkgd->bgrqk', q, k) * SCALE
    p, _, _ = masked_softmax(s, mask)
    return jnp.einsum('bgrqk,bkgd->bqgrd', p.astype(v.dtype), v)


def nsa_mixer(cols, q_pos, q_g, kc_g, ks_g, kw_g, cmp_w1, cmp_b1, cmp_w2, paged, win_buf):
    B, T, _ = cols.shape
    H, G, R, d = N_MIX_HEADS, NSA_KV_HEADS, NSA_GROUP, HEAD_DIM
    q = rmsnorm(cols[..., :H * d].reshape(B, T, H, d), q_g)
    gates = jax.nn.sigmoid(cols[..., H * d:H * d + H * N_BRANCH].astype(jnp.float32)).reshape(B, T, G, R, N_BRANCH)
    kv = cols[..., H * d + H * N_BRANCH:].reshape(B, T, 6, G, d)
    ks = rope(rmsnorm(kv[:, :, 2], ks_g), q_pos)
    kw = rope(rmsnorm(kv[:, :, 4], kw_g), q_pos)
    new_rows = jnp.stack([kv[:, :, 0], kv[:, :, 1], ks, kv[:, :, 3]], axis=2)
    new_win = jnp.stack([kw, kv[:, :, 5]], axis=2)
    q_grp = q.reshape(B, T, G, R, d)
    q_rot = rope(q, q_pos).reshape(B, T, G, R, d)

    def rows(c):
        if paged is None:
            return new_rows[:, :, c]
        cache, table = paged
        past = cache[table, :, c].reshape(B, -1, G, d)
        return jnp.concatenate([past, new_rows[:, :, c]], axis=1)

    o_cmp, p_cmp = nsa_cmp_branch(q_grp, rows(0), rows(1), q_pos, cmp_w1, cmp_b1, cmp_w2, kc_g)
    o_sel = nsa_sel_branch(q_rot, rows(2), rows(3), q_pos, p_cmp)
    if win_buf is None:
        o_win, _, _ = banded_attend(q_rot, kw, kv[:, :, 5], NSA_WINDOW)
        win_state = new_win[:, -min(NSA_WINDOW, T):]
    else:
        wb = win_buf.shape[1]
        buf = jnp.concatenate([win_buf, new_win], axis=1)
        o_win = window_decode(q_rot, buf[:, :, 0], buf[:, :, 1], q_pos, q_pos[0] - wb, NSA_WINDOW)
        win_state = buf[:, -wb:]
    o = gates[..., 0:1] * o_cmp + gates[..., 1:2] * o_sel + gates[..., 2:3] * o_win
    return o.astype(cols.dtype).reshape(B, T, H * d), new_rows, win_state


def dilated_prompt(q, k, v, window, dil):
    B, T, Hg, d = q.shape
    S = T // dil

    def sub(a):
        return a.reshape(B, S, dil, Hg, d).transpose(0, 2, 1, 3, 4).reshape(B * dil, S, Hg, d)

    o, m, l = banded_attend(sub(q)[:, :, :, None], sub(k), sub(v), window // dil)
    o = o[:, :, :, 0].reshape(B, dil, S, Hg, d).transpose(0, 2, 1, 3, 4).reshape(B, T, Hg, d)
    m = m[..., 0].reshape(B, dil, S, Hg).transpose(0, 2, 1, 3).reshape(B, T, Hg)
    l = l[..., 0].reshape(B, dil, S, Hg).transpose(0, 2, 1, 3).reshape(B, T, Hg)
    return o, m, l


def dilated_decode(q, rows_k, rows_v, q_pos, base, window, dil):
    n_keys = window // dil + 1
    kpos = q_pos[:, None] - jnp.arange(n_keys)[None, :] * dil
    idx = jnp.clip(kpos - base, 0, rows_k.shape[1] - 1)
    k, v = rows_k[:, idx], rows_v[:, idx]
    s = jnp.einsum('bqhd,bqjhd->bhqj', q, k) * SCALE
    p, m, l = masked_softmax(s, (kpos >= 0)[None, None])
    o = jnp.einsum('bhqj,bqjhd->bqhd', p.astype(v.dtype), v)
    return o, m.transpose(0, 2, 1), l.transpose(0, 2, 1)


def dilated_mixer(cols, q_pos, q_g, k_g, bufs):
    B, T, _ = cols.shape
    Hg, d = DIL_HEADS, HEAD_DIM
    qkv = cols.reshape(B, T, 3, N_MIX_HEADS, d)
    q = rope(rmsnorm(qkv[:, :, 0], q_g), q_pos)
    k = rope(rmsnorm(qkv[:, :, 1], k_g), q_pos)
    v = qkv[:, :, 2]
    outs, ms, ls, states = [], [], [], []
    for g, (window, dil) in enumerate(DIL_PAIRS):
        qg, kg, vg = q[:, :, g * Hg:(g + 1) * Hg], k[:, :, g * Hg:(g + 1) * Hg], v[:, :, g * Hg:(g + 1) * Hg]
        new = jnp.stack([kg, vg], axis=2)
        if bufs is None:
            o, m, l = dilated_prompt(qg, kg, vg, window, dil)
            states.append(new[:, -min(window, T):])
        else:
            wb = bufs[g].shape[1]
            buf = jnp.concatenate([bufs[g], new], axis=1)
            o, m, l = dilated_decode(qg, buf[:, :, 0], buf[:, :, 1], q_pos, q_pos[0] - wb, window, dil)
            states.append(buf[:, -wb:])
        outs.append(o)
        ms.append(m)
        ls.append(l)
    m_all, l_all, o_all = jnp.stack(ms, 0), jnp.stack(ls, 0), jnp.stack(outs, 0)
    w = jnp.exp(m_all - jnp.max(m_all, axis=0, keepdims=True)) * l_all
    w = w / jnp.sum(w, axis=0, keepdims=True)
    o = jnp.sum(w[..., None].astype(o_all.dtype) * o_all, axis=0)
    return o.reshape(B, T, Hg * d), tuple(states)


def setup_inputs(seed: int = 0) -> dict:
    key = jax.random.key(seed)
    keys = iter(jax.random.split(key, 48))

    def nrm(shape, scale):
        return jax.random.normal(next(keys), shape, jnp.float32) * scale

    def gain(shape):
        return 1.0 + 0.05 * jax.random.normal(next(keys), shape, jnp.float32)

    n_pages = PAST_LEN // PAGE_SIZE
    n_pool = (DEC_BATCH * n_pages * 5) // 4
    G, d, Hm = NSA_KV_HEADS, HEAD_DIM, N_MEM_HEADS
    table = jax.random.permutation(next(keys), n_pool)[:DEC_BATCH * n_pages]
    table = table.reshape(DEC_BATCH, n_pages).astype(jnp.int32)
    out_nsa = (N_MIX_HEADS + Hm) * d
    out_dil = (DIL_HEADS + Hm) * d
    return {
        'x_prompt': nrm((BATCH, SEQ, D_MODEL), 1.0),
        'x_sample': nrm((DEC_BATCH, DEC_SEQ, D_MODEL), 1.0),
        'mem_prompt': nrm((BATCH, N_MEM, D_MODEL), 1.0),
        'cache_nsa_kv': nrm((n_pool, PAGE_SIZE, 4, G, d), 1.0),
        'page_table': table,
        'state_nsa_win': nrm((DEC_BATCH, min(NSA_WINDOW, PAST_LEN), 2, G, d), 1.0),
        'state_dil_0': nrm((DEC_BATCH, min(DIL_PAIRS[0][0], PAST_LEN), 2, DIL_HEADS, d), 1.0),
        'state_dil_1': nrm((DEC_BATCH, min(DIL_PAIRS[1][0], PAST_LEN), 2, DIL_HEADS, d), 1.0),
        'state_dil_2': nrm((DEC_BATCH, min(DIL_PAIRS[2][0], PAST_LEN), 2, DIL_HEADS, d), 1.0),
        'cache_mem_kv': nrm((DEPTH, DEC_BATCH, N_MEM, 2, Hm, d), 1.0),
        'ff_norm': gain((DEPTH, 2, D_MODEL)),
        'ff_w_gate': nrm((DEPTH, 2, D_MODEL, D_FF), D_MODEL ** -0.5),
        'ff_w_up': nrm((DEPTH, 2, D_MODEL, D_FF), D_MODEL ** -0.5),
        'ff_w_down': nrm((DEPTH, 2, D_FF, D_MODEL), D_FF ** -0.5),
        'mix_norm': gain((DEPTH, D_MODEL)),
        'mem_norm': gain((DEPTH, D_MODEL)),
        'w_mem_kv': nrm((DEPTH, D_MODEL, 2 * Hm * d), D_MODEL ** -0.5),
        'mem_q_g': gain((DEPTH, d)),
        'mem_k_g': gain((DEPTH, d)),
        'nsa_w_in': nrm((D_MODEL, NSA_IN), D_MODEL ** -0.5),
        'nsa_q_g': gain((d,)),
        'nsa_kc_g': gain((d,)),
        'nsa_ks_g': gain((d,)),
        'nsa_kw_g': gain((d,)),
        'nsa_cmp_w1': nrm((2, CMP_BLOCK, d, d), (CMP_BLOCK * d) ** -0.5),
        'nsa_cmp_b1': nrm((2, d), 0.02),
        'nsa_cmp_w2': nrm((2, d, d), d ** -0.5),
        'nsa_w_out': nrm((out_nsa, D_MODEL), out_nsa ** -0.5),
        'dil_w_in': nrm((D_MODEL, DIL_IN), D_MODEL ** -0.5),
        'dil_q_g': gain((d,)),
        'dil_k_g': gain((d,)),
        'dil_w_out': nrm((out_dil, D_MODEL), out_dil ** -0.5),
    }


def reference(x_prompt, x_sample, mem_prompt, cache_nsa_kv, page_table, state_nsa_win, state_dil_0, state_dil_1,
              state_dil_2, cache_mem_kv, ff_norm, ff_w_gate, ff_w_up, ff_w_down, mix_norm, mem_norm, w_mem_kv,
              mem_q_g, mem_k_g, nsa_w_in, nsa_q_g, nsa_kc_g, nsa_ks_g, nsa_kw_g, nsa_cmp_w1, nsa_cmp_b1, nsa_cmp_w2,
              nsa_w_out, dil_w_in, dil_q_g, dil_k_g, dil_w_out):
    mem_kv_prompt = jnp.stack(
        [memory_kv(mem_prompt, mem_norm[i], w_mem_kv[i], mem_k_g[i]) for i in range(DEPTH)], axis=0)

    def run(x, q_pos, mem_kv, paged, win_buf, dil_bufs):
        B, T, _ = x.shape
        nsa_rows, win_state, dil_states = None, None, None
        for i in range(DEPTH):
            x = x + 0.5 * swiglu(x, ff_norm[i, 0], ff_w_gate[i, 0], ff_w_up[i, 0], ff_w_down[i, 0])
            h = rmsnorm(x, mix_norm[i])
            use_nsa = i % N_MIXERS == 0
            proj = h @ (nsa_w_in if use_nsa else dil_w_in)
            cut = proj.shape[-1] - N_MEM_HEADS * HEAD_DIM
            if use_nsa:
                o_mix, nsa_rows, win_state = nsa_mixer(proj[..., :cut], q_pos, nsa_q_g, nsa_kc_g, nsa_ks_g, nsa_kw_g,
                                                       nsa_cmp_w1, nsa_cmp_b1, nsa_cmp_w2, paged, win_buf)
            else:
                o_mix, dil_states = dilated_mixer(proj[..., :cut], q_pos, dil_q_g, dil_k_g, dil_bufs)
            mq = rmsnorm(proj[..., cut:].reshape(B, T, N_MEM_HEADS, HEAD_DIM), mem_q_g[i])
            o_mem = memory_attend(mq, mem_kv[i])
            x = x + jnp.concatenate([o_mix, o_mem], axis=-1) @ (nsa_w_out if use_nsa else dil_w_out)
            x = x + 0.5 * swiglu(x, ff_norm[i, 1], ff_w_gate[i, 1], ff_w_up[i, 1], ff_w_down[i, 1])
        return x, nsa_rows, win_state, dil_states

    pos_p = jnp.arange(x_prompt.shape[1], dtype=jnp.int32)
    y_prompt, nsa_p, win_p, dil_p = run(x_prompt, pos_p, mem_kv_prompt, None, None, None)
    past_len = page_table.shape[1] * PAGE_SIZE
    pos_s = past_len + jnp.arange(x_sample.shape[1], dtype=jnp.int32)
    y_sample, nsa_s, win_s, dil_s = run(x_sample, pos_s, cache_mem_kv, (cache_nsa_kv, page_table), state_nsa_win,
                                        (state_dil_0, state_dil_1, state_dil_2))
    return (y_prompt, y_sample, nsa_p, nsa_s, win_p, win_s, dil_p[0], dil_s[0], dil_p[1], dil_s[1], dil_p[2],
            dil_s[2], mem_kv_prompt)
```

```python
import functools

import jax
import jax.numpy as jnp
from jax import lax
from jax.experimental import pallas as pl
from jax.experimental.pallas import tpu as pltpu

F32 = jnp.float32
BF16 = jnp.bfloat16

D_MODEL = 2048
HEAD_DIM = 128
N_MIX_HEADS = 12
N_MEM_HEADS = 4
N_MEM = 256
NSA_KV_HEADS = 3
NSA_GROUP = 4
CMP_BLOCK = 32
CMP_STRIDE = 16
SEL_BLOCK = 64
SEL_TOPK = 16
NSA_WINDOW = 512
DIL_PAIRS = ((128, 1), (512, 4), (2048, 16))
DIL_HEADS = 4
PAGE_SIZE = 128
ROPE_THETA = 10000.0
EPS = 1e-6
SCALE = HEAD_DIM ** -0.5
NEG = -1e30
FORCE_SCORE = 1e6

PROJ_N = 5120
NSA_GATE_BLK = 34
LANE = 128
VMEM_LIMIT = 56 * 1024 * 1024


def _params(sem):
    return pltpu.CompilerParams(dimension_semantics=sem, vmem_limit_bytes=VMEM_LIMIT)


def _dot(a, b):
    return jnp.dot(a, b, preferred_element_type=F32)


def _dot_t(a, b):
    return lax.dot_general(a, b, (((1,), (1,)), ((), ())), preferred_element_type=F32)


def _dot3(a, b):
    a1 = a.astype(BF16)
    r1 = a - a1.astype(F32)
    a2 = r1.astype(BF16)
    a3 = (r1 - a2.astype(F32)).astype(BF16)
    return _dot(a1, b) + _dot(a2, b) + _dot(a3, b)


def _rms(x, g):
    return x * lax.rsqrt(jnp.mean(x * x, axis=-1, keepdims=True) + EPS) * g


def _rope(x, cos, sin):
    return x * cos + pltpu.roll(x, HEAD_DIM // 2, 1) * sin


def _rows16(row, nrep):
    rid = lax.broadcasted_iota(jnp.int32, (16, LANE), 0) & (nrep - 1)
    out = jnp.zeros((16, LANE), F32)
    for r in range(nrep):
        piece = jnp.broadcast_to(row[:, r * LANE:(r + 1) * LANE], (16, LANE))
        out = jnp.where(rid == r, piece, out)
    return out


def _softmax_masked(s, mask):
    s = jnp.where(mask, s, NEG)
    m = jnp.max(s, axis=-1, keepdims=True)
    e = jnp.where(mask, jnp.exp(s - m), 0.0)
    l = jnp.sum(e, axis=-1, keepdims=True)
    return e / jnp.maximum(l, 1e-30), m, l


def _ffn_kernel(x_ref, g_ref, wg_ref, wu_ref, wd_ref, o_ref, h_ref, acc_ref, *, nf):
    f = pl.program_id(1)

    @pl.when(f == 0)
    def _():
        h_ref[...] = _rms(x_ref[...], g_ref[...]).astype(BF16)
        acc_ref[...] = jnp.zeros_like(acc_ref)

    h = h_ref[...]
    gate = _dot(h, wg_ref[...])
    up = _dot(h, wu_ref[...])
    a = (gate * jax.nn.sigmoid(gate) * up).astype(BF16)
    acc_ref[...] += _dot(a, wd_ref[...])

    @pl.when(f == nf - 1)
    def _():
        o_ref[...] = x_ref[...] + 0.5 * acc_ref[...]


def _ffn(x, g, wg, wu, wd, li, lj, tm, tf=512):
    M, D = x.shape
    F = wg.shape[-1]
    nf = F // tf
    return pl.pallas_call(
        functools.partial(_ffn_kernel, nf=nf),
        grid=(M // tm, nf),
        in_specs=[
            pl.BlockSpec((tm, D), lambda i, f: (i, 0)),
            pl.BlockSpec((None, None, 1, D), lambda i, f: (li, lj, 0, 0)),
            pl.BlockSpec((None, None, D, tf), lambda i, f: (li, lj, 0, f)),
            pl.BlockSpec((None, None, D, tf), lambda i, f: (li, lj, 0, f)),
            pl.BlockSpec((None, None, tf, D), lambda i, f: (li, lj, f, 0)),
        ],
        out_specs=pl.BlockSpec((tm, D), lambda i, f: (i, 0)),
        out_shape=jax.ShapeDtypeStruct((M, D), F32),
        scratch_shapes=[pltpu.VMEM((tm, D), BF16), pltpu.VMEM((tm, D), F32)],
        compiler_params=_params(("parallel", "arbitrary")),
        name="ffn",
    )(x, g, wg, wu, wd)


def _nmm_kernel(x_ref, g_ref, w_ref, o_ref, h_ref):
    @pl.when(pl.program_id(1) == 0)
    def _():
        h_ref[...] = _rms(x_ref[...], g_ref[...]).astype(BF16)

    o_ref[...] = _dot(h_ref[...], w_ref[...])


def _norm_matmul(x, g, w, tm, tn):
    M, D = x.shape
    N = w.shape[1]
    return pl.pallas_call(
        _nmm_kernel,
        grid=(M // tm, N // tn),
        in_specs=[
            pl.BlockSpec((tm, D), lambda i, j: (i, 0)),
            pl.BlockSpec((1, D), lambda i, j: (0, 0)),
            pl.BlockSpec((D, tn), lambda i, j: (0, j)),
        ],
        out_specs=pl.BlockSpec((tm, tn), lambda i, j: (i, j)),
        out_shape=jax.ShapeDtypeStruct((M, N), F32),
        scratch_shapes=[pltpu.VMEM((tm, D), BF16)],
        compiler_params=_params(("parallel", "arbitrary")),
        name="norm_matmul",
    )(x, g.reshape(1, D), w)


def _oproj_kernel(x_ref, a_ref, b_ref, w_ref, o_ref, *, ka):
    o_ref[...] = x_ref[...] + _dot(a_ref[...], w_ref[:ka, :]) + _dot(b_ref[...], w_ref[ka:, :])


def _out_proj(x, a, b, w, tm, tn=1024):
    M, D = x.shape
    ka, kb = a.shape[1], b.shape[1]
    return pl.pallas_call(
        functools.partial(_oproj_kernel, ka=ka),
        grid=(M // tm, D // tn),
        in_specs=[
            pl.BlockSpec((tm, tn), lambda i, j: (i, j)),
            pl.BlockSpec((tm, ka), lambda i, j: (i, 0)),
            pl.BlockSpec((tm, kb), lambda i, j: (i, 0)),
            pl.BlockSpec((ka + kb, tn), lambda i, j: (0, j)),
        ],
        out_specs=pl.BlockSpec((tm, tn), lambda i, j: (i, j)),
        out_shape=jax.ShapeDtypeStruct((M, D), F32),
        compiler_params=_params(("parallel", "parallel")),
        name="out_proj",
    )(x, a, b, w)


def _nsa_post_kernel(p_ref, cos_ref, sin_ref, g_ref, qn_ref, qr_ref, rows_ref, win_ref, kvb_ref, mq_ref):
    cos, sin = cos_ref[...], sin_ref[...]
    q_g, ks_g, kw_g, mq_g = g_ref[0:1, :], g_ref[1:2, :], g_ref[2:3, :], g_ref[3:4, :]

    def tile(i):
        return p_ref[:, i * LANE:(i + 1) * LANE]

    for h in range(N_MIX_HEADS):
        qn = _rms(tile(h), q_g)
        qn_ref[:, h * LANE:(h + 1) * LANE] = qn.astype(BF16)
        qr_ref[:, h * LANE:(h + 1) * LANE] = _rope(qn, cos, sin).astype(BF16)
    for g in range(NSA_KV_HEADS):
        kc, vc = tile(12 + g), tile(15 + g)
        ks = _rope(_rms(tile(18 + g), ks_g), cos, sin)
        vs = tile(21 + g)
        kw = _rope(_rms(tile(24 + g), kw_g), cos, sin)
        vw = tile(27 + g)
        for c, val in enumerate((kc, vc, ks, vs)):
            rows_ref[:, (c * 3 + g) * LANE:(c * 3 + g + 1) * LANE] = val
        for c, val in enumerate((kw, vw)):
            win_ref[:, (c * 3 + g) * LANE:(c * 3 + g + 1) * LANE] = val
        for c, val in enumerate((ks, vs, kw, vw)):
            kvb_ref[:, (c * 3 + g) * LANE:(c * 3 + g + 1) * LANE] = val.astype(BF16)
    for h in range(N_MEM_HEADS):
        mq_ref[:, h * LANE:(h + 1) * LANE] = _rms(tile(30 + h), mq_g).astype(BF16)


def _nsa_post(p, cos, sin, gains, tm):
    M = p.shape[0]
    row = lambda n: pl.BlockSpec((tm, n), lambda i: (i, 0))
    return pl.pallas_call(
        _nsa_post_kernel,
        grid=(M // tm,),
        in_specs=[row(PROJ_N), row(LANE), row(LANE), pl.BlockSpec((8, LANE), lambda i: (0, 0))],
        out_specs=[row(1536), row(1536), row(1536), row(768), row(1536), row(512)],
        out_shape=[
            jax.ShapeDtypeStruct((M, 1536), BF16),
            jax.ShapeDtypeStruct((M, 1536), BF16),
            jax.ShapeDtypeStruct((M, 1536), F32),
            jax.ShapeDtypeStruct((M, 768), F32),
            jax.ShapeDtypeStruct((M, 1536), BF16),
            jax.ShapeDtypeStruct((M, 512), BF16),
        ],
        compiler_params=_params(("parallel",)),
        name="nsa_post",
    )(p, cos, sin, gains)


def _dil_post_kernel(p_ref, cos_ref, sin_ref, g_ref, qr_ref, kf_ref, kb_ref, vb_ref, mq_ref):
    cos, sin = cos_ref[...], sin_ref[...]
    q_g, k_g, mq_g = g_ref[0:1, :], g_ref[1:2, :], g_ref[2:3, :]
    for h in range(N_MIX_HEADS):
        sl = slice(h * LANE, (h + 1) * LANE)
        qr_ref[:, sl] = _rope(_rms(p_ref[:, sl], q_g), cos, sin).astype(BF16)
        k = _rope(_rms(p_ref[:, (12 + h) * LANE:(13 + h) * LANE], k_g), cos, sin)
        kf_ref[:, sl] = k
        kb_ref[:, sl] = k.astype(BF16)
        vb_ref[:, sl] = p_ref[:, (24 + h) * LANE:(25 + h) * LANE].astype(BF16)
    for h in range(N_MEM_HEADS):
        mq_ref[:, h * LANE:(h + 1) * LANE] = _rms(p_ref[:, (36 + h) * LANE:(37 + h) * LANE], mq_g).astype(BF16)


def _dil_post(p, cos, sin, gains, tm):
    M = p.shape[0]
    row = lambda n: pl.BlockSpec((tm, n), lambda i: (i, 0))
    return pl.pallas_call(
        _dil_post_kernel,
        grid=(M // tm,),
        in_specs=[row(PROJ_N), row(LANE), row(LANE), pl.BlockSpec((8, LANE), lambda i: (0, 0))],
        out_specs=[row(1536), row(1536), row(1536), row(1536), row(512)],
        out_shape=[
            jax.ShapeDtypeStruct((M, 1536), BF16),
            jax.ShapeDtypeStruct((M, 1536), F32),
            jax.ShapeDtypeStruct((M, 1536), BF16),
            jax.ShapeDtypeStruct((M, 1536), BF16),
            jax.ShapeDtypeStruct((M, 512), BF16),
        ],
        compiler_params=_params(("parallel",)),
        name="dil_post",
    )(p, cos, sin, gains)


def _memkv_post_kernel(x_ref, g_ref, o_ref):
    for h in range(N_MEM_HEADS):
        sl = slice(h * LANE, (h + 1) * LANE)
        o_ref[:, sl] = _rms(x_ref[:, sl], g_ref[...])
    o_ref[:, 512:] = x_ref[:, 512:]


def _memkv_post(x, g, tm=256):
    M, N = x.shape
    return pl.pallas_call(
        _memkv_post_kernel,
        grid=(M // tm,),
        in_specs=[pl.BlockSpec((tm, N), lambda i: (i, 0)), pl.BlockSpec((1, LANE), lambda i: (0, 0))],
        out_specs=pl.BlockSpec((tm, N), lambda i: (i, 0)),
        out_shape=jax.ShapeDtypeStruct((M, N), F32),
        compiler_params=_params(("parallel",)),
        name="memkv_post",
    )(x, g.reshape(1, LANE))


def _mem_attn_kernel(q_ref, kv_ref, o_ref, *, tq):
    for h in range(N_MEM_HEADS):
        sl = slice(h * LANE, (h + 1) * LANE)
        if tq == 1:
            q = _rows16(q_ref[:, sl], 1).astype(BF16)
        else:
            q = q_ref[:, sl]
        k = kv_ref[:, sl].astype(BF16)
        v = kv_ref[:, 512 + h * LANE:512 + (h + 1) * LANE].astype(BF16)
        s = _dot_t(q, k) * SCALE
        m = jnp.max(s, axis=-1, keepdims=True)
        e = jnp.exp(s - m)
        p = e / jnp.sum(e, axis=-1, keepdims=True)
        o = _dot(p.astype(BF16), v)
        o_ref[:, sl] = o[0:tq, :].astype(o_ref.dtype)


def _mem_attn(q, kv, tq):
    B, T, _ = q.shape
    return pl.pallas_call(
        functools.partial(_mem_attn_kernel, tq=tq),
        grid=(B, T // tq),
        in_specs=[
            pl.BlockSpec((None, tq, 512), lambda b, i: (b, i, 0)),
            pl.BlockSpec((None, N_MEM, 1024), lambda b, i: (b, 0, 0)),
        ],
        out_specs=pl.BlockSpec((None, tq, 512), lambda b, i: (b, i, 0)),
        out_shape=jax.ShapeDtypeStruct((B, T, 512), q.dtype),
        compiler_params=_params(("parallel", "parallel")),
        name="mem_attn",
    )(q, kv)


def _gelu_tanh(x):
    return 0.5 * x * (1.0 + jnp.tanh(0.7978845608028654 * (x + 0.044715 * (x * x * x))))


def _compress(x_bf, w1, b1, w2):
    n = x_bf.shape[0]
    h = _dot(x_bf, w1)
    hid = b1 + h[:, :LANE] + pltpu.roll(h[:, LANE:], n - 1, 0)
    return _dot(_gelu_tanh(hid).astype(BF16), w2)


def _cmp_prompt_kernel(x_ref, w1_ref, b1_ref, w2_ref, kcg_ref, o_ref, xs_ref, *, n):
    kv = pl.program_id(1)
    for c in range(CMP_STRIDE):
        xs_ref[:, c * LANE:(c + 1) * LANE] = x_ref[pl.ds(c, n, stride=CMP_STRIDE), :].astype(BF16)
    out = _compress(xs_ref[...], w1_ref[...], b1_ref[...], w2_ref[...])
    out = jnp.where(kv == 0, _rms(out, kcg_ref[...]), out)
    rid = lax.broadcasted_iota(jnp.int32, out.shape, 0)
    o_ref[...] = jnp.where(rid < n - 1, out, 0.0).astype(BF16)


def _cmp_prompt(rows, w1r, b1, w2, kc_g):
    B, T, _ = rows.shape
    n = T // CMP_STRIDE
    return pl.pallas_call(
        functools.partial(_cmp_prompt_kernel, n=n),
        grid=(B, 2, NSA_KV_HEADS),
        in_specs=[
            pl.BlockSpec((None, T, LANE), lambda b, kv, g: (b, 0, kv * 3 + g)),
            pl.BlockSpec((None, CMP_STRIDE * LANE, 2 * LANE), lambda b, kv, g: (kv, 0, 0)),
            pl.BlockSpec((None, 1, LANE), lambda b, kv, g: (kv, 0, 0)),
            pl.BlockSpec((None, LANE, LANE), lambda b, kv, g: (kv, 0, 0)),
            pl.BlockSpec((1, LANE), lambda b, kv, g: (0, 0)),
        ],
        out_specs=pl.BlockSpec((None, None, None, n, LANE), lambda b, kv, g: (b, kv, g, 0, 0)),
        out_shape=jax.ShapeDtypeStruct((B, 2, NSA_KV_HEADS, n, LANE), BF16),
        scratch_shapes=[pltpu.VMEM((n, CMP_STRIDE * LANE), BF16)],
        compiler_params=_params(("parallel", "parallel", "parallel")),
        name="cmp_prompt",
    )(rows, w1r, b1, w2, kc_g)


def _online_update(s, mask, v, m_ref, l_ref, acc_ref):
    s = jnp.where(mask, s, NEG)
    m_old = m_ref[...]
    m_new = jnp.maximum(m_old, jnp.max(s, axis=-1, keepdims=True))
    e = jnp.where(mask, jnp.exp(s - m_new), 0.0)
    a = jnp.exp(m_old - m_new)
    l_ref[...] = a * l_ref[...] + jnp.sum(e, axis=-1, keepdims=True)
    acc_ref[...] = a * acc_ref[...] + _dot(e.astype(BF16), v)
    m_ref[...] = m_new


def _select_blocks(score, cur, n_valid_lanes):
    blk = lax.broadcasted_iota(jnp.int32, score.shape, 1)
    forced = (blk == 0) | (blk == cur) | (blk == cur - 1)
    sc = jnp.where(blk <= cur, jnp.where(forced, FORCE_SCORE, score), NEG)
    rank = jnp.zeros(score.shape, F32)
    for i in range(n_valid_lanes):
        si = sc[:, i:i + 1]
        ahead = (si > sc) | ((si == sc) & (blk > i))
        rank = rank + jnp.where(ahead, 1.0, 0.0)
    return (rank < SEL_TOPK) & (sc > 0.5 * NEG)


def _nsa_attn_kernel(qn_ref, qr_ref, gl_ref, kc_ref, vc_ref, ks_ref, vs_ref, kw_ref, vw_ref, cover_ref,
                     o_ref, m_ref, l_ref, acc_ref, *, tq, ns):
    qi = pl.program_id(2)
    R = NSA_GROUP
    t0 = qi * tq
    stack = lambda ref: jnp.concatenate([ref[:, r * LANE:(r + 1) * LANE] for r in range(R)], axis=0)
    rows4 = lambda x: jnp.concatenate([x] * R, axis=0)
    tpos_q = t0 + lax.broadcasted_iota(jnp.int32, (tq, 1), 0)
    tpos = t0 + (lax.broadcasted_iota(jnp.int32, (R * tq, 1), 0) & (tq - 1))
    col = lax.broadcasted_iota(jnp.int32, (R * tq, LANE), 1)

    s = _dot_t(stack(qn_ref), kc_ref[...]) * SCALE
    cmask = (CMP_STRIDE * col + (CMP_BLOCK - 1) <= tpos) & (col < kc_ref.shape[0] - 1)
    p, _, _ = _softmax_masked(s, cmask)
    o_cmp = _dot(p.astype(BF16), vc_ref[...])
    imp = p[0:tq] + p[tq:2 * tq] + p[2 * tq:3 * tq] + p[3 * tq:4 * tq]
    score = _dot3(imp, cover_ref[...])
    sel = _select_blocks(score, lax.shift_right_arithmetic(tpos_q, SEL_BLOCK.bit_length() - 1), ns)
    sel = jnp.where(sel, 1.0, 0.0).astype(BF16)

    q_rot = stack(qr_ref)
    blocks_per_tile = tq // SEL_BLOCK

    def reset():
        m_ref[...] = jnp.full_like(m_ref, NEG)
        l_ref[...] = jnp.zeros_like(l_ref)
        acc_ref[...] = jnp.zeros_like(acc_ref)

    def result():
        return acc_ref[...] / jnp.maximum(l_ref[...], 1e-30)

    reset()

    def sel_step(kt, carry):
        k = ks_ref[pl.ds(pl.multiple_of(kt * tq, tq), tq), :]
        v = vs_ref[pl.ds(pl.multiple_of(kt * tq, tq), tq), :]
        s = _dot_t(q_rot, k) * SCALE
        key_blk = lax.shift_right_arithmetic(lax.broadcasted_iota(jnp.int32, (LANE, tq), 1),
                                             SEL_BLOCK.bit_length() - 1)
        expand = lax.broadcasted_iota(jnp.int32, (LANE, tq), 0) == kt * blocks_per_tile + key_blk
        member = rows4(_dot(sel, jnp.where(expand, 1.0, 0.0).astype(BF16)))
        kpos = kt * tq + col
        _online_update(s, (member > 0.5) & (kpos <= tpos), v, m_ref, l_ref, acc_ref)
        return carry

    lax.fori_loop(0, qi + 1, sel_step, 0)
    o_sel = result()

    reset()

    def win_step(kt, carry):
        k = kw_ref[pl.ds(pl.multiple_of(kt * tq, tq), tq), :]
        v = vw_ref[pl.ds(pl.multiple_of(kt * tq, tq), tq), :]
        s = _dot_t(q_rot, k) * SCALE
        diff = tpos - (kt * tq + col)
        _online_update(s, (diff >= 0) & (diff <= NSA_WINDOW), v, m_ref, l_ref, acc_ref)
        return carry

    lax.fori_loop(jnp.maximum(qi - NSA_WINDOW // tq, 0), qi + 1, win_step, 0)
    o_win = result()

    gates = jax.nn.sigmoid(gl_ref[...])
    for r in range(R):
        rs = slice(r * tq, (r + 1) * tq)
        o = (gates[:, 3 * r:3 * r + 1] * o_cmp[rs] + gates[:, 3 * r + 1:3 * r + 2] * o_sel[rs]
             + gates[:, 3 * r + 2:3 * r + 3] * o_win[rs])
        o_ref[:, r * LANE:(r + 1) * LANE] = o.astype(BF16)


def _nsa_attn(qn, qr, proj, cmp_kv, kvb, cover, tq=128):
    B, T, _ = qn.shape
    nc = cmp_kv.shape[3]
    G = NSA_KV_HEADS
    qspec = pl.BlockSpec((None, tq, 4 * LANE), lambda b, g, i: (b, i, g))
    kvspec = lambda c: pl.BlockSpec((None, T, LANE), lambda b, g, i: (b, 0, c * 3 + g))
    return pl.pallas_call(
        functools.partial(_nsa_attn_kernel, tq=tq, ns=T // SEL_BLOCK),
        grid=(B, G, T // tq),
        in_specs=[
            qspec, qspec,
            pl.BlockSpec((None, tq, LANE), lambda b, g, i: (b, i, NSA_GATE_BLK + g)),
            pl.BlockSpec((None, None, None, nc, LANE), lambda b, g, i: (b, 0, g, 0, 0)),
            pl.BlockSpec((None, None, None, nc, LANE), lambda b, g, i: (b, 1, g, 0, 0)),
            kvspec(0), kvspec(1), kvspec(2), kvspec(3),
            pl.BlockSpec((nc, LANE), lambda b, g, i: (0, 0)),
        ],
        out_specs=qspec,
        out_shape=jax.ShapeDtypeStruct((B, T, 1536), BF16),
        scratch_shapes=[pltpu.VMEM((4 * tq, 1), F32), pltpu.VMEM((4 * tq, 1), F32),
                        pltpu.VMEM((4 * tq, LANE), F32)],
        compiler_params=_params(("parallel", "parallel", "arbitrary")),
        name="nsa_attn",
    )(qn, qr, proj, cmp_kv, cmp_kv, kvb, kvb, kvb, kvb, cover)


def _dil_band_kernel(q_ref, kp_ref, kc_ref, vp_ref, vc_ref, o_ref, st_ref, *, tq, window):
    i = pl.program_id(2)
    qpos = i * tq + lax.broadcasted_iota(jnp.int32, (tq, 2 * tq), 0)
    kpos = (i - 1) * tq + lax.broadcasted_iota(jnp.int32, (tq, 2 * tq), 1)
    diff = qpos - kpos
    mask = (diff >= 0) & (diff <= window) & (kpos >= 0)
    lane = lax.broadcasted_iota(jnp.int32, (tq, LANE), 1)
    stats = jnp.zeros((tq, LANE), F32)
    for h in range(DIL_HEADS):
        sl = slice(h * LANE, (h + 1) * LANE)
        k = jnp.concatenate([kp_ref[:, sl], kc_ref[:, sl]], axis=0)
        v = jnp.concatenate([vp_ref[:, sl], vc_ref[:, sl]], axis=0)
        p, m, l = _softmax_masked(_dot_t(q_ref[:, sl], k) * SCALE, mask)
        o_ref[:, sl] = _dot(p.astype(BF16), v)
        stats = jnp.where(lane == h, m, stats)
        stats = jnp.where(lane == DIL_HEADS + h, l, stats)
    st_ref[...] = stats


def _dil_band(q, k, v, g, dil, tq=128):
    B, T, C = q.shape
    S = T // dil
    nb = C // (4 * LANE)
    view = lambda a: a.reshape(B, S, dil * C)
    cur = pl.BlockSpec((None, tq, 4 * LANE), lambda b, r, i: (b, i, r * nb + g))
    prev = pl.BlockSpec((None, tq, 4 * LANE), lambda b, r, i: (b, jnp.maximum(i - 1, 0), r * nb + g))
    o, st = pl.pallas_call(
        functools.partial(_dil_band_kernel, tq=tq, window=DIL_PAIRS[g][0] // dil),
        grid=(B, dil, S // tq),
        in_specs=[cur, prev, cur, prev, cur],
        out_specs=[pl.BlockSpec((None, tq, 4 * LANE), lambda b, r, i: (b, i, r)),
                   pl.BlockSpec((None, tq, LANE), lambda b, r, i: (b, i, r))],
        out_shape=[jax.ShapeDtypeStruct((B, S, dil * 4 * LANE), F32),
                   jax.ShapeDtypeStruct((B, S, dil * LANE), F32)],
        compiler_params=_params(("parallel", "parallel", "parallel")),
        name=f"dil_band{g}",
    )(view(q), view(k), view(k), view(v), view(v))
    return o.reshape(B * T, 4 * LANE), st.reshape(B * T, LANE)


def _mix_groups(os_, ms, ls):
    m_all = jnp.maximum(jnp.maximum(ms[0], ms[1]), ms[2])
    ws = [jnp.exp(m - m_all) * l for m, l in zip(ms, ls)]
    tot = ws[0] + ws[1] + ws[2]
    return (ws[0] / tot) * os_[0] + (ws[1] / tot) * os_[1] + (ws[2] / tot) * os_[2]


def _dil_mix_kernel(o0_ref, o1_ref, o2_ref, s0_ref, s1_ref, s2_ref, o_ref):
    o_refs, s_refs = (o0_ref, o1_ref, o2_ref), (s0_ref, s1_ref, s2_ref)
    for h in range(DIL_HEADS):
        sl = slice(h * LANE, (h + 1) * LANE)
        ms = [s[:, h:h + 1] for s in s_refs]
        ls = [s[:, DIL_HEADS + h:DIL_HEADS + h + 1] for s in s_refs]
        o_ref[:, sl] = _mix_groups([o[:, sl] for o in o_refs], ms, ls).astype(BF16)


def _dil_mix(os_, sts, tm=512):
    M = os_[0].shape[0]
    ospec = pl.BlockSpec((tm, 4 * LANE), lambda i: (i, 0))
    sspec = pl.BlockSpec((tm, LANE), lambda i: (i, 0))
    return pl.pallas_call(
        _dil_mix_kernel,
        grid=(M // tm,),
        in_specs=[ospec] * 3 + [sspec] * 3,
        out_specs=ospec,
        out_shape=jax.ShapeDtypeStruct((M, 4 * LANE), BF16),
        compiler_params=_params(("parallel",)),
        name="dil_mix",
    )(*os_, *sts)


def _dec_select_kernel(tbl_ref, *refs, n_pages):
    pages = refs[:n_pages]
    (qn_ref, w1_ref, b1_ref, w2_ref, kcg_ref, cover_ref, sel_ref, ocmp_ref, xs_ref, p_ref) = refs[n_pages:]
    g, kv = pl.program_id(1), pl.program_id(2)
    n = n_pages * (PAGE_SIZE // CMP_STRIDE)
    ns = sel_ref.shape[-1]

    @pl.when((g == 0) & (kv == 0))
    def _():
        sel_ref[...] = jnp.zeros_like(sel_ref)

    for c in range(CMP_STRIDE):
        xc = jnp.concatenate([pg[pl.ds(c, PAGE_SIZE // CMP_STRIDE, stride=CMP_STRIDE), :] for pg in pages], axis=0)
        xs_ref[:, c * LANE:(c + 1) * LANE] = xc.astype(BF16)
    out = _compress(xs_ref[...], w1_ref[...], b1_ref[...], w2_ref[...])

    @pl.when(kv == 0)
    def _():
        kc = _rms(out, kcg_ref[...]).astype(BF16)
        q = _rows16(qn_ref[...], NSA_GROUP).astype(BF16)
        s = _dot_t(q, kc) * SCALE
        valid = lax.broadcasted_iota(jnp.int32, s.shape, 1) < n - 1
        p, _, _ = _softmax_masked(s, valid)
        p_ref[...] = p
        rid = lax.broadcasted_iota(jnp.int32, p.shape, 0)
        imp = jnp.sum(jnp.where(rid < NSA_GROUP, p, 0.0), axis=0, keepdims=True)
        score = _dot3(jnp.broadcast_to(imp, (8, n)), cover_ref[...])
        a = jnp.broadcast_to(score[0:1, :], (ns, ns))
        blk = lax.broadcasted_iota(jnp.int32, (ns, ns), 1)
        cur = n * CMP_STRIDE // SEL_BLOCK
        forced = (blk == 0) | (blk == cur) | (blk == cur - 1)
        a = jnp.where(blk <= cur, jnp.where(forced, FORCE_SCORE, a), NEG)
        at = a.T
        ii = lax.broadcasted_iota(jnp.int32, (ns, ns), 0)
        ahead = (at > a) | ((at == a) & (ii < blk))
        rank = jnp.sum(jnp.where(ahead, 1.0, 0.0), axis=0, keepdims=True)
        chosen = (rank < SEL_TOPK) & (a[0:1, :] > 0.5 * NEG)
        sel_ref[pl.ds(g, 1), :] = jnp.where(chosen, 1.0, 0.0)

    @pl.when(kv == 1)
    def _():
        o = _dot(p_ref[...].astype(BF16), out.astype(BF16))
        ocmp_ref[...] = o[0:NSA_GROUP, :]


def _dec_select(cache, table, qn, w1r, b1, w2, kc_g, cover):
    B, n_pages = table.shape
    n = n_pages * (PAGE_SIZE // CMP_STRIDE)
    ns = cover.shape[1]
    G = NSA_KV_HEADS
    page_spec = lambda p: pl.BlockSpec((None, PAGE_SIZE, LANE), lambda b, g, kv, tbl: (tbl[b, p], 0, kv * G + g))
    grid_spec = pltpu.PrefetchScalarGridSpec(
        num_scalar_prefetch=1,
        grid=(B, G, 2),
        in_specs=[page_spec(p) for p in range(n_pages)] + [
            pl.BlockSpec((None, 1, NSA_GROUP * LANE), lambda b, g, kv, tbl: (b, 0, g)),
            pl.BlockSpec((None, CMP_STRIDE * LANE, 2 * LANE), lambda b, g, kv, tbl: (kv, 0, 0)),
            pl.BlockSpec((None, 1, LANE), lambda b, g, kv, tbl: (kv, 0, 0)),
            pl.BlockSpec((None, LANE, LANE), lambda b, g, kv, tbl: (kv, 0, 0)),
            pl.BlockSpec((1, LANE), lambda b, g, kv, tbl: (0, 0)),
            pl.BlockSpec((n, ns), lambda b, g, kv, tbl: (0, 0)),
        ],
        out_specs=[pl.BlockSpec((None, 8, ns), lambda b, g, kv, tbl: (b, 0, 0)),
                   pl.BlockSpec((None, None, NSA_GROUP, LANE), lambda b, g, kv, tbl: (b, g, 0, 0))],
        scratch_shapes=[pltpu.VMEM((n, CMP_STRIDE * LANE), BF16), pltpu.VMEM((16, n), F32)],
    )
    return pl.pallas_call(
        functools.partial(_dec_select_kernel, n_pages=n_pages),
        grid_spec=grid_spec,
        out_shape=[jax.ShapeDtypeStruct((B, 8, ns), F32),
                   jax.ShapeDtypeStruct((B, G, NSA_GROUP, LANE), F32)],
        compiler_params=_params(("arbitrary", "arbitrary", "arbitrary")),
        name="dec_select",
    )(table, *([cache] * n_pages), qn, w1r, b1, w2, kc_g, cover)


def _dec_attn_kernel(tbl_ref, idx_ref, *refs, n_sel, cur):
    kblk, vblk = refs[:n_sel], refs[n_sel:2 * n_sel]
    (qr_ref, kvn_ref, win_ref_k, win_ref_v, gl_ref, ocmp_ref, o_ref) = refs[2 * n_sel:]
    b, g = pl.program_id(0), pl.program_id(1)
    R = NSA_GROUP
    q = _rows16(qr_ref[...], R).astype(BF16)
    qf = q.astype(F32)
    new = kvn_ref[...]
    ks_n, vs_n, kw_n, vw_n = (new[:, c * LANE:(c + 1) * LANE] for c in range(4))

    k = jnp.concatenate([r[...] for r in kblk], axis=0).astype(BF16)
    v = jnp.concatenate([r[...] for r in vblk], axis=0).astype(BF16)
    s = _dot_t(q, k) * SCALE
    blk_of = lax.shift_right_arithmetic(lax.broadcasted_iota(jnp.int32, s.shape, 1), SEL_BLOCK.bit_length() - 1)
    valid = jnp.zeros(s.shape, jnp.int32)
    for n in range(n_sel):
        is_past = jnp.where(idx_ref[(b * NSA_KV_HEADS + g) * n_sel + n] != cur, 1, 0)
        valid = jnp.where(blk_of == n, is_past, valid)
    valid = valid > 0
    s_new = jnp.sum(qf * ks_n, axis=-1, keepdims=True) * SCALE
    s = jnp.where(valid, s, NEG)
    m = jnp.maximum(jnp.max(s, axis=-1, keepdims=True), s_new)
    e = jnp.where(valid, jnp.exp(s - m), 0.0)
    e_new = jnp.exp(s_new - m)
    l = jnp.sum(e, axis=-1, keepdims=True) + e_new
    o_sel = (_dot(e.astype(BF16), v) + e_new.astype(BF16).astype(F32) * vs_n) / l

    s = _dot_t(q, win_ref_k[...].astype(BF16)) * SCALE
    s_new = jnp.sum(qf * kw_n, axis=-1, keepdims=True) * SCALE
    m = jnp.maximum(jnp.max(s, axis=-1, keepdims=True), s_new)
    e = jnp.exp(s - m)
    e_new = jnp.exp(s_new - m)
    l = jnp.sum(e, axis=-1, keepdims=True) + e_new
    o_win = (_dot(e.astype(BF16), win_ref_v[...].astype(BF16)) + e_new.astype(BF16).astype(F32) * vw_n) / l

    gates = jax.nn.sigmoid(gl_ref[...])
    o_cmp = ocmp_ref[...]
    for r in range(R):
        o = (gates[:, 3 * r:3 * r + 1] * o_cmp[r:r + 1] + gates[:, 3 * r + 1:3 * r + 2] * o_sel[r:r + 1]
             + gates[:, 3 * r + 2:3 * r + 3] * o_win[r:r + 1])
        o_ref[:, r * LANE:(r + 1) * LANE] = o


def _dec_attn(cache, table, idx, qr, kvb, win_state, proj, ocmp):
    B, n_pages = table.shape
    n_sel = SEL_TOPK
    cur = n_pages * PAGE_SIZE // SEL_BLOCK
    assert win_state.shape[1] <= NSA_WINDOW
    halves = PAGE_SIZE // SEL_BLOCK
    cache_h = cache.reshape(cache.shape[0] * halves, SEL_BLOCK, cache.shape[2])

    def blk_spec(n, c):
        def imap(b, g, tbl, idx):
            i = jnp.minimum(idx[(b * NSA_KV_HEADS + g) * n_sel + n], cur - 1)
            return (tbl[b, i // halves] * halves + i % halves, 0, c * 3 + g)
        return pl.BlockSpec((None, SEL_BLOCK, LANE), imap)

    wb = win_state.shape[1]
    grid_spec = pltpu.PrefetchScalarGridSpec(
        num_scalar_prefetch=2,
        grid=(B, NSA_KV_HEADS),
        in_specs=[blk_spec(n, 2) for n in range(n_sel)] + [blk_spec(n, 3) for n in range(n_sel)] + [
            pl.BlockSpec((None, 1, 4 * LANE), lambda b, g, tbl, idx: (b, 0, g)),
            pl.BlockSpec((None, 1, 4 * LANE), lambda b, g, tbl, idx: (b, 0, g)),
            pl.BlockSpec((None, wb, LANE), lambda b, g, tbl, idx: (b, 0, g)),
            pl.BlockSpec((None, wb, LANE), lambda b, g, tbl, idx: (b, 0, 3 + g)),
            pl.BlockSpec((None, 1, LANE), lambda b, g, tbl, idx: (b, 0, NSA_GATE_BLK + g)),
            pl.BlockSpec((None, None, NSA_GROUP, LANE), lambda b, g, tbl, idx: (b, g, 0, 0)),
        ],
        out_specs=pl.BlockSpec((None, 1, 4 * LANE), lambda b, g, tbl, idx: (b, 0, g)),
    )
    kvn = kvb.reshape(B, 1, 4, NSA_KV_HEADS, LANE).transpose(0, 1, 3, 2, 4).reshape(B, 1, 1536)
    return pl.pallas_call(
        functools.partial(_dec_attn_kernel, n_sel=n_sel, cur=cur),
        grid_spec=grid_spec,
        out_shape=jax.ShapeDtypeStruct((B, 1, 1536), F32),
        compiler_params=_params(("arbitrary", "arbitrary")),
        name="dec_attn",
    )(table, idx, *([cache_h] * (2 * n_sel)), qr, kvn, win_state, win_state, proj, ocmp)


def _dec_dil_kernel(q_ref, kn_ref, vn_ref, s0_ref, s1_ref, s2_ref, o_ref):
    qall = q_ref[...]
    kn = kn_ref[...].astype(BF16).astype(F32)
    vn = vn_ref[...].astype(BF16).astype(F32)
    for h in range(DIL_HEADS):
        os_, ms, ls = [], [], []
        for g, st in enumerate((s0_ref, s1_ref, s2_ref)):
            hs = slice((g * DIL_HEADS + h) * LANE, (g * DIL_HEADS + h + 1) * LANE)
            q = _rows16(qall[:, hs], 1).astype(BF16)
            k = st[:, h * LANE:(h + 1) * LANE].astype(BF16)
            v = st[:, (DIL_HEADS + h) * LANE:(DIL_HEADS + h + 1) * LANE].astype(BF16)
            s = _dot_t(q, k) * SCALE
            s_new = jnp.sum(q.astype(F32) * kn[:, hs], axis=-1, keepdims=True) * SCALE
            m = jnp.maximum(jnp.max(s, axis=-1, keepdims=True), s_new)
            e = jnp.exp(s - m)
            e_new = jnp.exp(s_new - m)
            l = jnp.sum(e, axis=-1, keepdims=True) + e_new
            ln = jnp.maximum(l, 1e-30)
            o = _dot((e / ln).astype(BF16), v) + (e_new / ln).astype(BF16).astype(F32) * vn[:, hs]
            os_.append(o)
            ms.append(m)
            ls.append(l)
        o_ref[:, h * LANE:(h + 1) * LANE] = _mix_groups(os_, ms, ls)[0:1, :]


def _dec_dil(qr, kn, vn, states):
    B = qr.shape[0]
    row = pl.BlockSpec((None, 1, 1536), lambda b: (b, 0, 0))
    in_specs = [row, row, row]
    views = []
    for g, (window, dil) in enumerate(DIL_PAIRS):
        st = states[g]
        assert st.shape[1] == window, "rolling buffer shorter than the window is not supported"
        n_keys = window // dil
        views.append(st.reshape(B, n_keys, dil * st.shape[2]))
        in_specs.append(pl.BlockSpec((None, n_keys, st.shape[2]), lambda b: (b, 0, 0)))
    return pl.pallas_call(
        _dec_dil_kernel,
        grid=(B,),
        in_specs=in_specs,
        out_specs=pl.BlockSpec((None, 1, 4 * LANE), lambda b: (b, 0, 0)),
        out_shape=jax.ShapeDtypeStruct((B, 1, 4 * LANE), F32),
        compiler_params=_params(("parallel",)),
        name="dec_dil",
    )(qr, kn, vn, *views)


def _rope_tables(pos):
    half = HEAD_DIM // 2
    inv = ROPE_THETA ** (-jnp.arange(half, dtype=F32) / half)
    ang = pos.astype(F32)[:, None] * inv
    cos, sin = jnp.cos(ang), jnp.sin(ang)
    return jnp.concatenate([cos, cos], axis=-1), jnp.concatenate([-sin, sin], axis=-1)


def _cover(nc, ns, rows, cols):
    c0 = jnp.arange(nc)[:, None] * CMP_STRIDE
    s0 = jnp.arange(ns)[None, :] * SEL_BLOCK
    cover = jnp.clip(jnp.minimum(c0 + CMP_BLOCK, s0 + SEL_BLOCK) - jnp.maximum(c0, s0), 0, CMP_BLOCK)
    cover = cover.astype(F32) / CMP_BLOCK
    return jnp.pad(cover, ((0, rows - nc), (0, cols - ns))).astype(BF16)


def _pad_gains(*gs):
    return jnp.pad(jnp.stack(gs, axis=0), ((0, 8 - len(gs)), (0, 0)))


def kernel(x_prompt, x_sample, mem_prompt, cache_nsa_kv, page_table, state_nsa_win, state_dil_0, state_dil_1,
           state_dil_2, cache_mem_kv, ff_norm, ff_w_gate, ff_w_up, ff_w_down, mix_norm, mem_norm, w_mem_kv,
           mem_q_g, mem_k_g, nsa_w_in, nsa_q_g, nsa_kc_g, nsa_ks_g, nsa_kw_g, nsa_cmp_w1, nsa_cmp_b1, nsa_cmp_w2,
           nsa_w_out, dil_w_in, dil_q_g, dil_k_g, dil_w_out):
    B, T, D = x_prompt.shape
    Bs = x_sample.shape[0]
    assert x_sample.shape[1] == 1, "the sample group is a single-token decode step"
    n_pages = page_table.shape[1]
    past_len = n_pages * PAGE_SIZE
    H, G, d = N_MIX_HEADS, NSA_KV_HEADS, HEAD_DIM

    wg, wu, wd = ff_w_gate.astype(BF16), ff_w_up.astype(BF16), ff_w_down.astype(BF16)
    ffg = ff_norm.reshape(ff_norm.shape[0], 2, 1, D)
    gate_w = jnp.pad(nsa_w_in[:, H * d:H * d + 3 * H].reshape(D, G, 3 * NSA_GROUP), ((0, 0), (0, 0), (0, LANE - 12)))
    nsa_w = jnp.concatenate([nsa_w_in[:, :H * d], nsa_w_in[:, H * d + 3 * H:], gate_w.reshape(D, G * LANE)], axis=1)
    nsa_w = jnp.pad(nsa_w, ((0, 0), (0, PROJ_N - nsa_w.shape[1]))).astype(BF16)
    dil_w = dil_w_in.astype(BF16)
    nsa_wo, dil_wo = nsa_w_out.astype(BF16), dil_w_out.astype(BF16)
    w1r = nsa_cmp_w1.reshape(2, 2, CMP_STRIDE, d, d).transpose(0, 2, 3, 1, 4).reshape(2, CMP_STRIDE * d, 2 * d)
    w1r = w1r.astype(BF16)
    cmp_b1 = nsa_cmp_b1.reshape(2, 1, d)
    cmp_w2 = nsa_cmp_w2.astype(BF16)
    kc_g = nsa_kc_g.reshape(1, d)

    mem2d = mem_prompt.reshape(B * N_MEM, D)
    mem_kv_p = []
    for i in range(2):
        kv = _norm_matmul(mem2d, mem_norm[i], w_mem_kv[i].astype(BF16), tm=256, tn=1024)
        mem_kv_p.append(_memkv_post(kv, mem_k_g[i]).reshape(B, N_MEM, 1024))

    cos_p, sin_p = _rope_tables(jnp.tile(jnp.arange(T, dtype=jnp.int32), B))
    cos_s, sin_s = _rope_tables(jnp.full((Bs,), past_len, jnp.int32))

    xp = _ffn(x_prompt.reshape(B * T, D), ffg, wg, wu, wd, 0, 0, tm=512)
    xs = _ffn(x_sample.reshape(Bs, D), ffg, wg, wu, wd, 0, 0, tm=Bs)
    nsa_gains = _pad_gains(nsa_q_g, nsa_ks_g, nsa_kw_g, mem_q_g[0])

    proj_p = _norm_matmul(xp, mix_norm[0], nsa_w, tm=512, tn=1024)
    qn_p, qr_p, rows_p, win_p, kvb_p, mq_p = _nsa_post(proj_p, cos_p, sin_p, nsa_gains, tm=256)
    cmp_kv = _cmp_prompt(rows_p.reshape(B, T, 1536), w1r, cmp_b1, cmp_w2, kc_g)
    nc_p = T // CMP_STRIDE
    cover_p = _cover(nc_p - 1, T // SEL_BLOCK, nc_p, LANE)
    o_mix_p = _nsa_attn(qn_p.reshape(B, T, 1536), qr_p.reshape(B, T, 1536), proj_p.reshape(B, T, PROJ_N),
                        cmp_kv, kvb_p.reshape(B, T, 1536), cover_p)
    o_mem_p = _mem_attn(mq_p.reshape(B, T, 512), mem_kv_p[0], tq=256)
    xp = _out_proj(xp, o_mix_p.reshape(B * T, 1536), o_mem_p.reshape(B * T, 512), nsa_wo, tm=512)

    proj_s = _norm_matmul(xs, mix_norm[0], nsa_w, tm=Bs, tn=1024)
    qn_s, qr_s, rows_s, win_s, kvb_s, mq_s = _nsa_post(proj_s, cos_s, sin_s, nsa_gains, tm=Bs)
    cache2d = cache_nsa_kv.reshape(cache_nsa_kv.shape[0], PAGE_SIZE, 4 * G * d)
    nc_s = past_len // CMP_STRIDE
    ns_s = -(-(past_len + 1) // SEL_BLOCK)
    ns_pad = -(-ns_s // LANE) * LANE
    cover_s = _cover(nc_s - 1, ns_s, nc_s, ns_pad)
    row3 = lambda a: a.astype(F32).reshape(Bs, 1, a.shape[-1])
    sel_mask, ocmp_s = _dec_select(cache2d, page_table, row3(qn_s), w1r, cmp_b1, cmp_w2, kc_g, cover_s)
    sel_val, sel_idx = lax.top_k(sel_mask[:, :G, :], SEL_TOPK)
    sel_idx = jnp.where(sel_val > 0.5, sel_idx, past_len // SEL_BLOCK).astype(jnp.int32)
    win_state2d = state_nsa_win.reshape(Bs, state_nsa_win.shape[1], 2 * G * d)
    o_mix_s = _dec_attn(cache2d, page_table, sel_idx.reshape(-1), row3(qr_s), row3(kvb_s), win_state2d,
                        proj_s.reshape(Bs, 1, PROJ_N), ocmp_s)
    mem_kv_s = cache_mem_kv.reshape(2, Bs, N_MEM, 1024)
    o_mem_s = _mem_attn(row3(mq_s), mem_kv_s[0], tq=1)
    xs = _out_proj(xs, o_mix_s.reshape(Bs, 1536).astype(BF16), o_mem_s.reshape(Bs, 512).astype(BF16), nsa_wo,
                   tm=Bs)

    xp = _ffn(xp, ffg, wg, wu, wd, 0, 1, tm=512)
    xs = _ffn(xs, ffg, wg, wu, wd, 0, 1, tm=Bs)

    xp = _ffn(xp, ffg, wg, wu, wd, 1, 0, tm=512)
    xs = _ffn(xs, ffg, wg, wu, wd, 1, 0, tm=Bs)
    dil_gains = _pad_gains(dil_q_g, dil_k_g, mem_q_g[1])

    dproj_p = _norm_matmul(xp, mix_norm[1], dil_w, tm=512, tn=1024)
    dq_p, dk_p, dkb_p, dvb_p, dmq_p = _dil_post(dproj_p, cos_p, sin_p, dil_gains, tm=256)
    band = [_dil_band(dq_p.reshape(B, T, 1536), dkb_p.reshape(B, T, 1536), dvb_p.reshape(B, T, 1536), g, dil)
            for g, (_, dil) in enumerate(DIL_PAIRS)]
    o_dil_p = _dil_mix([o for o, _ in band], [s for _, s in band])
    o_dmem_p = _mem_attn(dmq_p.reshape(B, T, 512), mem_kv_p[1], tq=256)
    xp = _out_proj(xp, o_dil_p, o_dmem_p.reshape(B * T, 512), dil_wo, tm=512)

    dproj_s = _norm_matmul(xs, mix_norm[1], dil_w, tm=Bs, tn=1024)
    dq_s, dk_s, _, _, dmq_s = _dil_post(dproj_s, cos_s, sin_s, dil_gains, tm=Bs)
    dv_s = dproj_s[:, 2 * H * d:3 * H * d]
    dil_states = (state_dil_0, state_dil_1, state_dil_2)
    o_dil_s = _dec_dil(row3(dq_s), row3(dk_s), row3(dv_s),
                       [s.reshape(Bs, s.shape[1], 2 * DIL_HEADS * d) for s in dil_states])
    o_dmem_s = _mem_attn(row3(dmq_s), mem_kv_s[1], tq=1)
    xs = _out_proj(xs, o_dil_s.reshape(Bs, 512).astype(BF16), o_dmem_s.reshape(Bs, 512).astype(BF16), dil_wo,
                   tm=Bs)

    xp = _ffn(xp, ffg, wg, wu, wd, 1, 1, tm=512)
    xs = _ffn(xs, ffg, wg, wu, wd, 1, 1, tm=Bs)

    nsa_kv_p = rows_p.reshape(B, T, 4, G, d)
    nsa_kv_s = rows_s.reshape(Bs, 1, 4, G, d)
    nsa_win_p = win_p.reshape(B, T, 2, G, d)[:, -min(NSA_WINDOW, T):]
    nsa_win_s = jnp.concatenate([state_nsa_win, win_s.reshape(Bs, 1, 2, G, d)], axis=1)[:, -state_nsa_win.shape[1]:]
    dv_p = dproj_p[:, 2 * H * d:3 * H * d]
    outs_dil = []
    for g, (window, _) in enumerate(DIL_PAIRS):
        hs = slice(g * DIL_HEADS * d, (g + 1) * DIL_HEADS * d)
        new_p = jnp.stack([dk_p[:, hs].reshape(B, T, DIL_HEADS, d), dv_p[:, hs].reshape(B, T, DIL_HEADS, d)], axis=2)
        new_s = jnp.stack([dk_s[:, hs].reshape(Bs, 1, DIL_HEADS, d), dv_s[:, hs].reshape(Bs, 1, DIL_HEADS, d)],
                          axis=2)
        st = dil_states[g]
        outs_dil.append(new_p[:, -min(window, T):])
        outs_dil.append(jnp.concatenate([st, new_s], axis=1)[:, -st.shape[1]:])
    mem_kv_out = jnp.stack([kv.reshape(B, N_MEM, 2, N_MEM_HEADS, d) for kv in mem_kv_p], axis=0)
    return (xp.reshape(B, T, D), xs.reshape(Bs, 1, D), nsa_kv_p, nsa_kv_s, nsa_win_p, nsa_win_s,
            *outs_dil, mem_kv_out)
```

```python
import functools

import jax
import jax.numpy as jnp
from jax import lax
from jax.experimental import pallas as pl
from jax.experimental.pallas import tpu as pltpu

F32 = jnp.float32
BF16 = jnp.bfloat16

D_MODEL = 2048
HEAD_DIM = 128
N_MIX_HEADS = 12
N_MEM_HEADS = 4
N_MEM = 256
NSA_KV_HEADS = 3
NSA_GROUP = 4
CMP_BLOCK = 32
CMP_STRIDE = 16
SEL_BLOCK = 64
SEL_TOPK = 16
NSA_WINDOW = 512
DIL_PAIRS = ((128, 1), (512, 4), (2048, 16))
DIL_HEADS = 4
PAGE_SIZE = 128
ROPE_THETA = 10000.0
EPS = 1e-6
SCALE = HEAD_DIM ** -0.5
NEG = -1e30
FORCE_SCORE = 1e6

PROJ_N = 5120
NSA_GATE_BLK = 34
LANE = 128
VMEM_LIMIT = 56 * 1024 * 1024


def _params(sem):
    return pltpu.CompilerParams(dimension_semantics=sem, vmem_limit_bytes=VMEM_LIMIT)


def _dot(a, b):
    return jnp.dot(a, b, preferred_element_type=F32)


def _dot_t(a, b):
    return lax.dot_general(a, b, (((1,), (1,)), ((), ())), preferred_element_type=F32)


def _dot3(a, b):
    a1 = a.astype(BF16)
    r1 = a - a1.astype(F32)
    a2 = r1.astype(BF16)
    a3 = (r1 - a2.astype(F32)).astype(BF16)
    return _dot(a1, b) + _dot(a2, b) + _dot(a3, b)


def _rms(x, g):
    return x * lax.rsqrt(jnp.mean(x * x, axis=-1, keepdims=True) + EPS) * g


def _rope(x, cos, sin):
    return x * cos + pltpu.roll(x, HEAD_DIM // 2, 1) * sin


def _rows16(row, nrep):
    rid = lax.broadcasted_iota(jnp.int32, (16, LANE), 0) & (nrep - 1)
    out = jnp.zeros((16, LANE), F32)
    for r in range(nrep):
        piece = jnp.broadcast_to(row[:, r * LANE:(r + 1) * LANE], (16, LANE))
        out = jnp.where(rid == r, piece, out)
    return out


def _softmax_masked(s, mask):
    s = jnp.where(mask, s, NEG)
    m = jnp.max(s, axis=-1, keepdims=True)
    e = jnp.where(mask, jnp.exp(s - m), 0.0)
    l = jnp.sum(e, axis=-1, keepdims=True)
    return e / jnp.maximum(l, 1e-30), m, l


def _ffn_kernel(x_ref, g_ref, wg_ref, wu_ref, wd_ref, o_ref, h_ref, acc_ref, *, nf):
    f = pl.program_id(1)

    @pl.when(f == 0)
    def _():
        h_ref[...] = _rms(x_ref[...], g_ref[...]).astype(BF16)
        acc_ref[...] = jnp.zeros_like(acc_ref)

    h = h_ref[...]
    gate = _dot(h, wg_ref[...])
    up = _dot(h, wu_ref[...])
    a = (gate * jax.nn.sigmoid(gate) * up).astype(BF16)
    acc_ref[...] += _dot(a, wd_ref[...])

    @pl.when(f == nf - 1)
    def _():
        o_ref[...] = x_ref[...] + 0.5 * acc_ref[...]


def _ffn(x, g, wg, wu, wd, li, lj, tm, tf=512):
    M, D = x.shape
    F = wg.shape[-1]
    nf = F // tf
    return pl.pallas_call(
        functools.partial(_ffn_kernel, nf=nf),
        grid=(M // tm, nf),
        in_specs=[
            pl.BlockSpec((tm, D), lambda i, f: (i, 0)),
            pl.BlockSpec((None, None, 1, D), lambda i, f: (li, lj, 0, 0)),
            pl.BlockSpec((None, None, D, tf), lambda i, f: (li, lj, 0, f)),
            pl.BlockSpec((None, None, D, tf), lambda i, f: (li, lj, 0, f)),
            pl.BlockSpec((None, None, tf, D), lambda i, f: (li, lj, f, 0)),
        ],
        out_specs=pl.BlockSpec((tm, D), lambda i, f: (i, 0)),
        out_shape=jax.ShapeDtypeStruct((M, D), F32),
        scratch_shapes=[pltpu.VMEM((tm, D), BF16), pltpu.VMEM((tm, D), F32)],
        compiler_params=_params(("parallel", "arbitrary")),
        name="ffn",
    )(x, g, wg, wu, wd)


def _nmm_kernel(x_ref, g_ref, w_ref, o_ref, h_ref):
    @pl.when(pl.program_id(1) == 0)
    def _():
        h_ref[...] = _rms(x_ref[...], g_ref[...]).astype(BF16)

    o_ref[...] = _dot(h_ref[...], w_ref[...])


def _norm_matmul(x, g, w, tm, tn):
    M, D = x.shape
    N = w.shape[1]
    return pl.pallas_call(
        _nmm_kernel,
        grid=(M // tm, N // tn),
        in_specs=[
            pl.BlockSpec((tm, D), lambda i, j: (i, 0)),
            pl.BlockSpec((1, D), lambda i, j: (0, 0)),
            pl.BlockSpec((D, tn), lambda i, j: (0, j)),
        ],
        out_specs=pl.BlockSpec((tm, tn), lambda i, j: (i, j)),
        out_shape=jax.ShapeDtypeStruct((M, N), F32),
        scratch_shapes=[pltpu.VMEM((tm, D), BF16)],
        compiler_params=_params(("parallel", "arbitrary")),
        name="norm_matmul",
    )(x, g.reshape(1, D), w)


def _oproj_kernel(x_ref, a_ref, b_ref, w_ref, o_ref, *, ka):
    o_ref[...] = x_ref[...] + _dot(a_ref[...], w_ref[:ka, :]) + _dot(b_ref[...], w_ref[ka:, :])


def _out_proj(x, a, b, w, tm, tn=1024):
    M, D = x.shape
    ka, kb = a.shape[1], b.shape[1]
    return pl.pallas_call(
        functools.partial(_oproj_kernel, ka=ka),
        grid=(M // tm, D // tn),
        in_specs=[
            pl.BlockSpec((tm, tn), lambda i, j: (i, j)),
            pl.BlockSpec((tm, ka), lambda i, j: (i, 0)),
            pl.BlockSpec((tm, kb), lambda i, j: (i, 0)),
            pl.BlockSpec((ka + kb, tn), lambda i, j: (0, j)),
        ],
        out_specs=pl.BlockSpec((tm, tn), lambda i, j: (i, j)),
        out_shape=jax.ShapeDtypeStruct((M, D), F32),
        compiler_params=_params(("parallel", "parallel")),
        name="out_proj",
    )(x, a, b, w)


def _put_rows(ref, row, rows_per_token, val):
    ref[pl.ds(row, val.shape[0], stride=rows_per_token), :] = val


def _get_rows(ref, row, rows_per_token, n):
    return ref[pl.ds(row, n, stride=rows_per_token), :]


NSA_ROWS = 4 * NSA_KV_HEADS
WIN_ROWS = 2 * NSA_KV_HEADS
KVH_ROWS = 2 * DIL_HEADS


def _nsa_post_kernel(p_ref, cos_ref, sin_ref, g_ref, qn_ref, qr_ref, cmp_ref, rows_ref, win_ref, kvb_ref, mq_ref):
    cos, sin = cos_ref[...], sin_ref[...]
    q_g, ks_g, kw_g, mq_g = g_ref[0:1, :], g_ref[1:2, :], g_ref[2:3, :], g_ref[3:4, :]

    def tile(i):
        return p_ref[:, i * LANE:(i + 1) * LANE]

    for h in range(N_MIX_HEADS):
        qn = _rms(tile(h), q_g)
        qn_ref[:, h * LANE:(h + 1) * LANE] = qn.astype(BF16)
        qr_ref[:, h * LANE:(h + 1) * LANE] = _rope(qn, cos, sin).astype(BF16)
    for g in range(NSA_KV_HEADS):
        kc, vc = tile(12 + g), tile(15 + g)
        ks = _rope(_rms(tile(18 + g), ks_g), cos, sin)
        vs = tile(21 + g)
        kw = _rope(_rms(tile(24 + g), kw_g), cos, sin)
        vw = tile(27 + g)
        for c, val in enumerate((kc, vc)):
            cmp_ref[:, (c * 3 + g) * LANE:(c * 3 + g + 1) * LANE] = val
        for c, val in enumerate((kc, vc, ks, vs)):
            _put_rows(rows_ref, g * 4 + c, NSA_ROWS, val)
        for c, val in enumerate((kw, vw)):
            _put_rows(win_ref, g * 2 + c, WIN_ROWS, val)
        for c, val in enumerate((ks, vs, kw, vw)):
            kvb_ref[:, (c * 3 + g) * LANE:(c * 3 + g + 1) * LANE] = val.astype(BF16)
    for h in range(N_MEM_HEADS):
        mq_ref[:, h * LANE:(h + 1) * LANE] = _rms(tile(30 + h), mq_g).astype(BF16)


def _nsa_post(p, cos, sin, gains, tm):
    M = p.shape[0]
    row = lambda n: pl.BlockSpec((tm, n), lambda i: (i, 0))
    flat = lambda r: pl.BlockSpec((tm * r, LANE), lambda i: (i, 0))
    return pl.pallas_call(
        _nsa_post_kernel,
        grid=(M // tm,),
        in_specs=[row(PROJ_N), row(LANE), row(LANE), pl.BlockSpec((8, LANE), lambda i: (0, 0))],
        out_specs=[row(1536), row(1536), row(768), flat(NSA_ROWS), flat(WIN_ROWS), row(1536), row(512)],
        out_shape=[
            jax.ShapeDtypeStruct((M, 1536), BF16),
            jax.ShapeDtypeStruct((M, 1536), BF16),
            jax.ShapeDtypeStruct((M, 768), F32),
            jax.ShapeDtypeStruct((M * NSA_ROWS, LANE), F32),
            jax.ShapeDtypeStruct((M * WIN_ROWS, LANE), F32),
            jax.ShapeDtypeStruct((M, 1536), BF16),
            jax.ShapeDtypeStruct((M, 512), BF16),
        ],
        compiler_params=_params(("parallel",)),
        name="nsa_post",
    )(p, cos, sin, gains)


def _dil_post_kernel(p_ref, cos_ref, sin_ref, g_ref, qr_ref, kb_ref, vb_ref, mq_ref, st0_ref, st1_ref, st2_ref):
    cos, sin = cos_ref[...], sin_ref[...]
    q_g, k_g, mq_g = g_ref[0:1, :], g_ref[1:2, :], g_ref[2:3, :]
    st_refs = (st0_ref, st1_ref, st2_ref)
    for h in range(N_MIX_HEADS):
        sl = slice(h * LANE, (h + 1) * LANE)
        qr_ref[:, sl] = _rope(_rms(p_ref[:, sl], q_g), cos, sin).astype(BF16)
        k = _rope(_rms(p_ref[:, (12 + h) * LANE:(13 + h) * LANE], k_g), cos, sin)
        v = p_ref[:, (24 + h) * LANE:(25 + h) * LANE]
        kb_ref[:, sl] = k.astype(BF16)
        vb_ref[:, sl] = v.astype(BF16)
        st = st_refs[h // DIL_HEADS]
        _put_rows(st, h % DIL_HEADS, KVH_ROWS, k)
        _put_rows(st, DIL_HEADS + h % DIL_HEADS, KVH_ROWS, v)
    for h in range(N_MEM_HEADS):
        mq_ref[:, h * LANE:(h + 1) * LANE] = _rms(p_ref[:, (36 + h) * LANE:(37 + h) * LANE], mq_g).astype(BF16)


def _dil_post(p, cos, sin, gains, tm):
    M = p.shape[0]
    row = lambda n: pl.BlockSpec((tm, n), lambda i: (i, 0))
    flat = pl.BlockSpec((tm * KVH_ROWS, LANE), lambda i: (i, 0))
    st_shape = jax.ShapeDtypeStruct((M * KVH_ROWS, LANE), F32)
    return pl.pallas_call(
        _dil_post_kernel,
        grid=(M // tm,),
        in_specs=[row(PROJ_N), row(LANE), row(LANE), pl.BlockSpec((8, LANE), lambda i: (0, 0))],
        out_specs=[row(1536), row(1536), row(1536), row(512), flat, flat, flat],
        out_shape=[
            jax.ShapeDtypeStruct((M, 1536), BF16),
            jax.ShapeDtypeStruct((M, 1536), BF16),
            jax.ShapeDtypeStruct((M, 1536), BF16),
            jax.ShapeDtypeStruct((M, 512), BF16),
            st_shape, st_shape, st_shape,
        ],
        compiler_params=_params(("parallel",)),
        name="dil_post",
    )(p, cos, sin, gains)


def _memkv_post_kernel(x_ref, g_ref, o_ref):
    for h in range(N_MEM_HEADS):
        _put_rows(o_ref, h, KVH_ROWS, _rms(x_ref[:, h * LANE:(h + 1) * LANE], g_ref[...]))
        _put_rows(o_ref, N_MEM_HEADS + h, KVH_ROWS, x_ref[:, (N_MEM_HEADS + h) * LANE:(N_MEM_HEADS + h + 1) * LANE])


def _memkv_post(x, g, tm=256):
    M, N = x.shape
    return pl.pallas_call(
        _memkv_post_kernel,
        grid=(M // tm,),
        in_specs=[pl.BlockSpec((tm, N), lambda i: (i, 0)), pl.BlockSpec((1, LANE), lambda i: (0, 0))],
        out_specs=pl.BlockSpec((tm * KVH_ROWS, LANE), lambda i: (i, 0)),
        out_shape=jax.ShapeDtypeStruct((M * KVH_ROWS, LANE), F32),
        compiler_params=_params(("parallel",)),
        name="memkv_post",
    )(x, g.reshape(1, LANE))


def _mem_attn_kernel(q_ref, kv_ref, o_ref, *, tq):
    for h in range(N_MEM_HEADS):
        sl = slice(h * LANE, (h + 1) * LANE)
        if tq == 1:
            q = _rows16(q_ref[:, sl], 1).astype(BF16)
        else:
            q = q_ref[:, sl]
        k = _get_rows(kv_ref, h, KVH_ROWS, N_MEM).astype(BF16)
        v = _get_rows(kv_ref, N_MEM_HEADS + h, KVH_ROWS, N_MEM).astype(BF16)
        s = _dot_t(q, k) * SCALE
        m = jnp.max(s, axis=-1, keepdims=True)
        e = jnp.exp(s - m)
        p = e / jnp.sum(e, axis=-1, keepdims=True)
        o = _dot(p.astype(BF16), v)
        o_ref[:, sl] = o[0:tq, :].astype(o_ref.dtype)


def _mem_attn(q, kv, tq):
    B, T, _ = q.shape
    return pl.pallas_call(
        functools.partial(_mem_attn_kernel, tq=tq),
        grid=(B, T // tq),
        in_specs=[
            pl.BlockSpec((None, tq, 512), lambda b, i: (b, i, 0)),
            pl.BlockSpec((None, N_MEM * KVH_ROWS, LANE), lambda b, i: (b, 0, 0)),
        ],
        out_specs=pl.BlockSpec((None, tq, 512), lambda b, i: (b, i, 0)),
        out_shape=jax.ShapeDtypeStruct((B, T, 512), q.dtype),
        compiler_params=_params(("parallel", "parallel")),
        name="mem_attn",
    )(q, kv)


def _gelu_tanh(x):
    return 0.5 * x * (1.0 + jnp.tanh(0.7978845608028654 * (x + 0.044715 * (x * x * x))))


def _compress_finish(h, b1, w2):
    n = h.shape[0]
    hid = b1 + h[:, :LANE] + pltpu.roll(h[:, LANE:], n - 1, 0)
    return _dot(_gelu_tanh(hid).astype(BF16), w2)


def _compress(x_bf, w1, b1, w2):
    return _compress_finish(_dot(x_bf, w1), b1, w2)


def _cmp_prompt_kernel(x_ref, w1_ref, b1_ref, w2_ref, kcg_ref, o_ref, xs_ref, *, n):
    kv = pl.program_id(1)
    for c in range(CMP_STRIDE):
        xs_ref[:, c * LANE:(c + 1) * LANE] = x_ref[pl.ds(c, n, stride=CMP_STRIDE), :].astype(BF16)
    out = _compress(xs_ref[...], w1_ref[...], b1_ref[...], w2_ref[...])
    out = jnp.where(kv == 0, _rms(out, kcg_ref[...]), out)
    rid = lax.broadcasted_iota(jnp.int32, out.shape, 0)
    o_ref[...] = jnp.where(rid < n - 1, out, 0.0).astype(BF16)


def _cmp_prompt(rows, w1r, b1, w2, kc_g):
    B, T, _ = rows.shape
    n = T // CMP_STRIDE
    return pl.pallas_call(
        functools.partial(_cmp_prompt_kernel, n=n),
        grid=(B, 2, NSA_KV_HEADS),
        in_specs=[
            pl.BlockSpec((None, T, LANE), lambda b, kv, g: (b, 0, kv * 3 + g)),
            pl.BlockSpec((None, CMP_STRIDE * LANE, 2 * LANE), lambda b, kv, g: (kv, 0, 0)),
            pl.BlockSpec((None, 1, LANE), lambda b, kv, g: (kv, 0, 0)),
            pl.BlockSpec((None, LANE, LANE), lambda b, kv, g: (kv, 0, 0)),
            pl.BlockSpec((1, LANE), lambda b, kv, g: (0, 0)),
        ],
        out_specs=pl.BlockSpec((None, None, None, n, LANE), lambda b, kv, g: (b, kv, g, 0, 0)),
        out_shape=jax.ShapeDtypeStruct((B, 2, NSA_KV_HEADS, n, LANE), BF16),
        scratch_shapes=[pltpu.VMEM((n, CMP_STRIDE * LANE), BF16)],
        compiler_params=_params(("parallel", "parallel", "parallel")),
        name="cmp_prompt",
    )(rows, w1r, b1, w2, kc_g)


def _online_update(s, mask, v, m_ref, l_ref, acc_ref):
    s = jnp.where(mask, s, NEG)
    m_old = m_ref[...]
    m_new = jnp.maximum(m_old, jnp.max(s, axis=-1, keepdims=True))
    e = jnp.where(mask, jnp.exp(s - m_new), 0.0)
    a = jnp.exp(m_old - m_new)
    l_ref[...] = a * l_ref[...] + jnp.sum(e, axis=-1, keepdims=True)
    acc_ref[...] = a * acc_ref[...] + _dot(e.astype(BF16), v)
    m_ref[...] = m_new


def _select_blocks(score, cur, n_valid_lanes):
    blk = lax.broadcasted_iota(jnp.int32, score.shape, 1)
    forced = (blk == 0) | (blk == cur) | (blk == cur - 1)
    sc = jnp.where(blk <= cur, jnp.where(forced, FORCE_SCORE, score), NEG)
    rank = jnp.zeros(score.shape, F32)
    for i in range(n_valid_lanes):
        si = sc[:, i:i + 1]
        ahead = (si > sc) | ((si == sc) & (blk > i))
        rank = rank + jnp.where(ahead, 1.0, 0.0)
    return (rank < SEL_TOPK) & (sc > 0.5 * NEG)


def _nsa_attn_kernel(qn_ref, qr_ref, gl_ref, kc_ref, vc_ref, ks_ref, vs_ref, kw_ref, vw_ref, cover_ref,
                     o_ref, m_ref, l_ref, acc_ref, *, tq, ns):
    qi = pl.program_id(2)
    R = NSA_GROUP
    t0 = qi * tq
    stack = lambda ref: jnp.concatenate([ref[:, r * LANE:(r + 1) * LANE] for r in range(R)], axis=0)
    rows4 = lambda x: jnp.concatenate([x] * R, axis=0)
    tpos_q = t0 + lax.broadcasted_iota(jnp.int32, (tq, 1), 0)
    tpos = t0 + (lax.broadcasted_iota(jnp.int32, (R * tq, 1), 0) & (tq - 1))
    col = lax.broadcasted_iota(jnp.int32, (R * tq, LANE), 1)

    s = _dot_t(stack(qn_ref), kc_ref[...]) * SCALE
    cmask = (CMP_STRIDE * col + (CMP_BLOCK - 1) <= tpos) & (col < kc_ref.shape[0] - 1)
    p, _, _ = _softmax_masked(s, cmask)
    o_cmp = _dot(p.astype(BF16), vc_ref[...])
    imp = p[0:tq] + p[tq:2 * tq] + p[2 * tq:3 * tq] + p[3 * tq:4 * tq]
    score = _dot3(imp, cover_ref[...])
    sel = _select_blocks(score, lax.shift_right_arithmetic(tpos_q, SEL_BLOCK.bit_length() - 1), ns)
    sel = jnp.where(sel, 1.0, 0.0).astype(BF16)

    q_rot = stack(qr_ref)
    blocks_per_tile = tq // SEL_BLOCK

    def reset():
        m_ref[...] = jnp.full_like(m_ref, NEG)
        l_ref[...] = jnp.zeros_like(l_ref)
        acc_ref[...] = jnp.zeros_like(acc_ref)

    def result():
        return acc_ref[...] / jnp.maximum(l_ref[...], 1e-30)

    reset()

    def sel_step(kt, carry):
        k = ks_ref[pl.ds(pl.multiple_of(kt * tq, tq), tq), :]
        v = vs_ref[pl.ds(pl.multiple_of(kt * tq, tq), tq), :]
        s = _dot_t(q_rot, k) * SCALE
        key_blk = lax.shift_right_arithmetic(lax.broadcasted_iota(jnp.int32, (LANE, tq), 1),
                                             SEL_BLOCK.bit_length() - 1)
        expand = lax.broadcasted_iota(jnp.int32, (LANE, tq), 0) == kt * blocks_per_tile + key_blk
        member = rows4(_dot(sel, jnp.where(expand, 1.0, 0.0).astype(BF16)))
        kpos = kt * tq + col
        _online_update(s, (member > 0.5) & (kpos <= tpos), v, m_ref, l_ref, acc_ref)
        return carry

    lax.fori_loop(0, qi + 1, sel_step, 0)
    o_sel = result()

    reset()

    def win_step(kt, carry):
        k = kw_ref[pl.ds(pl.multiple_of(kt * tq, tq), tq), :]
        v = vw_ref[pl.ds(pl.multiple_of(kt * tq, tq), tq), :]
        s = _dot_t(q_rot, k) * SCALE
        diff = tpos - (kt * tq + col)
        _online_update(s, (diff >= 0) & (diff <= NSA_WINDOW), v, m_ref, l_ref, acc_ref)
        return carry

    lax.fori_loop(jnp.maximum(qi - NSA_WINDOW // tq, 0), qi + 1, win_step, 0)
    o_win = result()

    gates = jax.nn.sigmoid(gl_ref[...])
    for r in range(R):
        rs = slice(r * tq, (r + 1) * tq)
        o = (gates[:, 3 * r:3 * r + 1] * o_cmp[rs] + gates[:, 3 * r + 1:3 * r + 2] * o_sel[rs]
             + gates[:, 3 * r + 2:3 * r + 3] * o_win[rs])
        o_ref[:, r * LANE:(r + 1) * LANE] = o.astype(BF16)


def _nsa_attn(qn, qr, proj, cmp_kv, kvb, cover, tq=128):
    B, T, _ = qn.shape
    nc = cmp_kv.shape[3]
    G = NSA_KV_HEADS
    qspec = pl.BlockSpec((None, tq, 4 * LANE), lambda b, g, i: (b, i, g))
    kvspec = lambda c: pl.BlockSpec((None, T, LANE), lambda b, g, i: (b, 0, c * 3 + g))
    return pl.pallas_call(
        functools.partial(_nsa_attn_kernel, tq=tq, ns=T // SEL_BLOCK),
        grid=(B, G, T // tq),
        in_specs=[
            qspec, qspec,
            pl.BlockSpec((None, tq, LANE), lambda b, g, i: (b, i, NSA_GATE_BLK + g)),
            pl.BlockSpec((None, None, None, nc, LANE), lambda b, g, i: (b, 0, g, 0, 0)),
            pl.BlockSpec((None, None, None, nc, LANE), lambda b, g, i: (b, 1, g, 0, 0)),
            kvspec(0), kvspec(1), kvspec(2), kvspec(3),
            pl.BlockSpec((nc, LANE), lambda b, g, i: (0, 0)),
        ],
        out_specs=qspec,
        out_shape=jax.ShapeDtypeStruct((B, T, 1536), BF16),
        scratch_shapes=[pltpu.VMEM((4 * tq, 1), F32), pltpu.VMEM((4 * tq, 1), F32),
                        pltpu.VMEM((4 * tq, LANE), F32)],
        compiler_params=_params(("parallel", "parallel", "arbitrary")),
        name="nsa_attn",
    )(qn, qr, proj, cmp_kv, cmp_kv, kvb, kvb, kvb, kvb, cover)


def _dil_band_kernel(q_ref, kp_ref, kc_ref, vp_ref, vc_ref, o_ref, st_ref, *, tq, window):
    i = pl.program_id(2)
    qpos = i * tq + lax.broadcasted_iota(jnp.int32, (tq, 2 * tq), 0)
    kpos = (i - 1) * tq + lax.broadcasted_iota(jnp.int32, (tq, 2 * tq), 1)
    diff = qpos - kpos
    mask = (diff >= 0) & (diff <= window) & (kpos >= 0)
    lane = lax.broadcasted_iota(jnp.int32, (tq, LANE), 1)
    stats = jnp.zeros((tq, LANE), F32)
    for h in range(DIL_HEADS):
        sl = slice(h * LANE, (h + 1) * LANE)
        k = jnp.concatenate([kp_ref[:, sl], kc_ref[:, sl]], axis=0)
        v = jnp.concatenate([vp_ref[:, sl], vc_ref[:, sl]], axis=0)
        p, m, l = _softmax_masked(_dot_t(q_ref[:, sl], k) * SCALE, mask)
        o_ref[:, sl] = _dot(p.astype(BF16), v)
        stats = jnp.where(lane == h, m, stats)
        stats = jnp.where(lane == DIL_HEADS + h, l, stats)
    st_ref[...] = stats


def _dil_band(q, k, v, g, dil, tq=128):
    B, T, C = q.shape
    S = T // dil
    nb = C // (4 * LANE)
    view = lambda a: a.reshape(B, S, dil * C)
    cur = pl.BlockSpec((None, tq, 4 * LANE), lambda b, r, i: (b, i, r * nb + g))
    prev = pl.BlockSpec((None, tq, 4 * LANE), lambda b, r, i: (b, jnp.maximum(i - 1, 0), r * nb + g))
    o, st = pl.pallas_call(
        functools.partial(_dil_band_kernel, tq=tq, window=DIL_PAIRS[g][0] // dil),
        grid=(B, dil, S // tq),
        in_specs=[cur, prev, cur, prev, cur],
        out_specs=[pl.BlockSpec((None, tq, 4 * LANE), lambda b, r, i: (b, i, r)),
                   pl.BlockSpec((None, tq, LANE), lambda b, r, i: (b, i, r))],
        out_shape=[jax.ShapeDtypeStruct((B, S, dil * 4 * LANE), F32),
                   jax.ShapeDtypeStruct((B, S, dil * LANE), F32)],
        compiler_params=_params(("parallel", "parallel", "parallel")),
        name=f"dil_band{g}",
    )(view(q), view(k), view(k), view(v), view(v))
    return o.reshape(B * T, 4 * LANE), st.reshape(B * T, LANE)


def _mix_groups(os_, ms, ls):
    m_all = jnp.maximum(jnp.maximum(ms[0], ms[1]), ms[2])
    ws = [jnp.exp(m - m_all) * l for m, l in zip(ms, ls)]
    tot = ws[0] + ws[1] + ws[2]
    return (ws[0] / tot) * os_[0] + (ws[1] / tot) * os_[1] + (ws[2] / tot) * os_[2]


def _dil_mix_kernel(o0_ref, o1_ref, o2_ref, s0_ref, s1_ref, s2_ref, o_ref):
    o_refs, s_refs = (o0_ref, o1_ref, o2_ref), (s0_ref, s1_ref, s2_ref)
    for h in range(DIL_HEADS):
        sl = slice(h * LANE, (h + 1) * LANE)
        ms = [s[:, h:h + 1] for s in s_refs]
        ls = [s[:, DIL_HEADS + h:DIL_HEADS + h + 1] for s in s_refs]
        o_ref[:, sl] = _mix_groups([o[:, sl] for o in o_refs], ms, ls).astype(BF16)


def _dil_mix(os_, sts, tm=512):
    M = os_[0].shape[0]
    ospec = pl.BlockSpec((tm, 4 * LANE), lambda i: (i, 0))
    sspec = pl.BlockSpec((tm, LANE), lambda i: (i, 0))
    return pl.pallas_call(
        _dil_mix_kernel,
        grid=(M // tm,),
        in_specs=[ospec] * 3 + [sspec] * 3,
        out_specs=ospec,
        out_shape=jax.ShapeDtypeStruct((M, 4 * LANE), BF16),
        compiler_params=_params(("parallel",)),
        name="dil_mix",
    )(*os_, *sts)


DEC_PAGES_PER_STEP = 16


def _dec_select_kernel(tbl_ref, *refs, n_pp, n_steps):
    pages = refs[:n_pp]
    (qn_ref, w1_ref, b1_ref, w2_ref, kcg_ref, cover_ref, idx_ref, ocmp_ref, xs_ref, h_ref) = refs[n_pp:]
    j = pl.program_id(1)
    cpp = PAGE_SIZE // CMP_STRIDE
    rows = n_pp * cpp
    n = n_steps * rows
    ns = cover_ref.shape[1]

    for g in range(NSA_KV_HEADS):
        for kv in range(2):
            for c in range(CMP_STRIDE):
                xc = jnp.concatenate(
                    [_get_rows(pg, c * NSA_ROWS + g * 4 + kv, CMP_STRIDE * NSA_ROWS, cpp) for pg in pages], axis=0)
                xs_ref[:, c * LANE:(c + 1) * LANE] = xc.astype(BF16)
            h_ref[g * 2 + kv, pl.ds(pl.multiple_of(j * rows, rows), rows), :] = _dot(xs_ref[...], w1_ref[kv])

    @pl.when(j == n_steps - 1)
    def _():
        idx_ref[...] = jnp.zeros_like(idx_ref)
        for g in range(NSA_KV_HEADS):
            kc = _rms(_compress_finish(h_ref[g * 2], b1_ref[0], w2_ref[0]), kcg_ref[...]).astype(BF16)
            vc = _compress_finish(h_ref[g * 2 + 1], b1_ref[1], w2_ref[1]).astype(BF16)
            q = _rows16(qn_ref[:, g * NSA_GROUP * LANE:(g + 1) * NSA_GROUP * LANE], NSA_GROUP).astype(BF16)
            s = _dot_t(q, kc) * SCALE
            valid = lax.broadcasted_iota(jnp.int32, s.shape, 1) < n - 1
            p, _, _ = _softmax_masked(s, valid)
            ocmp_ref[g] = _dot(p.astype(BF16), vc)[0:NSA_GROUP, :]
            rid = lax.broadcasted_iota(jnp.int32, p.shape, 0)
            imp = jnp.sum(jnp.where(rid < NSA_GROUP, p, 0.0), axis=0, keepdims=True)
            score = _dot3(jnp.broadcast_to(imp, (8, n)), cover_ref[...])

            a = jnp.broadcast_to(score[0:1, :], (ns, ns))
            lane = lax.broadcasted_iota(jnp.int32, (ns, ns), 1)
            sub = lax.broadcasted_iota(jnp.int32, (ns, ns), 0)
            cur = n * CMP_STRIDE // SEL_BLOCK
            forced = (lane == 0) | (lane == cur) | (lane == cur - 1)
            a = jnp.where(lane <= cur, jnp.where(forced, FORCE_SCORE, a), NEG)
            at = a.T
            ahead_r = (at > a) | ((at == a) & (sub < lane))
            chosen_r = ((jnp.sum(jnp.where(ahead_r, 1.0, 0.0), axis=0, keepdims=True) < SEL_TOPK)
                        & (a[0:1, :] > 0.5 * NEG))
            ahead_c = (a > at) | ((a == at) & (lane < sub))
            chosen_c = ((jnp.sum(jnp.where(ahead_c, 1.0, 0.0), axis=1, keepdims=True) < SEL_TOPK)
                        & (at[:, 0:1] > 0.5 * NEG))
            before = jnp.sum(jnp.where(chosen_r & (lane < sub), 1.0, 0.0), axis=1, keepdims=True)
            slot = lax.broadcasted_iota(jnp.int32, (ns, LANE), 1)
            onehot = chosen_c & (before == slot.astype(F32))
            blk = lax.broadcasted_iota(jnp.int32, (ns, LANE), 0)
            picked = jnp.sum(jnp.where(onehot, blk.astype(F32), 0.0), axis=0, keepdims=True)
            filled = jnp.sum(jnp.where(onehot, 1.0, 0.0), axis=0, keepdims=True)
            idx_ref[g:g + 1, :] = jnp.where(filled > 0.5, picked, float(cur)).astype(jnp.int32)


def _dec_select(cache, table, qn, w1r, b1, w2, kc_g, cover):
    B, n_pages = table.shape
    n_pp = DEC_PAGES_PER_STEP
    n_steps = n_pages // n_pp
    n = n_pages * (PAGE_SIZE // CMP_STRIDE)
    ns = cover.shape[1]
    G = NSA_KV_HEADS
    const = lambda *shape: pl.BlockSpec(shape, lambda b, j, tbl: (0,) * len(shape))
    page_spec = lambda p: pl.BlockSpec((None, PAGE_SIZE * NSA_ROWS, LANE), lambda b, j, tbl: (tbl[b, j * n_pp + p], 0, 0))
    grid_spec = pltpu.PrefetchScalarGridSpec(
        num_scalar_prefetch=1,
        grid=(B, n_steps),
        in_specs=[page_spec(p) for p in range(n_pp)] + [
            pl.BlockSpec((None, 1, 1536), lambda b, j, tbl: (b, 0, 0)),
            const(2, CMP_STRIDE * LANE, 2 * LANE), const(2, 1, LANE), const(2, LANE, LANE), const(1, LANE),
            const(n, ns),
        ],
        out_specs=[pl.BlockSpec((None, 8, LANE), lambda b, j, tbl: (b, 0, 0)),
                   pl.BlockSpec((None, G, NSA_GROUP, LANE), lambda b, j, tbl: (b, 0, 0, 0))],
        scratch_shapes=[pltpu.VMEM((n_pp * (PAGE_SIZE // CMP_STRIDE), CMP_STRIDE * LANE), BF16),
                        pltpu.VMEM((2 * G, n, 2 * LANE), F32)],
    )
    return pl.pallas_call(
        functools.partial(_dec_select_kernel, n_pp=n_pp, n_steps=n_steps),
        grid_spec=grid_spec,
        out_shape=[jax.ShapeDtypeStruct((B, 8, LANE), jnp.int32),
                   jax.ShapeDtypeStruct((B, G, NSA_GROUP, LANE), F32)],
        compiler_params=_params(("arbitrary", "arbitrary")),
        name="dec_select",
    )(table, *([cache] * n_pp), qn, w1r, b1, w2, kc_g, cover)


def _dec_attn_kernel(tbl_ref, idx_ref, *refs, n_sel, cur, wb):
    blocks = refs[:n_sel]
    (qr_ref, kvn_ref, win_ref, gl_ref, ocmp_ref, o_ref) = refs[n_sel:]
    b, g = pl.program_id(0), pl.program_id(1)
    R = NSA_GROUP
    q = _rows16(qr_ref[...], R).astype(BF16)
    qf = q.astype(F32)
    new = kvn_ref[...]
    ks_n, vs_n, kw_n, vw_n = (new[:, c * LANE:(c + 1) * LANE] for c in range(4))

    k = jnp.concatenate([_get_rows(r, g * 4 + 2, NSA_ROWS, SEL_BLOCK) for r in blocks], axis=0).astype(BF16)
    v = jnp.concatenate([_get_rows(r, g * 4 + 3, NSA_ROWS, SEL_BLOCK) for r in blocks], axis=0).astype(BF16)
    s = _dot_t(q, k) * SCALE
    blk_of = lax.shift_right_arithmetic(lax.broadcasted_iota(jnp.int32, s.shape, 1), SEL_BLOCK.bit_length() - 1)
    valid = jnp.zeros(s.shape, jnp.int32)
    for n in range(n_sel):
        is_past = jnp.where(idx_ref[(b * NSA_KV_HEADS + g) * n_sel + n] != cur, 1, 0)
        valid = jnp.where(blk_of == n, is_past, valid)
    valid = valid > 0
    s_new = jnp.sum(qf * ks_n, axis=-1, keepdims=True) * SCALE
    s = jnp.where(valid, s, NEG)
    m = jnp.maximum(jnp.max(s, axis=-1, keepdims=True), s_new)
    e = jnp.where(valid, jnp.exp(s - m), 0.0)
    e_new = jnp.exp(s_new - m)
    l = jnp.sum(e, axis=-1, keepdims=True) + e_new
    o_sel = (_dot(e.astype(BF16), v) + e_new.astype(BF16).astype(F32) * vs_n) / l

    s = _dot_t(q, _get_rows(win_ref, g * 2, WIN_ROWS, wb).astype(BF16)) * SCALE
    s_new = jnp.sum(qf * kw_n, axis=-1, keepdims=True) * SCALE
    m = jnp.maximum(jnp.max(s, axis=-1, keepdims=True), s_new)
    e = jnp.exp(s - m)
    e_new = jnp.exp(s_new - m)
    l = jnp.sum(e, axis=-1, keepdims=True) + e_new
    o_win = (_dot(e.astype(BF16), _get_rows(win_ref, g * 2 + 1, WIN_ROWS, wb).astype(BF16))
             + e_new.astype(BF16).astype(F32) * vw_n) / l

    gates = jax.nn.sigmoid(gl_ref[...])
    o_cmp = ocmp_ref[...]
    for r in range(R):
        o = (gates[:, 3 * r:3 * r + 1] * o_cmp[r:r + 1] + gates[:, 3 * r + 1:3 * r + 2] * o_sel[r:r + 1]
             + gates[:, 3 * r + 2:3 * r + 3] * o_win[r:r + 1])
        o_ref[:, r * LANE:(r + 1) * LANE] = o


def _dec_attn(cache, table, idx, qr, kvb, win_state, proj, ocmp):
    B, n_pages = table.shape
    n_sel = SEL_TOPK
    cur = n_pages * PAGE_SIZE // SEL_BLOCK
    wb = win_state.shape[1] // WIN_ROWS
    assert wb <= NSA_WINDOW
    halves = PAGE_SIZE // SEL_BLOCK

    def blk_spec(n):
        def imap(b, g, tbl, idx):
            i = jnp.minimum(idx[(b * NSA_KV_HEADS + g) * n_sel + n], cur - 1)
            return (tbl[b, i // halves], i % halves, 0)
        return pl.BlockSpec((None, SEL_BLOCK * NSA_ROWS, LANE), imap)

    grid_spec = pltpu.PrefetchScalarGridSpec(
        num_scalar_prefetch=2,
        grid=(B, NSA_KV_HEADS),
        in_specs=[blk_spec(n) for n in range(n_sel)] + [
            pl.BlockSpec((None, 1, 4 * LANE), lambda b, g, tbl, idx: (b, 0, g)),
            pl.BlockSpec((None, 1, 4 * LANE), lambda b, g, tbl, idx: (b, 0, g)),
            pl.BlockSpec((None, wb * WIN_ROWS, LANE), lambda b, g, tbl, idx: (b, 0, 0)),
            pl.BlockSpec((None, 1, LANE), lambda b, g, tbl, idx: (b, 0, NSA_GATE_BLK + g)),
            pl.BlockSpec((None, None, NSA_GROUP, LANE), lambda b, g, tbl, idx: (b, g, 0, 0)),
        ],
        out_specs=pl.BlockSpec((None, 1, 4 * LANE), lambda b, g, tbl, idx: (b, 0, g)),
    )
    kvn = kvb.reshape(B, 1, 4, NSA_KV_HEADS, LANE).transpose(0, 1, 3, 2, 4).reshape(B, 1, 1536)
    return pl.pallas_call(
        functools.partial(_dec_attn_kernel, n_sel=n_sel, cur=cur, wb=wb),
        grid_spec=grid_spec,
        out_shape=jax.ShapeDtypeStruct((B, 1, 1536), F32),
        compiler_params=_params(("arbitrary", "arbitrary")),
        name="dec_attn",
    )(table, idx, *([cache] * n_sel), qr, kvn, win_state, proj, ocmp)


def _dec_dil_kernel(q_ref, kn_ref, vn_ref, s0_ref, s1_ref, s2_ref, o_ref):
    qall = q_ref[...]
    kn = kn_ref[...].astype(BF16).astype(F32)
    vn = vn_ref[...].astype(BF16).astype(F32)
    for h in range(DIL_HEADS):
        os_, ms, ls = [], [], []
        for g, st in enumerate((s0_ref, s1_ref, s2_ref)):
            hs = slice((g * DIL_HEADS + h) * LANE, (g * DIL_HEADS + h + 1) * LANE)
            window, dil = DIL_PAIRS[g]
            q = _rows16(qall[:, hs], 1).astype(BF16)
            k = _get_rows(st, h, KVH_ROWS * dil, window // dil).astype(BF16)
            v = _get_rows(st, DIL_HEADS + h, KVH_ROWS * dil, window // dil).astype(BF16)
            s = _dot_t(q, k) * SCALE
            s_new = jnp.sum(q.astype(F32) * kn[:, hs], axis=-1, keepdims=True) * SCALE
            m = jnp.maximum(jnp.max(s, axis=-1, keepdims=True), s_new)
            e = jnp.exp(s - m)
            e_new = jnp.exp(s_new - m)
            l = jnp.sum(e, axis=-1, keepdims=True) + e_new
            ln = jnp.maximum(l, 1e-30)
            o = _dot((e / ln).astype(BF16), v) + (e_new / ln).astype(BF16).astype(F32) * vn[:, hs]
            os_.append(o)
            ms.append(m)
            ls.append(l)
        o_ref[:, h * LANE:(h + 1) * LANE] = _mix_groups(os_, ms, ls)[0:1, :]


def _dec_dil(qr, kn, vn, states):
    B = qr.shape[0]
    row = pl.BlockSpec((None, 1, 1536), lambda b: (b, 0, 0))
    in_specs = [row, row, row]
    for g, (window, dil) in enumerate(DIL_PAIRS):
        assert states[g].shape[1] == window * KVH_ROWS, "rolling buffer shorter than the window is not supported"
        in_specs.append(pl.BlockSpec((None, window * KVH_ROWS, LANE), lambda b: (b, 0, 0)))
    return pl.pallas_call(
        _dec_dil_kernel,
        grid=(B,),
        in_specs=in_specs,
        out_specs=pl.BlockSpec((None, 1, 4 * LANE), lambda b: (b, 0, 0)),
        out_shape=jax.ShapeDtypeStruct((B, 1, 4 * LANE), F32),
        compiler_params=_params(("parallel",)),
        name="dec_dil",
    )(qr, kn, vn, *states)


def _rope_tables(pos):
    half = HEAD_DIM // 2
    inv = ROPE_THETA ** (-jnp.arange(half, dtype=F32) / half)
    ang = pos.astype(F32)[:, None] * inv
    cos, sin = jnp.cos(ang), jnp.sin(ang)
    return jnp.concatenate([cos, cos], axis=-1), jnp.concatenate([-sin, sin], axis=-1)


def _cover(nc, ns, rows, cols):
    c0 = jnp.arange(nc)[:, None] * CMP_STRIDE
    s0 = jnp.arange(ns)[None, :] * SEL_BLOCK
    cover = jnp.clip(jnp.minimum(c0 + CMP_BLOCK, s0 + SEL_BLOCK) - jnp.maximum(c0, s0), 0, CMP_BLOCK)
    cover = cover.astype(F32) / CMP_BLOCK
    return jnp.pad(cover, ((0, rows - nc), (0, cols - ns))).astype(BF16)


def _pad_gains(*gs):
    return jnp.pad(jnp.stack(gs, axis=0), ((0, 8 - len(gs)), (0, 0)))


def kernel(x_prompt, x_sample, mem_prompt, cache_nsa_kv, page_table, state_nsa_win, state_dil_0, state_dil_1,
           state_dil_2, cache_mem_kv, ff_norm, ff_w_gate, ff_w_up, ff_w_down, mix_norm, mem_norm, w_mem_kv,
           mem_q_g, mem_k_g, nsa_w_in, nsa_q_g, nsa_kc_g, nsa_ks_g, nsa_kw_g, nsa_cmp_w1, nsa_cmp_b1, nsa_cmp_w2,
           nsa_w_out, dil_w_in, dil_q_g, dil_k_g, dil_w_out):
    B, T, D = x_prompt.shape
    Bs = x_sample.shape[0]
    assert x_sample.shape[1] == 1, "the sample group is a single-token decode step"
    n_pages = page_table.shape[1]
    past_len = n_pages * PAGE_SIZE
    H, G, d = N_MIX_HEADS, NSA_KV_HEADS, HEAD_DIM

    wg, wu, wd = ff_w_gate.astype(BF16), ff_w_up.astype(BF16), ff_w_down.astype(BF16)
    ffg = ff_norm.reshape(ff_norm.shape[0], 2, 1, D)
    gate_w = jnp.pad(nsa_w_in[:, H * d:H * d + 3 * H].reshape(D, G, 3 * NSA_GROUP), ((0, 0), (0, 0), (0, LANE - 12)))
    nsa_w = jnp.concatenate([nsa_w_in[:, :H * d], nsa_w_in[:, H * d + 3 * H:], gate_w.reshape(D, G * LANE)], axis=1)
    nsa_w = jnp.pad(nsa_w, ((0, 0), (0, PROJ_N - nsa_w.shape[1]))).astype(BF16)
    dil_w = dil_w_in.astype(BF16)
    nsa_wo, dil_wo = nsa_w_out.astype(BF16), dil_w_out.astype(BF16)
    w1r = nsa_cmp_w1.reshape(2, 2, CMP_STRIDE, d, d).transpose(0, 2, 3, 1, 4).reshape(2, CMP_STRIDE * d, 2 * d)
    w1r = w1r.astype(BF16)
    cmp_b1 = nsa_cmp_b1.reshape(2, 1, d)
    cmp_w2 = nsa_cmp_w2.astype(BF16)
    kc_g = nsa_kc_g.reshape(1, d)

    n_pool = cache_nsa_kv.shape[0]
    cache_rows = cache_nsa_kv.transpose(0, 1, 3, 2, 4).reshape(n_pool, PAGE_SIZE * NSA_ROWS, LANE)
    win_rows = state_nsa_win.transpose(0, 1, 3, 2, 4).reshape(Bs, state_nsa_win.shape[1] * WIN_ROWS, LANE)
    dil_states = (state_dil_0, state_dil_1, state_dil_2)
    dil_rows = [s.reshape(Bs, s.shape[1] * KVH_ROWS, LANE) for s in dil_states]
    mem_rows_s = cache_mem_kv.reshape(2, Bs, N_MEM * KVH_ROWS, LANE)

    mem2d = mem_prompt.reshape(B * N_MEM, D)
    mem_rows_p = []
    for i in range(2):
        kv = _norm_matmul(mem2d, mem_norm[i], w_mem_kv[i].astype(BF16), tm=256, tn=1024)
        mem_rows_p.append(_memkv_post(kv, mem_k_g[i]).reshape(B, N_MEM * KVH_ROWS, LANE))

    cos_p, sin_p = _rope_tables(jnp.tile(jnp.arange(T, dtype=jnp.int32), B))
    cos_s, sin_s = _rope_tables(jnp.full((Bs,), past_len, jnp.int32))
    row3 = lambda a: a.astype(F32).reshape(Bs, 1, a.shape[-1])

    xp = _ffn(x_prompt.reshape(B * T, D), ffg, wg, wu, wd, 0, 0, tm=512)
    xs = _ffn(x_sample.reshape(Bs, D), ffg, wg, wu, wd, 0, 0, tm=Bs)
    nsa_gains = _pad_gains(nsa_q_g, nsa_ks_g, nsa_kw_g, mem_q_g[0])

    proj_p = _norm_matmul(xp, mix_norm[0], nsa_w, tm=512, tn=1024)
    qn_p, qr_p, cmp_p, rows_p, win_p, kvb_p, mq_p = _nsa_post(proj_p, cos_p, sin_p, nsa_gains, tm=256)
    cmp_kv = _cmp_prompt(cmp_p.reshape(B, T, 768), w1r, cmp_b1, cmp_w2, kc_g)
    nc_p = T // CMP_STRIDE
    cover_p = _cover(nc_p - 1, T // SEL_BLOCK, nc_p, LANE)
    o_mix_p = _nsa_attn(qn_p.reshape(B, T, 1536), qr_p.reshape(B, T, 1536), proj_p.reshape(B, T, PROJ_N),
                        cmp_kv, kvb_p.reshape(B, T, 1536), cover_p)
    o_mem_p = _mem_attn(mq_p.reshape(B, T, 512), mem_rows_p[0], tq=256)
    xp = _out_proj(xp, o_mix_p.reshape(B * T, 1536), o_mem_p.reshape(B * T, 512), nsa_wo, tm=512)

    proj_s = _norm_matmul(xs, mix_norm[0], nsa_w, tm=Bs, tn=1024)
    qn_s, qr_s, _, rows_s, win_s, kvb_s, mq_s = _nsa_post(proj_s, cos_s, sin_s, nsa_gains, tm=Bs)
    nc_s = past_len // CMP_STRIDE
    ns_s = -(-(past_len + 1) // SEL_BLOCK)
    cover_s = _cover(nc_s - 1, ns_s, nc_s, -(-ns_s // LANE) * LANE)
    sel_idx, ocmp_s = _dec_select(cache_rows, page_table, row3(qn_s), w1r, cmp_b1, cmp_w2, kc_g, cover_s)
    o_mix_s = _dec_attn(cache_rows, page_table, sel_idx[:, :G, :SEL_TOPK].reshape(-1), row3(qr_s), row3(kvb_s),
                        win_rows, proj_s.reshape(Bs, 1, PROJ_N), ocmp_s)
    o_mem_s = _mem_attn(row3(mq_s), mem_rows_s[0], tq=1)
    xs = _out_proj(xs, o_mix_s.reshape(Bs, 1536).astype(BF16), o_mem_s.reshape(Bs, 512).astype(BF16), nsa_wo,
                   tm=Bs)

    xp = _ffn(xp, ffg, wg, wu, wd, 0, 1, tm=512)
    xs = _ffn(xs, ffg, wg, wu, wd, 0, 1, tm=Bs)

    xp = _ffn(xp, ffg, wg, wu, wd, 1, 0, tm=512)
    xs = _ffn(xs, ffg, wg, wu, wd, 1, 0, tm=Bs)
    dil_gains = _pad_gains(dil_q_g, dil_k_g, mem_q_g[1])

    dproj_p = _norm_matmul(xp, mix_norm[1], dil_w, tm=512, tn=1024)
    dq_p, dkb_p, dvb_p, dmq_p, *dnew_p = _dil_post(dproj_p, cos_p, sin_p, dil_gains, tm=256)
    band = [_dil_band(dq_p.reshape(B, T, 1536), dkb_p.reshape(B, T, 1536), dvb_p.reshape(B, T, 1536), g, dil)
            for g, (_, dil) in enumerate(DIL_PAIRS)]
    o_dil_p = _dil_mix([o for o, _ in band], [s for _, s in band])
    o_dmem_p = _mem_attn(dmq_p.reshape(B, T, 512), mem_rows_p[1], tq=256)
    xp = _out_proj(xp, o_dil_p, o_dmem_p.reshape(B * T, 512), dil_wo, tm=512)

    dproj_s = _norm_matmul(xs, mix_norm[1], dil_w, tm=Bs, tn=1024)
    dq_s, _, _, dmq_s, *dnew_s = _dil_post(dproj_s, cos_s, sin_s, dil_gains, tm=Bs)
    dnew_s = [s.reshape(Bs, 1, 2, DIL_HEADS, d) for s in dnew_s]
    dk_s = jnp.concatenate([s[:, :, 0].reshape(Bs, 1, DIL_HEADS * d) for s in dnew_s], axis=-1)
    dv_s = jnp.concatenate([s[:, :, 1].reshape(Bs, 1, DIL_HEADS * d) for s in dnew_s], axis=-1)
    o_dil_s = _dec_dil(row3(dq_s), dk_s, dv_s, dil_rows)
    o_dmem_s = _mem_attn(row3(dmq_s), mem_rows_s[1], tq=1)
    xs = _out_proj(xs, o_dil_s.reshape(Bs, 512).astype(BF16), o_dmem_s.reshape(Bs, 512).astype(BF16), dil_wo,
                   tm=Bs)

    xp = _ffn(xp, ffg, wg, wu, wd, 1, 1, tm=512)
    xs = _ffn(xs, ffg, wg, wu, wd, 1, 1, tm=Bs)

    unrow = lambda a, n, outer, inner: a.reshape(n, -1, outer, inner, d).transpose(0, 1, 3, 2, 4)
    nsa_kv_p = unrow(rows_p, B, G, 4)
    nsa_kv_s = unrow(rows_s, Bs, G, 4)
    nsa_win_p = unrow(win_p, B, G, 2)[:, -min(NSA_WINDOW, T):]
    nsa_win_s = jnp.concatenate([state_nsa_win, unrow(win_s, Bs, G, 2)], axis=1)[:, -state_nsa_win.shape[1]:]
    outs_dil = []
    for g, (window, _) in enumerate(DIL_PAIRS):
        st = dil_states[g]
        outs_dil.append(dnew_p[g].reshape(B, T, 2, DIL_HEADS, d)[:, -min(window, T):])
        outs_dil.append(jnp.concatenate([st, dnew_s[g]], axis=1)[:, -st.shape[1]:])
    mem_kv_out = jnp.stack([kv.reshape(B, N_MEM, 2, N_MEM_HEADS, d) for kv in mem_rows_p], axis=0)
    return (xp.reshape(B, T, D), xs.reshape(Bs, 1, D), nsa_kv_p, nsa_kv_s, nsa_win_p, nsa_win_s,
            *outs_dil, mem_kv_out)
```

```python
import functools

import jax
import jax.numpy as jnp
from jax import lax
from jax.experimental import pallas as pl
from jax.experimental.pallas import tpu as pltpu

F32 = jnp.float32
BF16 = jnp.bfloat16

D_MODEL = 2048
HEAD_DIM = 128
N_MIX_HEADS = 12
N_MEM_HEADS = 4
N_MEM = 256
NSA_KV_HEADS = 3
NSA_GROUP = 4
CMP_BLOCK = 32
CMP_STRIDE = 16
SEL_BLOCK = 64
SEL_TOPK = 16
NSA_WINDOW = 512
DIL_PAIRS = ((128, 1), (512, 4), (2048, 16))
DIL_HEADS = 4
PAGE_SIZE = 128
ROPE_THETA = 10000.0
EPS = 1e-6
SCALE = HEAD_DIM ** -0.5
NEG = -1e30
FORCE_SCORE = 1e6

PROJ_N = 5120
NSA_GATE_BLK = 34
LANE = 128
VMEM_LIMIT = 56 * 1024 * 1024


def _params(sem):
    return pltpu.CompilerParams(dimension_semantics=sem, vmem_limit_bytes=VMEM_LIMIT)


def _dot(a, b):
    return jnp.dot(a, b, preferred_element_type=F32)


def _dot_t(a, b):
    return lax.dot_general(a, b, (((1,), (1,)), ((), ())), preferred_element_type=F32)


def _dot3(a, b):
    a1 = a.astype(BF16)
    r1 = a - a1.astype(F32)
    a2 = r1.astype(BF16)
    a3 = (r1 - a2.astype(F32)).astype(BF16)
    return _dot(a1, b) + _dot(a2, b) + _dot(a3, b)


def _rms(x, g):
    return x * lax.rsqrt(jnp.mean(x * x, axis=-1, keepdims=True) + EPS) * g


def _rope(x, cos, sin):
    return x * cos + pltpu.roll(x, HEAD_DIM // 2, 1) * sin


def _rows16(row, nrep):
    rid = lax.broadcasted_iota(jnp.int32, (16, LANE), 0) & (nrep - 1)
    out = jnp.zeros((16, LANE), F32)
    for r in range(nrep):
        piece = jnp.broadcast_to(row[:, r * LANE:(r + 1) * LANE], (16, LANE))
        out = jnp.where(rid == r, piece, out)
    return out


def _softmax_masked(s, mask):
    s = jnp.where(mask, s, NEG)
    m = jnp.max(s, axis=-1, keepdims=True)
    e = jnp.where(mask, jnp.exp(s - m), 0.0)
    l = jnp.sum(e, axis=-1, keepdims=True)
    return e / jnp.maximum(l, 1e-30), m, l


def _ffn_kernel(x_ref, g_ref, wg_ref, wu_ref, wd_ref, o_ref, h_ref, acc_ref, *, nf):
    f = pl.program_id(1)

    @pl.when(f == 0)
    def _():
        h_ref[...] = _rms(x_ref[...], g_ref[...]).astype(BF16)
        acc_ref[...] = jnp.zeros_like(acc_ref)

    h = h_ref[...]
    gate = _dot(h, wg_ref[...])
    up = _dot(h, wu_ref[...])
    a = (gate * jax.nn.sigmoid(gate) * up).astype(BF16)
    acc_ref[...] += _dot(a, wd_ref[...])

    @pl.when(f == nf - 1)
    def _():
        o_ref[...] = x_ref[...] + 0.5 * acc_ref[...]


def _ffn(x, g, wg, wu, wd, li, lj, tm, tf=512):
    M, D = x.shape
    F = wg.shape[-1]
    nf = F // tf
    return pl.pallas_call(
        functools.partial(_ffn_kernel, nf=nf),
        grid=(M // tm, nf),
        in_specs=[
            pl.BlockSpec((tm, D), lambda i, f: (i, 0)),
            pl.BlockSpec((None, None, 1, D), lambda i, f: (li, lj, 0, 0)),
            pl.BlockSpec((None, None, D, tf), lambda i, f: (li, lj, 0, f)),
            pl.BlockSpec((None, None, D, tf), lambda i, f: (li, lj, 0, f)),
            pl.BlockSpec((None, None, tf, D), lambda i, f: (li, lj, f, 0)),
        ],
        out_specs=pl.BlockSpec((tm, D), lambda i, f: (i, 0)),
        out_shape=jax.ShapeDtypeStruct((M, D), F32),
        scratch_shapes=[pltpu.VMEM((tm, D), BF16), pltpu.VMEM((tm, D), F32)],
        compiler_params=_params(("parallel", "arbitrary")),
        name="ffn",
    )(x, g, wg, wu, wd)


def _nmm_kernel(x_ref, g_ref, w_ref, o_ref, h_ref):
    @pl.when(pl.program_id(1) == 0)
    def _():
        h_ref[...] = _rms(x_ref[...], g_ref[...]).astype(BF16)

    o_ref[...] = _dot(h_ref[...], w_ref[...])


def _norm_matmul(x, g, w, tm, tn):
    M, D = x.shape
    N = w.shape[1]
    return pl.pallas_call(
        _nmm_kernel,
        grid=(M // tm, N // tn),
        in_specs=[
            pl.BlockSpec((tm, D), lambda i, j: (i, 0)),
            pl.BlockSpec((1, D), lambda i, j: (0, 0)),
            pl.BlockSpec((D, tn), lambda i, j: (0, j)),
        ],
        out_specs=pl.BlockSpec((tm, tn), lambda i, j: (i, j)),
        out_shape=jax.ShapeDtypeStruct((M, N), F32),
        scratch_shapes=[pltpu.VMEM((tm, D), BF16)],
        compiler_params=_params(("parallel", "arbitrary")),
        name="norm_matmul",
    )(x, g.reshape(1, D), w)


def _oproj_kernel(x_ref, a_ref, b_ref, w_ref, o_ref, *, ka):
    o_ref[...] = x_ref[...] + _dot(a_ref[...], w_ref[:ka, :]) + _dot(b_ref[...], w_ref[ka:, :])


def _out_proj(x, a, b, w, tm, tn=1024):
    M, D = x.shape
    ka, kb = a.shape[1], b.shape[1]
    return pl.pallas_call(
        functools.partial(_oproj_kernel, ka=ka),
        grid=(M // tm, D // tn),
        in_specs=[
            pl.BlockSpec((tm, tn), lambda i, j: (i, j)),
            pl.BlockSpec((tm, ka), lambda i, j: (i, 0)),
            pl.BlockSpec((tm, kb), lambda i, j: (i, 0)),
            pl.BlockSpec((ka + kb, tn), lambda i, j: (0, j)),
        ],
        out_specs=pl.BlockSpec((tm, tn), lambda i, j: (i, j)),
        out_shape=jax.ShapeDtypeStruct((M, D), F32),
        compiler_params=_params(("parallel", "parallel")),
        name="out_proj",
    )(x, a, b, w)


def _put_rows(ref, row, rows_per_token, val):
    ref[pl.ds(row, val.shape[0], stride=rows_per_token), :] = val


def _get_rows(ref, row, rows_per_token, n):
    return ref[pl.ds(row, n, stride=rows_per_token), :]


NSA_ROWS = 4 * NSA_KV_HEADS
WIN_ROWS = 2 * NSA_KV_HEADS
KVH_ROWS = 2 * DIL_HEADS


def _nsa_post_kernel(p_ref, cos_ref, sin_ref, g_ref, qn_ref, qr_ref, cmp_ref, rows_ref, win_ref, kvb_ref, mq_ref):
    cos, sin = cos_ref[...], sin_ref[...]
    q_g, ks_g, kw_g, mq_g = g_ref[0:1, :], g_ref[1:2, :], g_ref[2:3, :], g_ref[3:4, :]

    def tile(i):
        return p_ref[:, i * LANE:(i + 1) * LANE]

    for h in range(N_MIX_HEADS):
        qn = _rms(tile(h), q_g)
        qn_ref[:, h * LANE:(h + 1) * LANE] = qn.astype(BF16)
        qr_ref[:, h * LANE:(h + 1) * LANE] = _rope(qn, cos, sin).astype(BF16)
    for g in range(NSA_KV_HEADS):
        kc, vc = tile(12 + g), tile(15 + g)
        ks = _rope(_rms(tile(18 + g), ks_g), cos, sin)
        vs = tile(21 + g)
        kw = _rope(_rms(tile(24 + g), kw_g), cos, sin)
        vw = tile(27 + g)
        for c, val in enumerate((kc, vc)):
            cmp_ref[:, (c * 3 + g) * LANE:(c * 3 + g + 1) * LANE] = val
        for c, val in enumerate((kc, vc, ks, vs)):
            _put_rows(rows_ref, g * 4 + c, NSA_ROWS, val)
        for c, val in enumerate((kw, vw)):
            _put_rows(win_ref, g * 2 + c, WIN_ROWS, val)
        for c, val in enumerate((ks, vs, kw, vw)):
            kvb_ref[:, (c * 3 + g) * LANE:(c * 3 + g + 1) * LANE] = val.astype(BF16)
    for h in range(N_MEM_HEADS):
        mq_ref[:, h * LANE:(h + 1) * LANE] = _rms(tile(30 + h), mq_g).astype(BF16)


def _nsa_post(p, cos, sin, gains, tm):
    M = p.shape[0]
    row = lambda n: pl.BlockSpec((tm, n), lambda i: (i, 0))
    flat = lambda r: pl.BlockSpec((tm * r, LANE), lambda i: (i, 0))
    return pl.pallas_call(
        _nsa_post_kernel,
        grid=(M // tm,),
        in_specs=[row(PROJ_N), row(LANE), row(LANE), pl.BlockSpec((8, LANE), lambda i: (0, 0))],
        out_specs=[row(1536), row(1536), row(768), flat(NSA_ROWS), flat(WIN_ROWS), row(1536), row(512)],
        out_shape=[
            jax.ShapeDtypeStruct((M, 1536), BF16),
            jax.ShapeDtypeStruct((M, 1536), BF16),
            jax.ShapeDtypeStruct((M, 768), F32),
            jax.ShapeDtypeStruct((M * NSA_ROWS, LANE), F32),
            jax.ShapeDtypeStruct((M * WIN_ROWS, LANE), F32),
            jax.ShapeDtypeStruct((M, 1536), BF16),
            jax.ShapeDtypeStruct((M, 512), BF16),
        ],
        compiler_params=_params(("parallel",)),
        name="nsa_post",
    )(p, cos, sin, gains)


def _dil_post_kernel(p_ref, cos_ref, sin_ref, g_ref, qr_ref, kb_ref, vb_ref, mq_ref, st0_ref, st1_ref, st2_ref):
    cos, sin = cos_ref[...], sin_ref[...]
    q_g, k_g, mq_g = g_ref[0:1, :], g_ref[1:2, :], g_ref[2:3, :]
    st_refs = (st0_ref, st1_ref, st2_ref)
    for h in range(N_MIX_HEADS):
        sl = slice(h * LANE, (h + 1) * LANE)
        qr_ref[:, sl] = _rope(_rms(p_ref[:, sl], q_g), cos, sin).astype(BF16)
        k = _rope(_rms(p_ref[:, (12 + h) * LANE:(13 + h) * LANE], k_g), cos, sin)
        v = p_ref[:, (24 + h) * LANE:(25 + h) * LANE]
        kb_ref[:, sl] = k.astype(BF16)
        vb_ref[:, sl] = v.astype(BF16)
        st = st_refs[h // DIL_HEADS]
        _put_rows(st, h % DIL_HEADS, KVH_ROWS, k)
        _put_rows(st, DIL_HEADS + h % DIL_HEADS, KVH_ROWS, v)
    for h in range(N_MEM_HEADS):
        mq_ref[:, h * LANE:(h + 1) * LANE] = _rms(p_ref[:, (36 + h) * LANE:(37 + h) * LANE], mq_g).astype(BF16)


def _dil_post(p, cos, sin, gains, tm):
    M = p.shape[0]
    row = lambda n: pl.BlockSpec((tm, n), lambda i: (i, 0))
    flat = pl.BlockSpec((tm * KVH_ROWS, LANE), lambda i: (i, 0))
    st_shape = jax.ShapeDtypeStruct((M * KVH_ROWS, LANE), F32)
    return pl.pallas_call(
        _dil_post_kernel,
        grid=(M // tm,),
        in_specs=[row(PROJ_N), row(LANE), row(LANE), pl.BlockSpec((8, LANE), lambda i: (0, 0))],
        out_specs=[row(1536), row(1536), row(1536), row(512), flat, flat, flat],
        out_shape=[
            jax.ShapeDtypeStruct((M, 1536), BF16),
            jax.ShapeDtypeStruct((M, 1536), BF16),
            jax.ShapeDtypeStruct((M, 1536), BF16),
            jax.ShapeDtypeStruct((M, 512), BF16),
            st_shape, st_shape, st_shape,
        ],
        compiler_params=_params(("parallel",)),
        name="dil_post",
    )(p, cos, sin, gains)


def _memkv_post_kernel(x_ref, g_ref, o_ref):
    for h in range(N_MEM_HEADS):
        _put_rows(o_ref, h, KVH_ROWS, _rms(x_ref[:, h * LANE:(h + 1) * LANE], g_ref[...]))
        _put_rows(o_ref, N_MEM_HEADS + h, KVH_ROWS, x_ref[:, (N_MEM_HEADS + h) * LANE:(N_MEM_HEADS + h + 1) * LANE])


def _memkv_post(x, g, tm=256):
    M, N = x.shape
    return pl.pallas_call(
        _memkv_post_kernel,
        grid=(M // tm,),
        in_specs=[pl.BlockSpec((tm, N), lambda i: (i, 0)), pl.BlockSpec((1, LANE), lambda i: (0, 0))],
        out_specs=pl.BlockSpec((tm * KVH_ROWS, LANE), lambda i: (i, 0)),
        out_shape=jax.ShapeDtypeStruct((M * KVH_ROWS, LANE), F32),
        compiler_params=_params(("parallel",)),
        name="memkv_post",
    )(x, g.reshape(1, LANE))


def _mem_attn_kernel(q_ref, kv_ref, o_ref, *, tq):
    for h in range(N_MEM_HEADS):
        sl = slice(h * LANE, (h + 1) * LANE)
        if tq == 1:
            q = _rows16(q_ref[:, sl], 1).astype(BF16)
        else:
            q = q_ref[:, sl]
        k = _get_rows(kv_ref, h, KVH_ROWS, N_MEM).astype(BF16)
        v = _get_rows(kv_ref, N_MEM_HEADS + h, KVH_ROWS, N_MEM).astype(BF16)
        s = _dot_t(q, k) * SCALE
        m = jnp.max(s, axis=-1, keepdims=True)
        e = jnp.exp(s - m)
        p = e / jnp.sum(e, axis=-1, keepdims=True)
        o = _dot(p.astype(BF16), v)
        o_ref[:, sl] = o[0:tq, :].astype(o_ref.dtype)


def _mem_attn(q, kv, tq):
    B, T, _ = q.shape
    return pl.pallas_call(
        functools.partial(_mem_attn_kernel, tq=tq),
        grid=(B, T // tq),
        in_specs=[
            pl.BlockSpec((None, tq, 512), lambda b, i: (b, i, 0)),
            pl.BlockSpec((None, N_MEM * KVH_ROWS, LANE), lambda b, i: (b, 0, 0)),
        ],
        out_specs=pl.BlockSpec((None, tq, 512), lambda b, i: (b, i, 0)),
        out_shape=jax.ShapeDtypeStruct((B, T, 512), q.dtype),
        compiler_params=_params(("parallel", "parallel")),
        name="mem_attn",
    )(q, kv)


def _gelu_tanh(x):
    return 0.5 * x * (1.0 + jnp.tanh(0.7978845608028654 * (x + 0.044715 * (x * x * x))))


def _compress_finish(h, b1, w2):
    n = h.shape[0]
    hid = b1 + h[:, :LANE] + pltpu.roll(h[:, LANE:], n - 1, 0)
    return _dot(_gelu_tanh(hid).astype(BF16), w2)


def _compress(x_bf, w1, b1, w2):
    return _compress_finish(_dot(x_bf, w1), b1, w2)


def _cmp_prompt_kernel(x_ref, w1_ref, b1_ref, w2_ref, kcg_ref, o_ref, xs_ref, *, n):
    kv = pl.program_id(1)
    for c in range(CMP_STRIDE):
        xs_ref[:, c * LANE:(c + 1) * LANE] = x_ref[pl.ds(c, n, stride=CMP_STRIDE), :].astype(BF16)
    out = _compress(xs_ref[...], w1_ref[...], b1_ref[...], w2_ref[...])
    out = jnp.where(kv == 0, _rms(out, kcg_ref[...]), out)
    rid = lax.broadcasted_iota(jnp.int32, out.shape, 0)
    o_ref[...] = jnp.where(rid < n - 1, out, 0.0).astype(BF16)


def _cmp_prompt(rows, w1r, b1, w2, kc_g):
    B, T, _ = rows.shape
    n = T // CMP_STRIDE
    return pl.pallas_call(
        functools.partial(_cmp_prompt_kernel, n=n),
        grid=(B, 2, NSA_KV_HEADS),
        in_specs=[
            pl.BlockSpec((None, T, LANE), lambda b, kv, g: (b, 0, kv * 3 + g)),
            pl.BlockSpec((None, CMP_STRIDE * LANE, 2 * LANE), lambda b, kv, g: (kv, 0, 0)),
            pl.BlockSpec((None, 1, LANE), lambda b, kv, g: (kv, 0, 0)),
            pl.BlockSpec((None, LANE, LANE), lambda b, kv, g: (kv, 0, 0)),
            pl.BlockSpec((1, LANE), lambda b, kv, g: (0, 0)),
        ],
        out_specs=pl.BlockSpec((None, None, None, n, LANE), lambda b, kv, g: (b, kv, g, 0, 0)),
        out_shape=jax.ShapeDtypeStruct((B, 2, NSA_KV_HEADS, n, LANE), BF16),
        scratch_shapes=[pltpu.VMEM((n, CMP_STRIDE * LANE), BF16)],
        compiler_params=_params(("parallel", "parallel", "parallel")),
        name="cmp_prompt",
    )(rows, w1r, b1, w2, kc_g)


def _select_blocks(score, cur, n_blocks):
    tq = score.shape[0]
    blk = lax.broadcasted_iota(jnp.int32, score.shape, 1)
    forced = (blk == 0) | (blk == cur) | (blk == cur - 1)
    sc = jnp.where(blk <= cur, jnp.where(forced, FORCE_SCORE, score), NEG)
    sct = sc.T[0:n_blocks, :]
    bi = lax.broadcasted_iota(jnp.int32, sct.shape, 0)
    rank = jnp.zeros(sct.shape, F32)
    for i in range(n_blocks):
        si = sct[i:i + 1, :]
        ahead = (si > sct) | ((si == sct) & (bi > i))
        rank = rank + jnp.where(ahead, 1.0, 0.0)
    chosen = jnp.where((rank < SEL_TOPK) & (sct > 0.5 * NEG), 1.0, 0.0)
    return jnp.concatenate([chosen, jnp.zeros((LANE - n_blocks, tq), F32)], axis=0).T


def _score_tile(q, k_ref, kt, slot, bias, s_ref, m_ref, tk):
    s = _dot_t(q, k_ref[pl.ds(pl.multiple_of(kt * tk, tk), tk), :]) * SCALE
    if bias is not None:
        s = s + bias
    s_ref[slot] = s
    m_ref[...] = jnp.maximum(m_ref[...], s)


def _value_tile(v_ref, kt, slot, s_ref, m_ref, l_ref, acc_ref, tk):
    s = s_ref[slot]
    e = jnp.where(s > 0.5 * NEG, jnp.exp(s - m_ref[...]), 0.0)
    l_ref[...] += e
    acc_ref[...] += _dot(e.astype(BF16), v_ref[pl.ds(pl.multiple_of(kt * tk, tk), tk), :])


def _nsa_attn_kernel(qn_ref, qr_ref, gl_ref, kc_ref, vc_ref, ks_ref, vs_ref, kw_ref, vw_ref, cover_ref,
                     o_ref, s_ref, m_ref, l_ref, acc_ref, sel_ref, *, tq, ns):
    qi = pl.program_id(2)
    R = NSA_GROUP
    t0 = qi * tq
    stack = lambda ref: jnp.concatenate([ref[:, r * LANE:(r + 1) * LANE] for r in range(R)], axis=0)
    rows4 = lambda x: jnp.concatenate([x] * R, axis=0)
    tpos_q = t0 + lax.broadcasted_iota(jnp.int32, (tq, 1), 0)
    row_in = lax.broadcasted_iota(jnp.int32, (R * tq, 1), 0) & (tq - 1)
    col = lax.broadcasted_iota(jnp.int32, (R * tq, tq), 1)

    s = _dot_t(stack(qn_ref), kc_ref[...]) * SCALE
    cblk = lax.broadcasted_iota(jnp.int32, (R * tq, LANE), 1)
    cmask = (CMP_STRIDE * cblk + (CMP_BLOCK - 1) <= t0 + row_in) & (cblk < kc_ref.shape[0] - 1)
    p, _, _ = _softmax_masked(s, cmask)
    o_cmp = _dot(p.astype(BF16), vc_ref[...])
    imp = p[0:tq] + p[tq:2 * tq] + p[2 * tq:3 * tq] + p[3 * tq:4 * tq]
    score = _dot3(imp, cover_ref[...])
    cur = lax.shift_right_arithmetic(tpos_q, SEL_BLOCK.bit_length() - 1)

    @pl.when(t0 + tq <= SEL_TOPK * SEL_BLOCK)
    def _():
        sel_ref[...] = jnp.where(lax.broadcasted_iota(jnp.int32, (tq, LANE), 1) <= cur, 1.0, 0.0).astype(BF16)

    @pl.when(t0 + tq > SEL_TOPK * SEL_BLOCK)
    def _():
        sel_ref[...] = _select_blocks(score, cur, ns).astype(BF16)

    sel = sel_ref[...]
    q_rot = stack(qr_ref)
    blocks_per_tile = tq // SEL_BLOCK
    causal = jnp.where(col <= row_in, 0.0, NEG)
    far = jnp.where(col >= row_in, 0.0, NEG)

    def reset():
        m_ref[...] = jnp.full_like(m_ref, NEG)
        l_ref[...] = jnp.zeros_like(l_ref)
        acc_ref[...] = jnp.zeros_like(acc_ref)

    def row_max():
        m_ref[...] = jnp.broadcast_to(jnp.max(m_ref[...], axis=-1, keepdims=True), m_ref.shape)

    def result():
        return acc_ref[...] / jnp.maximum(jnp.sum(l_ref[...], axis=-1, keepdims=True), 1e-30)

    def member_bias(kt):
        key_blk = lax.shift_right_arithmetic(lax.broadcasted_iota(jnp.int32, (LANE, tq), 1),
                                             SEL_BLOCK.bit_length() - 1)
        expand = lax.broadcasted_iota(jnp.int32, (LANE, tq), 0) == kt * blocks_per_tile + key_blk
        member = _dot(sel, jnp.where(expand, 1.0, 0.0).astype(BF16))
        return rows4((member - 1.0) * (-NEG))

    reset()

    def sel_scores(kt, carry):
        _score_tile(q_rot, ks_ref, kt, kt, member_bias(kt), s_ref, m_ref, tq)
        return carry

    lax.fori_loop(0, qi, sel_scores, 0)
    _score_tile(q_rot, ks_ref, qi, qi, member_bias(qi) + causal, s_ref, m_ref, tq)
    row_max()

    def sel_values(kt, carry):
        _value_tile(vs_ref, kt, kt, s_ref, m_ref, l_ref, acc_ref, tq)
        return carry

    lax.fori_loop(0, qi + 1, sel_values, 0)
    o_sel = result()

    reset()
    n_back = NSA_WINDOW // tq
    for back in range(n_back + 1):
        bias = jnp.where(qi >= back, 0.0, NEG)
        if back == 0:
            bias = causal
        elif back == n_back:
            bias = far + bias
        _score_tile(q_rot, kw_ref, jnp.maximum(qi - back, 0), back, bias, s_ref, m_ref, tq)
    row_max()
    for back in range(n_back + 1):
        _value_tile(vw_ref, jnp.maximum(qi - back, 0), back, s_ref, m_ref, l_ref, acc_ref, tq)
    o_win = result()

    gates = jax.nn.sigmoid(gl_ref[...])
    for r in range(R):
        rs = slice(r * tq, (r + 1) * tq)
        o = (gates[:, 3 * r:3 * r + 1] * o_cmp[rs] + gates[:, 3 * r + 1:3 * r + 2] * o_sel[rs]
             + gates[:, 3 * r + 2:3 * r + 3] * o_win[rs])
        o_ref[:, r * LANE:(r + 1) * LANE] = o.astype(BF16)


def _nsa_attn(qn, qr, proj, cmp_kv, kvb, cover, tq=256):
    B, T, _ = qn.shape
    nc = cmp_kv.shape[3]
    G = NSA_KV_HEADS
    assert tq % LANE == 0 and NSA_WINDOW % tq == 0 and nc == LANE and T // SEL_BLOCK <= LANE
    rows = NSA_GROUP * tq
    qspec = pl.BlockSpec((None, tq, 4 * LANE), lambda b, g, i: (b, i, g))
    kvspec = lambda c: pl.BlockSpec((None, T, LANE), lambda b, g, i: (b, 0, c * 3 + g))
    return pl.pallas_call(
        functools.partial(_nsa_attn_kernel, tq=tq, ns=T // SEL_BLOCK),
        grid=(B, G, T // tq),
        in_specs=[
            qspec, qspec,
            pl.BlockSpec((None, tq, LANE), lambda b, g, i: (b, i, NSA_GATE_BLK + g)),
            pl.BlockSpec((None, None, None, nc, LANE), lambda b, g, i: (b, 0, g, 0, 0)),
            pl.BlockSpec((None, None, None, nc, LANE), lambda b, g, i: (b, 1, g, 0, 0)),
            kvspec(0), kvspec(1), kvspec(2), kvspec(3),
            pl.BlockSpec((nc, LANE), lambda b, g, i: (0, 0)),
        ],
        out_specs=qspec,
        out_shape=jax.ShapeDtypeStruct((B, T, 1536), BF16),
        scratch_shapes=[pltpu.VMEM((T // tq, rows, tq), F32),
                        pltpu.VMEM((rows, tq), F32),
                        pltpu.VMEM((rows, tq), F32),
                        pltpu.VMEM((rows, LANE), F32),
                        pltpu.VMEM((tq, LANE), BF16)],
        compiler_params=_params(("parallel", "parallel", "arbitrary")),
        name="nsa_attn",
    )(qn, qr, proj, cmp_kv, cmp_kv, kvb, kvb, kvb, kvb, cover)


def _dil_band_kernel(q_ref, kp_ref, kc_ref, vp_ref, vc_ref, o_ref, st_ref, *, tq, window):
    i = pl.program_id(2)
    qpos = i * tq + lax.broadcasted_iota(jnp.int32, (tq, 2 * tq), 0)
    kpos = (i - 1) * tq + lax.broadcasted_iota(jnp.int32, (tq, 2 * tq), 1)
    diff = qpos - kpos
    mask = (diff >= 0) & (diff <= window) & (kpos >= 0)
    lane = lax.broadcasted_iota(jnp.int32, (tq, LANE), 1)
    stats = jnp.zeros((tq, LANE), F32)
    for h in range(DIL_HEADS):
        sl = slice(h * LANE, (h + 1) * LANE)
        k = jnp.concatenate([kp_ref[:, sl], kc_ref[:, sl]], axis=0)
        v = jnp.concatenate([vp_ref[:, sl], vc_ref[:, sl]], axis=0)
        p, m, l = _softmax_masked(_dot_t(q_ref[:, sl], k) * SCALE, mask)
        o_ref[:, sl] = _dot(p.astype(BF16), v)
        stats = jnp.where(lane == h, m, stats)
        stats = jnp.where(lane == DIL_HEADS + h, l, stats)
    st_ref[...] = stats


def _dil_band(q, k, v, g, dil, tq=128):
    B, T, C = q.shape
    S = T // dil
    nb = C // (4 * LANE)
    view = lambda a: a.reshape(B, S, dil * C)
    cur = pl.BlockSpec((None, tq, 4 * LANE), lambda b, r, i: (b, i, r * nb + g))
    prev = pl.BlockSpec((None, tq, 4 * LANE), lambda b, r, i: (b, jnp.maximum(i - 1, 0), r * nb + g))
    o, st = pl.pallas_call(
        functools.partial(_dil_band_kernel, tq=tq, window=DIL_PAIRS[g][0] // dil),
        grid=(B, dil, S // tq),
        in_specs=[cur, prev, cur, prev, cur],
        out_specs=[pl.BlockSpec((None, tq, 4 * LANE), lambda b, r, i: (b, i, r)),
                   pl.BlockSpec((None, tq, LANE), lambda b, r, i: (b, i, r))],
        out_shape=[jax.ShapeDtypeStruct((B, S, dil * 4 * LANE), F32),
                   jax.ShapeDtypeStruct((B, S, dil * LANE), F32)],
        compiler_params=_params(("parallel", "parallel", "parallel")),
        name=f"dil_band{g}",
    )(view(q), view(k), view(k), view(v), view(v))
    return o.reshape(B * T, 4 * LANE), st.reshape(B * T, LANE)


def _mix_groups(os_, ms, ls):
    m_all = jnp.maximum(jnp.maximum(ms[0], ms[1]), ms[2])
    ws = [jnp.exp(m - m_all) * l for m, l in zip(ms, ls)]
    tot = ws[0] + ws[1] + ws[2]
    return (ws[0] / tot) * os_[0] + (ws[1] / tot) * os_[1] + (ws[2] / tot) * os_[2]


def _dil_mix_kernel(o0_ref, o1_ref, o2_ref, s0_ref, s1_ref, s2_ref, o_ref):
    o_refs, s_refs = (o0_ref, o1_ref, o2_ref), (s0_ref, s1_ref, s2_ref)
    for h in range(DIL_HEADS):
        sl = slice(h * LANE, (h + 1) * LANE)
        ms = [s[:, h:h + 1] for s in s_refs]
        ls = [s[:, DIL_HEADS + h:DIL_HEADS + h + 1] for s in s_refs]
        o_ref[:, sl] = _mix_groups([o[:, sl] for o in o_refs], ms, ls).astype(BF16)


def _dil_mix(os_, sts, tm=512):
    M = os_[0].shape[0]
    ospec = pl.BlockSpec((tm, 4 * LANE), lambda i: (i, 0))
    sspec = pl.BlockSpec((tm, LANE), lambda i: (i, 0))
    return pl.pallas_call(
        _dil_mix_kernel,
        grid=(M // tm,),
        in_specs=[ospec] * 3 + [sspec] * 3,
        out_specs=ospec,
        out_shape=jax.ShapeDtypeStruct((M, 4 * LANE), BF16),
        compiler_params=_params(("parallel",)),
        name="dil_mix",
    )(*os_, *sts)


DEC_PAGES_PER_STEP = 16


def _dec_select_kernel(tbl_ref, *refs, n_pp, n_steps):
    pages = refs[:n_pp]
    (qn_ref, w1_ref, b1_ref, w2_ref, kcg_ref, cover_ref, idx_ref, ocmp_ref, h_ref) = refs[n_pp:]
    j = pl.program_id(1)
    cpp = PAGE_SIZE // CMP_STRIDE
    rows = n_pp * cpp
    n = n_steps * rows
    ns = cover_ref.shape[1]

    for g in range(NSA_KV_HEADS):
        for kv in range(2):
            x = jnp.concatenate(
                [_get_rows(pg, g * 4 + kv, NSA_ROWS, PAGE_SIZE).reshape(cpp, CMP_STRIDE * LANE) for pg in pages],
                axis=0)
            h_ref[g * 2 + kv, pl.ds(pl.multiple_of(j * rows, rows), rows), :] = _dot(x.astype(BF16), w1_ref[kv])

    @pl.when(j == n_steps - 1)
    def _():
        idx_ref[...] = jnp.zeros_like(idx_ref)
        for g in range(NSA_KV_HEADS):
            kc = _rms(_compress_finish(h_ref[g * 2], b1_ref[0], w2_ref[0]), kcg_ref[...]).astype(BF16)
            vc = _compress_finish(h_ref[g * 2 + 1], b1_ref[1], w2_ref[1]).astype(BF16)
            q = _rows16(qn_ref[:, g * NSA_GROUP * LANE:(g + 1) * NSA_GROUP * LANE], NSA_GROUP).astype(BF16)
            s = _dot_t(q, kc) * SCALE
            valid = lax.broadcasted_iota(jnp.int32, s.shape, 1) < n - 1
            p, _, _ = _softmax_masked(s, valid)
            ocmp_ref[g] = _dot(p.astype(BF16), vc)[0:NSA_GROUP, :]
            rid = lax.broadcasted_iota(jnp.int32, p.shape, 0)
            imp = jnp.sum(jnp.where(rid < NSA_GROUP, p, 0.0), axis=0, keepdims=True)
            score = _dot3(jnp.broadcast_to(imp, (8, n)), cover_ref[...])

            a = jnp.broadcast_to(score[0:1, :], (ns, ns))
            lane = lax.broadcasted_iota(jnp.int32, (ns, ns), 1)
            sub = lax.broadcasted_iota(jnp.int32, (ns, ns), 0)
            cur = n * CMP_STRIDE // SEL_BLOCK
            forced = (lane == 0) | (lane == cur) | (lane == cur - 1)
            a = jnp.where(lane <= cur, jnp.where(forced, FORCE_SCORE, a), NEG)
            at = a.T
            ahead_r = (at > a) | ((at == a) & (sub < lane))
            chosen_r = ((jnp.sum(jnp.where(ahead_r, 1.0, 0.0), axis=0, keepdims=True) < SEL_TOPK)
                        & (a[0:1, :] > 0.5 * NEG))
            ahead_c = (a > at) | ((a == at) & (lane < sub))
            chosen_c = ((jnp.sum(jnp.where(ahead_c, 1.0, 0.0), axis=1, keepdims=True) < SEL_TOPK)
                        & (at[:, 0:1] > 0.5 * NEG))
            before = jnp.sum(jnp.where(chosen_r & (lane < sub), 1.0, 0.0), axis=1, keepdims=True)
            slot = lax.broadcasted_iota(jnp.int32, (ns, LANE), 1)
            onehot = chosen_c & (before == slot.astype(F32))
            blk = lax.broadcasted_iota(jnp.int32, (ns, LANE), 0)
            picked = jnp.sum(jnp.where(onehot, blk.astype(F32), 0.0), axis=0, keepdims=True)
            filled = jnp.sum(jnp.where(onehot, 1.0, 0.0), axis=0, keepdims=True)
            idx_ref[g:g + 1, :] = jnp.where(filled > 0.5, picked, float(cur)).astype(jnp.int32)


def _dec_select(cache, table, qn, w1r, b1, w2, kc_g, cover):
    B, n_pages = table.shape
    n_pp = DEC_PAGES_PER_STEP
    n_steps = n_pages // n_pp
    n = n_pages * (PAGE_SIZE // CMP_STRIDE)
    ns = cover.shape[1]
    G = NSA_KV_HEADS
    const = lambda *shape: pl.BlockSpec(shape, lambda b, j, tbl: (0,) * len(shape))
    page_spec = lambda p: pl.BlockSpec((None, PAGE_SIZE * NSA_ROWS, LANE), lambda b, j, tbl: (tbl[b, j * n_pp + p], 0, 0))
    grid_spec = pltpu.PrefetchScalarGridSpec(
        num_scalar_prefetch=1,
        grid=(B, n_steps),
        in_specs=[page_spec(p) for p in range(n_pp)] + [
            pl.BlockSpec((None, 1, 1536), lambda b, j, tbl: (b, 0, 0)),
            const(2, CMP_STRIDE * LANE, 2 * LANE), const(2, 1, LANE), const(2, LANE, LANE), const(1, LANE),
            const(n, ns),
        ],
        out_specs=[pl.BlockSpec((None, 8, LANE), lambda b, j, tbl: (b, 0, 0)),
                   pl.BlockSpec((None, G, NSA_GROUP, LANE), lambda b, j, tbl: (b, 0, 0, 0))],
        scratch_shapes=[pltpu.VMEM((2 * G, n, 2 * LANE), F32)],
    )
    return pl.pallas_call(
        functools.partial(_dec_select_kernel, n_pp=n_pp, n_steps=n_steps),
        grid_spec=grid_spec,
        out_shape=[jax.ShapeDtypeStruct((B, 8, LANE), jnp.int32),
                   jax.ShapeDtypeStruct((B, G, NSA_GROUP, LANE), F32)],
        compiler_params=_params(("arbitrary", "arbitrary")),
        name="dec_select",
    )(table, *([cache] * n_pp), qn, w1r, b1, w2, kc_g, cover)


def _dec_attn_kernel(tbl_ref, idx_ref, *refs, n_sel, cur, wb):
    blocks = refs[:n_sel]
    (qr_ref, kvn_ref, win_ref, gl_ref, ocmp_ref, o_ref) = refs[n_sel:]
    b, g = pl.program_id(0), pl.program_id(1)
    R = NSA_GROUP
    q = _rows16(qr_ref[...], R).astype(BF16)
    qf = q.astype(F32)
    new = kvn_ref[...]
    ks_n, vs_n, kw_n, vw_n = (new[:, c * LANE:(c + 1) * LANE] for c in range(4))

    k = jnp.concatenate([_get_rows(r, g * 4 + 2, NSA_ROWS, SEL_BLOCK) for r in blocks], axis=0).astype(BF16)
    v = jnp.concatenate([_get_rows(r, g * 4 + 3, NSA_ROWS, SEL_BLOCK) for r in blocks], axis=0).astype(BF16)
    s = _dot_t(q, k) * SCALE
    blk_of = lax.shift_right_arithmetic(lax.broadcasted_iota(jnp.int32, s.shape, 1), SEL_BLOCK.bit_length() - 1)
    valid = jnp.zeros(s.shape, jnp.int32)
    for n in range(n_sel):
        is_past = jnp.where(idx_ref[(b * NSA_KV_HEADS + g) * n_sel + n] != cur, 1, 0)
        valid = jnp.where(blk_of == n, is_past, valid)
    valid = valid > 0
    s_new = jnp.sum(qf * ks_n, axis=-1, keepdims=True) * SCALE
    s = jnp.where(valid, s, NEG)
    m = jnp.maximum(jnp.max(s, axis=-1, keepdims=True), s_new)
    e = jnp.where(valid, jnp.exp(s - m), 0.0)
    e_new = jnp.exp(s_new - m)
    l = jnp.sum(e, axis=-1, keepdims=True) + e_new
    o_sel = (_dot(e.astype(BF16), v) + e_new.astype(BF16).astype(F32) * vs_n) / l

    s = _dot_t(q, _get_rows(win_ref, g * 2, WIN_ROWS, wb).astype(BF16)) * SCALE
    s_new = jnp.sum(qf * kw_n, axis=-1, keepdims=True) * SCALE
    m = jnp.maximum(jnp.max(s, axis=-1, keepdims=True), s_new)
    e = jnp.exp(s - m)
    e_new = jnp.exp(s_new - m)
    l = jnp.sum(e, axis=-1, keepdims=True) + e_new
    o_win = (_dot(e.astype(BF16), _get_rows(win_ref, g * 2 + 1, WIN_ROWS, wb).astype(BF16))
             + e_new.astype(BF16).astype(F32) * vw_n) / l

    gates = jax.nn.sigmoid(gl_ref[...])
    o_cmp = ocmp_ref[...]
    for r in range(R):
        o = (gates[:, 3 * r:3 * r + 1] * o_cmp[r:r + 1] + gates[:, 3 * r + 1:3 * r + 2] * o_sel[r:r + 1]
             + gates[:, 3 * r + 2:3 * r + 3] * o_win[r:r + 1])
        o_ref[:, r * LANE:(r + 1) * LANE] = o


def _dec_attn(cache, table, idx, qr, kvb, win_state, proj, ocmp):
    B, n_pages = table.shape
    n_sel = SEL_TOPK
    cur = n_pages * PAGE_SIZE // SEL_BLOCK
    wb = win_state.shape[1] // WIN_ROWS
    assert wb <= NSA_WINDOW
    halves = PAGE_SIZE // SEL_BLOCK

    def blk_spec(n):
        def imap(b, g, tbl, idx):
            i = jnp.minimum(idx[(b * NSA_KV_HEADS + g) * n_sel + n], cur - 1)
            return (tbl[b, i // halves], i % halves, 0)
        return pl.BlockSpec((None, SEL_BLOCK * NSA_ROWS, LANE), imap)

    grid_spec = pltpu.PrefetchScalarGridSpec(
        num_scalar_prefetch=2,
        grid=(B, NSA_KV_HEADS),
        in_specs=[blk_spec(n) for n in range(n_sel)] + [
            pl.BlockSpec((None, 1, 4 * LANE), lambda b, g, tbl, idx: (b, 0, g)),
            pl.BlockSpec((None, 1, 4 * LANE), lambda b, g, tbl, idx: (b, 0, g)),
            pl.BlockSpec((None, wb * WIN_ROWS, LANE), lambda b, g, tbl, idx: (b, 0, 0)),
            pl.BlockSpec((None, 1, LANE), lambda b, g, tbl, idx: (b, 0, NSA_GATE_BLK + g)),
            pl.BlockSpec((None, None, NSA_GROUP, LANE), lambda b, g, tbl, idx: (b, g, 0, 0)),
        ],
        out_specs=pl.BlockSpec((None, 1, 4 * LANE), lambda b, g, tbl, idx: (b, 0, g)),
    )
    kvn = kvb.reshape(B, 1, 4, NSA_KV_HEADS, LANE).transpose(0, 1, 3, 2, 4).reshape(B, 1, 1536)
    return pl.pallas_call(
        functools.partial(_dec_attn_kernel, n_sel=n_sel, cur=cur, wb=wb),
        grid_spec=grid_spec,
        out_shape=jax.ShapeDtypeStruct((B, 1, 1536), F32),
        compiler_params=_params(("arbitrary", "arbitrary")),
        name="dec_attn",
    )(table, idx, *([cache] * n_sel), qr, kvn, win_state, proj, ocmp)


def _dec_dil_kernel(q_ref, n0_ref, n1_ref, n2_ref, s0_ref, s1_ref, s2_ref, o_ref, t0_ref, t1_ref, t2_ref):
    qall = q_ref[...]
    states, news, outs = (s0_ref, s1_ref, s2_ref), (n0_ref, n1_ref, n2_ref), (t0_ref, t1_ref, t2_ref)

    for st, new, out in zip(states, news, outs):
        keep = st.shape[0] - KVH_ROWS
        out[0:keep, :] = st[KVH_ROWS:, :]
        out[keep:, :] = new[...]

    for h in range(DIL_HEADS):
        os_, ms, ls = [], [], []
        for g, st in enumerate(states):
            hs = slice((g * DIL_HEADS + h) * LANE, (g * DIL_HEADS + h + 1) * LANE)
            window, dil = DIL_PAIRS[g]
            kn = news[g][h:h + 1, :].astype(BF16).astype(F32)
            vn = news[g][DIL_HEADS + h:DIL_HEADS + h + 1, :].astype(BF16).astype(F32)
            q = _rows16(qall[:, hs], 1).astype(BF16)
            k = _get_rows(st, h, KVH_ROWS * dil, window // dil).astype(BF16)
            v = _get_rows(st, DIL_HEADS + h, KVH_ROWS * dil, window // dil).astype(BF16)
            s = _dot_t(q, k) * SCALE
            s_new = jnp.sum(q.astype(F32) * kn, axis=-1, keepdims=True) * SCALE
            m = jnp.maximum(jnp.max(s, axis=-1, keepdims=True), s_new)
            e = jnp.exp(s - m)
            e_new = jnp.exp(s_new - m)
            l = jnp.sum(e, axis=-1, keepdims=True) + e_new
            ln = jnp.maximum(l, 1e-30)
            o = _dot((e / ln).astype(BF16), v) + (e_new / ln).astype(BF16).astype(F32) * vn
            os_.append(o)
            ms.append(m)
            ls.append(l)
        o_ref[:, h * LANE:(h + 1) * LANE] = _mix_groups(os_, ms, ls)[0:1, :]


def _dec_dil(qr, news, states):
    B = qr.shape[0]
    in_specs = [pl.BlockSpec((None, 1, 1536), lambda b: (b, 0, 0))]
    in_specs += [pl.BlockSpec((None, KVH_ROWS, LANE), lambda b: (b, 0, 0))] * len(DIL_PAIRS)
    st_specs = []
    for g, (window, dil) in enumerate(DIL_PAIRS):
        assert states[g].shape[1] == window * KVH_ROWS, "rolling buffer shorter than the window is not supported"
        st_specs.append(pl.BlockSpec((None, window * KVH_ROWS, LANE), lambda b: (b, 0, 0)))
    return pl.pallas_call(
        _dec_dil_kernel,
        grid=(B,),
        in_specs=in_specs + st_specs,
        out_specs=[pl.BlockSpec((None, 1, 4 * LANE), lambda b: (b, 0, 0))] + st_specs,
        out_shape=[jax.ShapeDtypeStruct((B, 1, 4 * LANE), F32)]
        + [jax.ShapeDtypeStruct(s.shape, F32) for s in states],
        compiler_params=_params(("parallel",)),
        name="dec_dil",
    )(qr, *news, *states)


def _rope_tables(pos):
    half = HEAD_DIM // 2
    inv = ROPE_THETA ** (-jnp.arange(half, dtype=F32) / half)
    ang = pos.astype(F32)[:, None] * inv
    cos, sin = jnp.cos(ang), jnp.sin(ang)
    return jnp.concatenate([cos, cos], axis=-1), jnp.concatenate([-sin, sin], axis=-1)


def _cover(nc, ns, rows, cols):
    c0 = jnp.arange(nc)[:, None] * CMP_STRIDE
    s0 = jnp.arange(ns)[None, :] * SEL_BLOCK
    cover = jnp.clip(jnp.minimum(c0 + CMP_BLOCK, s0 + SEL_BLOCK) - jnp.maximum(c0, s0), 0, CMP_BLOCK)
    cover = cover.astype(F32) / CMP_BLOCK
    return jnp.pad(cover, ((0, rows - nc), (0, cols - ns))).astype(BF16)


def _pad_gains(*gs):
    return jnp.pad(jnp.stack(gs, axis=0), ((0, 8 - len(gs)), (0, 0)))


def kernel(x_prompt, x_sample, mem_prompt, cache_nsa_kv, page_table, state_nsa_win, state_dil_0, state_dil_1,
           state_dil_2, cache_mem_kv, ff_norm, ff_w_gate, ff_w_up, ff_w_down, mix_norm, mem_norm, w_mem_kv,
           mem_q_g, mem_k_g, nsa_w_in, nsa_q_g, nsa_kc_g, nsa_ks_g, nsa_kw_g, nsa_cmp_w1, nsa_cmp_b1, nsa_cmp_w2,
           nsa_w_out, dil_w_in, dil_q_g, dil_k_g, dil_w_out):
    B, T, D = x_prompt.shape
    Bs = x_sample.shape[0]
    assert x_sample.shape[1] == 1, "the sample group is a single-token decode step"
    n_pages = page_table.shape[1]
    past_len = n_pages * PAGE_SIZE
    H, G, d = N_MIX_HEADS, NSA_KV_HEADS, HEAD_DIM

    wg, wu, wd = ff_w_gate.astype(BF16), ff_w_up.astype(BF16), ff_w_down.astype(BF16)
    ffg = ff_norm.reshape(ff_norm.shape[0], 2, 1, D)
    gate_w = jnp.pad(nsa_w_in[:, H * d:H * d + 3 * H].reshape(D, G, 3 * NSA_GROUP), ((0, 0), (0, 0), (0, LANE - 12)))
    nsa_w = jnp.concatenate([nsa_w_in[:, :H * d], nsa_w_in[:, H * d + 3 * H:], gate_w.reshape(D, G * LANE)], axis=1)
    nsa_w = jnp.pad(nsa_w, ((0, 0), (0, PROJ_N - nsa_w.shape[1]))).astype(BF16)
    dil_w = dil_w_in.astype(BF16)
    nsa_wo, dil_wo = nsa_w_out.astype(BF16), dil_w_out.astype(BF16)
    w1r = nsa_cmp_w1.reshape(2, 2, CMP_STRIDE, d, d).transpose(0, 2, 3, 1, 4).reshape(2, CMP_STRIDE * d, 2 * d)
    w1r = w1r.astype(BF16)
    cmp_b1 = nsa_cmp_b1.reshape(2, 1, d)
    cmp_w2 = nsa_cmp_w2.astype(BF16)
    kc_g = nsa_kc_g.reshape(1, d)

    n_pool = cache_nsa_kv.shape[0]
    cache_rows = cache_nsa_kv.transpose(0, 1, 3, 2, 4).reshape(n_pool, PAGE_SIZE * NSA_ROWS, LANE)
    win_rows = state_nsa_win.transpose(0, 1, 3, 2, 4).reshape(Bs, state_nsa_win.shape[1] * WIN_ROWS, LANE)
    dil_states = (state_dil_0, state_dil_1, state_dil_2)
    dil_rows = [s.reshape(Bs, s.shape[1] * KVH_ROWS, LANE) for s in dil_states]
    mem_rows_s = cache_mem_kv.reshape(2, Bs, N_MEM * KVH_ROWS, LANE)

    mem2d = mem_prompt.reshape(B * N_MEM, D)
    mem_rows_p = []
    for i in range(2):
        kv = _norm_matmul(mem2d, mem_norm[i], w_mem_kv[i].astype(BF16), tm=256, tn=1024)
        mem_rows_p.append(_memkv_post(kv, mem_k_g[i]).reshape(B, N_MEM * KVH_ROWS, LANE))

    cos_p, sin_p = _rope_tables(jnp.tile(jnp.arange(T, dtype=jnp.int32), B))
    cos_s, sin_s = _rope_tables(jnp.full((Bs,), past_len, jnp.int32))
    row3 = lambda a: a.astype(F32).reshape(Bs, 1, a.shape[-1])

    xp = _ffn(x_prompt.reshape(B * T, D), ffg, wg, wu, wd, 0, 0, tm=512)
    xs = _ffn(x_sample.reshape(Bs, D), ffg, wg, wu, wd, 0, 0, tm=Bs)
    nsa_gains = _pad_gains(nsa_q_g, nsa_ks_g, nsa_kw_g, mem_q_g[0])

    proj_p = _norm_matmul(xp, mix_norm[0], nsa_w, tm=512, tn=1024)
    qn_p, qr_p, cmp_p, rows_p, win_p, kvb_p, mq_p = _nsa_post(proj_p, cos_p, sin_p, nsa_gains, tm=256)
    cmp_kv = _cmp_prompt(cmp_p.reshape(B, T, 768), w1r, cmp_b1, cmp_w2, kc_g)
    nc_p = T // CMP_STRIDE
    cover_p = _cover(nc_p - 1, T // SEL_BLOCK, nc_p, LANE)
    o_mix_p = _nsa_attn(qn_p.reshape(B, T, 1536), qr_p.reshape(B, T, 1536), proj_p.reshape(B, T, PROJ_N),
                        cmp_kv, kvb_p.reshape(B, T, 1536), cover_p)
    o_mem_p = _mem_attn(mq_p.reshape(B, T, 512), mem_rows_p[0], tq=256)
    xp = _out_proj(xp, o_mix_p.reshape(B * T, 1536), o_mem_p.reshape(B * T, 512), nsa_wo, tm=512)

    proj_s = _norm_matmul(xs, mix_norm[0], nsa_w, tm=Bs, tn=1024)
    qn_s, qr_s, _, rows_s, win_s, kvb_s, mq_s = _nsa_post(proj_s, cos_s, sin_s, nsa_gains, tm=Bs)
    nc_s = past_len // CMP_STRIDE
    ns_s = -(-(past_len + 1) // SEL_BLOCK)
    cover_s = _cover(nc_s - 1, ns_s, nc_s, -(-ns_s // LANE) * LANE)
    sel_idx, ocmp_s = _dec_select(cache_rows, page_table, row3(qn_s), w1r, cmp_b1, cmp_w2, kc_g, cover_s)
    o_mix_s = _dec_attn(cache_rows, page_table, sel_idx[:, :G, :SEL_TOPK].reshape(-1), row3(qr_s), row3(kvb_s),
                        win_rows, proj_s.reshape(Bs, 1, PROJ_N), ocmp_s)
    o_mem_s = _mem_attn(row3(mq_s), mem_rows_s[0], tq=1)
    xs = _out_proj(xs, o_mix_s.reshape(Bs, 1536).astype(BF16), o_mem_s.reshape(Bs, 512).astype(BF16), nsa_wo,
                   tm=Bs)

    xp = _ffn(xp, ffg, wg, wu, wd, 0, 1, tm=512)
    xs = _ffn(xs, ffg, wg, wu, wd, 0, 1, tm=Bs)

    xp = _ffn(xp, ffg, wg, wu, wd, 1, 0, tm=512)
    xs = _ffn(xs, ffg, wg, wu, wd, 1, 0, tm=Bs)
    dil_gains = _pad_gains(dil_q_g, dil_k_g, mem_q_g[1])

    dproj_p = _norm_matmul(xp, mix_norm[1], dil_w, tm=512, tn=1024)
    dq_p, dkb_p, dvb_p, dmq_p, *dnew_p = _dil_post(dproj_p, cos_p, sin_p, dil_gains, tm=256)
    band = [_dil_band(dq_p.reshape(B, T, 1536), dkb_p.reshape(B, T, 1536), dvb_p.reshape(B, T, 1536), g, dil)
            for g, (_, dil) in enumerate(DIL_PAIRS)]
    o_dil_p = _dil_mix([o for o, _ in band], [s for _, s in band])
    o_dmem_p = _mem_attn(dmq_p.reshape(B, T, 512), mem_rows_p[1], tq=256)
    xp = _out_proj(xp, o_dil_p, o_dmem_p.reshape(B * T, 512), dil_wo, tm=512)

    dproj_s = _norm_matmul(xs, mix_norm[1], dil_w, tm=Bs, tn=1024)
    dq_s, _, _, dmq_s, *dnew_s = _dil_post(dproj_s, cos_s, sin_s, dil_gains, tm=Bs)
    o_dil_s, *dil_rows_out = _dec_dil(row3(dq_s), [s.reshape(Bs, KVH_ROWS, LANE) for s in dnew_s], dil_rows)
    o_dmem_s = _mem_attn(row3(dmq_s), mem_rows_s[1], tq=1)
    xs = _out_proj(xs, o_dil_s.reshape(Bs, 512).astype(BF16), o_dmem_s.reshape(Bs, 512).astype(BF16), dil_wo,
                   tm=Bs)

    xp = _ffn(xp, ffg, wg, wu, wd, 1, 1, tm=512)
    xs = _ffn(xs, ffg, wg, wu, wd, 1, 1, tm=Bs)

    unrow = lambda a, n, outer, inner: a.reshape(n, -1, outer, inner, d).transpose(0, 1, 3, 2, 4)
    nsa_kv_p = unrow(rows_p, B, G, 4)
    nsa_kv_s = unrow(rows_s, Bs, G, 4)
    nsa_win_p = unrow(win_p, B, G, 2)[:, -min(NSA_WINDOW, T):]
    nsa_win_s = jnp.concatenate([state_nsa_win, unrow(win_s, Bs, G, 2)], axis=1)[:, -state_nsa_win.shape[1]:]
    outs_dil = []
    for g, (window, _) in enumerate(DIL_PAIRS):
        st = dil_states[g]
        outs_dil.append(dnew_p[g].reshape(B, T, 2, DIL_HEADS, d)[:, -min(window, T):])
        outs_dil.append(dil_rows_out[g].reshape(st.shape))
    mem_kv_out = jnp.stack([kv.reshape(B, N_MEM, 2, N_MEM_HEADS, d) for kv in mem_rows_p], axis=0)
    return (xp.reshape(B, T, D), xs.reshape(Bs, 1, D), nsa_kv_p, nsa_kv_s, nsa_win_p, nsa_win_s,
            *outs_dil, mem_kv_out)
```

```python
import functools

import jax
import jax.numpy as jnp
from jax import lax
from jax.experimental import pallas as pl
from jax.experimental.pallas import tpu as pltpu

F32 = jnp.float32
BF16 = jnp.bfloat16

D_MODEL = 2048
HEAD_DIM = 128
N_MIX_HEADS = 12
N_MEM_HEADS = 4
N_MEM = 256
NSA_KV_HEADS = 3
NSA_GROUP = 4
CMP_BLOCK = 32
CMP_STRIDE = 16
SEL_BLOCK = 64
SEL_TOPK = 16
NSA_WINDOW = 512
DIL_PAIRS = ((128, 1), (512, 4), (2048, 16))
DIL_HEADS = 4
PAGE_SIZE = 128
ROPE_THETA = 10000.0
EPS = 1e-6
SCALE = HEAD_DIM ** -0.5
NEG = -1e30
FORCE_SCORE = 1e6

PROJ_N = 5120
NSA_GATE_BLK = 34
LANE = 128
VMEM_LIMIT = 56 * 1024 * 1024


def _params(sem):
    return pltpu.CompilerParams(dimension_semantics=sem, vmem_limit_bytes=VMEM_LIMIT)


def _dot(a, b):
    return jnp.dot(a, b, preferred_element_type=F32)


def _dot_t(a, b):
    return lax.dot_general(a, b, (((1,), (1,)), ((), ())), preferred_element_type=F32)


def _dot3(a, b):
    a1 = a.astype(BF16)
    r1 = a - a1.astype(F32)
    a2 = r1.astype(BF16)
    a3 = (r1 - a2.astype(F32)).astype(BF16)
    return _dot(a1, b) + _dot(a2, b) + _dot(a3, b)


def _rms(x, g):
    return x * lax.rsqrt(jnp.mean(x * x, axis=-1, keepdims=True) + EPS) * g


def _rope(x, cos, sin):
    return x * cos + pltpu.roll(x, HEAD_DIM // 2, 1) * sin


def _rows16(row, nrep):
    rid = lax.broadcasted_iota(jnp.int32, (16, LANE), 0) & (nrep - 1)
    out = jnp.zeros((16, LANE), F32)
    for r in range(nrep):
        piece = jnp.broadcast_to(row[:, r * LANE:(r + 1) * LANE], (16, LANE))
        out = jnp.where(rid == r, piece, out)
    return out


def _softmax_masked(s, mask):
    s = jnp.where(mask, s, NEG)
    m = jnp.max(s, axis=-1, keepdims=True)
    e = jnp.where(mask, jnp.exp(s - m), 0.0)
    l = jnp.sum(e, axis=-1, keepdims=True)
    return e / jnp.maximum(l, 1e-30), m, l


def _ffn_kernel(x_ref, g_ref, wg_ref, wu_ref, wd_ref, o_ref, h_ref, *, nf):
    f = pl.program_id(1)

    @pl.when(f == 0)
    def _():
        h_ref[...] = _rms(x_ref[...], g_ref[...]).astype(BF16)
        o_ref[...] = jnp.zeros_like(o_ref)

    h = h_ref[...]
    gate = _dot(h, wg_ref[...].astype(BF16))
    up = _dot(h, wu_ref[...].astype(BF16))
    a = (gate * jax.nn.sigmoid(gate) * up).astype(BF16)
    o_ref[...] += _dot(a, wd_ref[...].astype(BF16))

    @pl.when(f == nf - 1)
    def _():
        o_ref[...] = x_ref[...] + 0.5 * o_ref[...]


def _ffn(x, g, wg, wu, wd, li, lj, tm, tf=256):
    M, D = x.shape
    F = wg.shape[-1]
    nf = F // tf
    return pl.pallas_call(
        functools.partial(_ffn_kernel, nf=nf),
        grid=(M // tm, nf),
        in_specs=[
            pl.BlockSpec((tm, D), lambda i, f: (i, 0), pipeline_mode=pl.Buffered(1)),
            pl.BlockSpec((None, None, 1, D), lambda i, f: (li, lj, 0, 0)),
            pl.BlockSpec((None, None, D, tf), lambda i, f: (li, lj, 0, f)),
            pl.BlockSpec((None, None, D, tf), lambda i, f: (li, lj, 0, f)),
            pl.BlockSpec((None, None, tf, D), lambda i, f: (li, lj, f, 0)),
        ],
        out_specs=pl.BlockSpec((tm, D), lambda i, f: (i, 0)),
        out_shape=jax.ShapeDtypeStruct((M, D), F32),
        scratch_shapes=[pltpu.VMEM((tm, D), BF16)],
        compiler_params=_params(("parallel", "arbitrary")),
        name="ffn",
    )(x, g, wg, wu, wd)


def _nmm_kernel(x_ref, g_ref, w_ref, o_ref, h_ref):
    @pl.when(pl.program_id(1) == 0)
    def _():
        h_ref[...] = _rms(x_ref[...], g_ref[...]).astype(BF16)

    o_ref[...] = _dot(h_ref[...], w_ref[...])


def _norm_matmul(x, g, w, tm, tn):
    M, D = x.shape
    N = w.shape[1]
    return pl.pallas_call(
        _nmm_kernel,
        grid=(M // tm, N // tn),
        in_specs=[
            pl.BlockSpec((tm, D), lambda i, j: (i, 0)),
            pl.BlockSpec((1, D), lambda i, j: (0, 0)),
            pl.BlockSpec((D, tn), lambda i, j: (0, j)),
        ],
        out_specs=pl.BlockSpec((tm, tn), lambda i, j: (i, j)),
        out_shape=jax.ShapeDtypeStruct((M, N), F32),
        scratch_shapes=[pltpu.VMEM((tm, D), BF16)],
        compiler_params=_params(("parallel", "arbitrary")),
        name="norm_matmul",
    )(x, g.reshape(1, D), w)


def _oproj_kernel(x_ref, a_ref, b_ref, w_ref, o_ref, *, ka):
    o_ref[...] = x_ref[...] + _dot(a_ref[...], w_ref[:ka, :]) + _dot(b_ref[...], w_ref[ka:, :])


def _out_proj(x, a, b, w, tm, tn=1024):
    M, D = x.shape
    ka, kb = a.shape[1], b.shape[1]
    return pl.pallas_call(
        functools.partial(_oproj_kernel, ka=ka),
        grid=(M // tm, D // tn),
        in_specs=[
            pl.BlockSpec((tm, tn), lambda i, j: (i, j)),
            pl.BlockSpec((tm, ka), lambda i, j: (i, 0)),
            pl.BlockSpec((tm, kb), lambda i, j: (i, 0)),
            pl.BlockSpec((ka + kb, tn), lambda i, j: (0, j)),
        ],
        out_specs=pl.BlockSpec((tm, tn), lambda i, j: (i, j)),
        out_shape=jax.ShapeDtypeStruct((M, D), F32),
        compiler_params=_params(("parallel", "parallel")),
        name="out_proj",
    )(x, a, b, w)


def _put_rows(ref, row, rows_per_token, val):
    ref[pl.ds(row, val.shape[0], stride=rows_per_token), :] = val


def _get_rows(ref, row, rows_per_token, n):
    return ref[pl.ds(row, n, stride=rows_per_token), :]


NSA_ROWS = 4 * NSA_KV_HEADS
WIN_ROWS = 2 * NSA_KV_HEADS
KVH_ROWS = 2 * DIL_HEADS


def _nsa_post_kernel(p_ref, cos_ref, sin_ref, g_ref, qn_ref, qr_ref, cmp_ref, rows_ref, win_ref, kvb_ref, mq_ref):
    cos, sin = cos_ref[...], sin_ref[...]
    q_g, ks_g, kw_g, mq_g = g_ref[0:1, :], g_ref[1:2, :], g_ref[2:3, :], g_ref[3:4, :]

    def tile(i):
        return p_ref[:, i * LANE:(i + 1) * LANE]

    for h in range(N_MIX_HEADS):
        qn = _rms(tile(h), q_g)
        qn_ref[:, h * LANE:(h + 1) * LANE] = qn.astype(BF16)
        qr_ref[:, h * LANE:(h + 1) * LANE] = _rope(qn, cos, sin).astype(BF16)
    for g in range(NSA_KV_HEADS):
        kc, vc = tile(12 + g), tile(15 + g)
        ks = _rope(_rms(tile(18 + g), ks_g), cos, sin)
        vs = tile(21 + g)
        kw = _rope(_rms(tile(24 + g), kw_g), cos, sin)
        vw = tile(27 + g)
        for c, val in enumerate((kc, vc)):
            cmp_ref[:, (c * 3 + g) * LANE:(c * 3 + g + 1) * LANE] = val
        for c, val in enumerate((kc, vc, ks, vs)):
            _put_rows(rows_ref, g * 4 + c, NSA_ROWS, val)
        for c, val in enumerate((kw, vw)):
            _put_rows(win_ref, g * 2 + c, WIN_ROWS, val)
        for c, val in enumerate((ks, vs, kw, vw)):
            kvb_ref[:, (c * 3 + g) * LANE:(c * 3 + g + 1) * LANE] = val.astype(BF16)
    for h in range(N_MEM_HEADS):
        mq_ref[:, h * LANE:(h + 1) * LANE] = _rms(tile(30 + h), mq_g).astype(BF16)


def _nsa_post(p, cos, sin, gains, tm):
    M = p.shape[0]
    row = lambda n: pl.BlockSpec((tm, n), lambda i: (i, 0))
    flat = lambda r: pl.BlockSpec((tm * r, LANE), lambda i: (i, 0))
    return pl.pallas_call(
        _nsa_post_kernel,
        grid=(M // tm,),
        in_specs=[row(PROJ_N), row(LANE), row(LANE), pl.BlockSpec((8, LANE), lambda i: (0, 0))],
        out_specs=[row(1536), row(1536), row(768), flat(NSA_ROWS), flat(WIN_ROWS), row(1536), row(512)],
        out_shape=[
            jax.ShapeDtypeStruct((M, 1536), BF16),
            jax.ShapeDtypeStruct((M, 1536), BF16),
            jax.ShapeDtypeStruct((M, 768), F32),
            jax.ShapeDtypeStruct((M * NSA_ROWS, LANE), F32),
            jax.ShapeDtypeStruct((M * WIN_ROWS, LANE), F32),
            jax.ShapeDtypeStruct((M, 1536), BF16),
            jax.ShapeDtypeStruct((M, 512), BF16),
        ],
        compiler_params=_params(("parallel",)),
        name="nsa_post",
    )(p, cos, sin, gains)


def _dil_post_kernel(p_ref, cos_ref, sin_ref, g_ref, qr_ref, kb_ref, vb_ref, mq_ref, st0_ref, st1_ref, st2_ref):
    cos, sin = cos_ref[...], sin_ref[...]
    q_g, k_g, mq_g = g_ref[0:1, :], g_ref[1:2, :], g_ref[2:3, :]
    st_refs = (st0_ref, st1_ref, st2_ref)
    for h in range(N_MIX_HEADS):
        sl = slice(h * LANE, (h + 1) * LANE)
        qr_ref[:, sl] = _rope(_rms(p_ref[:, sl], q_g), cos, sin).astype(BF16)
        k = _rope(_rms(p_ref[:, (12 + h) * LANE:(13 + h) * LANE], k_g), cos, sin)
        v = p_ref[:, (24 + h) * LANE:(25 + h) * LANE]
        kb_ref[:, sl] = k.astype(BF16)
        vb_ref[:, sl] = v.astype(BF16)
        st = st_refs[h // DIL_HEADS]
        _put_rows(st, h % DIL_HEADS, KVH_ROWS, k)
        _put_rows(st, DIL_HEADS + h % DIL_HEADS, KVH_ROWS, v)
    for h in range(N_MEM_HEADS):
        mq_ref[:, h * LANE:(h + 1) * LANE] = _rms(p_ref[:, (36 + h) * LANE:(37 + h) * LANE], mq_g).astype(BF16)


def _dil_post(p, cos, sin, gains, tm):
    M = p.shape[0]
    row = lambda n: pl.BlockSpec((tm, n), lambda i: (i, 0))
    flat = pl.BlockSpec((tm * KVH_ROWS, LANE), lambda i: (i, 0))
    st_shape = jax.ShapeDtypeStruct((M * KVH_ROWS, LANE), F32)
    return pl.pallas_call(
        _dil_post_kernel,
        grid=(M // tm,),
        in_specs=[row(PROJ_N), row(LANE), row(LANE), pl.BlockSpec((8, LANE), lambda i: (0, 0))],
        out_specs=[row(1536), row(1536), row(1536), row(512), flat, flat, flat],
        out_shape=[
            jax.ShapeDtypeStruct((M, 1536), BF16),
            jax.ShapeDtypeStruct((M, 1536), BF16),
            jax.ShapeDtypeStruct((M, 1536), BF16),
            jax.ShapeDtypeStruct((M, 512), BF16),
            st_shape, st_shape, st_shape,
        ],
        compiler_params=_params(("parallel",)),
        name="dil_post",
    )(p, cos, sin, gains)


def _memkv_post_kernel(x_ref, g_ref, o_ref):
    for h in range(N_MEM_HEADS):
        _put_rows(o_ref, h, KVH_ROWS, _rms(x_ref[:, h * LANE:(h + 1) * LANE], g_ref[...]))
        _put_rows(o_ref, N_MEM_HEADS + h, KVH_ROWS, x_ref[:, (N_MEM_HEADS + h) * LANE:(N_MEM_HEADS + h + 1) * LANE])


def _memkv_post(x, g, tm=256):
    M, N = x.shape
    return pl.pallas_call(
        _memkv_post_kernel,
        grid=(M // tm,),
        in_specs=[pl.BlockSpec((tm, N), lambda i: (i, 0)), pl.BlockSpec((1, LANE), lambda i: (0, 0))],
        out_specs=pl.BlockSpec((tm * KVH_ROWS, LANE), lambda i: (i, 0)),
        out_shape=jax.ShapeDtypeStruct((M * KVH_ROWS, LANE), F32),
        compiler_params=_params(("parallel",)),
        name="memkv_post",
    )(x, g.reshape(1, LANE))


def _mem_attn_kernel(q_ref, kv_ref, o_ref, *, tq):
    for h in range(N_MEM_HEADS):
        sl = slice(h * LANE, (h + 1) * LANE)
        if tq == 1:
            q = _rows16(q_ref[:, sl], 1).astype(BF16)
        else:
            q = q_ref[:, sl]
        k = _get_rows(kv_ref, h, KVH_ROWS, N_MEM).astype(BF16)
        v = _get_rows(kv_ref, N_MEM_HEADS + h, KVH_ROWS, N_MEM).astype(BF16)
        s = _dot_t(q, k) * SCALE
        m = jnp.max(s, axis=-1, keepdims=True)
        e = jnp.exp(s - m)
        p = e / jnp.sum(e, axis=-1, keepdims=True)
        o = _dot(p.astype(BF16), v)
        o_ref[:, sl] = o[0:tq, :].astype(o_ref.dtype)


def _mem_attn(q, kv, tq):
    B, T, _ = q.shape
    return pl.pallas_call(
        functools.partial(_mem_attn_kernel, tq=tq),
        grid=(B, T // tq),
        in_specs=[
            pl.BlockSpec((None, tq, 512), lambda b, i: (b, i, 0)),
            pl.BlockSpec((None, N_MEM * KVH_ROWS, LANE), lambda b, i: (b, 0, 0)),
        ],
        out_specs=pl.BlockSpec((None, tq, 512), lambda b, i: (b, i, 0)),
        out_shape=jax.ShapeDtypeStruct((B, T, 512), q.dtype),
        compiler_params=_params(("parallel", "parallel")),
        name="mem_attn",
    )(q, kv)


def _gelu_tanh(x):
    return 0.5 * x * (1.0 + jnp.tanh(0.7978845608028654 * (x + 0.044715 * (x * x * x))))


def _compress_finish(h, b1, w2):
    n = h.shape[0]
    hid = b1 + h[:, :LANE] + pltpu.roll(h[:, LANE:], n - 1, 0)
    return _dot(_gelu_tanh(hid).astype(BF16), w2)


def _compress(x_bf, w1, b1, w2):
    return _compress_finish(_dot(x_bf, w1), b1, w2)


def _cmp_prompt_kernel(x_ref, w1_ref, b1_ref, w2_ref, kcg_ref, o_ref, xs_ref, *, n):
    kv = pl.program_id(1)
    for c in range(CMP_STRIDE):
        xs_ref[:, c * LANE:(c + 1) * LANE] = x_ref[pl.ds(c, n, stride=CMP_STRIDE), :].astype(BF16)
    out = _compress(xs_ref[...], w1_ref[...], b1_ref[...], w2_ref[...])
    out = jnp.where(kv == 0, _rms(out, kcg_ref[...]), out)
    rid = lax.broadcasted_iota(jnp.int32, out.shape, 0)
    o_ref[...] = jnp.where(rid < n - 1, out, 0.0).astype(BF16)


def _cmp_prompt(rows, w1r, b1, w2, kc_g):
    B, T, _ = rows.shape
    n = T // CMP_STRIDE
    return pl.pallas_call(
        functools.partial(_cmp_prompt_kernel, n=n),
        grid=(B, 2, NSA_KV_HEADS),
        in_specs=[
            pl.BlockSpec((None, T, LANE), lambda b, kv, g: (b, 0, kv * 3 + g)),
            pl.BlockSpec((None, CMP_STRIDE * LANE, 2 * LANE), lambda b, kv, g: (kv, 0, 0)),
            pl.BlockSpec((None, 1, LANE), lambda b, kv, g: (kv, 0, 0)),
            pl.BlockSpec((None, LANE, LANE), lambda b, kv, g: (kv, 0, 0)),
            pl.BlockSpec((1, LANE), lambda b, kv, g: (0, 0)),
        ],
        out_specs=pl.BlockSpec((None, None, None, n, LANE), lambda b, kv, g: (b, kv, g, 0, 0)),
        out_shape=jax.ShapeDtypeStruct((B, 2, NSA_KV_HEADS, n, LANE), BF16),
        scratch_shapes=[pltpu.VMEM((n, CMP_STRIDE * LANE), BF16)],
        compiler_params=_params(("parallel", "parallel", "parallel")),
        name="cmp_prompt",
    )(rows, w1r, b1, w2, kc_g)


def _select_blocks(score, cur, n_blocks):
    tq = score.shape[0]
    blk = lax.broadcasted_iota(jnp.int32, score.shape, 1)
    forced = (blk == 0) | (blk == cur) | (blk == cur - 1)
    sc = jnp.where(blk <= cur, jnp.where(forced, FORCE_SCORE, score), NEG)
    sct = sc.T[0:n_blocks, :]
    bi = lax.broadcasted_iota(jnp.int32, sct.shape, 0)
    rank = jnp.zeros(sct.shape, F32)
    for i in range(n_blocks):
        si = sct[i:i + 1, :]
        ahead = (si > sct) | ((si == sct) & (bi > i))
        rank = rank + jnp.where(ahead, 1.0, 0.0)
    chosen = jnp.where((rank < SEL_TOPK) & (sct > 0.5 * NEG), 1.0, 0.0)
    return jnp.concatenate([chosen, jnp.zeros((LANE - n_blocks, tq), F32)], axis=0).T


def _score_tile(q, k_ref, kt, slot, bias, s_ref, m_ref, tk, first=False):
    s = _dot_t(q, k_ref[pl.ds(pl.multiple_of(kt * tk, tk), tk), :]) * SCALE
    if bias is not None:
        s = s + bias
    s_ref[slot] = s
    m_ref[...] = s if first else jnp.maximum(m_ref[...], s)


def _value_tile(v_ref, kt, slot, s_ref, m_ref, l_ref, acc_ref, tk, first=False):
    e = jnp.exp(s_ref[slot] - m_ref[...])
    pv = _dot(e.astype(BF16), v_ref[pl.ds(pl.multiple_of(kt * tk, tk), tk), :])
    if first:
        l_ref[...] = e
        acc_ref[...] = pv
    else:
        l_ref[...] += e
        acc_ref[...] += pv


def _nsa_attn_kernel(qn_ref, qr_ref, gl_ref, kc_ref, vc_ref, ks_ref, vs_ref, kw_ref, vw_ref, cover_ref,
                     o_ref, s_ref, m_ref, l_ref, acc_ref, sel_ref, *, tq, ns):
    qi = pl.program_id(2)
    R = NSA_GROUP
    t0 = qi * tq
    stack = lambda ref: jnp.concatenate([ref[:, r * LANE:(r + 1) * LANE] for r in range(R)], axis=0)
    rows4 = lambda x: jnp.concatenate([x] * R, axis=0)
    tpos_q = t0 + lax.broadcasted_iota(jnp.int32, (tq, 1), 0)
    row_in = lax.broadcasted_iota(jnp.int32, (R * tq, 1), 0) & (tq - 1)
    col = lax.broadcasted_iota(jnp.int32, (R * tq, tq), 1)

    s = _dot_t(stack(qn_ref), kc_ref[...]) * SCALE
    cblk = lax.broadcasted_iota(jnp.int32, (R * tq, LANE), 1)
    cmask = (CMP_STRIDE * cblk + (CMP_BLOCK - 1) <= t0 + row_in) & (cblk < kc_ref.shape[0] - 1)
    p, _, _ = _softmax_masked(s, cmask)
    o_cmp = _dot(p.astype(BF16), vc_ref[...])
    imp = p[0:tq] + p[tq:2 * tq] + p[2 * tq:3 * tq] + p[3 * tq:4 * tq]
    score = _dot3(imp, cover_ref[...])
    cur = lax.shift_right_arithmetic(tpos_q, SEL_BLOCK.bit_length() - 1)

    @pl.when(t0 + tq <= SEL_TOPK * SEL_BLOCK)
    def _():
        sel_ref[...] = jnp.where(lax.broadcasted_iota(jnp.int32, (tq, LANE), 1) <= cur, 1.0, 0.0).astype(BF16)

    @pl.when(t0 + tq > SEL_TOPK * SEL_BLOCK)
    def _():
        sel_ref[...] = _select_blocks(score, cur, ns).astype(BF16)

    sel = sel_ref[...]
    q_rot = stack(qr_ref)
    blocks_per_tile = tq // SEL_BLOCK
    causal = jnp.where(col <= row_in, 0.0, NEG)
    far = jnp.where(col >= row_in, 0.0, NEG)

    def row_max():
        m_ref[...] = jnp.broadcast_to(jnp.max(m_ref[...], axis=-1, keepdims=True), m_ref.shape)

    def result():
        return acc_ref[...] / jnp.maximum(jnp.sum(l_ref[...], axis=-1, keepdims=True), 1e-30)

    def member_bias(kt):
        key_blk = lax.shift_right_arithmetic(lax.broadcasted_iota(jnp.int32, (LANE, tq), 1),
                                             SEL_BLOCK.bit_length() - 1)
        expand = lax.broadcasted_iota(jnp.int32, (LANE, tq), 0) == kt * blocks_per_tile + key_blk
        member = _dot(sel, jnp.where(expand, 1.0, 0.0).astype(BF16))
        return rows4((member - 1.0) * (-NEG))

    _score_tile(q_rot, ks_ref, qi, qi, member_bias(qi) + causal, s_ref, m_ref, tq, first=True)

    def sel_scores(kt, carry):
        _score_tile(q_rot, ks_ref, kt, kt, member_bias(kt), s_ref, m_ref, tq)
        return carry

    lax.fori_loop(0, qi, sel_scores, 0)
    row_max()
    _value_tile(vs_ref, qi, qi, s_ref, m_ref, l_ref, acc_ref, tq, first=True)

    def sel_values(kt, carry):
        _value_tile(vs_ref, kt, kt, s_ref, m_ref, l_ref, acc_ref, tq)
        return carry

    lax.fori_loop(0, qi, sel_values, 0)
    o_sel = result()

    n_back = NSA_WINDOW // tq
    for back in range(n_back + 1):
        bias = jnp.where(qi >= back, 0.0, NEG)
        if back == 0:
            bias = causal
        elif back == n_back:
            bias = far + bias
        _score_tile(q_rot, kw_ref, jnp.maximum(qi - back, 0), back, bias, s_ref, m_ref, tq, first=back == 0)
    row_max()
    for back in range(n_back + 1):
        _value_tile(vw_ref, jnp.maximum(qi - back, 0), back, s_ref, m_ref, l_ref, acc_ref, tq, first=back == 0)
    o_win = result()

    gates = jax.nn.sigmoid(gl_ref[...])
    for r in range(R):
        rs = slice(r * tq, (r + 1) * tq)
        o = (gates[:, 3 * r:3 * r + 1] * o_cmp[rs] + gates[:, 3 * r + 1:3 * r + 2] * o_sel[rs]
             + gates[:, 3 * r + 2:3 * r + 3] * o_win[rs])
        o_ref[:, r * LANE:(r + 1) * LANE] = o.astype(BF16)


def _nsa_attn(qn, qr, proj, cmp_kv, kvb, cover, tq=256):
    B, T, _ = qn.shape
    nc = cmp_kv.shape[3]
    G = NSA_KV_HEADS
    assert tq % LANE == 0 and NSA_WINDOW % tq == 0 and nc == LANE and T // SEL_BLOCK <= LANE
    rows = NSA_GROUP * tq
    qspec = pl.BlockSpec((None, tq, 4 * LANE), lambda b, g, i: (b, i, g))
    kvspec = lambda c: pl.BlockSpec((None, T, LANE), lambda b, g, i: (b, 0, c * 3 + g))
    return pl.pallas_call(
        functools.partial(_nsa_attn_kernel, tq=tq, ns=T // SEL_BLOCK),
        grid=(B, G, T // tq),
        in_specs=[
            qspec, qspec,
            pl.BlockSpec((None, tq, LANE), lambda b, g, i: (b, i, NSA_GATE_BLK + g)),
            pl.BlockSpec((None, None, None, nc, LANE), lambda b, g, i: (b, 0, g, 0, 0)),
            pl.BlockSpec((None, None, None, nc, LANE), lambda b, g, i: (b, 1, g, 0, 0)),
            kvspec(0), kvspec(1), kvspec(2), kvspec(3),
            pl.BlockSpec((nc, LANE), lambda b, g, i: (0, 0)),
        ],
        out_specs=qspec,
        out_shape=jax.ShapeDtypeStruct((B, T, 1536), BF16),
        scratch_shapes=[pltpu.VMEM((T // tq, rows, tq), F32),
                        pltpu.VMEM((rows, tq), F32),
                        pltpu.VMEM((rows, tq), F32),
                        pltpu.VMEM((rows, LANE), F32),
                        pltpu.VMEM((tq, LANE), BF16)],
        compiler_params=_params(("parallel", "parallel", "arbitrary")),
        name="nsa_attn",
    )(qn, qr, proj, cmp_kv, cmp_kv, kvb, kvb, kvb, kvb, cover)


def _dil_band_kernel(q_ref, kp_ref, kc_ref, vp_ref, vc_ref, o_ref, st_ref, *, tq, window):
    i = pl.program_id(2)
    qpos = i * tq + lax.broadcasted_iota(jnp.int32, (tq, 2 * tq), 0)
    kpos = (i - 1) * tq + lax.broadcasted_iota(jnp.int32, (tq, 2 * tq), 1)
    diff = qpos - kpos
    mask = (diff >= 0) & (diff <= window) & (kpos >= 0)
    lane = lax.broadcasted_iota(jnp.int32, (tq, LANE), 1)
    stats = jnp.zeros((tq, LANE), F32)
    for h in range(DIL_HEADS):
        sl = slice(h * LANE, (h + 1) * LANE)
        k = jnp.concatenate([kp_ref[:, sl], kc_ref[:, sl]], axis=0)
        v = jnp.concatenate([vp_ref[:, sl], vc_ref[:, sl]], axis=0)
        p, m, l = _softmax_masked(_dot_t(q_ref[:, sl], k) * SCALE, mask)
        o_ref[:, sl] = _dot(p.astype(BF16), v)
        stats = jnp.where(lane == h, m, stats)
        stats = jnp.where(lane == DIL_HEADS + h, l, stats)
    st_ref[...] = stats


def _dil_band(q, k, v, g, dil, tq=128):
    B, T, C = q.shape
    S = T // dil
    nb = C // (4 * LANE)
    view = lambda a: a.reshape(B, S, dil * C)
    cur = pl.BlockSpec((None, tq, 4 * LANE), lambda b, r, i: (b, i, r * nb + g))
    prev = pl.BlockSpec((None, tq, 4 * LANE), lambda b, r, i: (b, jnp.maximum(i - 1, 0), r * nb + g))
    o, st = pl.pallas_call(
        functools.partial(_dil_band_kernel, tq=tq, window=DIL_PAIRS[g][0] // dil),
        grid=(B, dil, S // tq),
        in_specs=[cur, prev, cur, prev, cur],
        out_specs=[pl.BlockSpec((None, tq, 4 * LANE), lambda b, r, i: (b, i, r)),
                   pl.BlockSpec((None, tq, LANE), lambda b, r, i: (b, i, r))],
        out_shape=[jax.ShapeDtypeStruct((B, S, dil * 4 * LANE), F32),
                   jax.ShapeDtypeStruct((B, S, dil * LANE), F32)],
        compiler_params=_params(("parallel", "parallel", "parallel")),
        name=f"dil_band{g}",
    )(view(q), view(k), view(k), view(v), view(v))
    return o.reshape(B * T, 4 * LANE), st.reshape(B * T, LANE)


def _mix_groups(os_, ms, ls):
    m_all = jnp.maximum(jnp.maximum(ms[0], ms[1]), ms[2])
    ws = [jnp.exp(m - m_all) * l for m, l in zip(ms, ls)]
    tot = ws[0] + ws[1] + ws[2]
    return (ws[0] / tot) * os_[0] + (ws[1] / tot) * os_[1] + (ws[2] / tot) * os_[2]


def _dil_mix_kernel(o0_ref, o1_ref, o2_ref, s0_ref, s1_ref, s2_ref, o_ref):
    o_refs, s_refs = (o0_ref, o1_ref, o2_ref), (s0_ref, s1_ref, s2_ref)
    for h in range(DIL_HEADS):
        sl = slice(h * LANE, (h + 1) * LANE)
        ms = [s[:, h:h + 1] for s in s_refs]
        ls = [s[:, DIL_HEADS + h:DIL_HEADS + h + 1] for s in s_refs]
        o_ref[:, sl] = _mix_groups([o[:, sl] for o in o_refs], ms, ls).astype(BF16)


def _dil_mix(os_, sts, tm=512):
    M = os_[0].shape[0]
    ospec = pl.BlockSpec((tm, 4 * LANE), lambda i: (i, 0))
    sspec = pl.BlockSpec((tm, LANE), lambda i: (i, 0))
    return pl.pallas_call(
        _dil_mix_kernel,
        grid=(M // tm,),
        in_specs=[ospec] * 3 + [sspec] * 3,
        out_specs=ospec,
        out_shape=jax.ShapeDtypeStruct((M, 4 * LANE), BF16),
        compiler_params=_params(("parallel",)),
        name="dil_mix",
    )(*os_, *sts)


DEC_PAGES_PER_STEP = 16


def _dec_select_kernel(tbl_ref, *refs, n_pp, n_steps):
    pages = refs[:n_pp]
    (qn_ref, w1_ref, b1_ref, w2_ref, kcg_ref, cover_ref, idx_ref, ocmp_ref, h_ref) = refs[n_pp:]
    j = pl.program_id(1)
    cpp = PAGE_SIZE // CMP_STRIDE
    rows = n_pp * cpp
    n = n_steps * rows
    ns = cover_ref.shape[1]

    for g in range(NSA_KV_HEADS):
        for kv in range(2):
            x = jnp.concatenate(
                [_get_rows(pg, g * 4 + kv, NSA_ROWS, PAGE_SIZE).reshape(cpp, CMP_STRIDE * LANE) for pg in pages],
                axis=0)
            h_ref[g * 2 + kv, pl.ds(pl.multiple_of(j * rows, rows), rows), :] = _dot(x.astype(BF16), w1_ref[kv])

    @pl.when(j == n_steps - 1)
    def _():
        idx_ref[...] = jnp.zeros_like(idx_ref)
        for g in range(NSA_KV_HEADS):
            kc = _rms(_compress_finish(h_ref[g * 2], b1_ref[0], w2_ref[0]), kcg_ref[...]).astype(BF16)
            vc = _compress_finish(h_ref[g * 2 + 1], b1_ref[1], w2_ref[1]).astype(BF16)
            q = _rows16(qn_ref[:, g * NSA_GROUP * LANE:(g + 1) * NSA_GROUP * LANE], NSA_GROUP).astype(BF16)
            s = _dot_t(q, kc) * SCALE
            valid = lax.broadcasted_iota(jnp.int32, s.shape, 1) < n - 1
            p, _, _ = _softmax_masked(s, valid)
            ocmp_ref[g] = _dot(p.astype(BF16), vc)[0:NSA_GROUP, :]
            rid = lax.broadcasted_iota(jnp.int32, p.shape, 0)
            imp = jnp.sum(jnp.where(rid < NSA_GROUP, p, 0.0), axis=0, keepdims=True)
            score = _dot3(jnp.broadcast_to(imp, (8, n)), cover_ref[...])

            a = jnp.broadcast_to(score[0:1, :], (ns, ns))
            lane = lax.broadcasted_iota(jnp.int32, (ns, ns), 1)
            sub = lax.broadcasted_iota(jnp.int32, (ns, ns), 0)
            cur = n * CMP_STRIDE // SEL_BLOCK
            forced = (lane == 0) | (lane == cur) | (lane == cur - 1)
            a = jnp.where(lane <= cur, jnp.where(forced, FORCE_SCORE, a), NEG)
            at = a.T
            ahead_r = (at > a) | ((at == a) & (sub < lane))
            chosen_r = ((jnp.sum(jnp.where(ahead_r, 1.0, 0.0), axis=0, keepdims=True) < SEL_TOPK)
                        & (a[0:1, :] > 0.5 * NEG))
            ahead_c = (a > at) | ((a == at) & (lane < sub))
            chosen_c = ((jnp.sum(jnp.where(ahead_c, 1.0, 0.0), axis=1, keepdims=True) < SEL_TOPK)
                        & (at[:, 0:1] > 0.5 * NEG))
            before = jnp.sum(jnp.where(chosen_r & (lane < sub), 1.0, 0.0), axis=1, keepdims=True)
            slot = lax.broadcasted_iota(jnp.int32, (ns, LANE), 1)
            onehot = chosen_c & (before == slot.astype(F32))
            blk = lax.broadcasted_iota(jnp.int32, (ns, LANE), 0)
            picked = jnp.sum(jnp.where(onehot, blk.astype(F32), 0.0), axis=0, keepdims=True)
            filled = jnp.sum(jnp.where(onehot, 1.0, 0.0), axis=0, keepdims=True)
            idx_ref[g:g + 1, :] = jnp.where(filled > 0.5, picked, float(cur)).astype(jnp.int32)


def _dec_select(cache, table, qn, w1r, b1, w2, kc_g, cover):
    B, n_pages = table.shape
    n_pp = DEC_PAGES_PER_STEP
    n_steps = n_pages // n_pp
    n = n_pages * (PAGE_SIZE // CMP_STRIDE)
    ns = cover.shape[1]
    G = NSA_KV_HEADS
    const = lambda *shape: pl.BlockSpec(shape, lambda b, j, tbl: (0,) * len(shape))
    page_spec = lambda p: pl.BlockSpec((None, PAGE_SIZE * NSA_ROWS, LANE), lambda b, j, tbl: (tbl[b, j * n_pp + p], 0, 0))
    grid_spec = pltpu.PrefetchScalarGridSpec(
        num_scalar_prefetch=1,
        grid=(B, n_steps),
        in_specs=[page_spec(p) for p in range(n_pp)] + [
            pl.BlockSpec((None, 1, 1536), lambda b, j, tbl: (b, 0, 0)),
            const(2, CMP_STRIDE * LANE, 2 * LANE), const(2, 1, LANE), const(2, LANE, LANE), const(1, LANE),
            const(n, ns),
        ],
        out_specs=[pl.BlockSpec((None, 8, LANE), lambda b, j, tbl: (b, 0, 0)),
                   pl.BlockSpec((None, G, NSA_GROUP, LANE), lambda b, j, tbl: (b, 0, 0, 0))],
        scratch_shapes=[pltpu.VMEM((2 * G, n, 2 * LANE), F32)],
    )
    return pl.pallas_call(
        functools.partial(_dec_select_kernel, n_pp=n_pp, n_steps=n_steps),
        grid_spec=grid_spec,
        out_shape=[jax.ShapeDtypeStruct((B, 8, LANE), jnp.int32),
                   jax.ShapeDtypeStruct((B, G, NSA_GROUP, LANE), F32)],
        compiler_params=_params(("arbitrary", "arbitrary")),
        name="dec_select",
    )(table, *([cache] * n_pp), qn, w1r, b1, w2, kc_g, cover)


def _dec_attn_kernel(tbl_ref, idx_ref, *refs, n_sel, cur, wb):
    blocks = refs[:n_sel]
    (qr_ref, kvn_ref, win_ref, gl_ref, ocmp_ref, o_ref) = refs[n_sel:]
    b, g = pl.program_id(0), pl.program_id(1)
    R = NSA_GROUP
    q = _rows16(qr_ref[...], R).astype(BF16)
    qf = q.astype(F32)
    new = kvn_ref[...]
    ks_n, vs_n, kw_n, vw_n = (new[:, c * LANE:(c + 1) * LANE] for c in range(4))

    k = jnp.concatenate([_get_rows(r, g * 4 + 2, NSA_ROWS, SEL_BLOCK) for r in blocks], axis=0).astype(BF16)
    v = jnp.concatenate([_get_rows(r, g * 4 + 3, NSA_ROWS, SEL_BLOCK) for r in blocks], axis=0).astype(BF16)
    s = _dot_t(q, k) * SCALE
    blk_of = lax.shift_right_arithmetic(lax.broadcasted_iota(jnp.int32, s.shape, 1), SEL_BLOCK.bit_length() - 1)
    valid = jnp.zeros(s.shape, jnp.int32)
    for n in range(n_sel):
        is_past = jnp.where(idx_ref[(b * NSA_KV_HEADS + g) * n_sel + n] != cur, 1, 0)
        valid = jnp.where(blk_of == n, is_past, valid)
    valid = valid > 0
    s_new = jnp.sum(qf * ks_n, axis=-1, keepdims=True) * SCALE
    s = jnp.where(valid, s, NEG)
    m = jnp.maximum(jnp.max(s, axis=-1, keepdims=True), s_new)
    e = jnp.where(valid, jnp.exp(s - m), 0.0)
    e_new = jnp.exp(s_new - m)
    l = jnp.sum(e, axis=-1, keepdims=True) + e_new
    o_sel = (_dot(e.astype(BF16), v) + e_new.astype(BF16).astype(F32) * vs_n) / l

    s = _dot_t(q, _get_rows(win_ref, g * 2, WIN_ROWS, wb).astype(BF16)) * SCALE
    s_new = jnp.sum(qf * kw_n, axis=-1, keepdims=True) * SCALE
    m = jnp.maximum(jnp.max(s, axis=-1, keepdims=True), s_new)
    e = jnp.exp(s - m)
    e_new = jnp.exp(s_new - m)
    l = jnp.sum(e, axis=-1, keepdims=True) + e_new
    o_win = (_dot(e.astype(BF16), _get_rows(win_ref, g * 2 + 1, WIN_ROWS, wb).astype(BF16))
             + e_new.astype(BF16).astype(F32) * vw_n) / l

    gates = jax.nn.sigmoid(gl_ref[...])
    o_cmp = ocmp_ref[...]
    for r in range(R):
        o = (gates[:, 3 * r:3 * r + 1] * o_cmp[r:r + 1] + gates[:, 3 * r + 1:3 * r + 2] * o_sel[r:r + 1]
             + gates[:, 3 * r + 2:3 * r + 3] * o_win[r:r + 1])
        o_ref[:, r * LANE:(r + 1) * LANE] = o


def _dec_attn(cache, table, idx, qr, kvb, win_state, proj, ocmp):
    B, n_pages = table.shape
    n_sel = SEL_TOPK
    cur = n_pages * PAGE_SIZE // SEL_BLOCK
    wb = win_state.shape[1] // WIN_ROWS
    assert wb <= NSA_WINDOW
    halves = PAGE_SIZE // SEL_BLOCK

    def blk_spec(n):
        def imap(b, g, tbl, idx):
            i = jnp.minimum(idx[(b * NSA_KV_HEADS + g) * n_sel + n], cur - 1)
            return (tbl[b, i // halves], i % halves, 0)
        return pl.BlockSpec((None, SEL_BLOCK * NSA_ROWS, LANE), imap)

    grid_spec = pltpu.PrefetchScalarGridSpec(
        num_scalar_prefetch=2,
        grid=(B, NSA_KV_HEADS),
        in_specs=[blk_spec(n) for n in range(n_sel)] + [
            pl.BlockSpec((None, 1, 4 * LANE), lambda b, g, tbl, idx: (b, 0, g)),
            pl.BlockSpec((None, 1, 4 * LANE), lambda b, g, tbl, idx: (b, 0, g)),
            pl.BlockSpec((None, wb * WIN_ROWS, LANE), lambda b, g, tbl, idx: (b, 0, 0)),
            pl.BlockSpec((None, 1, LANE), lambda b, g, tbl, idx: (b, 0, NSA_GATE_BLK + g)),
            pl.BlockSpec((None, None, NSA_GROUP, LANE), lambda b, g, tbl, idx: (b, g, 0, 0)),
        ],
        out_specs=pl.BlockSpec((None, 1, 4 * LANE), lambda b, g, tbl, idx: (b, 0, g)),
    )
    kvn = kvb.reshape(B, 1, 4, NSA_KV_HEADS, LANE).transpose(0, 1, 3, 2, 4).reshape(B, 1, 1536)
    return pl.pallas_call(
        functools.partial(_dec_attn_kernel, n_sel=n_sel, cur=cur, wb=wb),
        grid_spec=grid_spec,
        out_shape=jax.ShapeDtypeStruct((B, 1, 1536), F32),
        compiler_params=_params(("arbitrary", "arbitrary")),
        name="dec_attn",
    )(table, idx, *([cache] * n_sel), qr, kvn, win_state, proj, ocmp)


def _dec_dil_kernel(q_ref, n0_ref, n1_ref, n2_ref, s0_ref, s1_ref, s2_ref, o_ref, t0_ref, t1_ref, t2_ref):
    qall = q_ref[...]
    states, news, outs = (s0_ref, s1_ref, s2_ref), (n0_ref, n1_ref, n2_ref), (t0_ref, t1_ref, t2_ref)

    for st, new, out in zip(states, news, outs):
        keep = st.shape[0] - KVH_ROWS
        out[0:keep, :] = st[KVH_ROWS:, :]
        out[keep:, :] = new[...]

    for h in range(DIL_HEADS):
        os_, ms, ls = [], [], []
        for g, st in enumerate(states):
            hs = slice((g * DIL_HEADS + h) * LANE, (g * DIL_HEADS + h + 1) * LANE)
            window, dil = DIL_PAIRS[g]
            kn = news[g][h:h + 1, :].astype(BF16).astype(F32)
            vn = news[g][DIL_HEADS + h:DIL_HEADS + h + 1, :].astype(BF16).astype(F32)
            q = _rows16(qall[:, hs], 1).astype(BF16)
            k = _get_rows(st, h, KVH_ROWS * dil, window // dil).astype(BF16)
            v = _get_rows(st, DIL_HEADS + h, KVH_ROWS * dil, window // dil).astype(BF16)
            s = _dot_t(q, k) * SCALE
            s_new = jnp.sum(q.astype(F32) * kn, axis=-1, keepdims=True) * SCALE
            m = jnp.maximum(jnp.max(s, axis=-1, keepdims=True), s_new)
            e = jnp.exp(s - m)
            e_new = jnp.exp(s_new - m)
            l = jnp.sum(e, axis=-1, keepdims=True) + e_new
            ln = jnp.maximum(l, 1e-30)
            o = _dot((e / ln).astype(BF16), v) + (e_new / ln).astype(BF16).astype(F32) * vn
            os_.append(o)
            ms.append(m)
            ls.append(l)
        o_ref[:, h * LANE:(h + 1) * LANE] = _mix_groups(os_, ms, ls)[0:1, :]


def _dec_dil(qr, news, states):
    B = qr.shape[0]
    in_specs = [pl.BlockSpec((None, 1, 1536), lambda b: (b, 0, 0))]
    in_specs += [pl.BlockSpec((None, KVH_ROWS, LANE), lambda b: (b, 0, 0))] * len(DIL_PAIRS)
    st_specs = []
    for g, (window, dil) in enumerate(DIL_PAIRS):
        assert states[g].shape[1] == window * KVH_ROWS, "rolling buffer shorter than the window is not supported"
        st_specs.append(pl.BlockSpec((None, window * KVH_ROWS, LANE), lambda b: (b, 0, 0)))
    return pl.pallas_call(
        _dec_dil_kernel,
        grid=(B,),
        in_specs=in_specs + st_specs,
        out_specs=[pl.BlockSpec((None, 1, 4 * LANE), lambda b: (b, 0, 0))] + st_specs,
        out_shape=[jax.ShapeDtypeStruct((B, 1, 4 * LANE), F32)]
        + [jax.ShapeDtypeStruct(s.shape, F32) for s in states],
        compiler_params=_params(("parallel",)),
        name="dec_dil",
    )(qr, *news, *states)


def _rope_tables(pos):
    half = HEAD_DIM // 2
    inv = ROPE_THETA ** (-jnp.arange(half, dtype=F32) / half)
    ang = pos.astype(F32)[:, None] * inv
    cos, sin = jnp.cos(ang), jnp.sin(ang)
    return jnp.concatenate([cos, cos], axis=-1), jnp.concatenate([-sin, sin], axis=-1)


def _cover(nc, ns, rows, cols):
    c0 = jnp.arange(nc)[:, None] * CMP_STRIDE
    s0 = jnp.arange(ns)[None, :] * SEL_BLOCK
    cover = jnp.clip(jnp.minimum(c0 + CMP_BLOCK, s0 + SEL_BLOCK) - jnp.maximum(c0, s0), 0, CMP_BLOCK)
    cover = cover.astype(F32) / CMP_BLOCK
    return jnp.pad(cover, ((0, rows - nc), (0, cols - ns))).astype(BF16)


def _pad_gains(*gs):
    return jnp.pad(jnp.stack(gs, axis=0), ((0, 8 - len(gs)), (0, 0)))


def kernel(x_prompt, x_sample, mem_prompt, cache_nsa_kv, page_table, state_nsa_win, state_dil_0, state_dil_1,
           state_dil_2, cache_mem_kv, ff_norm, ff_w_gate, ff_w_up, ff_w_down, mix_norm, mem_norm, w_mem_kv,
           mem_q_g, mem_k_g, nsa_w_in, nsa_q_g, nsa_kc_g, nsa_ks_g, nsa_kw_g, nsa_cmp_w1, nsa_cmp_b1, nsa_cmp_w2,
           nsa_w_out, dil_w_in, dil_q_g, dil_k_g, dil_w_out):
    B, T, D = x_prompt.shape
    Bs = x_sample.shape[0]
    assert x_sample.shape[1] == 1, "the sample group is a single-token decode step"
    n_pages = page_table.shape[1]
    past_len = n_pages * PAGE_SIZE
    H, G, d = N_MIX_HEADS, NSA_KV_HEADS, HEAD_DIM

    wg, wu, wd = ff_w_gate, ff_w_up, ff_w_down
    ffg = ff_norm.reshape(ff_norm.shape[0], 2, 1, D)
    gate_w = jnp.pad(nsa_w_in[:, H * d:H * d + 3 * H].reshape(D, G, 3 * NSA_GROUP), ((0, 0), (0, 0), (0, LANE - 12)))
    nsa_w = jnp.concatenate([nsa_w_in[:, :H * d], nsa_w_in[:, H * d + 3 * H:], gate_w.reshape(D, G * LANE)], axis=1)
    nsa_w = jnp.pad(nsa_w, ((0, 0), (0, PROJ_N - nsa_w.shape[1]))).astype(BF16)
    dil_w = dil_w_in.astype(BF16)
    nsa_wo, dil_wo = nsa_w_out.astype(BF16), dil_w_out.astype(BF16)
    w1r = nsa_cmp_w1.reshape(2, 2, CMP_STRIDE, d, d).transpose(0, 2, 3, 1, 4).reshape(2, CMP_STRIDE * d, 2 * d)
    w1r = w1r.astype(BF16)
    cmp_b1 = nsa_cmp_b1.reshape(2, 1, d)
    cmp_w2 = nsa_cmp_w2.astype(BF16)
    kc_g = nsa_kc_g.reshape(1, d)

    n_pool = cache_nsa_kv.shape[0]
    cache_rows = cache_nsa_kv.transpose(0, 1, 3, 2, 4).reshape(n_pool, PAGE_SIZE * NSA_ROWS, LANE)
    win_rows = state_nsa_win.transpose(0, 1, 3, 2, 4).reshape(Bs, state_nsa_win.shape[1] * WIN_ROWS, LANE)
    dil_states = (state_dil_0, state_dil_1, state_dil_2)
    dil_rows = [s.reshape(Bs, s.shape[1] * KVH_ROWS, LANE) for s in dil_states]
    mem_rows_s = cache_mem_kv.reshape(2, Bs, N_MEM * KVH_ROWS, LANE)

    mem2d = mem_prompt.reshape(B * N_MEM, D)
    mem_rows_p = []
    for i in range(2):
        kv = _norm_matmul(mem2d, mem_norm[i], w_mem_kv[i].astype(BF16), tm=256, tn=1024)
        mem_rows_p.append(_memkv_post(kv, mem_k_g[i]).reshape(B, N_MEM * KVH_ROWS, LANE))

    cos_p, sin_p = _rope_tables(jnp.tile(jnp.arange(T, dtype=jnp.int32), B))
    cos_s, sin_s = _rope_tables(jnp.full((Bs,), past_len, jnp.int32))
    row3 = lambda a: a.astype(F32).reshape(Bs, 1, a.shape[-1])

    xp = _ffn(x_prompt.reshape(B * T, D), ffg, wg, wu, wd, 0, 0, tm=1024)
    xs = _ffn(x_sample.reshape(Bs, D), ffg, wg, wu, wd, 0, 0, tm=Bs)
    nsa_gains = _pad_gains(nsa_q_g, nsa_ks_g, nsa_kw_g, mem_q_g[0])

    proj_p = _norm_matmul(xp, mix_norm[0], nsa_w, tm=1024, tn=1024)
    qn_p, qr_p, cmp_p, rows_p, win_p, kvb_p, mq_p = _nsa_post(proj_p, cos_p, sin_p, nsa_gains, tm=256)
    cmp_kv = _cmp_prompt(cmp_p.reshape(B, T, 768), w1r, cmp_b1, cmp_w2, kc_g)
    nc_p = T // CMP_STRIDE
    cover_p = _cover(nc_p - 1, T // SEL_BLOCK, nc_p, LANE)
    o_mix_p = _nsa_attn(qn_p.reshape(B, T, 1536), qr_p.reshape(B, T, 1536), proj_p.reshape(B, T, PROJ_N),
                        cmp_kv, kvb_p.reshape(B, T, 1536), cover_p)
    o_mem_p = _mem_attn(mq_p.reshape(B, T, 512), mem_rows_p[0], tq=512)
    xp = _out_proj(xp, o_mix_p.reshape(B * T, 1536), o_mem_p.reshape(B * T, 512), nsa_wo, tm=1024)

    proj_s = _norm_matmul(xs, mix_norm[0], nsa_w, tm=Bs, tn=1024)
    qn_s, qr_s, _, rows_s, win_s, kvb_s, mq_s = _nsa_post(proj_s, cos_s, sin_s, nsa_gains, tm=Bs)
    nc_s = past_len // CMP_STRIDE
    ns_s = -(-(past_len + 1) // SEL_BLOCK)
    cover_s = _cover(nc_s - 1, ns_s, nc_s, -(-ns_s // LANE) * LANE)
    sel_idx, ocmp_s = _dec_select(cache_rows, page_table, row3(qn_s), w1r, cmp_b1, cmp_w2, kc_g, cover_s)
    o_mix_s = _dec_attn(cache_rows, page_table, sel_idx[:, :G, :SEL_TOPK].reshape(-1), row3(qr_s), row3(kvb_s),
                        win_rows, proj_s.reshape(Bs, 1, PROJ_N), ocmp_s)
    o_mem_s = _mem_attn(row3(mq_s), mem_rows_s[0], tq=1)
    xs = _out_proj(xs, o_mix_s.reshape(Bs, 1536).astype(BF16), o_mem_s.reshape(Bs, 512).astype(BF16), nsa_wo,
                   tm=Bs)

    xp = _ffn(xp, ffg, wg, wu, wd, 0, 1, tm=1024)
    xs = _ffn(xs, ffg, wg, wu, wd, 0, 1, tm=Bs)

    xp = _ffn(xp, ffg, wg, wu, wd, 1, 0, tm=1024)
    xs = _ffn(xs, ffg, wg, wu, wd, 1, 0, tm=Bs)
    dil_gains = _pad_gains(dil_q_g, dil_k_g, mem_q_g[1])

    dproj_p = _norm_matmul(xp, mix_norm[1], dil_w, tm=1024, tn=1024)
    dq_p, dkb_p, dvb_p, dmq_p, *dnew_p = _dil_post(dproj_p, cos_p, sin_p, dil_gains, tm=256)
    band = [_dil_band(dq_p.reshape(B, T, 1536), dkb_p.reshape(B, T, 1536), dvb_p.reshape(B, T, 1536), g, dil)
            for g, (_, dil) in enumerate(DIL_PAIRS)]
    o_dil_p = _dil_mix([o for o, _ in band], [s for _, s in band])
    o_dmem_p = _mem_attn(dmq_p.reshape(B, T, 512), mem_rows_p[1], tq=512)
    xp = _out_proj(xp, o_dil_p, o_dmem_p.reshape(B * T, 512), dil_wo, tm=1024)

    dproj_s = _norm_matmul(xs, mix_norm[1], dil_w, tm=Bs, tn=1024)
    dq_s, _, _, dmq_s, *dnew_s = _dil_post(dproj_s, cos_s, sin_s, dil_gains, tm=Bs)
    o_dil_s, *dil_rows_out = _dec_dil(row3(dq_s), [s.reshape(Bs, KVH_ROWS, LANE) for s in dnew_s], dil_rows)
    o_dmem_s = _mem_attn(row3(dmq_s), mem_rows_s[1], tq=1)
    xs = _out_proj(xs, o_dil_s.reshape(Bs, 512).astype(BF16), o_dmem_s.reshape(Bs, 512).astype(BF16), dil_wo,
                   tm=Bs)

    xp = _ffn(xp, ffg, wg, wu, wd, 1, 1, tm=1024)
    xs = _ffn(xs, ffg, wg, wu, wd, 1, 1, tm=Bs)

    unrow = lambda a, n, outer, inner: a.reshape(n, -1, outer, inner, d).transpose(0, 1, 3, 2, 4)
    nsa_kv_p = unrow(rows_p, B, G, 4)
    nsa_kv_s = unrow(rows_s, Bs, G, 4)
    nsa_win_p = unrow(win_p, B, G, 2)[:, -min(NSA_WINDOW, T):]
    nsa_win_s = jnp.concatenate([state_nsa_win, unrow(win_s, Bs, G, 2)], axis=1)[:, -state_nsa_win.shape[1]:]
    outs_dil = []
    for g, (window, _) in enumerate(DIL_PAIRS):
        st = dil_states[g]
        outs_dil.append(dnew_p[g].reshape(B, T, 2, DIL_HEADS, d)[:, -min(window, T):])
        outs_dil.append(dil_rows_out[g].reshape(st.shape))
    mem_kv_out = jnp.stack([kv.reshape(B, N_MEM, 2, N_MEM_HEADS, d) for kv in mem_rows_p], axis=0)
    return (xp.reshape(B, T, D), xs.reshape(Bs, 1, D), nsa_kv_p, nsa_kv_s, nsa_win_p, nsa_win_s,
            *outs_dil, mem_kv_out)
```

```python
import functools

import jax
import jax.numpy as jnp
from jax import lax
from jax.experimental import pallas as pl
from jax.experimental.pallas import tpu as pltpu

F32 = jnp.float32
BF16 = jnp.bfloat16

D_MODEL = 2048
HEAD_DIM = 128
N_MIX_HEADS = 12
N_MEM_HEADS = 4
N_MEM = 256
NSA_KV_HEADS = 3
NSA_GROUP = 4
CMP_BLOCK = 32
CMP_STRIDE = 16
SEL_BLOCK = 64
SEL_TOPK = 16
NSA_WINDOW = 512
DIL_PAIRS = ((128, 1), (512, 4), (2048, 16))
DIL_HEADS = 4
PAGE_SIZE = 128
ROPE_THETA = 10000.0
EPS = 1e-6
SCALE = HEAD_DIM ** -0.5
NEG = -1e30
FORCE_SCORE = 1e6

PROJ_N = 5120
NSA_GATE_BLK = 34
LANE = 128
VMEM_LIMIT = 56 * 1024 * 1024


def _params(sem):
    return pltpu.CompilerParams(dimension_semantics=sem, vmem_limit_bytes=VMEM_LIMIT)


def _dot(a, b):
    return jnp.dot(a, b, preferred_element_type=F32)


def _dot_t(a, b):
    return lax.dot_general(a, b, (((1,), (1,)), ((), ())), preferred_element_type=F32)


def _dot3(a, b):
    a1 = a.astype(BF16)
    r1 = a - a1.astype(F32)
    a2 = r1.astype(BF16)
    a3 = (r1 - a2.astype(F32)).astype(BF16)
    return _dot(a1, b) + _dot(a2, b) + _dot(a3, b)


def _rms(x, g):
    return x * lax.rsqrt(jnp.mean(x * x, axis=-1, keepdims=True) + EPS) * g


def _rope(x, cos, sin):
    return x * cos + pltpu.roll(x, HEAD_DIM // 2, 1) * sin


def _rows16(row, nrep):
    rid = lax.broadcasted_iota(jnp.int32, (16, LANE), 0) & (nrep - 1)
    out = jnp.zeros((16, LANE), F32)
    for r in range(nrep):
        piece = jnp.broadcast_to(row[:, r * LANE:(r + 1) * LANE], (16, LANE))
        out = jnp.where(rid == r, piece, out)
    return out


def _softmax_masked(s, mask):
    s = jnp.where(mask, s, NEG)
    m = jnp.max(s, axis=-1, keepdims=True)
    e = jnp.where(mask, jnp.exp(s - m), 0.0)
    l = jnp.sum(e, axis=-1, keepdims=True)
    return e / jnp.maximum(l, 1e-30), m, l


def _ffn_step(f, nf, x_ref, g_ref, wg, wu, wd, o_ref, h_ref):
    @pl.when(f == 0)
    def _():
        h_ref[...] = _rms(x_ref[...], g_ref[...]).astype(BF16)
        o_ref[...] = jnp.zeros_like(o_ref)

    h = h_ref[...]
    gate = _dot(h, wg)
    up = _dot(h, wu)
    a = (gate * jax.nn.sigmoid(gate) * up).astype(BF16)
    o_ref[...] += _dot(a, wd)

    @pl.when(f == nf - 1)
    def _():
        o_ref[...] = x_ref[...] + 0.5 * o_ref[...]


def _ffn_kernel(x_ref, g_ref, wg_ref, wu_ref, wd_ref, o_ref, h_ref, *, nf):
    _ffn_step(pl.program_id(1), nf, x_ref, g_ref, wg_ref[...], wu_ref[...], wd_ref[...], o_ref, h_ref)


def _ffn(x, g, wg, wu, wd, li, lj, tm, tf=512):
    M, D = x.shape
    F = wg.shape[-1]
    nf = F // tf
    return pl.pallas_call(
        functools.partial(_ffn_kernel, nf=nf),
        grid=(M // tm, nf),
        in_specs=[
            pl.BlockSpec((tm, D), lambda i, f: (i, 0), pipeline_mode=pl.Buffered(1)),
            pl.BlockSpec((None, None, 1, D), lambda i, f: (li, lj, 0, 0)),
            pl.BlockSpec((D, tf), lambda i, f: (0, f)),
            pl.BlockSpec((D, tf), lambda i, f: (0, f)),
            pl.BlockSpec((tf, D), lambda i, f: (f, 0)),
        ],
        out_specs=pl.BlockSpec((tm, D), lambda i, f: (i, 0)),
        out_shape=jax.ShapeDtypeStruct((M, D), F32),
        scratch_shapes=[pltpu.VMEM((tm, D), BF16)],
        compiler_params=_params(("parallel", "arbitrary")),
        name="ffn",
    )(x, g, wg, wu, wd)


def _ffn_cast_kernel(x_ref, g_ref, wg_ref, wu_ref, wd_ref, o_ref, wgb_ref, wub_ref, wdb_ref, h_ref, *, nf):
    wgb_ref[...] = wg_ref[...].astype(BF16)
    wub_ref[...] = wu_ref[...].astype(BF16)
    wdb_ref[...] = wd_ref[...].astype(BF16)
    _ffn_step(pl.program_id(0), nf, x_ref, g_ref, wgb_ref[...], wub_ref[...], wdb_ref[...], o_ref, h_ref)


def _ffn_cast(x, g, wg, wu, wd, li, lj, tf=512):
    M, D = x.shape
    F = wg.shape[-1]
    nf = F // tf
    return pl.pallas_call(
        functools.partial(_ffn_cast_kernel, nf=nf),
        grid=(nf,),
        in_specs=[
            pl.BlockSpec((M, D), lambda f: (0, 0)),
            pl.BlockSpec((None, None, 1, D), lambda f: (li, lj, 0, 0)),
            pl.BlockSpec((None, None, D, tf), lambda f: (li, lj, 0, f)),
            pl.BlockSpec((None, None, D, tf), lambda f: (li, lj, 0, f)),
            pl.BlockSpec((None, None, tf, D), lambda f: (li, lj, f, 0)),
        ],
        out_specs=[pl.BlockSpec((M, D), lambda f: (0, 0)),
                   pl.BlockSpec((D, tf), lambda f: (0, f)),
                   pl.BlockSpec((D, tf), lambda f: (0, f)),
                   pl.BlockSpec((tf, D), lambda f: (f, 0))],
        out_shape=[jax.ShapeDtypeStruct((M, D), F32),
                   jax.ShapeDtypeStruct((D, F), BF16),
                   jax.ShapeDtypeStruct((D, F), BF16),
                   jax.ShapeDtypeStruct((F, D), BF16)],
        scratch_shapes=[pltpu.VMEM((M, D), BF16)],
        compiler_params=_params(("arbitrary",)),
        name="ffn_cast",
    )(x, g, wg, wu, wd)


def _nmm_kernel(x_ref, g_ref, w_ref, o_ref, h_ref):
    @pl.when(pl.program_id(1) == 0)
    def _():
        h_ref[...] = _rms(x_ref[...], g_ref[...]).astype(BF16)

    o_ref[...] = _dot(h_ref[...], w_ref[...])


def _norm_matmul(x, g, w, tm, tn):
    M, D = x.shape
    N = w.shape[1]
    return pl.pallas_call(
        _nmm_kernel,
        grid=(M // tm, N // tn),
        in_specs=[
            pl.BlockSpec((tm, D), lambda i, j: (i, 0)),
            pl.BlockSpec((1, D), lambda i, j: (0, 0)),
            pl.BlockSpec((D, tn), lambda i, j: (0, j)),
        ],
        out_specs=pl.BlockSpec((tm, tn), lambda i, j: (i, j)),
        out_shape=jax.ShapeDtypeStruct((M, N), F32),
        scratch_shapes=[pltpu.VMEM((tm, D), BF16)],
        compiler_params=_params(("parallel", "arbitrary")),
        name="norm_matmul",
    )(x, g.reshape(1, D), w)


def _oproj_kernel(x_ref, a_ref, b_ref, w_ref, o_ref, *, ka):
    o_ref[...] = x_ref[...] + _dot(a_ref[...], w_ref[:ka, :]) + _dot(b_ref[...], w_ref[ka:, :])


def _out_proj(x, a, b, w, tm, tn=1024):
    M, D = x.shape
    ka, kb = a.shape[1], b.shape[1]
    return pl.pallas_call(
        functools.partial(_oproj_kernel, ka=ka),
        grid=(M // tm, D // tn),
        in_specs=[
            pl.BlockSpec((tm, tn), lambda i, j: (i, j)),
            pl.BlockSpec((tm, ka), lambda i, j: (i, 0)),
            pl.BlockSpec((tm, kb), lambda i, j: (i, 0)),
            pl.BlockSpec((ka + kb, tn), lambda i, j: (0, j)),
        ],
        out_specs=pl.BlockSpec((tm, tn), lambda i, j: (i, j)),
        out_shape=jax.ShapeDtypeStruct((M, D), F32),
        compiler_params=_params(("parallel", "parallel")),
        name="out_proj",
    )(x, a, b, w)


def _put_rows(ref, row, rows_per_token, val):
    ref[pl.ds(row, val.shape[0], stride=rows_per_token), :] = val


def _get_rows(ref, row, rows_per_token, n):
    return ref[pl.ds(row, n, stride=rows_per_token), :]


NSA_ROWS = 4 * NSA_KV_HEADS
WIN_ROWS = 2 * NSA_KV_HEADS
KVH_ROWS = 2 * DIL_HEADS


def _nsa_post_kernel(p_ref, cos_ref, sin_ref, g_ref, qn_ref, qr_ref, cmp_ref, rows_ref, win_ref, kvb_ref, mq_ref):
    cos, sin = cos_ref[...], sin_ref[...]
    q_g, ks_g, kw_g, mq_g = g_ref[0:1, :], g_ref[1:2, :], g_ref[2:3, :], g_ref[3:4, :]

    def tile(i):
        return p_ref[:, i * LANE:(i + 1) * LANE]

    for h in range(N_MIX_HEADS):
        qn = _rms(tile(h), q_g)
        qn_ref[:, h * LANE:(h + 1) * LANE] = qn.astype(BF16)
        qr_ref[:, h * LANE:(h + 1) * LANE] = _rope(qn, cos, sin).astype(BF16)
    for g in range(NSA_KV_HEADS):
        kc, vc = tile(12 + g), tile(15 + g)
        ks = _rope(_rms(tile(18 + g), ks_g), cos, sin)
        vs = tile(21 + g)
        kw = _rope(_rms(tile(24 + g), kw_g), cos, sin)
        vw = tile(27 + g)
        for c, val in enumerate((kc, vc)):
            cmp_ref[:, (c * 3 + g) * LANE:(c * 3 + g + 1) * LANE] = val
        for c, val in enumerate((kc, vc, ks, vs)):
            _put_rows(rows_ref, g * 4 + c, NSA_ROWS, val)
        for c, val in enumerate((kw, vw)):
            _put_rows(win_ref, g * 2 + c, WIN_ROWS, val)
        for c, val in enumerate((ks, vs, kw, vw)):
            kvb_ref[:, (c * 3 + g) * LANE:(c * 3 + g + 1) * LANE] = val.astype(BF16)
    for h in range(N_MEM_HEADS):
        mq_ref[:, h * LANE:(h + 1) * LANE] = _rms(tile(30 + h), mq_g).astype(BF16)


def _nsa_post(p, cos, sin, gains, tm):
    M = p.shape[0]
    row = lambda n: pl.BlockSpec((tm, n), lambda i: (i, 0))
    flat = lambda r: pl.BlockSpec((tm * r, LANE), lambda i: (i, 0))
    return pl.pallas_call(
        _nsa_post_kernel,
        grid=(M // tm,),
        in_specs=[row(PROJ_N), row(LANE), row(LANE), pl.BlockSpec((8, LANE), lambda i: (0, 0))],
        out_specs=[row(1536), row(1536), row(768), flat(NSA_ROWS), flat(WIN_ROWS), row(1536), row(512)],
        out_shape=[
            jax.ShapeDtypeStruct((M, 1536), BF16),
            jax.ShapeDtypeStruct((M, 1536), BF16),
            jax.ShapeDtypeStruct((M, 768), F32),
            jax.ShapeDtypeStruct((M * NSA_ROWS, LANE), F32),
            jax.ShapeDtypeStruct((M * WIN_ROWS, LANE), F32),
            jax.ShapeDtypeStruct((M, 1536), BF16),
            jax.ShapeDtypeStruct((M, 512), BF16),
        ],
        compiler_params=_params(("parallel",)),
        name="nsa_post",
    )(p, cos, sin, gains)


def _dil_post_kernel(p_ref, cos_ref, sin_ref, g_ref, qr_ref, kb_ref, vb_ref, mq_ref, st0_ref, st1_ref, st2_ref):
    cos, sin = cos_ref[...], sin_ref[...]
    q_g, k_g, mq_g = g_ref[0:1, :], g_ref[1:2, :], g_ref[2:3, :]
    st_refs = (st0_ref, st1_ref, st2_ref)
    for h in range(N_MIX_HEADS):
        sl = slice(h * LANE, (h + 1) * LANE)
        qr_ref[:, sl] = _rope(_rms(p_ref[:, sl], q_g), cos, sin).astype(BF16)
        k = _rope(_rms(p_ref[:, (12 + h) * LANE:(13 + h) * LANE], k_g), cos, sin)
        v = p_ref[:, (24 + h) * LANE:(25 + h) * LANE]
        kb_ref[:, sl] = k.astype(BF16)
        vb_ref[:, sl] = v.astype(BF16)
        st = st_refs[h // DIL_HEADS]
        _put_rows(st, h % DIL_HEADS, KVH_ROWS, k)
        _put_rows(st, DIL_HEADS + h % DIL_HEADS, KVH_ROWS, v)
    for h in range(N_MEM_HEADS):
        mq_ref[:, h * LANE:(h + 1) * LANE] = _rms(p_ref[:, (36 + h) * LANE:(37 + h) * LANE], mq_g).astype(BF16)


def _dil_post(p, cos, sin, gains, tm):
    M = p.shape[0]
    row = lambda n: pl.BlockSpec((tm, n), lambda i: (i, 0))
    flat = pl.BlockSpec((tm * KVH_ROWS, LANE), lambda i: (i, 0))
    st_shape = jax.ShapeDtypeStruct((M * KVH_ROWS, LANE), F32)
    return pl.pallas_call(
        _dil_post_kernel,
        grid=(M // tm,),
        in_specs=[row(PROJ_N), row(LANE), row(LANE), pl.BlockSpec((8, LANE), lambda i: (0, 0))],
        out_specs=[row(1536), row(1536), row(1536), row(512), flat, flat, flat],
        out_shape=[
            jax.ShapeDtypeStruct((M, 1536), BF16),
            jax.ShapeDtypeStruct((M, 1536), BF16),
            jax.ShapeDtypeStruct((M, 1536), BF16),
            jax.ShapeDtypeStruct((M, 512), BF16),
            st_shape, st_shape, st_shape,
        ],
        compiler_params=_params(("parallel",)),
        name="dil_post",
    )(p, cos, sin, gains)


def _memkv_post_kernel(x_ref, g_ref, o_ref):
    for h in range(N_MEM_HEADS):
        _put_rows(o_ref, h, KVH_ROWS, _rms(x_ref[:, h * LANE:(h + 1) * LANE], g_ref[...]))
        _put_rows(o_ref, N_MEM_HEADS + h, KVH_ROWS, x_ref[:, (N_MEM_HEADS + h) * LANE:(N_MEM_HEADS + h + 1) * LANE])


def _memkv_post(x, g, tm=256):
    M, N = x.shape
    return pl.pallas_call(
        _memkv_post_kernel,
        grid=(M // tm,),
        in_specs=[pl.BlockSpec((tm, N), lambda i: (i, 0)), pl.BlockSpec((1, LANE), lambda i: (0, 0))],
        out_specs=pl.BlockSpec((tm * KVH_ROWS, LANE), lambda i: (i, 0)),
        out_shape=jax.ShapeDtypeStruct((M * KVH_ROWS, LANE), F32),
        compiler_params=_params(("parallel",)),
        name="memkv_post",
    )(x, g.reshape(1, LANE))


def _mem_attn_kernel(q_ref, kv_ref, o_ref, *, tq):
    for h in range(N_MEM_HEADS):
        sl = slice(h * LANE, (h + 1) * LANE)
        if tq == 1:
            q = _rows16(q_ref[:, sl], 1).astype(BF16)
        else:
            q = q_ref[:, sl]
        k = _get_rows(kv_ref, h, KVH_ROWS, N_MEM).astype(BF16)
        v = _get_rows(kv_ref, N_MEM_HEADS + h, KVH_ROWS, N_MEM).astype(BF16)
        s = _dot_t(q, k) * SCALE
        m = jnp.max(s, axis=-1, keepdims=True)
        e = jnp.exp(s - m)
        p = e / jnp.sum(e, axis=-1, keepdims=True)
        o = _dot(p.astype(BF16), v)
        o_ref[:, sl] = o[0:tq, :].astype(o_ref.dtype)


def _mem_attn(q, kv, tq):
    B, T, _ = q.shape
    return pl.pallas_call(
        functools.partial(_mem_attn_kernel, tq=tq),
        grid=(B, T // tq),
        in_specs=[
            pl.BlockSpec((None, tq, 512), lambda b, i: (b, i, 0)),
            pl.BlockSpec((None, N_MEM * KVH_ROWS, LANE), lambda b, i: (b, 0, 0)),
        ],
        out_specs=pl.BlockSpec((None, tq, 512), lambda b, i: (b, i, 0)),
        out_shape=jax.ShapeDtypeStruct((B, T, 512), q.dtype),
        compiler_params=_params(("parallel", "parallel")),
        name="mem_attn",
    )(q, kv)


def _gelu_tanh(x):
    return 0.5 * x * (1.0 + jnp.tanh(0.7978845608028654 * (x + 0.044715 * (x * x * x))))


def _compress_finish(h, b1, w2):
    n = h.shape[0]
    hid = b1 + h[:, :LANE] + pltpu.roll(h[:, LANE:], n - 1, 0)
    return _dot(_gelu_tanh(hid).astype(BF16), w2)


def _compress(x_bf, w1, b1, w2):
    return _compress_finish(_dot(x_bf, w1), b1, w2)


def _cmp_prompt_kernel(x_ref, w1_ref, b1_ref, w2_ref, kcg_ref, o_ref, xs_ref, *, n):
    kv = pl.program_id(1)
    for c in range(CMP_STRIDE):
        xs_ref[:, c * LANE:(c + 1) * LANE] = x_ref[pl.ds(c, n, stride=CMP_STRIDE), :].astype(BF16)
    out = _compress(xs_ref[...], w1_ref[...], b1_ref[...], w2_ref[...])
    out = jnp.where(kv == 0, _rms(out, kcg_ref[...]), out)
    rid = lax.broadcasted_iota(jnp.int32, out.shape, 0)
    o_ref[...] = jnp.where(rid < n - 1, out, 0.0).astype(BF16)


def _cmp_prompt(rows, w1r, b1, w2, kc_g):
    B, T, _ = rows.shape
    n = T // CMP_STRIDE
    return pl.pallas_call(
        functools.partial(_cmp_prompt_kernel, n=n),
        grid=(B, 2, NSA_KV_HEADS),
        in_specs=[
            pl.BlockSpec((None, T, LANE), lambda b, kv, g: (b, 0, kv * 3 + g)),
            pl.BlockSpec((None, CMP_STRIDE * LANE, 2 * LANE), lambda b, kv, g: (kv, 0, 0)),
            pl.BlockSpec((None, 1, LANE), lambda b, kv, g: (kv, 0, 0)),
            pl.BlockSpec((None, LANE, LANE), lambda b, kv, g: (kv, 0, 0)),
            pl.BlockSpec((1, LANE), lambda b, kv, g: (0, 0)),
        ],
        out_specs=pl.BlockSpec((None, None, None, n, LANE), lambda b, kv, g: (b, kv, g, 0, 0)),
        out_shape=jax.ShapeDtypeStruct((B, 2, NSA_KV_HEADS, n, LANE), BF16),
        scratch_shapes=[pltpu.VMEM((n, CMP_STRIDE * LANE), BF16)],
        compiler_params=_params(("parallel", "parallel", "parallel")),
        name="cmp_prompt",
    )(rows, w1r, b1, w2, kc_g)


def _select_blocks(score, cur, n_blocks):
    tq = score.shape[0]
    blk = lax.broadcasted_iota(jnp.int32, score.shape, 1)
    forced = (blk == 0) | (blk == cur) | (blk == cur - 1)
    sc = jnp.where(blk <= cur, jnp.where(forced, FORCE_SCORE, score), NEG)
    sct = sc.T[0:n_blocks, :]
    bi = lax.broadcasted_iota(jnp.int32, sct.shape, 0)
    rank = jnp.zeros(sct.shape, F32)
    for i in range(n_blocks):
        si = sct[i:i + 1, :]
        ahead = (si > sct) | ((si == sct) & (bi > i))
        rank = rank + jnp.where(ahead, 1.0, 0.0)
    chosen = jnp.where((rank < SEL_TOPK) & (sct > 0.5 * NEG), 1.0, 0.0)
    return jnp.concatenate([chosen, jnp.zeros((LANE - n_blocks, tq), F32)], axis=0).T


def _score_tile(q, k_ref, kt, slot, bias, s_ref, m_ref, tk, first=False):
    s = _dot_t(q, k_ref[pl.ds(pl.multiple_of(kt * tk, tk), tk), :]) * SCALE
    if bias is not None:
        s = s + bias
    s_ref[slot] = s
    m = functools.reduce(jnp.maximum, [s[:, c * LANE:(c + 1) * LANE] for c in range(tk // LANE)])
    m_ref[...] = m if first else jnp.maximum(m_ref[...], m)


def _value_tile(v_ref, kt, slot, s_ref, m_ref, l_ref, acc_ref, tk, first=False):
    m = m_ref[...]
    es = [jnp.exp(s_ref[slot, :, c * LANE:(c + 1) * LANE] - m) for c in range(tk // LANE)]
    pv = _dot(jnp.concatenate(es, axis=1).astype(BF16), v_ref[pl.ds(pl.multiple_of(kt * tk, tk), tk), :])
    l = functools.reduce(lambda a, b: a + b, es)
    if first:
        l_ref[...] = l
        acc_ref[...] = pv
    else:
        l_ref[...] += l
        acc_ref[...] += pv


def _nsa_attn_kernel(qn_ref, qr_ref, gl_ref, kc_ref, vc_ref, ks_ref, vs_ref, kw_ref, vw_ref, cover_ref,
                     o_ref, s_ref, m_ref, l_ref, acc_ref, sel_ref, *, tq, ns):
    qi = pl.program_id(2)
    R = NSA_GROUP
    t0 = qi * tq
    stack = lambda ref: jnp.concatenate([ref[:, r * LANE:(r + 1) * LANE] for r in range(R)], axis=0)
    rows4 = lambda x: jnp.concatenate([x] * R, axis=0)
    tpos_q = t0 + lax.broadcasted_iota(jnp.int32, (tq, 1), 0)
    row_in = lax.broadcasted_iota(jnp.int32, (R * tq, 1), 0) & (tq - 1)
    col = lax.broadcasted_iota(jnp.int32, (R * tq, tq), 1)

    s = _dot_t(stack(qn_ref), kc_ref[...]) * SCALE
    cblk = lax.broadcasted_iota(jnp.int32, (R * tq, LANE), 1)
    cmask = (CMP_STRIDE * cblk + (CMP_BLOCK - 1) <= t0 + row_in) & (cblk < kc_ref.shape[0] - 1)
    p, _, _ = _softmax_masked(s, cmask)
    o_cmp = _dot(p.astype(BF16), vc_ref[...])
    imp = p[0:tq] + p[tq:2 * tq] + p[2 * tq:3 * tq] + p[3 * tq:4 * tq]
    score = _dot3(imp, cover_ref[...])
    cur = lax.shift_right_arithmetic(tpos_q, SEL_BLOCK.bit_length() - 1)

    @pl.when(t0 + tq <= SEL_TOPK * SEL_BLOCK)
    def _():
        sel_ref[...] = jnp.where(lax.broadcasted_iota(jnp.int32, (tq, LANE), 1) <= cur, 1.0, 0.0).astype(BF16)

    @pl.when(t0 + tq > SEL_TOPK * SEL_BLOCK)
    def _():
        sel_ref[...] = _select_blocks(score, cur, ns).astype(BF16)

    sel = sel_ref[...]
    q_rot = stack(qr_ref)
    blocks_per_tile = tq // SEL_BLOCK
    causal = jnp.where(col <= row_in, 0.0, NEG)
    far = jnp.where(col >= row_in, 0.0, NEG)

    def row_max():
        m_ref[...] = jnp.broadcast_to(jnp.max(m_ref[...], axis=-1, keepdims=True), m_ref.shape)

    def result():
        return acc_ref[...] / jnp.maximum(jnp.sum(l_ref[...], axis=-1, keepdims=True), 1e-30)

    def member_bias(kt):
        key_blk = lax.shift_right_arithmetic(lax.broadcasted_iota(jnp.int32, (LANE, tq), 1),
                                             SEL_BLOCK.bit_length() - 1)
        expand = lax.broadcasted_iota(jnp.int32, (LANE, tq), 0) == kt * blocks_per_tile + key_blk
        member = _dot(sel, jnp.where(expand, 1.0, 0.0).astype(BF16))
        return rows4((member - 1.0) * (-NEG))

    _score_tile(q_rot, ks_ref, qi, qi, member_bias(qi) + causal, s_ref, m_ref, tq, first=True)

    def sel_scores(kt, carry):
        _score_tile(q_rot, ks_ref, kt, kt, member_bias(kt), s_ref, m_ref, tq)
        return carry

    lax.fori_loop(0, qi, sel_scores, 0)
    row_max()
    _value_tile(vs_ref, qi, qi, s_ref, m_ref, l_ref, acc_ref, tq, first=True)

    def sel_values(kt, carry):
        _value_tile(vs_ref, kt, kt, s_ref, m_ref, l_ref, acc_ref, tq)
        return carry

    lax.fori_loop(0, qi, sel_values, 0)
    o_sel = result()

    n_back = NSA_WINDOW // tq
    for back in range(n_back + 1):
        bias = jnp.where(qi >= back, 0.0, NEG)
        if back == 0:
            bias = causal
        elif back == n_back:
            bias = far + bias
        _score_tile(q_rot, kw_ref, jnp.maximum(qi - back, 0), back, bias, s_ref, m_ref, tq, first=back == 0)
    row_max()
    for back in range(n_back + 1):
        _value_tile(vw_ref, jnp.maximum(qi - back, 0), back, s_ref, m_ref, l_ref, acc_ref, tq, first=back == 0)
    o_win = result()

    gates = jax.nn.sigmoid(gl_ref[...])
    for r in range(R):
        rs = slice(r * tq, (r + 1) * tq)
        o = (gates[:, 3 * r:3 * r + 1] * o_cmp[rs] + gates[:, 3 * r + 1:3 * r + 2] * o_sel[rs]
             + gates[:, 3 * r + 2:3 * r + 3] * o_win[rs])
        o_ref[:, r * LANE:(r + 1) * LANE] = o.astype(BF16)


def _nsa_attn(qn, qr, proj, cmp_kv, kvb, cover, tq=256):
    B, T, _ = qn.shape
    nc = cmp_kv.shape[3]
    G = NSA_KV_HEADS
    assert tq % LANE == 0 and NSA_WINDOW % tq == 0 and nc == LANE and T // SEL_BLOCK <= LANE
    rows = NSA_GROUP * tq
    qspec = pl.BlockSpec((None, tq, 4 * LANE), lambda b, g, i: (b, i, g))
    kvspec = lambda c: pl.BlockSpec((None, T, LANE), lambda b, g, i: (b, 0, c * 3 + g))
    return pl.pallas_call(
        functools.partial(_nsa_attn_kernel, tq=tq, ns=T // SEL_BLOCK),
        grid=(B, G, T // tq),
        in_specs=[
            qspec, qspec,
            pl.BlockSpec((None, tq, LANE), lambda b, g, i: (b, i, NSA_GATE_BLK + g)),
            pl.BlockSpec((None, None, None, nc, LANE), lambda b, g, i: (b, 0, g, 0, 0)),
            pl.BlockSpec((None, None, None, nc, LANE), lambda b, g, i: (b, 1, g, 0, 0)),
            kvspec(0), kvspec(1), kvspec(2), kvspec(3),
            pl.BlockSpec((nc, LANE), lambda b, g, i: (0, 0)),
        ],
        out_specs=qspec,
        out_shape=jax.ShapeDtypeStruct((B, T, 1536), BF16),
        scratch_shapes=[pltpu.VMEM((T // tq, rows, tq), F32),
                        pltpu.VMEM((rows, LANE), F32),
                        pltpu.VMEM((rows, LANE), F32),
                        pltpu.VMEM((rows, LANE), F32),
                        pltpu.VMEM((tq, LANE), BF16)],
        compiler_params=_params(("parallel", "parallel", "arbitrary")),
        name="nsa_attn",
    )(qn, qr, proj, cmp_kv, cmp_kv, kvb, kvb, kvb, kvb, cover)


def _dil_band_kernel(q_ref, kp_ref, kc_ref, vp_ref, vc_ref, o_ref, st_ref, *, tq, window):
    i = pl.program_id(2)
    qpos = i * tq + lax.broadcasted_iota(jnp.int32, (tq, 2 * tq), 0)
    kpos = (i - 1) * tq + lax.broadcasted_iota(jnp.int32, (tq, 2 * tq), 1)
    diff = qpos - kpos
    mask = (diff >= 0) & (diff <= window) & (kpos >= 0)
    lane = lax.broadcasted_iota(jnp.int32, (tq, LANE), 1)
    stats = jnp.zeros((tq, LANE), F32)
    for h in range(DIL_HEADS):
        sl = slice(h * LANE, (h + 1) * LANE)
        k = jnp.concatenate([kp_ref[:, sl], kc_ref[:, sl]], axis=0)
        v = jnp.concatenate([vp_ref[:, sl], vc_ref[:, sl]], axis=0)
        p, m, l = _softmax_masked(_dot_t(q_ref[:, sl], k) * SCALE, mask)
        o_ref[:, sl] = _dot(p.astype(BF16), v)
        stats = jnp.where(lane == h, m, stats)
        stats = jnp.where(lane == DIL_HEADS + h, l, stats)
    st_ref[...] = stats


def _dil_band(q, k, v, g, dil, tq=128):
    B, T, C = q.shape
    S = T // dil
    nb = C // (4 * LANE)
    view = lambda a: a.reshape(B, S, dil * C)
    cur = pl.BlockSpec((None, tq, 4 * LANE), lambda b, r, i: (b, i, r * nb + g))
    prev = pl.BlockSpec((None, tq, 4 * LANE), lambda b, r, i: (b, jnp.maximum(i - 1, 0), r * nb + g))
    o, st = pl.pallas_call(
        functools.partial(_dil_band_kernel, tq=tq, window=DIL_PAIRS[g][0] // dil),
        grid=(B, dil, S // tq),
        in_specs=[cur, prev, cur, prev, cur],
        out_specs=[pl.BlockSpec((None, tq, 4 * LANE), lambda b, r, i: (b, i, r)),
                   pl.BlockSpec((None, tq, LANE), lambda b, r, i: (b, i, r))],
        out_shape=[jax.ShapeDtypeStruct((B, S, dil * 4 * LANE), F32),
                   jax.ShapeDtypeStruct((B, S, dil * LANE), F32)],
        compiler_params=_params(("parallel", "parallel", "parallel")),
        name=f"dil_band{g}",
    )(view(q), view(k), view(k), view(v), view(v))
    return o.reshape(B * T, 4 * LANE), st.reshape(B * T, LANE)


def _mix_groups(os_, ms, ls):
    m_all = jnp.maximum(jnp.maximum(ms[0], ms[1]), ms[2])
    ws = [jnp.exp(m - m_all) * l for m, l in zip(ms, ls)]
    tot = ws[0] + ws[1] + ws[2]
    return (ws[0] / tot) * os_[0] + (ws[1] / tot) * os_[1] + (ws[2] / tot) * os_[2]


def _dil_mix_kernel(o0_ref, o1_ref, o2_ref, s0_ref, s1_ref, s2_ref, o_ref):
    o_refs, s_refs = (o0_ref, o1_ref, o2_ref), (s0_ref, s1_ref, s2_ref)
    for h in range(DIL_HEADS):
        sl = slice(h * LANE, (h + 1) * LANE)
        ms = [s[:, h:h + 1] for s in s_refs]
        ls = [s[:, DIL_HEADS + h:DIL_HEADS + h + 1] for s in s_refs]
        o_ref[:, sl] = _mix_groups([o[:, sl] for o in o_refs], ms, ls).astype(BF16)


def _dil_mix(os_, sts, tm=512):
    M = os_[0].shape[0]
    ospec = pl.BlockSpec((tm, 4 * LANE), lambda i: (i, 0))
    sspec = pl.BlockSpec((tm, LANE), lambda i: (i, 0))
    return pl.pallas_call(
        _dil_mix_kernel,
        grid=(M // tm,),
        in_specs=[ospec] * 3 + [sspec] * 3,
        out_specs=ospec,
        out_shape=jax.ShapeDtypeStruct((M, 4 * LANE), BF16),
        compiler_params=_params(("parallel",)),
        name="dil_mix",
    )(*os_, *sts)


DEC_PAGES_PER_STEP = 16


def _dec_select_kernel(tbl_ref, *refs, n_pp, n_steps):
    pages = refs[:n_pp]
    (qn_ref, w1_ref, b1_ref, w2_ref, kcg_ref, cover_ref, idx_ref, ocmp_ref, h_ref) = refs[n_pp:]
    j = pl.program_id(1)
    cpp = PAGE_SIZE // CMP_STRIDE
    rows = n_pp * cpp
    n = n_steps * rows
    ns = cover_ref.shape[1]

    for g in range(NSA_KV_HEADS):
        for kv in range(2):
            x = jnp.concatenate(
                [_get_rows(pg, g * 4 + kv, NSA_ROWS, PAGE_SIZE).reshape(cpp, CMP_STRIDE * LANE) for pg in pages],
                axis=0)
            h_ref[g * 2 + kv, pl.ds(pl.multiple_of(j * rows, rows), rows), :] = _dot(x.astype(BF16), w1_ref[kv])

    @pl.when(j == n_steps - 1)
    def _():
        idx_ref[...] = jnp.zeros_like(idx_ref)
        for g in range(NSA_KV_HEADS):
            kc = _rms(_compress_finish(h_ref[g * 2], b1_ref[0], w2_ref[0]), kcg_ref[...]).astype(BF16)
            vc = _compress_finish(h_ref[g * 2 + 1], b1_ref[1], w2_ref[1]).astype(BF16)
            q = _rows16(qn_ref[:, g * NSA_GROUP * LANE:(g + 1) * NSA_GROUP * LANE], NSA_GROUP).astype(BF16)
            s = _dot_t(q, kc) * SCALE
            valid = lax.broadcasted_iota(jnp.int32, s.shape, 1) < n - 1
            p, _, _ = _softmax_masked(s, valid)
            ocmp_ref[g] = _dot(p.astype(BF16), vc)[0:NSA_GROUP, :]
            rid = lax.broadcasted_iota(jnp.int32, p.shape, 0)
            imp = jnp.sum(jnp.where(rid < NSA_GROUP, p, 0.0), axis=0, keepdims=True)
            score = _dot3(jnp.broadcast_to(imp, (8, n)), cover_ref[...])

            a = jnp.broadcast_to(score[0:1, :], (ns, ns))
            lane = lax.broadcasted_iota(jnp.int32, (ns, ns), 1)
            sub = lax.broadcasted_iota(jnp.int32, (ns, ns), 0)
            cur = n * CMP_STRIDE // SEL_BLOCK
            forced = (lane == 0) | (lane == cur) | (lane == cur - 1)
            a = jnp.where(lane <= cur, jnp.where(forced, FORCE_SCORE, a), NEG)
            at = a.T
            ahead_r = (at > a) | ((at == a) & (sub < lane))
            chosen_r = ((jnp.sum(jnp.where(ahead_r, 1.0, 0.0), axis=0, keepdims=True) < SEL_TOPK)
                        & (a[0:1, :] > 0.5 * NEG))
            ahead_c = (a > at) | ((a == at) & (lane < sub))
            chosen_c = ((jnp.sum(jnp.where(ahead_c, 1.0, 0.0), axis=1, keepdims=True) < SEL_TOPK)
                        & (at[:, 0:1] > 0.5 * NEG))
            before = jnp.sum(jnp.where(chosen_r & (lane < sub), 1.0, 0.0), axis=1, keepdims=True)
            slot = lax.broadcasted_iota(jnp.int32, (ns, LANE), 1)
            onehot = chosen_c & (before == slot.astype(F32))
            blk = lax.broadcasted_iota(jnp.int32, (ns, LANE), 0)
            picked = jnp.sum(jnp.where(onehot, blk.astype(F32), 0.0), axis=0, keepdims=True)
            filled = jnp.sum(jnp.where(onehot, 1.0, 0.0), axis=0, keepdims=True)
            idx_ref[g:g + 1, :] = jnp.where(filled > 0.5, picked, float(cur)).astype(jnp.int32)


def _dec_select(cache, table, qn, w1r, b1, w2, kc_g, cover):
    B, n_pages = table.shape
    n_pp = DEC_PAGES_PER_STEP
    n_steps = n_pages // n_pp
    n = n_pages * (PAGE_SIZE // CMP_STRIDE)
    ns = cover.shape[1]
    G = NSA_KV_HEADS
    const = lambda *shape: pl.BlockSpec(shape, lambda b, j, tbl: (0,) * len(shape))
    page_spec = lambda p: pl.BlockSpec((None, PAGE_SIZE * NSA_ROWS, LANE), lambda b, j, tbl: (tbl[b, j * n_pp + p], 0, 0))
    grid_spec = pltpu.PrefetchScalarGridSpec(
        num_scalar_prefetch=1,
        grid=(B, n_steps),
        in_specs=[page_spec(p) for p in range(n_pp)] + [
            pl.BlockSpec((None, 1, 1536), lambda b, j, tbl: (b, 0, 0)),
            const(2, CMP_STRIDE * LANE, 2 * LANE), const(2, 1, LANE), const(2, LANE, LANE), const(1, LANE),
            const(n, ns),
        ],
        out_specs=[pl.BlockSpec((None, 8, LANE), lambda b, j, tbl: (b, 0, 0)),
                   pl.BlockSpec((None, G, NSA_GROUP, LANE), lambda b, j, tbl: (b, 0, 0, 0))],
        scratch_shapes=[pltpu.VMEM((2 * G, n, 2 * LANE), F32)],
    )
    return pl.pallas_call(
        functools.partial(_dec_select_kernel, n_pp=n_pp, n_steps=n_steps),
        grid_spec=grid_spec,
        out_shape=[jax.ShapeDtypeStruct((B, 8, LANE), jnp.int32),
                   jax.ShapeDtypeStruct((B, G, NSA_GROUP, LANE), F32)],
        compiler_params=_params(("arbitrary", "arbitrary")),
        name="dec_select",
    )(table, *([cache] * n_pp), qn, w1r, b1, w2, kc_g, cover)


def _dec_attn_kernel(tbl_ref, idx_ref, *refs, n_sel, cur, wb):
    blocks = refs[:n_sel]
    (qr_ref, kvn_ref, win_ref, gl_ref, ocmp_ref, o_ref) = refs[n_sel:]
    b, g = pl.program_id(0), pl.program_id(1)
    R = NSA_GROUP
    q = _rows16(qr_ref[...], R).astype(BF16)
    qf = q.astype(F32)
    new = kvn_ref[...]
    ks_n, vs_n, kw_n, vw_n = (new[:, c * LANE:(c + 1) * LANE] for c in range(4))

    k = jnp.concatenate([_get_rows(r, g * 4 + 2, NSA_ROWS, SEL_BLOCK) for r in blocks], axis=0).astype(BF16)
    v = jnp.concatenate([_get_rows(r, g * 4 + 3, NSA_ROWS, SEL_BLOCK) for r in blocks], axis=0).astype(BF16)
    s = _dot_t(q, k) * SCALE
    blk_of = lax.shift_right_arithmetic(lax.broadcasted_iota(jnp.int32, s.shape, 1), SEL_BLOCK.bit_length() - 1)
    valid = jnp.zeros(s.shape, jnp.int32)
    for n in range(n_sel):
        is_past = jnp.where(idx_ref[(b * NSA_KV_HEADS + g) * n_sel + n] != cur, 1, 0)
        valid = jnp.where(blk_of == n, is_past, valid)
    valid = valid > 0
    s_new = jnp.sum(qf * ks_n, axis=-1, keepdims=True) * SCALE
    s = jnp.where(valid, s, NEG)
    m = jnp.maximum(jnp.max(s, axis=-1, keepdims=True), s_new)
    e = jnp.where(valid, jnp.exp(s - m), 0.0)
    e_new = jnp.exp(s_new - m)
    l = jnp.sum(e, axis=-1, keepdims=True) + e_new
    o_sel = (_dot(e.astype(BF16), v) + e_new.astype(BF16).astype(F32) * vs_n) / l

    s = _dot_t(q, _get_rows(win_ref, g * 2, WIN_ROWS, wb).astype(BF16)) * SCALE
    s_new = jnp.sum(qf * kw_n, axis=-1, keepdims=True) * SCALE
    m = jnp.maximum(jnp.max(s, axis=-1, keepdims=True), s_new)
    e = jnp.exp(s - m)
    e_new = jnp.exp(s_new - m)
    l = jnp.sum(e, axis=-1, keepdims=True) + e_new
    o_win = (_dot(e.astype(BF16), _get_rows(win_ref, g * 2 + 1, WIN_ROWS, wb).astype(BF16))
             + e_new.astype(BF16).astype(F32) * vw_n) / l

    gates = jax.nn.sigmoid(gl_ref[...])
    o_cmp = ocmp_ref[...]
    for r in range(R):
        o = (gates[:, 3 * r:3 * r + 1] * o_cmp[r:r + 1] + gates[:, 3 * r + 1:3 * r + 2] * o_sel[r:r + 1]
             + gates[:, 3 * r + 2:3 * r + 3] * o_win[r:r + 1])
        o_ref[:, r * LANE:(r + 1) * LANE] = o


def _dec_attn(cache, table, idx, qr, kvb, win_state, proj, ocmp):
    B, n_pages = table.shape
    n_sel = SEL_TOPK
    cur = n_pages * PAGE_SIZE // SEL_BLOCK
    wb = win_state.shape[1] // WIN_ROWS
    assert wb <= NSA_WINDOW
    halves = PAGE_SIZE // SEL_BLOCK

    def blk_spec(n):
        def imap(b, g, tbl, idx):
            i = jnp.minimum(idx[(b * NSA_KV_HEADS + g) * n_sel + n], cur - 1)
            return (tbl[b, i // halves], i % halves, 0)
        return pl.BlockSpec((None, SEL_BLOCK * NSA_ROWS, LANE), imap)

    grid_spec = pltpu.PrefetchScalarGridSpec(
        num_scalar_prefetch=2,
        grid=(B, NSA_KV_HEADS),
        in_specs=[blk_spec(n) for n in range(n_sel)] + [
            pl.BlockSpec((None, 1, 4 * LANE), lambda b, g, tbl, idx: (b, 0, g)),
            pl.BlockSpec((None, 1, 4 * LANE), lambda b, g, tbl, idx: (b, 0, g)),
            pl.BlockSpec((None, wb * WIN_ROWS, LANE), lambda b, g, tbl, idx: (b, 0, 0)),
            pl.BlockSpec((None, 1, LANE), lambda b, g, tbl, idx: (b, 0, NSA_GATE_BLK + g)),
            pl.BlockSpec((None, None, NSA_GROUP, LANE), lambda b, g, tbl, idx: (b, g, 0, 0)),
        ],
        out_specs=pl.BlockSpec((None, 1, 4 * LANE), lambda b, g, tbl, idx: (b, 0, g)),
    )
    kvn = kvb.reshape(B, 1, 4, NSA_KV_HEADS, LANE).transpose(0, 1, 3, 2, 4).reshape(B, 1, 1536)
    return pl.pallas_call(
        functools.partial(_dec_attn_kernel, n_sel=n_sel, cur=cur, wb=wb),
        grid_spec=grid_spec,
        out_shape=jax.ShapeDtypeStruct((B, 1, 1536), F32),
        compiler_params=_params(("arbitrary", "arbitrary")),
        name="dec_attn",
    )(table, idx, *([cache] * n_sel), qr, kvn, win_state, proj, ocmp)


def _dec_dil_kernel(q_ref, n0_ref, n1_ref, n2_ref, s0_ref, s1_ref, s2_ref, o_ref, t0_ref, t1_ref, t2_ref):
    qall = q_ref[...]
    states, news, outs = (s0_ref, s1_ref, s2_ref), (n0_ref, n1_ref, n2_ref), (t0_ref, t1_ref, t2_ref)

    for st, new, out in zip(states, news, outs):
        keep = st.shape[0] - KVH_ROWS
        out[0:keep, :] = st[KVH_ROWS:, :]
        out[keep:, :] = new[...]

    for h in range(DIL_HEADS):
        os_, ms, ls = [], [], []
        for g, st in enumerate(states):
            hs = slice((g * DIL_HEADS + h) * LANE, (g * DIL_HEADS + h + 1) * LANE)
            window, dil = DIL_PAIRS[g]
            kn = news[g][h:h + 1, :].astype(BF16).astype(F32)
            vn = news[g][DIL_HEADS + h:DIL_HEADS + h + 1, :].astype(BF16).astype(F32)
            q = _rows16(qall[:, hs], 1).astype(BF16)
            k = _get_rows(st, h, KVH_ROWS * dil, window // dil).astype(BF16)
            v = _get_rows(st, DIL_HEADS + h, KVH_ROWS * dil, window // dil).astype(BF16)
            s = _dot_t(q, k) * SCALE
            s_new = jnp.sum(q.astype(F32) * kn, axis=-1, keepdims=True) * SCALE
            m = jnp.maximum(jnp.max(s, axis=-1, keepdims=True), s_new)
            e = jnp.exp(s - m)
            e_new = jnp.exp(s_new - m)
            l = jnp.sum(e, axis=-1, keepdims=True) + e_new
            ln = jnp.maximum(l, 1e-30)
            o = _dot((e / ln).astype(BF16), v) + (e_new / ln).astype(BF16).astype(F32) * vn
            os_.append(o)
            ms.append(m)
            ls.append(l)
        o_ref[:, h * LANE:(h + 1) * LANE] = _mix_groups(os_, ms, ls)[0:1, :]


def _dec_dil(qr, news, states):
    B = qr.shape[0]
    in_specs = [pl.BlockSpec((None, 1, 1536), lambda b: (b, 0, 0))]
    in_specs += [pl.BlockSpec((None, KVH_ROWS, LANE), lambda b: (b, 0, 0))] * len(DIL_PAIRS)
    st_specs = []
    for g, (window, dil) in enumerate(DIL_PAIRS):
        assert states[g].shape[1] == window * KVH_ROWS, "rolling buffer shorter than the window is not supported"
        st_specs.append(pl.BlockSpec((None, window * KVH_ROWS, LANE), lambda b: (b, 0, 0)))
    return pl.pallas_call(
        _dec_dil_kernel,
        grid=(B,),
        in_specs=in_specs + st_specs,
        out_specs=[pl.BlockSpec((None, 1, 4 * LANE), lambda b: (b, 0, 0))] + st_specs,
        out_shape=[jax.ShapeDtypeStruct((B, 1, 4 * LANE), F32)]
        + [jax.ShapeDtypeStruct(s.shape, F32) for s in states],
        compiler_params=_params(("parallel",)),
        name="dec_dil",
    )(qr, *news, *states)


def _rope_tables(pos):
    half = HEAD_DIM // 2
    inv = ROPE_THETA ** (-jnp.arange(half, dtype=F32) / half)
    ang = pos.astype(F32)[:, None] * inv
    cos, sin = jnp.cos(ang), jnp.sin(ang)
    return jnp.concatenate([cos, cos], axis=-1), jnp.concatenate([-sin, sin], axis=-1)


def _cover(nc, ns, rows, cols):
    c0 = jnp.arange(nc)[:, None] * CMP_STRIDE
    s0 = jnp.arange(ns)[None, :] * SEL_BLOCK
    cover = jnp.clip(jnp.minimum(c0 + CMP_BLOCK, s0 + SEL_BLOCK) - jnp.maximum(c0, s0), 0, CMP_BLOCK)
    cover = cover.astype(F32) / CMP_BLOCK
    return jnp.pad(cover, ((0, rows - nc), (0, cols - ns))).astype(BF16)


def _pad_gains(*gs):
    return jnp.pad(jnp.stack(gs, axis=0), ((0, 8 - len(gs)), (0, 0)))


def kernel(x_prompt, x_sample, mem_prompt, cache_nsa_kv, page_table, state_nsa_win, state_dil_0, state_dil_1,
           state_dil_2, cache_mem_kv, ff_norm, ff_w_gate, ff_w_up, ff_w_down, mix_norm, mem_norm, w_mem_kv,
           mem_q_g, mem_k_g, nsa_w_in, nsa_q_g, nsa_kc_g, nsa_ks_g, nsa_kw_g, nsa_cmp_w1, nsa_cmp_b1, nsa_cmp_w2,
           nsa_w_out, dil_w_in, dil_q_g, dil_k_g, dil_w_out):
    B, T, D = x_prompt.shape
    Bs = x_sample.shape[0]
    assert x_sample.shape[1] == 1, "the sample group is a single-token decode step"
    n_pages = page_table.shape[1]
    past_len = n_pages * PAGE_SIZE
    H, G, d = N_MIX_HEADS, NSA_KV_HEADS, HEAD_DIM

    wg, wu, wd = ff_w_gate, ff_w_up, ff_w_down
    ffg = ff_norm.reshape(ff_norm.shape[0], 2, 1, D)
    gate_w = jnp.pad(nsa_w_in[:, H * d:H * d + 3 * H].reshape(D, G, 3 * NSA_GROUP), ((0, 0), (0, 0), (0, LANE - 12)))
    nsa_w = jnp.concatenate([nsa_w_in[:, :H * d], nsa_w_in[:, H * d + 3 * H:], gate_w.reshape(D, G * LANE)], axis=1)
    nsa_w = jnp.pad(nsa_w, ((0, 0), (0, PROJ_N - nsa_w.shape[1]))).astype(BF16)
    dil_w = dil_w_in.astype(BF16)
    nsa_wo, dil_wo = nsa_w_out.astype(BF16), dil_w_out.astype(BF16)
    w1r = nsa_cmp_w1.reshape(2, 2, CMP_STRIDE, d, d).transpose(0, 2, 3, 1, 4).reshape(2, CMP_STRIDE * d, 2 * d)
    w1r = w1r.astype(BF16)
    cmp_b1 = nsa_cmp_b1.reshape(2, 1, d)
    cmp_w2 = nsa_cmp_w2.astype(BF16)
    kc_g = nsa_kc_g.reshape(1, d)

    n_pool = cache_nsa_kv.shape[0]
    cache_rows = cache_nsa_kv.transpose(0, 1, 3, 2, 4).reshape(n_pool, PAGE_SIZE * NSA_ROWS, LANE)
    win_rows = state_nsa_win.transpose(0, 1, 3, 2, 4).reshape(Bs, state_nsa_win.shape[1] * WIN_ROWS, LANE)
    dil_states = (state_dil_0, state_dil_1, state_dil_2)
    dil_rows = [s.reshape(Bs, s.shape[1] * KVH_ROWS, LANE) for s in dil_states]
    mem_rows_s = cache_mem_kv.reshape(2, Bs, N_MEM * KVH_ROWS, LANE)

    mem2d = mem_prompt.reshape(B * N_MEM, D)
    mem_rows_p = []
    for i in range(2):
        kv = _norm_matmul(mem2d, mem_norm[i], w_mem_kv[i].astype(BF16), tm=256, tn=1024)
        mem_rows_p.append(_memkv_post(kv, mem_k_g[i]).reshape(B, N_MEM * KVH_ROWS, LANE))

    cos_p, sin_p = _rope_tables(jnp.tile(jnp.arange(T, dtype=jnp.int32), B))
    cos_s, sin_s = _rope_tables(jnp.full((Bs,), past_len, jnp.int32))
    row3 = lambda a: a.astype(F32).reshape(Bs, 1, a.shape[-1])

    xs, *wb = _ffn_cast(x_sample.reshape(Bs, D), ffg, wg, wu, wd, 0, 0)
    xp = _ffn(x_prompt.reshape(B * T, D), ffg, *wb, 0, 0, tm=1024)
    nsa_gains = _pad_gains(nsa_q_g, nsa_ks_g, nsa_kw_g, mem_q_g[0])

    proj_p = _norm_matmul(xp, mix_norm[0], nsa_w, tm=1024, tn=1024)
    qn_p, qr_p, cmp_p, rows_p, win_p, kvb_p, mq_p = _nsa_post(proj_p, cos_p, sin_p, nsa_gains, tm=256)
    cmp_kv = _cmp_prompt(cmp_p.reshape(B, T, 768), w1r, cmp_b1, cmp_w2, kc_g)
    nc_p = T // CMP_STRIDE
    cover_p = _cover(nc_p - 1, T // SEL_BLOCK, nc_p, LANE)
    o_mix_p = _nsa_attn(qn_p.reshape(B, T, 1536), qr_p.reshape(B, T, 1536), proj_p.reshape(B, T, PROJ_N),
                        cmp_kv, kvb_p.reshape(B, T, 1536), cover_p)
    o_mem_p = _mem_attn(mq_p.reshape(B, T, 512), mem_rows_p[0], tq=512)
    xp = _out_proj(xp, o_mix_p.reshape(B * T, 1536), o_mem_p.reshape(B * T, 512), nsa_wo, tm=1024)

    proj_s = _norm_matmul(xs, mix_norm[0], nsa_w, tm=Bs, tn=1024)
    qn_s, qr_s, _, rows_s, win_s, kvb_s, mq_s = _nsa_post(proj_s, cos_s, sin_s, nsa_gains, tm=Bs)
    nc_s = past_len // CMP_STRIDE
    ns_s = -(-(past_len + 1) // SEL_BLOCK)
    cover_s = _cover(nc_s - 1, ns_s, nc_s, -(-ns_s // LANE) * LANE)
    sel_idx, ocmp_s = _dec_select(cache_rows, page_table, row3(qn_s), w1r, cmp_b1, cmp_w2, kc_g, cover_s)
    o_mix_s = _dec_attn(cache_rows, page_table, sel_idx[:, :G, :SEL_TOPK].reshape(-1), row3(qr_s), row3(kvb_s),
                        win_rows, proj_s.reshape(Bs, 1, PROJ_N), ocmp_s)
    o_mem_s = _mem_attn(row3(mq_s), mem_rows_s[0], tq=1)
    xs = _out_proj(xs, o_mix_s.reshape(Bs, 1536).astype(BF16), o_mem_s.reshape(Bs, 512).astype(BF16), nsa_wo,
                   tm=Bs)

    xs, *wb = _ffn_cast(xs, ffg, wg, wu, wd, 0, 1)
    xp = _ffn(xp, ffg, *wb, 0, 1, tm=1024)

    xs, *wb = _ffn_cast(xs, ffg, wg, wu, wd, 1, 0)
    xp = _ffn(xp, ffg, *wb, 1, 0, tm=1024)
    dil_gains = _pad_gains(dil_q_g, dil_k_g, mem_q_g[1])

    dproj_p = _norm_matmul(xp, mix_norm[1], dil_w, tm=1024, tn=1024)
    dq_p, dkb_p, dvb_p, dmq_p, *dnew_p = _dil_post(dproj_p, cos_p, sin_p, dil_gains, tm=256)
    band = [_dil_band(dq_p.reshape(B, T, 1536), dkb_p.reshape(B, T, 1536), dvb_p.reshape(B, T, 1536), g, dil)
            for g, (_, dil) in enumerate(DIL_PAIRS)]
    o_dil_p = _dil_mix([o for o, _ in band], [s for _, s in band])
    o_dmem_p = _mem_attn(dmq_p.reshape(B, T, 512), mem_rows_p[1], tq=512)
    xp = _out_proj(xp, o_dil_p, o_dmem_p.reshape(B * T, 512), dil_wo, tm=1024)

    dproj_s = _norm_matmul(xs, mix_norm[1], dil_w, tm=Bs, tn=1024)
    dq_s, _, _, dmq_s, *dnew_s = _dil_post(dproj_s, cos_s, sin_s, dil_gains, tm=Bs)
    o_dil_s, *dil_rows_out = _dec_dil(row3(dq_s), [s.reshape(Bs, KVH_ROWS, LANE) for s in dnew_s], dil_rows)
    o_dmem_s = _mem_attn(row3(dmq_s), mem_rows_s[1], tq=1)
    xs = _out_proj(xs, o_dil_s.reshape(Bs, 512).astype(BF16), o_dmem_s.reshape(Bs, 512).astype(BF16), dil_wo,
                   tm=Bs)

    xs, *wb = _ffn_cast(xs, ffg, wg, wu, wd, 1, 1)
    xp = _ffn(xp, ffg, *wb, 1, 1, tm=1024)

    unrow = lambda a, n, outer, inner: a.reshape(n, -1, outer, inner, d).transpose(0, 1, 3, 2, 4)
    nsa_kv_p = unrow(rows_p, B, G, 4)
    nsa_kv_s = unrow(rows_s, Bs, G, 4)
    nsa_win_p = unrow(win_p, B, G, 2)[:, -min(NSA_WINDOW, T):]
    nsa_win_s = jnp.concatenate([state_nsa_win, unrow(win_s, Bs, G, 2)], axis=1)[:, -state_nsa_win.shape[1]:]
    outs_dil = []
    for g, (window, _) in enumerate(DIL_PAIRS):
        st = dil_states[g]
        outs_dil.append(dnew_p[g].reshape(B, T, 2, DIL_HEADS, d)[:, -min(window, T):])
        outs_dil.append(dil_rows_out[g].reshape(st.shape))
    mem_kv_out = jnp.stack([kv.reshape(B, N_MEM, 2, N_MEM_HEADS, d) for kv in mem_rows_p], axis=0)
    return (xp.reshape(B, T, D), xs.reshape(Bs, 1, D), nsa_kv_p, nsa_kv_s, nsa_win_p, nsa_win_s,
            *outs_dil, mem_kv_out)
```

```python
import functools

import jax
import jax.numpy as jnp
from jax import lax
from jax.experimental import pallas as pl
from jax.experimental.pallas import tpu as pltpu

F32 = jnp.float32
BF16 = jnp.bfloat16

D_MODEL = 2048
HEAD_DIM = 128
N_MIX_HEADS = 12
N_MEM_HEADS = 4
N_MEM = 256
NSA_KV_HEADS = 3
NSA_GROUP = 4
CMP_BLOCK = 32
CMP_STRIDE = 16
SEL_BLOCK = 64
SEL_TOPK = 16
NSA_WINDOW = 512
DIL_PAIRS = ((128, 1), (512, 4), (2048, 16))
DIL_HEADS = 4
PAGE_SIZE = 128
ROPE_THETA = 10000.0
EPS = 1e-6
SCALE = HEAD_DIM ** -0.5
NEG = -1e30
FORCE_SCORE = 1e6

PROJ_N = 5120
NSA_GATE_BLK = 34
LANE = 128
VMEM_LIMIT = 56 * 1024 * 1024


def _params(sem):
    return pltpu.CompilerParams(dimension_semantics=sem, vmem_limit_bytes=VMEM_LIMIT)


def _dot(a, b):
    return jnp.dot(a, b, preferred_element_type=F32)


def _dot_t(a, b):
    return lax.dot_general(a, b, (((1,), (1,)), ((), ())), preferred_element_type=F32)


def _dot3(a, b):
    a1 = a.astype(BF16)
    r1 = a - a1.astype(F32)
    a2 = r1.astype(BF16)
    a3 = (r1 - a2.astype(F32)).astype(BF16)
    return _dot(a1, b) + _dot(a2, b) + _dot(a3, b)


def _rms(x, g):
    return x * lax.rsqrt(jnp.mean(x * x, axis=-1, keepdims=True) + EPS) * g


def _rope(x, cos, sin):
    return x * cos + pltpu.roll(x, HEAD_DIM // 2, 1) * sin


def _rows16(row, nrep):
    rid = lax.broadcasted_iota(jnp.int32, (16, LANE), 0) & (nrep - 1)
    out = jnp.zeros((16, LANE), F32)
    for r in range(nrep):
        piece = jnp.broadcast_to(row[:, r * LANE:(r + 1) * LANE], (16, LANE))
        out = jnp.where(rid == r, piece, out)
    return out


def _softmax_masked(s, mask):
    s = jnp.where(mask, s, NEG)
    m = jnp.max(s, axis=-1, keepdims=True)
    e = jnp.where(mask, jnp.exp(s - m), 0.0)
    l = jnp.sum(e, axis=-1, keepdims=True)
    return e / jnp.maximum(l, 1e-30), m, l


def _ffn_step(f, nf, x_ref, g_ref, wg, wu, wd, o_ref, h_ref, acc_ref):
    @pl.when(f == 0)
    def _():
        h_ref[...] = _rms(x_ref[...], g_ref[...]).astype(BF16)
        acc_ref[...] = jnp.zeros_like(acc_ref)

    h = h_ref[...]
    gate = _dot(h, wg)
    up = _dot(h, wu)
    a = (gate * jax.nn.sigmoid(gate) * up).astype(BF16)
    acc_ref[...] += _dot(a, wd)

    @pl.when(f == nf - 1)
    def _():
        o_ref[...] = x_ref[...] + 0.5 * acc_ref[...]


def _ffn_kernel(x_ref, g_ref, wg_ref, wu_ref, wd_ref, o_ref, h_ref, acc_ref, *, nf):
    _ffn_step(pl.program_id(1), nf, x_ref, g_ref, wg_ref[...], wu_ref[...], wd_ref[...], o_ref, h_ref, acc_ref)


def _ffn(x, g, wg, wu, wd, li, lj, tm, tf=512):
    M, D = x.shape
    F = wg.shape[-1]
    nf = F // tf
    return pl.pallas_call(
        functools.partial(_ffn_kernel, nf=nf),
        grid=(M // tm, nf),
        in_specs=[
            pl.BlockSpec((tm, D), lambda i, f: (i, 0)),
            pl.BlockSpec((None, None, 1, D), lambda i, f: (li, lj, 0, 0)),
            pl.BlockSpec((D, tf), lambda i, f: (0, f)),
            pl.BlockSpec((D, tf), lambda i, f: (0, f)),
            pl.BlockSpec((tf, D), lambda i, f: (f, 0)),
        ],
        out_specs=pl.BlockSpec((tm, D), lambda i, f: (i, 0)),
        out_shape=jax.ShapeDtypeStruct((M, D), F32),
        scratch_shapes=[pltpu.VMEM((tm, D), BF16), pltpu.VMEM((tm, D), F32)],
        compiler_params=_params(("parallel", "arbitrary")),
        name="ffn",
    )(x, g, wg, wu, wd)


def _ffn_cast_kernel(x_ref, g_ref, wg_ref, wu_ref, wd_ref, o_ref, wgb_ref, wub_ref, wdb_ref, h_ref, *, nf):
    wgb_ref[...] = wg_ref[...].astype(BF16)
    wub_ref[...] = wu_ref[...].astype(BF16)
    wdb_ref[...] = wd_ref[...].astype(BF16)
    _ffn_step(pl.program_id(0), nf, x_ref, g_ref, wgb_ref[...], wub_ref[...], wdb_ref[...], o_ref, h_ref, o_ref)


def _ffn_cast(x, g, wg, wu, wd, li, lj, tf=512):
    M, D = x.shape
    F = wg.shape[-1]
    nf = F // tf
    return pl.pallas_call(
        functools.partial(_ffn_cast_kernel, nf=nf),
        grid=(nf,),
        in_specs=[
            pl.BlockSpec((M, D), lambda f: (0, 0)),
            pl.BlockSpec((None, None, 1, D), lambda f: (li, lj, 0, 0)),
            pl.BlockSpec((None, None, D, tf), lambda f: (li, lj, 0, f)),
            pl.BlockSpec((None, None, D, tf), lambda f: (li, lj, 0, f)),
            pl.BlockSpec((None, None, tf, D), lambda f: (li, lj, f, 0)),
        ],
        out_specs=[pl.BlockSpec((M, D), lambda f: (0, 0)),
                   pl.BlockSpec((D, tf), lambda f: (0, f)),
                   pl.BlockSpec((D, tf), lambda f: (0, f)),
                   pl.BlockSpec((tf, D), lambda f: (f, 0))],
        out_shape=[jax.ShapeDtypeStruct((M, D), F32),
                   jax.ShapeDtypeStruct((D, F), BF16),
                   jax.ShapeDtypeStruct((D, F), BF16),
                   jax.ShapeDtypeStruct((F, D), BF16)],
        scratch_shapes=[pltpu.VMEM((M, D), BF16)],
        compiler_params=_params(("arbitrary",)),
        name="ffn_cast",
    )(x, g, wg, wu, wd)


def _nmm_kernel(x_ref, g_ref, w_ref, o_ref, h_ref):
    @pl.when(pl.program_id(1) == 0)
    def _():
        h_ref[...] = _rms(x_ref[...], g_ref[...]).astype(BF16)

    o_ref[...] = _dot(h_ref[...], w_ref[...])


def _norm_matmul(x, g, w, tm, tn):
    M, D = x.shape
    N = w.shape[1]
    return pl.pallas_call(
        _nmm_kernel,
        grid=(M // tm, N // tn),
        in_specs=[
            pl.BlockSpec((tm, D), lambda i, j: (i, 0)),
            pl.BlockSpec((1, D), lambda i, j: (0, 0)),
            pl.BlockSpec((D, tn), lambda i, j: (0, j)),
        ],
        out_specs=pl.BlockSpec((tm, tn), lambda i, j: (i, j)),
        out_shape=jax.ShapeDtypeStruct((M, N), F32),
        scratch_shapes=[pltpu.VMEM((tm, D), BF16)],
        compiler_params=_params(("parallel", "arbitrary")),
        name="norm_matmul",
    )(x, g.reshape(1, D), w)


def _oproj_kernel(x_ref, a_ref, b_ref, w_ref, o_ref, *, ka):
    o_ref[...] = x_ref[...] + _dot(a_ref[...], w_ref[:ka, :]) + _dot(b_ref[...], w_ref[ka:, :])


def _out_proj(x, a, b, w, tm, tn=1024):
    M, D = x.shape
    ka, kb = a.shape[1], b.shape[1]
    return pl.pallas_call(
        functools.partial(_oproj_kernel, ka=ka),
        grid=(M // tm, D // tn),
        in_specs=[
            pl.BlockSpec((tm, tn), lambda i, j: (i, j)),
            pl.BlockSpec((tm, ka), lambda i, j: (i, 0)),
            pl.BlockSpec((tm, kb), lambda i, j: (i, 0)),
            pl.BlockSpec((ka + kb, tn), lambda i, j: (0, j)),
        ],
        out_specs=pl.BlockSpec((tm, tn), lambda i, j: (i, j)),
        out_shape=jax.ShapeDtypeStruct((M, D), F32),
        compiler_params=_params(("parallel", "parallel")),
        name="out_proj",
    )(x, a, b, w)


def _put_rows(ref, row, rows_per_token, val):
    ref[pl.ds(row, val.shape[0], stride=rows_per_token), :] = val


def _get_rows(ref, row, rows_per_token, n):
    return ref[pl.ds(row, n, stride=rows_per_token), :]


NSA_ROWS = 4 * NSA_KV_HEADS
WIN_ROWS = 2 * NSA_KV_HEADS
KVH_ROWS = 2 * DIL_HEADS


def _nsa_post_kernel(p_ref, cos_ref, sin_ref, g_ref, qn_ref, qr_ref, cmp_ref, rows_ref, win_ref, kvb_ref, mq_ref):
    cos, sin = cos_ref[...], sin_ref[...]
    q_g, ks_g, kw_g, mq_g = g_ref[0:1, :], g_ref[1:2, :], g_ref[2:3, :], g_ref[3:4, :]

    def tile(i):
        return p_ref[:, i * LANE:(i + 1) * LANE]

    for h in range(N_MIX_HEADS):
        qn = _rms(tile(h), q_g)
        qn_ref[:, h * LANE:(h + 1) * LANE] = qn.astype(BF16)
        qr_ref[:, h * LANE:(h + 1) * LANE] = _rope(qn, cos, sin).astype(BF16)
    for g in range(NSA_KV_HEADS):
        kc, vc = tile(12 + g), tile(15 + g)
        ks = _rope(_rms(tile(18 + g), ks_g), cos, sin)
        vs = tile(21 + g)
        kw = _rope(_rms(tile(24 + g), kw_g), cos, sin)
        vw = tile(27 + g)
        for c, val in enumerate((kc, vc)):
            cmp_ref[:, (c * 3 + g) * LANE:(c * 3 + g + 1) * LANE] = val
        for c, val in enumerate((kc, vc, ks, vs)):
            _put_rows(rows_ref, g * 4 + c, NSA_ROWS, val)
        for c, val in enumerate((kw, vw)):
            _put_rows(win_ref, g * 2 + c, WIN_ROWS, val)
        for c, val in enumerate((ks, vs, kw, vw)):
            kvb_ref[:, (c * 3 + g) * LANE:(c * 3 + g + 1) * LANE] = val.astype(BF16)
    for h in range(N_MEM_HEADS):
        mq_ref[:, h * LANE:(h + 1) * LANE] = _rms(tile(30 + h), mq_g).astype(BF16)


def _nsa_post(p, cos, sin, gains, tm):
    M = p.shape[0]
    row = lambda n: pl.BlockSpec((tm, n), lambda i: (i, 0))
    flat = lambda r: pl.BlockSpec((tm * r, LANE), lambda i: (i, 0))
    return pl.pallas_call(
        _nsa_post_kernel,
        grid=(M // tm,),
        in_specs=[row(PROJ_N), row(LANE), row(LANE), pl.BlockSpec((8, LANE), lambda i: (0, 0))],
        out_specs=[row(1536), row(1536), row(768), flat(NSA_ROWS), flat(WIN_ROWS), row(1536), row(512)],
        out_shape=[
            jax.ShapeDtypeStruct((M, 1536), BF16),
            jax.ShapeDtypeStruct((M, 1536), BF16),
            jax.ShapeDtypeStruct((M, 768), F32),
            jax.ShapeDtypeStruct((M * NSA_ROWS, LANE), F32),
            jax.ShapeDtypeStruct((M * WIN_ROWS, LANE), F32),
            jax.ShapeDtypeStruct((M, 1536), BF16),
            jax.ShapeDtypeStruct((M, 512), BF16),
        ],
        compiler_params=_params(("parallel",)),
        name="nsa_post",
    )(p, cos, sin, gains)


def _put_residues(ref, hh, val, dil, tmp_ref):
    sl = slice(hh * LANE, (hh + 1) * LANE)
    if dil == 1:
        ref[0, :, sl] = val.astype(BF16)
        return
    tmp_ref[...] = val
    n = val.shape[0] // dil
    for r in range(dil):
        ref[r, :, sl] = tmp_ref[pl.ds(r, n, stride=dil), :].astype(BF16)


def _dil_post_kernel(p_ref, cos_ref, sin_ref, g_ref, *refs, by_residue):
    if by_residue:
        mq_ref, st0_ref, st1_ref, st2_ref = refs[:4]
        qkv_refs, tmp_ref = refs[4:13], refs[13]
    else:
        qr_ref, mq_ref, st0_ref, st1_ref, st2_ref = refs
    cos, sin = cos_ref[...], sin_ref[...]
    q_g, k_g, mq_g = g_ref[0:1, :], g_ref[1:2, :], g_ref[2:3, :]
    st_refs = (st0_ref, st1_ref, st2_ref)
    for h in range(N_MIX_HEADS):
        g, hh = divmod(h, DIL_HEADS)
        sl = slice(h * LANE, (h + 1) * LANE)
        q = _rope(_rms(p_ref[:, sl], q_g), cos, sin)
        k = _rope(_rms(p_ref[:, (12 + h) * LANE:(13 + h) * LANE], k_g), cos, sin)
        v = p_ref[:, (24 + h) * LANE:(25 + h) * LANE]
        _put_rows(st_refs[g], hh, KVH_ROWS, k)
        _put_rows(st_refs[g], DIL_HEADS + hh, KVH_ROWS, v)
        if by_residue:
            for c, val in enumerate((q, k, v)):
                _put_residues(qkv_refs[3 * g + c], hh, val, DIL_PAIRS[g][1], tmp_ref)
        else:
            qr_ref[:, sl] = q.astype(BF16)
    for h in range(N_MEM_HEADS):
        mq_ref[:, h * LANE:(h + 1) * LANE] = _rms(p_ref[:, (36 + h) * LANE:(37 + h) * LANE], mq_g).astype(BF16)


def _dil_post(p, cos, sin, gains, tm, seq_len=None):
    M = p.shape[0]
    row = lambda n: pl.BlockSpec((tm, n), lambda i: (i, 0))
    flat = pl.BlockSpec((tm * KVH_ROWS, LANE), lambda i: (i, 0))
    st_shape = jax.ShapeDtypeStruct((M * KVH_ROWS, LANE), F32)
    out_specs = [row(512), flat, flat, flat]
    out_shape = [jax.ShapeDtypeStruct((M, 512), BF16), st_shape, st_shape, st_shape]
    scratch = []
    if seq_len is None:
        out_specs = [row(1536)] + out_specs
        out_shape = [jax.ShapeDtypeStruct((M, 1536), BF16)] + out_shape
    else:
        nt = seq_len // tm
        for _, dil in DIL_PAIRS:
            assert tm % (16 * dil) == 0
            spec = pl.BlockSpec((None, dil, tm // dil, 4 * LANE), lambda i: (i // nt, 0, i % nt, 0))
            shape = jax.ShapeDtypeStruct((M // seq_len, dil, seq_len // dil, 4 * LANE), BF16)
            out_specs += [spec] * 3
            out_shape += [shape] * 3
        scratch = [pltpu.VMEM((tm, LANE), F32)]
    return pl.pallas_call(
        functools.partial(_dil_post_kernel, by_residue=seq_len is not None),
        grid=(M // tm,),
        in_specs=[row(PROJ_N), row(LANE), row(LANE), pl.BlockSpec((8, LANE), lambda i: (0, 0))],
        out_specs=out_specs,
        out_shape=out_shape,
        scratch_shapes=scratch,
        compiler_params=_params(("parallel",)),
        name="dil_post",
    )(p, cos, sin, gains)


def _memkv_post_kernel(x_ref, g_ref, o_ref):
    for h in range(N_MEM_HEADS):
        _put_rows(o_ref, h, KVH_ROWS, _rms(x_ref[:, h * LANE:(h + 1) * LANE], g_ref[...]))
        _put_rows(o_ref, N_MEM_HEADS + h, KVH_ROWS, x_ref[:, (N_MEM_HEADS + h) * LANE:(N_MEM_HEADS + h + 1) * LANE])


def _memkv_post(x, g, tm=256):
    M, N = x.shape
    return pl.pallas_call(
        _memkv_post_kernel,
        grid=(M // tm,),
        in_specs=[pl.BlockSpec((tm, N), lambda i: (i, 0)), pl.BlockSpec((1, LANE), lambda i: (0, 0))],
        out_specs=pl.BlockSpec((tm * KVH_ROWS, LANE), lambda i: (i, 0)),
        out_shape=jax.ShapeDtypeStruct((M * KVH_ROWS, LANE), F32),
        compiler_params=_params(("parallel",)),
        name="memkv_post",
    )(x, g.reshape(1, LANE))


def _mem_attn_kernel(q_ref, kv_ref, o_ref, *, tq):
    for h in range(N_MEM_HEADS):
        sl = slice(h * LANE, (h + 1) * LANE)
        if tq == 1:
            q = _rows16(q_ref[:, sl], 1).astype(BF16)
        else:
            q = q_ref[:, sl]
        k = _get_rows(kv_ref, h, KVH_ROWS, N_MEM).astype(BF16)
        v = _get_rows(kv_ref, N_MEM_HEADS + h, KVH_ROWS, N_MEM).astype(BF16)
        s = _dot_t(q, k) * SCALE
        m = jnp.max(s, axis=-1, keepdims=True)
        e = jnp.exp(s - m)
        p = e / jnp.sum(e, axis=-1, keepdims=True)
        o = _dot(p.astype(BF16), v)
        o_ref[:, sl] = o[0:tq, :].astype(o_ref.dtype)


def _mem_attn(q, kv, tq):
    B, T, _ = q.shape
    return pl.pallas_call(
        functools.partial(_mem_attn_kernel, tq=tq),
        grid=(B, T // tq),
        in_specs=[
            pl.BlockSpec((None, tq, 512), lambda b, i: (b, i, 0)),
            pl.BlockSpec((None, N_MEM * KVH_ROWS, LANE), lambda b, i: (b, 0, 0)),
        ],
        out_specs=pl.BlockSpec((None, tq, 512), lambda b, i: (b, i, 0)),
        out_shape=jax.ShapeDtypeStruct((B, T, 512), q.dtype),
        compiler_params=_params(("parallel", "parallel")),
        name="mem_attn",
    )(q, kv)


def _gelu_tanh(x):
    return 0.5 * x * (1.0 + jnp.tanh(0.7978845608028654 * (x + 0.044715 * (x * x * x))))


def _compress_finish(h, b1, w2):
    n = h.shape[0]
    hid = b1 + h[:, :LANE] + pltpu.roll(h[:, LANE:], n - 1, 0)
    return _dot(_gelu_tanh(hid).astype(BF16), w2)


def _compress(x_bf, w1, b1, w2):
    return _compress_finish(_dot(x_bf, w1), b1, w2)


def _cmp_prompt_kernel(x_ref, w1_ref, b1_ref, w2_ref, kcg_ref, o_ref, xs_ref, *, n):
    kv = pl.program_id(1)
    for c in range(CMP_STRIDE):
        xs_ref[:, c * LANE:(c + 1) * LANE] = x_ref[pl.ds(c, n, stride=CMP_STRIDE), :].astype(BF16)
    out = _compress(xs_ref[...], w1_ref[...], b1_ref[...], w2_ref[...])
    out = jnp.where(kv == 0, _rms(out, kcg_ref[...]), out)
    rid = lax.broadcasted_iota(jnp.int32, out.shape, 0)
    o_ref[...] = jnp.where(rid < n - 1, out, 0.0).astype(BF16)


def _cmp_prompt(rows, w1r, b1, w2, kc_g):
    B, T, _ = rows.shape
    n = T // CMP_STRIDE
    return pl.pallas_call(
        functools.partial(_cmp_prompt_kernel, n=n),
        grid=(B, 2, NSA_KV_HEADS),
        in_specs=[
            pl.BlockSpec((None, T, LANE), lambda b, kv, g: (b, 0, kv * 3 + g)),
            pl.BlockSpec((None, CMP_STRIDE * LANE, 2 * LANE), lambda b, kv, g: (kv, 0, 0)),
            pl.BlockSpec((None, 1, LANE), lambda b, kv, g: (kv, 0, 0)),
            pl.BlockSpec((None, LANE, LANE), lambda b, kv, g: (kv, 0, 0)),
            pl.BlockSpec((1, LANE), lambda b, kv, g: (0, 0)),
        ],
        out_specs=pl.BlockSpec((None, None, None, n, LANE), lambda b, kv, g: (b, kv, g, 0, 0)),
        out_shape=jax.ShapeDtypeStruct((B, 2, NSA_KV_HEADS, n, LANE), BF16),
        scratch_shapes=[pltpu.VMEM((n, CMP_STRIDE * LANE), BF16)],
        compiler_params=_params(("parallel", "parallel", "parallel")),
        name="cmp_prompt",
    )(rows, w1r, b1, w2, kc_g)


def _select_blocks(score, cur, n_blocks):
    tq = score.shape[0]
    blk = lax.broadcasted_iota(jnp.int32, score.shape, 1)
    forced = (blk == 0) | (blk == cur) | (blk == cur - 1)
    sc = jnp.where(blk <= cur, jnp.where(forced, FORCE_SCORE, score), NEG)
    sct = sc.T[0:n_blocks, :]
    bi = lax.broadcasted_iota(jnp.int32, sct.shape, 0)
    rank = jnp.zeros(sct.shape, F32)
    for i in range(n_blocks):
        si = sct[i:i + 1, :]
        ahead = (si > sct) | ((si == sct) & (bi > i))
        rank = rank + jnp.where(ahead, 1.0, 0.0)
    chosen = jnp.where((rank < SEL_TOPK) & (sct > 0.5 * NEG), 1.0, 0.0)
    return jnp.concatenate([chosen, jnp.zeros((LANE - n_blocks, tq), F32)], axis=0).T


def _score_tile(q, k_ref, kt, slot, bias, s_ref, m_ref, tk, first=False):
    s = _dot_t(q, k_ref[pl.ds(pl.multiple_of(kt * tk, tk), tk), :]) * SCALE
    if bias is not None:
        s = s + bias
    s_ref[slot] = s
    m = functools.reduce(jnp.maximum, [s[:, c * LANE:(c + 1) * LANE] for c in range(tk // LANE)])
    m_ref[...] = m if first else jnp.maximum(m_ref[...], m)


def _value_tile(v_ref, kt, slot, s_ref, m_ref, l_ref, acc_ref, tk, first=False):
    m = m_ref[...]
    es = [jnp.exp(s_ref[slot, :, c * LANE:(c + 1) * LANE] - m) for c in range(tk // LANE)]
    pv = _dot(jnp.concatenate(es, axis=1).astype(BF16), v_ref[pl.ds(pl.multiple_of(kt * tk, tk), tk), :])
    l = functools.reduce(lambda a, b: a + b, es)
    if first:
        l_ref[...] = l
        acc_ref[...] = pv
    else:
        l_ref[...] += l
        acc_ref[...] += pv


def _nsa_attn_kernel(qn_ref, qr_ref, gl_ref, kc_ref, vc_ref, ks_ref, vs_ref, kw_ref, vw_ref, cover_ref,
                     o_ref, s_ref, m_ref, l_ref, acc_ref, sel_ref, *, tq, ns):
    qi = pl.program_id(2)
    R = NSA_GROUP
    t0 = qi * tq
    stack = lambda ref: jnp.concatenate([ref[:, r * LANE:(r + 1) * LANE] for r in range(R)], axis=0)
    rows4 = lambda x: jnp.concatenate([x] * R, axis=0)
    tpos_q = t0 + lax.broadcasted_iota(jnp.int32, (tq, 1), 0)
    row_in = lax.broadcasted_iota(jnp.int32, (R * tq, 1), 0) & (tq - 1)
    col = lax.broadcasted_iota(jnp.int32, (R * tq, tq), 1)

    s = _dot_t(stack(qn_ref), kc_ref[...]) * SCALE
    cblk = lax.broadcasted_iota(jnp.int32, (R * tq, LANE), 1)
    cmask = (CMP_STRIDE * cblk + (CMP_BLOCK - 1) <= t0 + row_in) & (cblk < kc_ref.shape[0] - 1)
    p, _, _ = _softmax_masked(s, cmask)
    o_cmp = _dot(p.astype(BF16), vc_ref[...])
    imp = p[0:tq] + p[tq:2 * tq] + p[2 * tq:3 * tq] + p[3 * tq:4 * tq]
    score = _dot3(imp, cover_ref[...])
    cur = lax.shift_right_arithmetic(tpos_q, SEL_BLOCK.bit_length() - 1)

    @pl.when(t0 + tq <= SEL_TOPK * SEL_BLOCK)
    def _():
        sel_ref[...] = jnp.where(lax.broadcasted_iota(jnp.int32, (tq, LANE), 1) <= cur, 1.0, 0.0).astype(BF16)

    @pl.when(t0 + tq > SEL_TOPK * SEL_BLOCK)
    def _():
        sel_ref[...] = _select_blocks(score, cur, ns).astype(BF16)

    sel = sel_ref[...]
    q_rot = stack(qr_ref)
    blocks_per_tile = tq // SEL_BLOCK
    causal = jnp.where(col <= row_in, 0.0, NEG)
    far = jnp.where(col >= row_in, 0.0, NEG)

    def row_max():
        m_ref[...] = jnp.broadcast_to(jnp.max(m_ref[...], axis=-1, keepdims=True), m_ref.shape)

    def result():
        return acc_ref[...] / jnp.maximum(jnp.sum(l_ref[...], axis=-1, keepdims=True), 1e-30)

    def member_bias(kt):
        key_blk = lax.shift_right_arithmetic(lax.broadcasted_iota(jnp.int32, (LANE, tq), 1),
                                             SEL_BLOCK.bit_length() - 1)
        expand = lax.broadcasted_iota(jnp.int32, (LANE, tq), 0) == kt * blocks_per_tile + key_blk
        member = _dot(sel, jnp.where(expand, 1.0, 0.0).astype(BF16))
        return rows4((member - 1.0) * (-NEG))

    _score_tile(q_rot, ks_ref, qi, qi, member_bias(qi) + causal, s_ref, m_ref, tq, first=True)

    def sel_scores(kt, carry):
        _score_tile(q_rot, ks_ref, kt, kt, member_bias(kt), s_ref, m_ref, tq)
        return carry

    lax.fori_loop(0, qi, sel_scores, 0)
    row_max()
    _value_tile(vs_ref, qi, qi, s_ref, m_ref, l_ref, acc_ref, tq, first=True)

    def sel_values(kt, carry):
        _value_tile(vs_ref, kt, kt, s_ref, m_ref, l_ref, acc_ref, tq)
        return carry

    lax.fori_loop(0, qi, sel_values, 0)
    o_sel = result()

    n_back = NSA_WINDOW // tq
    for back in range(n_back + 1):
        bias = jnp.where(qi >= back, 0.0, NEG)
        if back == 0:
            bias = causal
        elif back == n_back:
            bias = far + bias
        _score_tile(q_rot, kw_ref, jnp.maximum(qi - back, 0), back, bias, s_ref, m_ref, tq, first=back == 0)
    row_max()
    for back in range(n_back + 1):
        _value_tile(vw_ref, jnp.maximum(qi - back, 0), back, s_ref, m_ref, l_ref, acc_ref, tq, first=back == 0)
    o_win = result()

    gates = jax.nn.sigmoid(gl_ref[...])
    for r in range(R):
        rs = slice(r * tq, (r + 1) * tq)
        o = (gates[:, 3 * r:3 * r + 1] * o_cmp[rs] + gates[:, 3 * r + 1:3 * r + 2] * o_sel[rs]
             + gates[:, 3 * r + 2:3 * r + 3] * o_win[rs])
        o_ref[:, r * LANE:(r + 1) * LANE] = o.astype(BF16)


def _nsa_attn(qn, qr, proj, cmp_kv, kvb, cover, tq=256):
    B, T, _ = qn.shape
    nc = cmp_kv.shape[3]
    G = NSA_KV_HEADS
    assert tq % LANE == 0 and NSA_WINDOW % tq == 0 and nc == LANE and T // SEL_BLOCK <= LANE
    rows = NSA_GROUP * tq
    qspec = pl.BlockSpec((None, tq, 4 * LANE), lambda b, g, i: (b, i, g))
    kvspec = lambda c: pl.BlockSpec((None, T, LANE), lambda b, g, i: (b, 0, c * 3 + g))
    return pl.pallas_call(
        functools.partial(_nsa_attn_kernel, tq=tq, ns=T // SEL_BLOCK),
        grid=(B, G, T // tq),
        in_specs=[
            qspec, qspec,
            pl.BlockSpec((None, tq, LANE), lambda b, g, i: (b, i, NSA_GATE_BLK + g)),
            pl.BlockSpec((None, None, None, nc, LANE), lambda b, g, i: (b, 0, g, 0, 0)),
            pl.BlockSpec((None, None, None, nc, LANE), lambda b, g, i: (b, 1, g, 0, 0)),
            kvspec(0), kvspec(1), kvspec(2), kvspec(3),
            pl.BlockSpec((nc, LANE), lambda b, g, i: (0, 0)),
        ],
        out_specs=qspec,
        out_shape=jax.ShapeDtypeStruct((B, T, 1536), BF16),
        scratch_shapes=[pltpu.VMEM((T // tq, rows, tq), F32),
                        pltpu.VMEM((rows, LANE), F32),
                        pltpu.VMEM((rows, LANE), F32),
                        pltpu.VMEM((rows, LANE), F32),
                        pltpu.VMEM((tq, LANE), BF16)],
        compiler_params=_params(("parallel", "parallel", "arbitrary")),
        name="nsa_attn",
    )(qn, qr, proj, cmp_kv, cmp_kv, kvb, kvb, kvb, kvb, cover)


def _dil_band_kernel(q_ref, kp_ref, kc_ref, vp_ref, vc_ref, o_ref, st_ref, *, tq, window):
    i = pl.program_id(2)
    qpos = i * tq + lax.broadcasted_iota(jnp.int32, (tq, 2 * tq), 0)
    kpos = (i - 1) * tq + lax.broadcasted_iota(jnp.int32, (tq, 2 * tq), 1)
    diff = qpos - kpos
    mask = (diff >= 0) & (diff <= window) & (kpos >= 0)
    lane = lax.broadcasted_iota(jnp.int32, (tq, LANE), 1)
    stats = jnp.zeros((tq, LANE), F32)
    for h in range(DIL_HEADS):
        sl = slice(h * LANE, (h + 1) * LANE)
        k = jnp.concatenate([kp_ref[:, sl], kc_ref[:, sl]], axis=0)
        v = jnp.concatenate([vp_ref[:, sl], vc_ref[:, sl]], axis=0)
        p, m, l = _softmax_masked(_dot_t(q_ref[:, sl], k) * SCALE, mask)
        o_ref[:, sl] = _dot(p.astype(BF16), v)
        stats = jnp.where(lane == h, m, stats)
        stats = jnp.where(lane == DIL_HEADS + h, l, stats)
    st_ref[...] = stats


def _dil_band(q, k, v, g, tq=128):
    B, dil, S, _ = q.shape
    window = DIL_PAIRS[g][0] // dil
    assert window <= tq
    cur = lambda w: pl.BlockSpec((None, None, tq, w), lambda b, r, i: (b, r, i, 0))
    prev = pl.BlockSpec((None, None, tq, 4 * LANE), lambda b, r, i: (b, r, jnp.maximum(i - 1, 0), 0))
    return pl.pallas_call(
        functools.partial(_dil_band_kernel, tq=tq, window=window),
        grid=(B, dil, S // tq),
        in_specs=[cur(4 * LANE), prev, cur(4 * LANE), prev, cur(4 * LANE)],
        out_specs=[cur(4 * LANE), cur(LANE)],
        out_shape=[jax.ShapeDtypeStruct((B, dil, S, 4 * LANE), F32),
                   jax.ShapeDtypeStruct((B, dil, S, LANE), F32)],
        compiler_params=_params(("parallel", "parallel", "parallel")),
        name=f"dil_band{g}",
    )(q, k, k, v, v)


def _mix_groups(os_, ms, ls):
    m_all = jnp.maximum(jnp.maximum(ms[0], ms[1]), ms[2])
    ws = [jnp.exp(m - m_all) * l for m, l in zip(ms, ls)]
    tot = ws[0] + ws[1] + ws[2]
    return (ws[0] / tot) * os_[0] + (ws[1] / tot) * os_[1] + (ws[2] / tot) * os_[2]


def _dil_mix_kernel(o0_ref, o1_ref, o2_ref, s0_ref, s1_ref, s2_ref, o_ref, nat_ref):
    o_refs, s_refs = (o0_ref, o1_ref, o2_ref), (s0_ref, s1_ref, s2_ref)
    tm = o_ref.shape[0]

    def token_order(ref, sl):
        dil = ref.shape[0]
        if dil == 1:
            return ref[0, :, sl]
        for r in range(dil):
            nat_ref[pl.ds(r, tm // dil, stride=dil), :] = ref[r, :, sl]
        return nat_ref[...]

    stats = [token_order(s, slice(0, LANE)) for s in s_refs]
    for h in range(DIL_HEADS):
        sl = slice(h * LANE, (h + 1) * LANE)
        ms = [s[:, h:h + 1] for s in stats]
        ls = [s[:, DIL_HEADS + h:DIL_HEADS + h + 1] for s in stats]
        o_ref[:, sl] = _mix_groups([token_order(o, sl) for o in o_refs], ms, ls).astype(BF16)


def _dil_mix(os_, sts, tm=512):
    B, _, T, _ = os_[0].shape
    nt = T // tm
    in_specs = []
    for lanes, arrs in ((4 * LANE, os_), (LANE, sts)):
        for a in arrs:
            dil = a.shape[1]
            in_specs.append(pl.BlockSpec((None, dil, tm // dil, lanes), lambda b, i: (b, 0, i, 0)))
    return pl.pallas_call(
        _dil_mix_kernel,
        grid=(B, nt),
        in_specs=in_specs,
        out_specs=pl.BlockSpec((tm, 4 * LANE), lambda b, i: (b * nt + i, 0)),
        out_shape=jax.ShapeDtypeStruct((B * T, 4 * LANE), BF16),
        scratch_shapes=[pltpu.VMEM((tm, LANE), F32)],
        compiler_params=_params(("parallel", "parallel")),
        name="dil_mix",
    )(*os_, *sts)


DEC_PAGES_PER_STEP = 16


def _dec_select_kernel(tbl_ref, *refs, n_pp, n_steps):
    pages = refs[:n_pp]
    (qn_ref, w1_ref, b1_ref, w2_ref, kcg_ref, cover_ref, idx_ref, ocmp_ref, h_ref) = refs[n_pp:]
    j = pl.program_id(1)
    cpp = PAGE_SIZE // CMP_STRIDE
    rows = n_pp * cpp
    n = n_steps * rows
    ns = cover_ref.shape[1]

    for g in range(NSA_KV_HEADS):
        for kv in range(2):
            x = jnp.concatenate(
                [_get_rows(pg, g * 4 + kv, NSA_ROWS, PAGE_SIZE).reshape(cpp, CMP_STRIDE * LANE) for pg in pages],
                axis=0)
            h_ref[g * 2 + kv, pl.ds(pl.multiple_of(j * rows, rows), rows), :] = _dot(x.astype(BF16), w1_ref[kv])

    @pl.when(j == n_steps - 1)
    def _():
        idx_ref[...] = jnp.zeros_like(idx_ref)
        for g in range(NSA_KV_HEADS):
            kc = _rms(_compress_finish(h_ref[g * 2], b1_ref[0], w2_ref[0]), kcg_ref[...]).astype(BF16)
            vc = _compress_finish(h_ref[g * 2 + 1], b1_ref[1], w2_ref[1]).astype(BF16)
            q = _rows16(qn_ref[:, g * NSA_GROUP * LANE:(g + 1) * NSA_GROUP * LANE], NSA_GROUP).astype(BF16)
            s = _dot_t(q, kc) * SCALE
            valid = lax.broadcasted_iota(jnp.int32, s.shape, 1) < n - 1
            p, _, _ = _softmax_masked(s, valid)
            ocmp_ref[g] = _dot(p.astype(BF16), vc)[0:NSA_GROUP, :]
            rid = lax.broadcasted_iota(jnp.int32, p.shape, 0)
            imp = jnp.sum(jnp.where(rid < NSA_GROUP, p, 0.0), axis=0, keepdims=True)
            score = _dot3(jnp.broadcast_to(imp, (8, n)), cover_ref[...])

            a = jnp.broadcast_to(score[0:1, :], (ns, ns))
            lane = lax.broadcasted_iota(jnp.int32, (ns, ns), 1)
            sub = lax.broadcasted_iota(jnp.int32, (ns, ns), 0)
            cur = n * CMP_STRIDE // SEL_BLOCK
            forced = (lane == 0) | (lane == cur) | (lane == cur - 1)
            a = jnp.where(lane <= cur, jnp.where(forced, FORCE_SCORE, a), NEG)
            at = a.T
            ahead_r = (at > a) | ((at == a) & (sub < lane))
            chosen_r = ((jnp.sum(jnp.where(ahead_r, 1.0, 0.0), axis=0, keepdims=True) < SEL_TOPK)
                        & (a[0:1, :] > 0.5 * NEG))
            ahead_c = (a > at) | ((a == at) & (lane < sub))
            chosen_c = ((jnp.sum(jnp.where(ahead_c, 1.0, 0.0), axis=1, keepdims=True) < SEL_TOPK)
                        & (at[:, 0:1] > 0.5 * NEG))
            before = jnp.sum(jnp.where(chosen_r & (lane < sub), 1.0, 0.0), axis=1, keepdims=True)
            slot = lax.broadcasted_iota(jnp.int32, (ns, LANE), 1)
            onehot = chosen_c & (before == slot.astype(F32))
            blk = lax.broadcasted_iota(jnp.int32, (ns, LANE), 0)
            picked = jnp.sum(jnp.where(onehot, blk.astype(F32), 0.0), axis=0, keepdims=True)
            filled = jnp.sum(jnp.where(onehot, 1.0, 0.0), axis=0, keepdims=True)
            idx_ref[g:g + 1, :] = jnp.where(filled > 0.5, picked, float(cur)).astype(jnp.int32)


def _dec_select(cache, table, qn, w1r, b1, w2, kc_g, cover):
    B, n_pages = table.shape
    n_pp = DEC_PAGES_PER_STEP
    n_steps = n_pages // n_pp
    n = n_pages * (PAGE_SIZE // CMP_STRIDE)
    ns = cover.shape[1]
    G = NSA_KV_HEADS
    const = lambda *shape: pl.BlockSpec(shape, lambda b, j, tbl: (0,) * len(shape))
    page_spec = lambda p: pl.BlockSpec((None, PAGE_SIZE * NSA_ROWS, LANE), lambda b, j, tbl: (tbl[b, j * n_pp + p], 0, 0))
    grid_spec = pltpu.PrefetchScalarGridSpec(
        num_scalar_prefetch=1,
        grid=(B, n_steps),
        in_specs=[page_spec(p) for p in range(n_pp)] + [
            pl.BlockSpec((None, 1, 1536), lambda b, j, tbl: (b, 0, 0)),
            const(2, CMP_STRIDE * LANE, 2 * LANE), const(2, 1, LANE), const(2, LANE, LANE), const(1, LANE),
            const(n, ns),
        ],
        out_specs=[pl.BlockSpec((None, 8, LANE), lambda b, j, tbl: (b, 0, 0)),
                   pl.BlockSpec((None, G, NSA_GROUP, LANE), lambda b, j, tbl: (b, 0, 0, 0))],
        scratch_shapes=[pltpu.VMEM((2 * G, n, 2 * LANE), F32)],
    )
    return pl.pallas_call(
        functools.partial(_dec_select_kernel, n_pp=n_pp, n_steps=n_steps),
        grid_spec=grid_spec,
        out_shape=[jax.ShapeDtypeStruct((B, 8, LANE), jnp.int32),
                   jax.ShapeDtypeStruct((B, G, NSA_GROUP, LANE), F32)],
        compiler_params=_params(("arbitrary", "arbitrary")),
        name="dec_select",
    )(table, *([cache] * n_pp), qn, w1r, b1, w2, kc_g, cover)


def _dec_attn_kernel(tbl_ref, idx_ref, *refs, n_sel, cur, wb):
    blocks = refs[:n_sel]
    (qr_ref, kvn_ref, win_ref, gl_ref, ocmp_ref, o_ref) = refs[n_sel:]
    b, g = pl.program_id(0), pl.program_id(1)
    R = NSA_GROUP
    q = _rows16(qr_ref[...], R).astype(BF16)
    qf = q.astype(F32)
    new = kvn_ref[...]
    ks_n, vs_n, kw_n, vw_n = (new[:, c * LANE:(c + 1) * LANE] for c in range(4))

    k = jnp.concatenate([_get_rows(r, g * 4 + 2, NSA_ROWS, SEL_BLOCK) for r in blocks], axis=0).astype(BF16)
    v = jnp.concatenate([_get_rows(r, g * 4 + 3, NSA_ROWS, SEL_BLOCK) for r in blocks], axis=0).astype(BF16)
    s = _dot_t(q, k) * SCALE
    blk_of = lax.shift_right_arithmetic(lax.broadcasted_iota(jnp.int32, s.shape, 1), SEL_BLOCK.bit_length() - 1)
    valid = jnp.zeros(s.shape, jnp.int32)
    for n in range(n_sel):
        is_past = jnp.where(idx_ref[(b * NSA_KV_HEADS + g) * n_sel + n] != cur, 1, 0)
        valid = jnp.where(blk_of == n, is_past, valid)
    valid = valid > 0
    s_new = jnp.sum(qf * ks_n, axis=-1, keepdims=True) * SCALE
    s = jnp.where(valid, s, NEG)
    m = jnp.maximum(jnp.max(s, axis=-1, keepdims=True), s_new)
    e = jnp.where(valid, jnp.exp(s - m), 0.0)
    e_new = jnp.exp(s_new - m)
    l = jnp.sum(e, axis=-1, keepdims=True) + e_new
    o_sel = (_dot(e.astype(BF16), v) + e_new.astype(BF16).astype(F32) * vs_n) / l

    s = _dot_t(q, _get_rows(win_ref, g * 2, WIN_ROWS, wb).astype(BF16)) * SCALE
    s_new = jnp.sum(qf * kw_n, axis=-1, keepdims=True) * SCALE
    m = jnp.maximum(jnp.max(s, axis=-1, keepdims=True), s_new)
    e = jnp.exp(s - m)
    e_new = jnp.exp(s_new - m)
    l = jnp.sum(e, axis=-1, keepdims=True) + e_new
    o_win = (_dot(e.astype(BF16), _get_rows(win_ref, g * 2 + 1, WIN_ROWS, wb).astype(BF16))
             + e_new.astype(BF16).astype(F32) * vw_n) / l

    gates = jax.nn.sigmoid(gl_ref[...])
    o_cmp = ocmp_ref[...]
    for r in range(R):
        o = (gates[:, 3 * r:3 * r + 1] * o_cmp[r:r + 1] + gates[:, 3 * r + 1:3 * r + 2] * o_sel[r:r + 1]
             + gates[:, 3 * r + 2:3 * r + 3] * o_win[r:r + 1])
        o_ref[:, r * LANE:(r + 1) * LANE] = o


def _dec_attn(cache, table, idx, qr, kvb, win_state, proj, ocmp):
    B, n_pages = table.shape
    n_sel = SEL_TOPK
    cur = n_pages * PAGE_SIZE // SEL_BLOCK
    wb = win_state.shape[1] // WIN_ROWS
    assert wb <= NSA_WINDOW
    halves = PAGE_SIZE // SEL_BLOCK

    def blk_spec(n):
        def imap(b, g, tbl, idx):
            i = jnp.minimum(idx[(b * NSA_KV_HEADS + g) * n_sel + n], cur - 1)
            return (tbl[b, i // halves], i % halves, 0)
        return pl.BlockSpec((None, SEL_BLOCK * NSA_ROWS, LANE), imap)

    grid_spec = pltpu.PrefetchScalarGridSpec(
        num_scalar_prefetch=2,
        grid=(B, NSA_KV_HEADS),
        in_specs=[blk_spec(n) for n in range(n_sel)] + [
            pl.BlockSpec((None, 1, 4 * LANE), lambda b, g, tbl, idx: (b, 0, g)),
            pl.BlockSpec((None, 1, 4 * LANE), lambda b, g, tbl, idx: (b, 0, g)),
            pl.BlockSpec((None, wb * WIN_ROWS, LANE), lambda b, g, tbl, idx: (b, 0, 0)),
            pl.BlockSpec((None, 1, LANE), lambda b, g, tbl, idx: (b, 0, NSA_GATE_BLK + g)),
            pl.BlockSpec((None, None, NSA_GROUP, LANE), lambda b, g, tbl, idx: (b, g, 0, 0)),
        ],
        out_specs=pl.BlockSpec((None, 1, 4 * LANE), lambda b, g, tbl, idx: (b, 0, g)),
    )
    kvn = kvb.reshape(B, 1, 4, NSA_KV_HEADS, LANE).transpose(0, 1, 3, 2, 4).reshape(B, 1, 1536)
    return pl.pallas_call(
        functools.partial(_dec_attn_kernel, n_sel=n_sel, cur=cur, wb=wb),
        grid_spec=grid_spec,
        out_shape=jax.ShapeDtypeStruct((B, 1, 1536), F32),
        compiler_params=_params(("arbitrary", "arbitrary")),
        name="dec_attn",
    )(table, idx, *([cache] * n_sel), qr, kvn, win_state, proj, ocmp)


def _dec_dil_kernel(q_ref, n0_ref, n1_ref, n2_ref, s0_ref, s1_ref, s2_ref, o_ref, t0_ref, t1_ref, t2_ref):
    qall = q_ref[...]
    states, news, outs = (s0_ref, s1_ref, s2_ref), (n0_ref, n1_ref, n2_ref), (t0_ref, t1_ref, t2_ref)

    for st, new, out in zip(states, news, outs):
        keep = st.shape[0] - KVH_ROWS
        out[0:keep, :] = st[KVH_ROWS:, :]
        out[keep:, :] = new[...]

    for h in range(DIL_HEADS):
        os_, ms, ls = [], [], []
        for g, st in enumerate(states):
            hs = slice((g * DIL_HEADS + h) * LANE, (g * DIL_HEADS + h + 1) * LANE)
            window, dil = DIL_PAIRS[g]
            kn = news[g][h:h + 1, :].astype(BF16).astype(F32)
            vn = news[g][DIL_HEADS + h:DIL_HEADS + h + 1, :].astype(BF16).astype(F32)
            q = _rows16(qall[:, hs], 1).astype(BF16)
            k = _get_rows(st, h, KVH_ROWS * dil, window // dil).astype(BF16)
            v = _get_rows(st, DIL_HEADS + h, KVH_ROWS * dil, window // dil).astype(BF16)
            s = _dot_t(q, k) * SCALE
            s_new = jnp.sum(q.astype(F32) * kn, axis=-1, keepdims=True) * SCALE
            m = jnp.maximum(jnp.max(s, axis=-1, keepdims=True), s_new)
            e = jnp.exp(s - m)
            e_new = jnp.exp(s_new - m)
            l = jnp.sum(e, axis=-1, keepdims=True) + e_new
            ln = jnp.maximum(l, 1e-30)
            o = _dot((e / ln).astype(BF16), v) + (e_new / ln).astype(BF16).astype(F32) * vn
            os_.append(o)
            ms.append(m)
            ls.append(l)
        o_ref[:, h * LANE:(h + 1) * LANE] = _mix_groups(os_, ms, ls)[0:1, :]


def _dec_dil(qr, news, states):
    B = qr.shape[0]
    in_specs = [pl.BlockSpec((None, 1, 1536), lambda b: (b, 0, 0))]
    in_specs += [pl.BlockSpec((None, KVH_ROWS, LANE), lambda b: (b, 0, 0))] * len(DIL_PAIRS)
    st_specs = []
    for g, (window, dil) in enumerate(DIL_PAIRS):
        assert states[g].shape[1] == window * KVH_ROWS, "rolling buffer shorter than the window is not supported"
        st_specs.append(pl.BlockSpec((None, window * KVH_ROWS, LANE), lambda b: (b, 0, 0)))
    return pl.pallas_call(
        _dec_dil_kernel,
        grid=(B,),
        in_specs=in_specs + st_specs,
        out_specs=[pl.BlockSpec((None, 1, 4 * LANE), lambda b: (b, 0, 0))] + st_specs,
        out_shape=[jax.ShapeDtypeStruct((B, 1, 4 * LANE), F32)]
        + [jax.ShapeDtypeStruct(s.shape, F32) for s in states],
        compiler_params=_params(("parallel",)),
        name="dec_dil",
    )(qr, *news, *states)


def _rope_tables(pos):
    half = HEAD_DIM // 2
    inv = ROPE_THETA ** (-jnp.arange(half, dtype=F32) / half)
    ang = pos.astype(F32)[:, None] * inv
    cos, sin = jnp.cos(ang), jnp.sin(ang)
    return jnp.concatenate([cos, cos], axis=-1), jnp.concatenate([-sin, sin], axis=-1)


def _cover(nc, ns, rows, cols):
    c0 = jnp.arange(nc)[:, None] * CMP_STRIDE
    s0 = jnp.arange(ns)[None, :] * SEL_BLOCK
    cover = jnp.clip(jnp.minimum(c0 + CMP_BLOCK, s0 + SEL_BLOCK) - jnp.maximum(c0, s0), 0, CMP_BLOCK)
    cover = cover.astype(F32) / CMP_BLOCK
    return jnp.pad(cover, ((0, rows - nc), (0, cols - ns))).astype(BF16)


def _pad_gains(*gs):
    return jnp.pad(jnp.stack(gs, axis=0), ((0, 8 - len(gs)), (0, 0)))


def kernel(x_prompt, x_sample, mem_prompt, cache_nsa_kv, page_table, state_nsa_win, state_dil_0, state_dil_1,
           state_dil_2, cache_mem_kv, ff_norm, ff_w_gate, ff_w_up, ff_w_down, mix_norm, mem_norm, w_mem_kv,
           mem_q_g, mem_k_g, nsa_w_in, nsa_q_g, nsa_kc_g, nsa_ks_g, nsa_kw_g, nsa_cmp_w1, nsa_cmp_b1, nsa_cmp_w2,
           nsa_w_out, dil_w_in, dil_q_g, dil_k_g, dil_w_out):
    B, T, D = x_prompt.shape
    Bs = x_sample.shape[0]
    assert x_sample.shape[1] == 1, "the sample group is a single-token decode step"
    n_pages = page_table.shape[1]
    past_len = n_pages * PAGE_SIZE
    H, G, d = N_MIX_HEADS, NSA_KV_HEADS, HEAD_DIM

    wg, wu, wd = ff_w_gate, ff_w_up, ff_w_down
    ffg = ff_norm.reshape(ff_norm.shape[0], 2, 1, D)
    gate_w = jnp.pad(nsa_w_in[:, H * d:H * d + 3 * H].reshape(D, G, 3 * NSA_GROUP), ((0, 0), (0, 0), (0, LANE - 12)))
    nsa_w = jnp.concatenate([nsa_w_in[:, :H * d], nsa_w_in[:, H * d + 3 * H:], gate_w.reshape(D, G * LANE)], axis=1)
    nsa_w = jnp.pad(nsa_w, ((0, 0), (0, PROJ_N - nsa_w.shape[1]))).astype(BF16)
    dil_w = dil_w_in.astype(BF16)
    nsa_wo, dil_wo = nsa_w_out.astype(BF16), dil_w_out.astype(BF16)
    w1r = nsa_cmp_w1.reshape(2, 2, CMP_STRIDE, d, d).transpose(0, 2, 3, 1, 4).reshape(2, CMP_STRIDE * d, 2 * d)
    w1r = w1r.astype(BF16)
    cmp_b1 = nsa_cmp_b1.reshape(2, 1, d)
    cmp_w2 = nsa_cmp_w2.astype(BF16)
    kc_g = nsa_kc_g.reshape(1, d)

    n_pool = cache_nsa_kv.shape[0]
    cache_rows = cache_nsa_kv.transpose(0, 1, 3, 2, 4).reshape(n_pool, PAGE_SIZE * NSA_ROWS, LANE)
    win_rows = state_nsa_win.transpose(0, 1, 3, 2, 4).reshape(Bs, state_nsa_win.shape[1] * WIN_ROWS, LANE)
    dil_states = (state_dil_0, state_dil_1, state_dil_2)
    dil_rows = [s.reshape(Bs, s.shape[1] * KVH_ROWS, LANE) for s in dil_states]
    mem_rows_s = cache_mem_kv.reshape(2, Bs, N_MEM * KVH_ROWS, LANE)

    mem2d = mem_prompt.reshape(B * N_MEM, D)
    mem_rows_p = []
    for i in range(2):
        kv = _norm_matmul(mem2d, mem_norm[i], w_mem_kv[i].astype(BF16), tm=256, tn=1024)
        mem_rows_p.append(_memkv_post(kv, mem_k_g[i]).reshape(B, N_MEM * KVH_ROWS, LANE))

    cos_p, sin_p = _rope_tables(jnp.tile(jnp.arange(T, dtype=jnp.int32), B))
    cos_s, sin_s = _rope_tables(jnp.full((Bs,), past_len, jnp.int32))
    row3 = lambda a: a.astype(F32).reshape(Bs, 1, a.shape[-1])

    xs, *wb = _ffn_cast(x_sample.reshape(Bs, D), ffg, wg, wu, wd, 0, 0)
    xp = _ffn(x_prompt.reshape(B * T, D), ffg, *wb, 0, 0, tm=512)
    nsa_gains = _pad_gains(nsa_q_g, nsa_ks_g, nsa_kw_g, mem_q_g[0])

    proj_p = _norm_matmul(xp, mix_norm[0], nsa_w, tm=1024, tn=1024)
    qn_p, qr_p, cmp_p, rows_p, win_p, kvb_p, mq_p = _nsa_post(proj_p, cos_p, sin_p, nsa_gains, tm=256)
    cmp_kv = _cmp_prompt(cmp_p.reshape(B, T, 768), w1r, cmp_b1, cmp_w2, kc_g)
    nc_p = T // CMP_STRIDE
    cover_p = _cover(nc_p - 1, T // SEL_BLOCK, nc_p, LANE)
    o_mix_p = _nsa_attn(qn_p.reshape(B, T, 1536), qr_p.reshape(B, T, 1536), proj_p.reshape(B, T, PROJ_N),
                        cmp_kv, kvb_p.reshape(B, T, 1536), cover_p)
    o_mem_p = _mem_attn(mq_p.reshape(B, T, 512), mem_rows_p[0], tq=512)
    xp = _out_proj(xp, o_mix_p.reshape(B * T, 1536), o_mem_p.reshape(B * T, 512), nsa_wo, tm=1024)

    proj_s = _norm_matmul(xs, mix_norm[0], nsa_w, tm=Bs, tn=1024)
    qn_s, qr_s, _, rows_s, win_s, kvb_s, mq_s = _nsa_post(proj_s, cos_s, sin_s, nsa_gains, tm=Bs)
    nc_s = past_len // CMP_STRIDE
    ns_s = -(-(past_len + 1) // SEL_BLOCK)
    cover_s = _cover(nc_s - 1, ns_s, nc_s, -(-ns_s // LANE) * LANE)
    sel_idx, ocmp_s = _dec_select(cache_rows, page_table, row3(qn_s), w1r, cmp_b1, cmp_w2, kc_g, cover_s)
    o_mix_s = _dec_attn(cache_rows, page_table, sel_idx[:, :G, :SEL_TOPK].reshape(-1), row3(qr_s), row3(kvb_s),
                        win_rows, proj_s.reshape(Bs, 1, PROJ_N), ocmp_s)
    o_mem_s = _mem_attn(row3(mq_s), mem_rows_s[0], tq=1)
    xs = _out_proj(xs, o_mix_s.reshape(Bs, 1536).astype(BF16), o_mem_s.reshape(Bs, 512).astype(BF16), nsa_wo,
                   tm=Bs)

    xs, *wb = _ffn_cast(xs, ffg, wg, wu, wd, 0, 1)
    xp = _ffn(xp, ffg, *wb, 0, 1, tm=512)

    xs, *wb = _ffn_cast(xs, ffg, wg, wu, wd, 1, 0)
    xp = _ffn(xp, ffg, *wb, 1, 0, tm=512)
    dil_gains = _pad_gains(dil_q_g, dil_k_g, mem_q_g[1])

    dproj_p = _norm_matmul(xp, mix_norm[1], dil_w, tm=1024, tn=1024)
    dmq_p, *rest = _dil_post(dproj_p, cos_p, sin_p, dil_gains, tm=256, seq_len=T)
    dnew_p, dqkv_p = rest[:3], rest[3:]
    band = [_dil_band(*dqkv_p[3 * g:3 * g + 3], g) for g in range(len(DIL_PAIRS))]
    o_dil_p = _dil_mix([o for o, _ in band], [s for _, s in band])
    o_dmem_p = _mem_attn(dmq_p.reshape(B, T, 512), mem_rows_p[1], tq=512)
    xp = _out_proj(xp, o_dil_p, o_dmem_p.reshape(B * T, 512), dil_wo, tm=1024)

    dproj_s = _norm_matmul(xs, mix_norm[1], dil_w, tm=Bs, tn=1024)
    dq_s, dmq_s, *dnew_s = _dil_post(dproj_s, cos_s, sin_s, dil_gains, tm=Bs)
    o_dil_s, *dil_rows_out = _dec_dil(row3(dq_s), [s.reshape(Bs, KVH_ROWS, LANE) for s in dnew_s], dil_rows)
    o_dmem_s = _mem_attn(row3(dmq_s), mem_rows_s[1], tq=1)
    xs = _out_proj(xs, o_dil_s.reshape(Bs, 512).astype(BF16), o_dmem_s.reshape(Bs, 512).astype(BF16), dil_wo,
                   tm=Bs)

    xs, *wb = _ffn_cast(xs, ffg, wg, wu, wd, 1, 1)
    xp = _ffn(xp, ffg, *wb, 1, 1, tm=512)

    unrow = lambda a, n, outer, inner: a.reshape(n, -1, outer, inner, d).transpose(0, 1, 3, 2, 4)
    nsa_kv_p = unrow(rows_p, B, G, 4)
    nsa_kv_s = unrow(rows_s, Bs, G, 4)
    nsa_win_p = unrow(win_p, B, G, 2)[:, -min(NSA_WINDOW, T):]
    nsa_win_s = jnp.concatenate([state_nsa_win, unrow(win_s, Bs, G, 2)], axis=1)[:, -state_nsa_win.shape[1]:]
    outs_dil = []
    for g, (window, _) in enumerate(DIL_PAIRS):
        st = dil_states[g]
        outs_dil.append(dnew_p[g].reshape(B, T, 2, DIL_HEADS, d)[:, -min(window, T):])
        outs_dil.append(dil_rows_out[g].reshape(st.shape))
    mem_kv_out = jnp.stack([kv.reshape(B, N_MEM, 2, N_MEM_HEADS, d) for kv in mem_rows_p], axis=0)
    return (xp.reshape(B, T, D), xs.reshape(Bs, 1, D), nsa_kv_p, nsa_kv_s, nsa_win_p, nsa_win_s,
            *outs_dil, mem_kv_out)
```

```python
import functools

import jax
import jax.numpy as jnp
from jax import lax
from jax.experimental import pallas as pl
from jax.experimental.pallas import tpu as pltpu

F32 = jnp.float32
BF16 = jnp.bfloat16

D_MODEL = 2048
HEAD_DIM = 128
N_MIX_HEADS = 12
N_MEM_HEADS = 4
N_MEM = 256
NSA_KV_HEADS = 3
NSA_GROUP = 4
CMP_BLOCK = 32
CMP_STRIDE = 16
SEL_BLOCK = 64
SEL_TOPK = 16
NSA_WINDOW = 512
DIL_PAIRS = ((128, 1), (512, 4), (2048, 16))
DIL_HEADS = 4
PAGE_SIZE = 128
ROPE_THETA = 10000.0
EPS = 1e-6
SCALE = HEAD_DIM ** -0.5
NEG = -1e30
FORCE_SCORE = 1e6

PROJ_N = 5120
NSA_GATE_BLK = 34
LANE = 128
VMEM_LIMIT = 56 * 1024 * 1024


def _params(sem):
    return pltpu.CompilerParams(dimension_semantics=sem, vmem_limit_bytes=VMEM_LIMIT)


def _dot(a, b):
    return jnp.dot(a, b, preferred_element_type=F32)


def _dot_t(a, b):
    return lax.dot_general(a, b, (((1,), (1,)), ((), ())), preferred_element_type=F32)


def _dot3(a, b):
    a1 = a.astype(BF16)
    r1 = a - a1.astype(F32)
    a2 = r1.astype(BF16)
    a3 = (r1 - a2.astype(F32)).astype(BF16)
    return _dot(a1, b) + _dot(a2, b) + _dot(a3, b)


def _rms(x, g):
    return x * lax.rsqrt(jnp.mean(x * x, axis=-1, keepdims=True) + EPS) * g


def _rope(x, cos, sin):
    return x * cos + pltpu.roll(x, HEAD_DIM // 2, 1) * sin


def _rows16(row, nrep):
    rid = lax.broadcasted_iota(jnp.int32, (16, LANE), 0) & (nrep - 1)
    out = jnp.zeros((16, LANE), F32)
    for r in range(nrep):
        piece = jnp.broadcast_to(row[:, r * LANE:(r + 1) * LANE], (16, LANE))
        out = jnp.where(rid == r, piece, out)
    return out


def _softmax_masked(s, mask):
    s = jnp.where(mask, s, NEG)
    m = jnp.max(s, axis=-1, keepdims=True)
    e = jnp.where(mask, jnp.exp(s - m), 0.0)
    l = jnp.sum(e, axis=-1, keepdims=True)
    return e / jnp.maximum(l, 1e-30), m, l


def _ffn_step(f, nf, x_ref, g_ref, wg_ref, wu_ref, wd_ref, o_ref, h_ref, acc_ref):
    @pl.when(f == 0)
    def _():
        h_ref[...] = _rms(x_ref[...], g_ref[...]).astype(BF16)
        acc_ref[...] = jnp.zeros_like(acc_ref)

    h = h_ref[...]
    gate = _dot(h, wg_ref[...])
    up = _dot(h, wu_ref[...])
    a = (gate * jax.nn.sigmoid(gate) * up).astype(BF16)
    acc_ref[...] += _dot(a, wd_ref[...])

    @pl.when(f == nf - 1)
    def _():
        o_ref[...] = x_ref[...] + 0.5 * acc_ref[...]


def _ffn_kernel(x_ref, g_ref, wg_ref, wu_ref, wd_ref, o_ref, h_ref, *, nf):
    _ffn_step(pl.program_id(1), nf, x_ref, g_ref, wg_ref, wu_ref, wd_ref, o_ref, h_ref, o_ref)


def _ffn(x, g, wg, wu, wd, li, lj, tm, tf=512):
    M, D = x.shape
    F = wg.shape[-1]
    nf = F // tf
    return pl.pallas_call(
        functools.partial(_ffn_kernel, nf=nf),
        grid=(M // tm, nf),
        in_specs=[
            pl.BlockSpec((tm, D), lambda i, f: (i, 0)),
            pl.BlockSpec((None, None, 1, D), lambda i, f: (li, lj, 0, 0)),
            pl.BlockSpec((D, tf), lambda i, f: (0, f)),
            pl.BlockSpec((D, tf), lambda i, f: (0, f)),
            pl.BlockSpec((tf, D), lambda i, f: (f, 0)),
        ],
        out_specs=pl.BlockSpec((tm, D), lambda i, f: (i, 0)),
        out_shape=jax.ShapeDtypeStruct((M, D), F32),
        scratch_shapes=[pltpu.VMEM((tm, D), BF16)],
        compiler_params=_params(("parallel", "arbitrary")),
        name="ffn",
    )(x, g, wg, wu, wd)


def _ffn_cast_kernel(x_ref, g_ref, wg_ref, wu_ref, wd_ref, o_ref, wgb_ref, wub_ref, wdb_ref, h_ref, *, nf):
    wgb_ref[...] = wg_ref[...].astype(BF16)
    wub_ref[...] = wu_ref[...].astype(BF16)
    wdb_ref[...] = wd_ref[...].astype(BF16)
    _ffn_step(pl.program_id(0), nf, x_ref, g_ref, wgb_ref, wub_ref, wdb_ref, o_ref, h_ref, o_ref)


def _ffn_cast(x, g, wg, wu, wd, li, lj, tf=512):
    M, D = x.shape
    F = wg.shape[-1]
    nf = F // tf
    return pl.pallas_call(
        functools.partial(_ffn_cast_kernel, nf=nf),
        grid=(nf,),
        in_specs=[
            pl.BlockSpec((M, D), lambda f: (0, 0)),
            pl.BlockSpec((None, None, 1, D), lambda f: (li, lj, 0, 0)),
            pl.BlockSpec((None, None, D, tf), lambda f: (li, lj, 0, f)),
            pl.BlockSpec((None, None, D, tf), lambda f: (li, lj, 0, f)),
            pl.BlockSpec((None, None, tf, D), lambda f: (li, lj, f, 0)),
        ],
        out_specs=[pl.BlockSpec((M, D), lambda f: (0, 0)),
                   pl.BlockSpec((D, tf), lambda f: (0, f)),
                   pl.BlockSpec((D, tf), lambda f: (0, f)),
                   pl.BlockSpec((tf, D), lambda f: (f, 0))],
        out_shape=[jax.ShapeDtypeStruct((M, D), F32),
                   jax.ShapeDtypeStruct((D, F), BF16),
                   jax.ShapeDtypeStruct((D, F), BF16),
                   jax.ShapeDtypeStruct((F, D), BF16)],
        scratch_shapes=[pltpu.VMEM((M, D), BF16)],
        compiler_params=_params(("arbitrary",)),
        name="ffn_cast",
    )(x, g, wg, wu, wd)


def _nmm_kernel(x_ref, g_ref, w_ref, o_ref, h_ref):
    @pl.when(pl.program_id(1) == 0)
    def _():
        h_ref[...] = _rms(x_ref[...], g_ref[...]).astype(BF16)

    o_ref[...] = _dot(h_ref[...], w_ref[...])


def _norm_matmul(x, g, w, tm, tn):
    M, D = x.shape
    N = w.shape[1]
    return pl.pallas_call(
        _nmm_kernel,
        grid=(M // tm, N // tn),
        in_specs=[
            pl.BlockSpec((tm, D), lambda i, j: (i, 0)),
            pl.BlockSpec((1, D), lambda i, j: (0, 0)),
            pl.BlockSpec((D, tn), lambda i, j: (0, j)),
        ],
        out_specs=pl.BlockSpec((tm, tn), lambda i, j: (i, j)),
        out_shape=jax.ShapeDtypeStruct((M, N), F32),
        scratch_shapes=[pltpu.VMEM((tm, D), BF16)],
        compiler_params=_params(("parallel", "arbitrary")),
        name="norm_matmul",
    )(x, g.reshape(1, D), w)


def _oproj_kernel(x_ref, a_ref, b_ref, w_ref, o_ref, *, ka):
    o_ref[...] = x_ref[...] + _dot(a_ref[...], w_ref[:ka, :]) + _dot(b_ref[...], w_ref[ka:, :])


def _out_proj(x, a, b, w, tm, tn=1024):
    M, D = x.shape
    ka, kb = a.shape[1], b.shape[1]
    return pl.pallas_call(
        functools.partial(_oproj_kernel, ka=ka),
        grid=(M // tm, D // tn),
        in_specs=[
            pl.BlockSpec((tm, tn), lambda i, j: (i, j)),
            pl.BlockSpec((tm, ka), lambda i, j: (i, 0)),
            pl.BlockSpec((tm, kb), lambda i, j: (i, 0)),
            pl.BlockSpec((ka + kb, tn), lambda i, j: (0, j)),
        ],
        out_specs=pl.BlockSpec((tm, tn), lambda i, j: (i, j)),
        out_shape=jax.ShapeDtypeStruct((M, D), F32),
        compiler_params=_params(("parallel", "parallel")),
        name="out_proj",
    )(x, a, b, w)


def _put_rows(ref, row, rows_per_token, val):
    ref[pl.ds(row, val.shape[0], stride=rows_per_token), :] = val


def _get_rows(ref, row, rows_per_token, n):
    return ref[pl.ds(row, n, stride=rows_per_token), :]


NSA_ROWS = 4 * NSA_KV_HEADS
WIN_ROWS = 2 * NSA_KV_HEADS
KVH_ROWS = 2 * DIL_HEADS


def _nsa_post_kernel(p_ref, cos_ref, sin_ref, g_ref, qn_ref, qr_ref, cmp_ref, rows_ref, win_ref, kvb_ref, mq_ref):
    cos, sin = cos_ref[...], sin_ref[...]
    q_g, ks_g, kw_g, mq_g = g_ref[0:1, :], g_ref[1:2, :], g_ref[2:3, :], g_ref[3:4, :]

    def tile(i):
        return p_ref[:, i * LANE:(i + 1) * LANE]

    for h in range(N_MIX_HEADS):
        qn = _rms(tile(h), q_g)
        qn_ref[:, h * LANE:(h + 1) * LANE] = qn.astype(BF16)
        qr_ref[:, h * LANE:(h + 1) * LANE] = _rope(qn, cos, sin).astype(BF16)
    for g in range(NSA_KV_HEADS):
        kc, vc = tile(12 + g), tile(15 + g)
        ks = _rope(_rms(tile(18 + g), ks_g), cos, sin)
        vs = tile(21 + g)
        kw = _rope(_rms(tile(24 + g), kw_g), cos, sin)
        vw = tile(27 + g)
        for c, val in enumerate((kc, vc)):
            cmp_ref[:, (c * 3 + g) * LANE:(c * 3 + g + 1) * LANE] = val
        for c, val in enumerate((kc, vc, ks, vs)):
            _put_rows(rows_ref, g * 4 + c, NSA_ROWS, val)
        for c, val in enumerate((kw, vw)):
            _put_rows(win_ref, g * 2 + c, WIN_ROWS, val)
        for c, val in enumerate((ks, vs, kw, vw)):
            kvb_ref[:, (c * 3 + g) * LANE:(c * 3 + g + 1) * LANE] = val.astype(BF16)
    for h in range(N_MEM_HEADS):
        mq_ref[:, h * LANE:(h + 1) * LANE] = _rms(tile(30 + h), mq_g).astype(BF16)


def _nsa_post(p, cos, sin, gains, tm):
    M = p.shape[0]
    row = lambda n: pl.BlockSpec((tm, n), lambda i: (i, 0))
    flat = lambda r: pl.BlockSpec((tm * r, LANE), lambda i: (i, 0))
    return pl.pallas_call(
        _nsa_post_kernel,
        grid=(M // tm,),
        in_specs=[row(PROJ_N), row(LANE), row(LANE), pl.BlockSpec((8, LANE), lambda i: (0, 0))],
        out_specs=[row(1536), row(1536), row(768), flat(NSA_ROWS), flat(WIN_ROWS), row(1536), row(512)],
        out_shape=[
            jax.ShapeDtypeStruct((M, 1536), BF16),
            jax.ShapeDtypeStruct((M, 1536), BF16),
            jax.ShapeDtypeStruct((M, 768), F32),
            jax.ShapeDtypeStruct((M * NSA_ROWS, LANE), F32),
            jax.ShapeDtypeStruct((M * WIN_ROWS, LANE), F32),
            jax.ShapeDtypeStruct((M, 1536), BF16),
            jax.ShapeDtypeStruct((M, 512), BF16),
        ],
        compiler_params=_params(("parallel",)),
        name="nsa_post",
    )(p, cos, sin, gains)


def _put_residues(ref, hh, val, dil, tmp_ref):
    sl = slice(hh * LANE, (hh + 1) * LANE)
    if dil == 1:
        ref[0, :, sl] = val.astype(BF16)
        return
    tmp_ref[...] = val
    n = val.shape[0] // dil
    for r in range(dil):
        ref[r, :, sl] = tmp_ref[pl.ds(r, n, stride=dil), :].astype(BF16)


def _dil_post_kernel(p_ref, cos_ref, sin_ref, g_ref, *refs, by_residue):
    if by_residue:
        mq_ref, st0_ref, st1_ref, st2_ref = refs[:4]
        qkv_refs, tmp_ref = refs[4:13], refs[13]
    else:
        qr_ref, mq_ref, st0_ref, st1_ref, st2_ref = refs
    cos, sin = cos_ref[...], sin_ref[...]
    q_g, k_g, mq_g = g_ref[0:1, :], g_ref[1:2, :], g_ref[2:3, :]
    st_refs = (st0_ref, st1_ref, st2_ref)
    for h in range(N_MIX_HEADS):
        g, hh = divmod(h, DIL_HEADS)
        sl = slice(h * LANE, (h + 1) * LANE)
        q = _rope(_rms(p_ref[:, sl], q_g), cos, sin)
        k = _rope(_rms(p_ref[:, (12 + h) * LANE:(13 + h) * LANE], k_g), cos, sin)
        v = p_ref[:, (24 + h) * LANE:(25 + h) * LANE]
        _put_rows(st_refs[g], hh, KVH_ROWS, k)
        _put_rows(st_refs[g], DIL_HEADS + hh, KVH_ROWS, v)
        if by_residue:
            for c, val in enumerate((q, k, v)):
                _put_residues(qkv_refs[3 * g + c], hh, val, DIL_PAIRS[g][1], tmp_ref)
        else:
            qr_ref[:, sl] = q.astype(BF16)
    for h in range(N_MEM_HEADS):
        mq_ref[:, h * LANE:(h + 1) * LANE] = _rms(p_ref[:, (36 + h) * LANE:(37 + h) * LANE], mq_g).astype(BF16)


def _dil_post(p, cos, sin, gains, tm, seq_len=None):
    M = p.shape[0]
    row = lambda n: pl.BlockSpec((tm, n), lambda i: (i, 0))
    flat = pl.BlockSpec((tm * KVH_ROWS, LANE), lambda i: (i, 0))
    st_shape = jax.ShapeDtypeStruct((M * KVH_ROWS, LANE), F32)
    out_specs = [row(512), flat, flat, flat]
    out_shape = [jax.ShapeDtypeStruct((M, 512), BF16), st_shape, st_shape, st_shape]
    scratch = []
    if seq_len is None:
        out_specs = [row(1536)] + out_specs
        out_shape = [jax.ShapeDtypeStruct((M, 1536), BF16)] + out_shape
    else:
        nt = seq_len // tm
        for _, dil in DIL_PAIRS:
            assert tm % (16 * dil) == 0
            spec = pl.BlockSpec((None, dil, tm // dil, 4 * LANE), lambda i: (i // nt, 0, i % nt, 0))
            shape = jax.ShapeDtypeStruct((M // seq_len, dil, seq_len // dil, 4 * LANE), BF16)
            out_specs += [spec] * 3
            out_shape += [shape] * 3
        scratch = [pltpu.VMEM((tm, LANE), F32)]
    return pl.pallas_call(
        functools.partial(_dil_post_kernel, by_residue=seq_len is not None),
        grid=(M // tm,),
        in_specs=[row(PROJ_N), row(LANE), row(LANE), pl.BlockSpec((8, LANE), lambda i: (0, 0))],
        out_specs=out_specs,
        out_shape=out_shape,
        scratch_shapes=scratch,
        compiler_params=_params(("parallel",)),
        name="dil_post",
    )(p, cos, sin, gains)


def _memkv_post_kernel(x_ref, g_ref, o_ref):
    for h in range(N_MEM_HEADS):
        _put_rows(o_ref, h, KVH_ROWS, _rms(x_ref[:, h * LANE:(h + 1) * LANE], g_ref[...]))
        _put_rows(o_ref, N_MEM_HEADS + h, KVH_ROWS, x_ref[:, (N_MEM_HEADS + h) * LANE:(N_MEM_HEADS + h + 1) * LANE])


def _memkv_post(x, g, tm=256):
    M, N = x.shape
    return pl.pallas_call(
        _memkv_post_kernel,
        grid=(M // tm,),
        in_specs=[pl.BlockSpec((tm, N), lambda i: (i, 0)), pl.BlockSpec((1, LANE), lambda i: (0, 0))],
        out_specs=pl.BlockSpec((tm * KVH_ROWS, LANE), lambda i: (i, 0)),
        out_shape=jax.ShapeDtypeStruct((M * KVH_ROWS, LANE), F32),
        compiler_params=_params(("parallel",)),
        name="memkv_post",
    )(x, g.reshape(1, LANE))


def _mem_attn_kernel(q_ref, kv_ref, o_ref, *, tq):
    for h in range(N_MEM_HEADS):
        sl = slice(h * LANE, (h + 1) * LANE)
        if tq == 1:
            q = _rows16(q_ref[:, sl], 1).astype(BF16)
        else:
            q = q_ref[:, sl]
        k = _get_rows(kv_ref, h, KVH_ROWS, N_MEM).astype(BF16)
        v = _get_rows(kv_ref, N_MEM_HEADS + h, KVH_ROWS, N_MEM).astype(BF16)
        s = _dot_t(q, k) * SCALE
        m = jnp.max(s, axis=-1, keepdims=True)
        e = jnp.exp(s - m)
        p = e / jnp.sum(e, axis=-1, keepdims=True)
        o = _dot(p.astype(BF16), v)
        o_ref[:, sl] = o[0:tq, :].astype(o_ref.dtype)


def _mem_attn(q, kv, tq):
    B, T, _ = q.shape
    return pl.pallas_call(
        functools.partial(_mem_attn_kernel, tq=tq),
        grid=(B, T // tq),
        in_specs=[
            pl.BlockSpec((None, tq, 512), lambda b, i: (b, i, 0)),
            pl.BlockSpec((None, N_MEM * KVH_ROWS, LANE), lambda b, i: (b, 0, 0)),
        ],
        out_specs=pl.BlockSpec((None, tq, 512), lambda b, i: (b, i, 0)),
        out_shape=jax.ShapeDtypeStruct((B, T, 512), q.dtype),
        compiler_params=_params(("parallel", "parallel")),
        name="mem_attn",
    )(q, kv)


def _gelu_tanh(x):
    return 0.5 * x * (1.0 + jnp.tanh(0.7978845608028654 * (x + 0.044715 * (x * x * x))))


def _compress_finish(h, b1, w2):
    n = h.shape[0]
    hid = b1 + h[:, :LANE] + pltpu.roll(h[:, LANE:], n - 1, 0)
    return _dot(_gelu_tanh(hid).astype(BF16), w2)


def _compress(x_bf, w1, b1, w2):
    return _compress_finish(_dot(x_bf, w1), b1, w2)


def _cmp_prompt_kernel(x_ref, w1_ref, b1_ref, w2_ref, kcg_ref, o_ref, xs_ref, *, n):
    kv = pl.program_id(1)
    for c in range(CMP_STRIDE):
        xs_ref[:, c * LANE:(c + 1) * LANE] = x_ref[pl.ds(c, n, stride=CMP_STRIDE), :].astype(BF16)
    out = _compress(xs_ref[...], w1_ref[...], b1_ref[...], w2_ref[...])
    out = jnp.where(kv == 0, _rms(out, kcg_ref[...]), out)
    rid = lax.broadcasted_iota(jnp.int32, out.shape, 0)
    o_ref[...] = jnp.where(rid < n - 1, out, 0.0).astype(BF16)


def _cmp_prompt(rows, w1r, b1, w2, kc_g):
    B, T, _ = rows.shape
    n = T // CMP_STRIDE
    return pl.pallas_call(
        functools.partial(_cmp_prompt_kernel, n=n),
        grid=(B, 2, NSA_KV_HEADS),
        in_specs=[
            pl.BlockSpec((None, T, LANE), lambda b, kv, g: (b, 0, kv * 3 + g)),
            pl.BlockSpec((None, CMP_STRIDE * LANE, 2 * LANE), lambda b, kv, g: (kv, 0, 0)),
            pl.BlockSpec((None, 1, LANE), lambda b, kv, g: (kv, 0, 0)),
            pl.BlockSpec((None, LANE, LANE), lambda b, kv, g: (kv, 0, 0)),
            pl.BlockSpec((1, LANE), lambda b, kv, g: (0, 0)),
        ],
        out_specs=pl.BlockSpec((None, None, None, n, LANE), lambda b, kv, g: (b, kv, g, 0, 0)),
        out_shape=jax.ShapeDtypeStruct((B, 2, NSA_KV_HEADS, n, LANE), BF16),
        scratch_shapes=[pltpu.VMEM((n, CMP_STRIDE * LANE), BF16)],
        compiler_params=_params(("parallel", "parallel", "parallel")),
        name="cmp_prompt",
    )(rows, w1r, b1, w2, kc_g)


def _select_blocks(score, cur, n_blocks):
    tq = score.shape[0]
    blk = lax.broadcasted_iota(jnp.int32, score.shape, 1)
    forced = (blk == 0) | (blk == cur) | (blk == cur - 1)
    sc = jnp.where(blk <= cur, jnp.where(forced, FORCE_SCORE, score), NEG)
    sct = sc.T[0:n_blocks, :]
    bi = lax.broadcasted_iota(jnp.int32, sct.shape, 0)
    rank = jnp.zeros(sct.shape, F32)
    for i in range(n_blocks):
        si = sct[i:i + 1, :]
        ahead = (si > sct) | ((si == sct) & (bi > i))
        rank = rank + jnp.where(ahead, 1.0, 0.0)
    chosen = jnp.where((rank < SEL_TOPK) & (sct > 0.5 * NEG), 1.0, 0.0)
    return jnp.concatenate([chosen, jnp.zeros((LANE - n_blocks, tq), F32)], axis=0).T


def _score_tile(q, k_ref, kt, slot, bias, s_ref, m_ref, tk, first=False):
    s = _dot_t(q, k_ref[pl.ds(pl.multiple_of(kt * tk, tk), tk), :]) * SCALE
    if bias is not None:
        s = s + bias
    s_ref[slot] = s
    m = functools.reduce(jnp.maximum, [s[:, c * LANE:(c + 1) * LANE] for c in range(tk // LANE)])
    m_ref[...] = m if first else jnp.maximum(m_ref[...], m)


def _value_tile(v_ref, kt, slot, s_ref, m_ref, l_ref, acc_ref, tk, first=False):
    m = m_ref[...]
    es = [jnp.exp(s_ref[slot, :, c * LANE:(c + 1) * LANE] - m) for c in range(tk // LANE)]
    pv = _dot(jnp.concatenate(es, axis=1).astype(BF16), v_ref[pl.ds(pl.multiple_of(kt * tk, tk), tk), :])
    l = functools.reduce(lambda a, b: a + b, es)
    if first:
        l_ref[...] = l
        acc_ref[...] = pv
    else:
        l_ref[...] += l
        acc_ref[...] += pv


def _nsa_attn_kernel(qn_ref, qr_ref, gl_ref, kc_ref, vc_ref, ks_ref, vs_ref, kw_ref, vw_ref, cover_ref,
                     o_ref, s_ref, m_ref, l_ref, acc_ref, sel_ref, *, tq, ns):
    qi = pl.program_id(2)
    R = NSA_GROUP
    t0 = qi * tq
    stack = lambda ref: jnp.concatenate([ref[:, r * LANE:(r + 1) * LANE] for r in range(R)], axis=0)
    rows4 = lambda x: jnp.concatenate([x] * R, axis=0)
    tpos_q = t0 + lax.broadcasted_iota(jnp.int32, (tq, 1), 0)
    row_in = lax.broadcasted_iota(jnp.int32, (R * tq, 1), 0) & (tq - 1)
    col = lax.broadcasted_iota(jnp.int32, (R * tq, tq), 1)

    s = _dot_t(stack(qn_ref), kc_ref[...]) * SCALE
    cblk = lax.broadcasted_iota(jnp.int32, (R * tq, LANE), 1)
    cmask = (CMP_STRIDE * cblk + (CMP_BLOCK - 1) <= t0 + row_in) & (cblk < kc_ref.shape[0] - 1)
    p, _, _ = _softmax_masked(s, cmask)
    o_cmp = _dot(p.astype(BF16), vc_ref[...])
    imp = p[0:tq] + p[tq:2 * tq] + p[2 * tq:3 * tq] + p[3 * tq:4 * tq]
    score = _dot3(imp, cover_ref[...])
    cur = lax.shift_right_arithmetic(tpos_q, SEL_BLOCK.bit_length() - 1)

    @pl.when(t0 + tq <= SEL_TOPK * SEL_BLOCK)
    def _():
        sel_ref[...] = jnp.where(lax.broadcasted_iota(jnp.int32, (tq, LANE), 1) <= cur, 1.0, 0.0).astype(BF16)

    @pl.when(t0 + tq > SEL_TOPK * SEL_BLOCK)
    def _():
        sel_ref[...] = _select_blocks(score, cur, ns).astype(BF16)

    sel = sel_ref[...]
    q_rot = stack(qr_ref)
    blocks_per_tile = tq // SEL_BLOCK
    causal = jnp.where(col <= row_in, 0.0, NEG)
    far = jnp.where(col >= row_in, 0.0, NEG)

    def row_max():
        m_ref[...] = jnp.broadcast_to(jnp.max(m_ref[...], axis=-1, keepdims=True), m_ref.shape)

    def result():
        return acc_ref[...] / jnp.maximum(jnp.sum(l_ref[...], axis=-1, keepdims=True), 1e-30)

    def member_bias(kt):
        key_blk = lax.shift_right_arithmetic(lax.broadcasted_iota(jnp.int32, (LANE, tq), 1),
                                             SEL_BLOCK.bit_length() - 1)
        expand = lax.broadcasted_iota(jnp.int32, (LANE, tq), 0) == kt * blocks_per_tile + key_blk
        member = _dot(sel, jnp.where(expand, 1.0, 0.0).astype(BF16))
        return rows4((member - 1.0) * (-NEG))

    _score_tile(q_rot, ks_ref, qi, qi, member_bias(qi) + causal, s_ref, m_ref, tq, first=True)

    def sel_scores(kt, carry):
        _score_tile(q_rot, ks_ref, kt, kt, member_bias(kt), s_ref, m_ref, tq)
        return carry

    lax.fori_loop(0, qi, sel_scores, 0)
    row_max()
    _value_tile(vs_ref, qi, qi, s_ref, m_ref, l_ref, acc_ref, tq, first=True)

    def sel_values(kt, carry):
        _value_tile(vs_ref, kt, kt, s_ref, m_ref, l_ref, acc_ref, tq)
        return carry

    lax.fori_loop(0, qi, sel_values, 0)
    o_sel = result()

    n_back = NSA_WINDOW // tq
    for back in range(n_back + 1):
        bias = jnp.where(qi >= back, 0.0, NEG)
        if back == 0:
            bias = causal
        elif back == n_back:
            bias = far + bias
        _score_tile(q_rot, kw_ref, jnp.maximum(qi - back, 0), back, bias, s_ref, m_ref, tq, first=back == 0)
    row_max()
    for back in range(n_back + 1):
        _value_tile(vw_ref, jnp.maximum(qi - back, 0), back, s_ref, m_ref, l_ref, acc_ref, tq, first=back == 0)
    o_win = result()

    gates = jax.nn.sigmoid(gl_ref[...])
    for r in range(R):
        rs = slice(r * tq, (r + 1) * tq)
        o = (gates[:, 3 * r:3 * r + 1] * o_cmp[rs] + gates[:, 3 * r + 1:3 * r + 2] * o_sel[rs]
             + gates[:, 3 * r + 2:3 * r + 3] * o_win[rs])
        o_ref[:, r * LANE:(r + 1) * LANE] = o.astype(BF16)


def _nsa_attn(qn, qr, proj, cmp_kv, kvb, cover, tq=256):
    B, T, _ = qn.shape
    nc = cmp_kv.shape[3]
    G = NSA_KV_HEADS
    assert tq % LANE == 0 and NSA_WINDOW % tq == 0 and nc == LANE and T // SEL_BLOCK <= LANE
    rows = NSA_GROUP * tq
    qspec = pl.BlockSpec((None, tq, 4 * LANE), lambda b, g, i: (b, i, g))
    kvspec = lambda c: pl.BlockSpec((None, T, LANE), lambda b, g, i: (b, 0, c * 3 + g))
    return pl.pallas_call(
        functools.partial(_nsa_attn_kernel, tq=tq, ns=T // SEL_BLOCK),
        grid=(B, G, T // tq),
        in_specs=[
            qspec, qspec,
            pl.BlockSpec((None, tq, LANE), lambda b, g, i: (b, i, NSA_GATE_BLK + g)),
            pl.BlockSpec((None, None, None, nc, LANE), lambda b, g, i: (b, 0, g, 0, 0)),
            pl.BlockSpec((None, None, None, nc, LANE), lambda b, g, i: (b, 1, g, 0, 0)),
            kvspec(0), kvspec(1), kvspec(2), kvspec(3),
            pl.BlockSpec((nc, LANE), lambda b, g, i: (0, 0)),
        ],
        out_specs=qspec,
        out_shape=jax.ShapeDtypeStruct((B, T, 1536), BF16),
        scratch_shapes=[pltpu.VMEM((T // tq, rows, tq), F32),
                        pltpu.VMEM((rows, LANE), F32),
                        pltpu.VMEM((rows, LANE), F32),
                        pltpu.VMEM((rows, LANE), F32),
                        pltpu.VMEM((tq, LANE), BF16)],
        compiler_params=_params(("parallel", "parallel", "arbitrary")),
        name="nsa_attn",
    )(qn, qr, proj, cmp_kv, cmp_kv, kvb, kvb, kvb, kvb, cover)


def _dil_band_kernel(q_ref, kp_ref, kc_ref, vp_ref, vc_ref, o_ref, st_ref, *, tq, window):
    i = pl.program_id(2)
    qpos = i * tq + lax.broadcasted_iota(jnp.int32, (tq, 2 * tq), 0)
    kpos = (i - 1) * tq + lax.broadcasted_iota(jnp.int32, (tq, 2 * tq), 1)
    diff = qpos - kpos
    mask = (diff >= 0) & (diff <= window) & (kpos >= 0)
    lane = lax.broadcasted_iota(jnp.int32, (tq, LANE), 1)
    for r in range(q_ref.shape[0]):
        stats = jnp.zeros((tq, LANE), F32)
        for h in range(DIL_HEADS):
            sl = slice(h * LANE, (h + 1) * LANE)
            k = jnp.concatenate([kp_ref[r, :, sl], kc_ref[r, :, sl]], axis=0)
            v = jnp.concatenate([vp_ref[r, :, sl], vc_ref[r, :, sl]], axis=0)
            p, m, l = _softmax_masked(_dot_t(q_ref[r, :, sl], k) * SCALE, mask)
            o_ref[r, :, sl] = _dot(p.astype(BF16), v)
            stats = jnp.where(lane == h, m, stats)
            stats = jnp.where(lane == DIL_HEADS + h, l, stats)
        st_ref[r] = stats


def _dil_band(q, k, v, g, tq, nr):
    B, dil, S, _ = q.shape
    window = DIL_PAIRS[g][0] // dil
    assert window <= tq and S % tq == 0 and dil % nr == 0
    cur = lambda w: pl.BlockSpec((None, nr, tq, w), lambda b, r, i: (b, r, i, 0))
    prev = pl.BlockSpec((None, nr, tq, 4 * LANE), lambda b, r, i: (b, r, jnp.maximum(i - 1, 0), 0))
    return pl.pallas_call(
        functools.partial(_dil_band_kernel, tq=tq, window=window),
        grid=(B, dil // nr, S // tq),
        in_specs=[cur(4 * LANE), prev, cur(4 * LANE), prev, cur(4 * LANE)],
        out_specs=[cur(4 * LANE), cur(LANE)],
        out_shape=[jax.ShapeDtypeStruct((B, dil, S, 4 * LANE), F32),
                   jax.ShapeDtypeStruct((B, dil, S, LANE), F32)],
        compiler_params=_params(("parallel", "parallel", "parallel")),
        name=f"dil_band{g}",
    )(q, k, k, v, v)


def _mix_groups(os_, ms, ls):
    m_all = jnp.maximum(jnp.maximum(ms[0], ms[1]), ms[2])
    ws = [jnp.exp(m - m_all) * l for m, l in zip(ms, ls)]
    tot = ws[0] + ws[1] + ws[2]
    return (ws[0] / tot) * os_[0] + (ws[1] / tot) * os_[1] + (ws[2] / tot) * os_[2]


def _dil_mix_kernel(o0_ref, o1_ref, o2_ref, s0_ref, s1_ref, s2_ref, o_ref, nat_ref):
    o_refs, s_refs = (o0_ref, o1_ref, o2_ref), (s0_ref, s1_ref, s2_ref)
    tm = o_ref.shape[0]

    def token_order(ref, sl):
        dil = ref.shape[0]
        if dil == 1:
            return ref[0, :, sl]
        for r in range(dil):
            nat_ref[pl.ds(r, tm // dil, stride=dil), :] = ref[r, :, sl]
        return nat_ref[...]

    stats = [token_order(s, slice(0, LANE)) for s in s_refs]
    for h in range(DIL_HEADS):
        sl = slice(h * LANE, (h + 1) * LANE)
        ms = [s[:, h:h + 1] for s in stats]
        ls = [s[:, DIL_HEADS + h:DIL_HEADS + h + 1] for s in stats]
        o_ref[:, sl] = _mix_groups([token_order(o, sl) for o in o_refs], ms, ls).astype(BF16)


def _dil_mix(os_, sts, tm=512):
    B, _, T, _ = os_[0].shape
    nt = T // tm
    in_specs = []
    for lanes, arrs in ((4 * LANE, os_), (LANE, sts)):
        for a in arrs:
            dil = a.shape[1]
            in_specs.append(pl.BlockSpec((None, dil, tm // dil, lanes), lambda b, i: (b, 0, i, 0)))
    return pl.pallas_call(
        _dil_mix_kernel,
        grid=(B, nt),
        in_specs=in_specs,
        out_specs=pl.BlockSpec((tm, 4 * LANE), lambda b, i: (b * nt + i, 0)),
        out_shape=jax.ShapeDtypeStruct((B * T, 4 * LANE), BF16),
        scratch_shapes=[pltpu.VMEM((tm, LANE), F32)],
        compiler_params=_params(("parallel", "parallel")),
        name="dil_mix",
    )(*os_, *sts)


DEC_PAGES_PER_STEP = 16


def _dec_select_kernel(tbl_ref, *refs, n_pp, n_steps):
    pages = refs[:n_pp]
    (qn_ref, w1_ref, b1_ref, w2_ref, kcg_ref, cover_ref, idx_ref, ocmp_ref, h_ref) = refs[n_pp:]
    j = pl.program_id(1)
    cpp = PAGE_SIZE // CMP_STRIDE
    rows = n_pp * cpp
    n = n_steps * rows
    ns = cover_ref.shape[1]

    for g in range(NSA_KV_HEADS):
        for kv in range(2):
            x = jnp.concatenate(
                [_get_rows(pg, g * 4 + kv, NSA_ROWS, PAGE_SIZE).reshape(cpp, CMP_STRIDE * LANE) for pg in pages],
                axis=0)
            h_ref[g * 2 + kv, pl.ds(pl.multiple_of(j * rows, rows), rows), :] = _dot(x.astype(BF16), w1_ref[kv])

    @pl.when(j == n_steps - 1)
    def _():
        idx_ref[...] = jnp.zeros_like(idx_ref)
        for g in range(NSA_KV_HEADS):
            kc = _rms(_compress_finish(h_ref[g * 2], b1_ref[0], w2_ref[0]), kcg_ref[...]).astype(BF16)
            vc = _compress_finish(h_ref[g * 2 + 1], b1_ref[1], w2_ref[1]).astype(BF16)
            q = _rows16(qn_ref[:, g * NSA_GROUP * LANE:(g + 1) * NSA_GROUP * LANE], NSA_GROUP).astype(BF16)
            s = _dot_t(q, kc) * SCALE
            valid = lax.broadcasted_iota(jnp.int32, s.shape, 1) < n - 1
            p, _, _ = _softmax_masked(s, valid)
            ocmp_ref[g] = _dot(p.astype(BF16), vc)[0:NSA_GROUP, :]
            rid = lax.broadcasted_iota(jnp.int32, p.shape, 0)
            imp = jnp.sum(jnp.where(rid < NSA_GROUP, p, 0.0), axis=0, keepdims=True)
            score = _dot3(jnp.broadcast_to(imp, (8, n)), cover_ref[...])

            a = jnp.broadcast_to(score[0:1, :], (ns, ns))
            lane = lax.broadcasted_iota(jnp.int32, (ns, ns), 1)
            sub = lax.broadcasted_iota(jnp.int32, (ns, ns), 0)
            cur = n * CMP_STRIDE // SEL_BLOCK
            forced = (lane == 0) | (lane == cur) | (lane == cur - 1)
            a = jnp.where(lane <= cur, jnp.where(forced, FORCE_SCORE, a), NEG)
            at = a.T
            ahead_r = (at > a) | ((at == a) & (sub < lane))
            chosen_r = ((jnp.sum(jnp.where(ahead_r, 1.0, 0.0), axis=0, keepdims=True) < SEL_TOPK)
                        & (a[0:1, :] > 0.5 * NEG))
            ahead_c = (a > at) | ((a == at) & (lane < sub))
            chosen_c = ((jnp.sum(jnp.where(ahead_c, 1.0, 0.0), axis=1, keepdims=True) < SEL_TOPK)
                        & (at[:, 0:1] > 0.5 * NEG))
            before = jnp.sum(jnp.where(chosen_r & (lane < sub), 1.0, 0.0), axis=1, keepdims=True)
            slot = lax.broadcasted_iota(jnp.int32, (ns, LANE), 1)
            onehot = chosen_c & (before == slot.astype(F32))
            blk = lax.broadcasted_iota(jnp.int32, (ns, LANE), 0)
            picked = jnp.sum(jnp.where(onehot, blk.astype(F32), 0.0), axis=0, keepdims=True)
            filled = jnp.sum(jnp.where(onehot, 1.0, 0.0), axis=0, keepdims=True)
            idx_ref[g:g + 1, :] = jnp.where(filled > 0.5, picked, float(cur)).astype(jnp.int32)


def _dec_select(cache, table, qn, w1r, b1, w2, kc_g, cover):
    B, n_pages = table.shape
    n_pp = DEC_PAGES_PER_STEP
    n_steps = n_pages // n_pp
    n = n_pages * (PAGE_SIZE // CMP_STRIDE)
    ns = cover.shape[1]
    G = NSA_KV_HEADS
    const = lambda *shape: pl.BlockSpec(shape, lambda b, j, tbl: (0,) * len(shape))
    page_spec = lambda p: pl.BlockSpec((None, PAGE_SIZE * NSA_ROWS, LANE), lambda b, j, tbl: (tbl[b, j * n_pp + p], 0, 0))
    grid_spec = pltpu.PrefetchScalarGridSpec(
        num_scalar_prefetch=1,
        grid=(B, n_steps),
        in_specs=[page_spec(p) for p in range(n_pp)] + [
            pl.BlockSpec((None, 1, 1536), lambda b, j, tbl: (b, 0, 0)),
            const(2, CMP_STRIDE * LANE, 2 * LANE), const(2, 1, LANE), const(2, LANE, LANE), const(1, LANE),
            const(n, ns),
        ],
        out_specs=[pl.BlockSpec((None, 8, LANE), lambda b, j, tbl: (b, 0, 0)),
                   pl.BlockSpec((None, G, NSA_GROUP, LANE), lambda b, j, tbl: (b, 0, 0, 0))],
        scratch_shapes=[pltpu.VMEM((2 * G, n, 2 * LANE), F32)],
    )
    return pl.pallas_call(
        functools.partial(_dec_select_kernel, n_pp=n_pp, n_steps=n_steps),
        grid_spec=grid_spec,
        out_shape=[jax.ShapeDtypeStruct((B, 8, LANE), jnp.int32),
                   jax.ShapeDtypeStruct((B, G, NSA_GROUP, LANE), F32)],
        compiler_params=_params(("arbitrary", "arbitrary")),
        name="dec_select",
    )(table, *([cache] * n_pp), qn, w1r, b1, w2, kc_g, cover)


def _dec_attn_kernel(tbl_ref, idx_ref, *refs, n_sel, cur, wb):
    blocks = refs[:n_sel]
    (qr_ref, kvn_ref, win_ref, gl_ref, ocmp_ref, o_ref) = refs[n_sel:]
    b, g = pl.program_id(0), pl.program_id(1)
    R = NSA_GROUP
    q = _rows16(qr_ref[...], R).astype(BF16)
    qf = q.astype(F32)
    new = kvn_ref[...]
    ks_n, vs_n, kw_n, vw_n = (new[:, c * LANE:(c + 1) * LANE] for c in range(4))

    k = jnp.concatenate([_get_rows(r, g * 4 + 2, NSA_ROWS, SEL_BLOCK) for r in blocks], axis=0).astype(BF16)
    v = jnp.concatenate([_get_rows(r, g * 4 + 3, NSA_ROWS, SEL_BLOCK) for r in blocks], axis=0).astype(BF16)
    s = _dot_t(q, k) * SCALE
    blk_of = lax.shift_right_arithmetic(lax.broadcasted_iota(jnp.int32, s.shape, 1), SEL_BLOCK.bit_length() - 1)
    valid = jnp.zeros(s.shape, jnp.int32)
    for n in range(n_sel):
        is_past = jnp.where(idx_ref[(b * NSA_KV_HEADS + g) * n_sel + n] != cur, 1, 0)
        valid = jnp.where(blk_of == n, is_past, valid)
    valid = valid > 0
    s_new = jnp.sum(qf * ks_n, axis=-1, keepdims=True) * SCALE
    s = jnp.where(valid, s, NEG)
    m = jnp.maximum(jnp.max(s, axis=-1, keepdims=True), s_new)
    e = jnp.where(valid, jnp.exp(s - m), 0.0)
    e_new = jnp.exp(s_new - m)
    l = jnp.sum(e, axis=-1, keepdims=True) + e_new
    o_sel = (_dot(e.astype(BF16), v) + e_new.astype(BF16).astype(F32) * vs_n) / l

    s = _dot_t(q, _get_rows(win_ref, g * 2, WIN_ROWS, wb).astype(BF16)) * SCALE
    s_new = jnp.sum(qf * kw_n, axis=-1, keepdims=True) * SCALE
    m = jnp.maximum(jnp.max(s, axis=-1, keepdims=True), s_new)
    e = jnp.exp(s - m)
    e_new = jnp.exp(s_new - m)
    l = jnp.sum(e, axis=-1, keepdims=True) + e_new
    o_win = (_dot(e.astype(BF16), _get_rows(win_ref, g * 2 + 1, WIN_ROWS, wb).astype(BF16))
             + e_new.astype(BF16).astype(F32) * vw_n) / l

    gates = jax.nn.sigmoid(gl_ref[...])
    o_cmp = ocmp_ref[...]
    for r in range(R):
        o = (gates[:, 3 * r:3 * r + 1] * o_cmp[r:r + 1] + gates[:, 3 * r + 1:3 * r + 2] * o_sel[r:r + 1]
             + gates[:, 3 * r + 2:3 * r + 3] * o_win[r:r + 1])
        o_ref[:, r * LANE:(r + 1) * LANE] = o


def _dec_attn(cache, table, idx, qr, kvb, win_state, proj, ocmp):
    B, n_pages = table.shape
    n_sel = SEL_TOPK
    cur = n_pages * PAGE_SIZE // SEL_BLOCK
    wb = win_state.shape[1] // WIN_ROWS
    assert wb <= NSA_WINDOW
    halves = PAGE_SIZE // SEL_BLOCK

    def blk_spec(n):
        def imap(b, g, tbl, idx):
            i = jnp.minimum(idx[(b * NSA_KV_HEADS + g) * n_sel + n], cur - 1)
            return (tbl[b, i // halves], i % halves, 0)
        return pl.BlockSpec((None, SEL_BLOCK * NSA_ROWS, LANE), imap)

    grid_spec = pltpu.PrefetchScalarGridSpec(
        num_scalar_prefetch=2,
        grid=(B, NSA_KV_HEADS),
        in_specs=[blk_spec(n) for n in range(n_sel)] + [
            pl.BlockSpec((None, 1, 4 * LANE), lambda b, g, tbl, idx: (b, 0, g)),
            pl.BlockSpec((None, 1, 4 * LANE), lambda b, g, tbl, idx: (b, 0, g)),
            pl.BlockSpec((None, wb * WIN_ROWS, LANE), lambda b, g, tbl, idx: (b, 0, 0)),
            pl.BlockSpec((None, 1, LANE), lambda b, g, tbl, idx: (b, 0, NSA_GATE_BLK + g)),
            pl.BlockSpec((None, None, NSA_GROUP, LANE), lambda b, g, tbl, idx: (b, g, 0, 0)),
        ],
        out_specs=pl.BlockSpec((None, 1, 4 * LANE), lambda b, g, tbl, idx: (b, 0, g)),
    )
    kvn = kvb.reshape(B, 1, 4, NSA_KV_HEADS, LANE).transpose(0, 1, 3, 2, 4).reshape(B, 1, 1536)
    return pl.pallas_call(
        functools.partial(_dec_attn_kernel, n_sel=n_sel, cur=cur, wb=wb),
        grid_spec=grid_spec,
        out_shape=jax.ShapeDtypeStruct((B, 1, 1536), F32),
        compiler_params=_params(("arbitrary", "arbitrary")),
        name="dec_attn",
    )(table, idx, *([cache] * n_sel), qr, kvn, win_state, proj, ocmp)


def _dec_dil_kernel(q_ref, n0_ref, n1_ref, n2_ref, s0_ref, s1_ref, s2_ref, o_ref, t0_ref, t1_ref, t2_ref):
    qall = q_ref[...]
    states, news, outs = (s0_ref, s1_ref, s2_ref), (n0_ref, n1_ref, n2_ref), (t0_ref, t1_ref, t2_ref)

    for st, new, out in zip(states, news, outs):
        keep = st.shape[0] - KVH_ROWS
        out[0:keep, :] = st[KVH_ROWS:, :]
        out[keep:, :] = new[...]

    for h in range(DIL_HEADS):
        os_, ms, ls = [], [], []
        for g, st in enumerate(states):
            hs = slice((g * DIL_HEADS + h) * LANE, (g * DIL_HEADS + h + 1) * LANE)
            window, dil = DIL_PAIRS[g]
            kn = news[g][h:h + 1, :].astype(BF16).astype(F32)
            vn = news[g][DIL_HEADS + h:DIL_HEADS + h + 1, :].astype(BF16).astype(F32)
            q = _rows16(qall[:, hs], 1).astype(BF16)
            k = _get_rows(st, h, KVH_ROWS * dil, window // dil).astype(BF16)
            v = _get_rows(st, DIL_HEADS + h, KVH_ROWS * dil, window // dil).astype(BF16)
            s = _dot_t(q, k) * SCALE
            s_new = jnp.sum(q.astype(F32) * kn, axis=-1, keepdims=True) * SCALE
            m = jnp.maximum(jnp.max(s, axis=-1, keepdims=True), s_new)
            e = jnp.exp(s - m)
            e_new = jnp.exp(s_new - m)
            l = jnp.sum(e, axis=-1, keepdims=True) + e_new
            ln = jnp.maximum(l, 1e-30)
            o = _dot((e / ln).astype(BF16), v) + (e_new / ln).astype(BF16).astype(F32) * vn
            os_.append(o)
            ms.append(m)
            ls.append(l)
        o_ref[:, h * LANE:(h + 1) * LANE] = _mix_groups(os_, ms, ls)[0:1, :]


def _dec_dil(qr, news, states):
    B = qr.shape[0]
    in_specs = [pl.BlockSpec((None, 1, 1536), lambda b: (b, 0, 0))]
    in_specs += [pl.BlockSpec((None, KVH_ROWS, LANE), lambda b: (b, 0, 0))] * len(DIL_PAIRS)
    st_specs = []
    for g, (window, dil) in enumerate(DIL_PAIRS):
        assert states[g].shape[1] == window * KVH_ROWS, "rolling buffer shorter than the window is not supported"
        st_specs.append(pl.BlockSpec((None, window * KVH_ROWS, LANE), lambda b: (b, 0, 0)))
    return pl.pallas_call(
        _dec_dil_kernel,
        grid=(B,),
        in_specs=in_specs + st_specs,
        out_specs=[pl.BlockSpec((None, 1, 4 * LANE), lambda b: (b, 0, 0))] + st_specs,
        out_shape=[jax.ShapeDtypeStruct((B, 1, 4 * LANE), F32)]
        + [jax.ShapeDtypeStruct(s.shape, F32) for s in states],
        compiler_params=_params(("parallel",)),
        name="dec_dil",
    )(qr, *news, *states)


def _rope_tables(pos):
    half = HEAD_DIM // 2
    inv = ROPE_THETA ** (-jnp.arange(half, dtype=F32) / half)
    ang = pos.astype(F32)[:, None] * inv
    cos, sin = jnp.cos(ang), jnp.sin(ang)
    return jnp.concatenate([cos, cos], axis=-1), jnp.concatenate([-sin, sin], axis=-1)


def _cover(nc, ns, rows, cols):
    c0 = jnp.arange(nc)[:, None] * CMP_STRIDE
    s0 = jnp.arange(ns)[None, :] * SEL_BLOCK
    cover = jnp.clip(jnp.minimum(c0 + CMP_BLOCK, s0 + SEL_BLOCK) - jnp.maximum(c0, s0), 0, CMP_BLOCK)
    cover = cover.astype(F32) / CMP_BLOCK
    return jnp.pad(cover, ((0, rows - nc), (0, cols - ns))).astype(BF16)


def _pad_gains(*gs):
    return jnp.pad(jnp.stack(gs, axis=0), ((0, 8 - len(gs)), (0, 0)))


def kernel(x_prompt, x_sample, mem_prompt, cache_nsa_kv, page_table, state_nsa_win, state_dil_0, state_dil_1,
           state_dil_2, cache_mem_kv, ff_norm, ff_w_gate, ff_w_up, ff_w_down, mix_norm, mem_norm, w_mem_kv,
           mem_q_g, mem_k_g, nsa_w_in, nsa_q_g, nsa_kc_g, nsa_ks_g, nsa_kw_g, nsa_cmp_w1, nsa_cmp_b1, nsa_cmp_w2,
           nsa_w_out, dil_w_in, dil_q_g, dil_k_g, dil_w_out):
    B, T, D = x_prompt.shape
    Bs = x_sample.shape[0]
    assert x_sample.shape[1] == 1, "the sample group is a single-token decode step"
    n_pages = page_table.shape[1]
    past_len = n_pages * PAGE_SIZE
    H, G, d = N_MIX_HEADS, NSA_KV_HEADS, HEAD_DIM

    wg, wu, wd = ff_w_gate, ff_w_up, ff_w_down
    ffg = ff_norm.reshape(ff_norm.shape[0], 2, 1, D)
    gate_w = jnp.pad(nsa_w_in[:, H * d:H * d + 3 * H].reshape(D, G, 3 * NSA_GROUP), ((0, 0), (0, 0), (0, LANE - 12)))
    nsa_w = jnp.concatenate([nsa_w_in[:, :H * d], nsa_w_in[:, H * d + 3 * H:], gate_w.reshape(D, G * LANE)], axis=1)
    nsa_w = jnp.pad(nsa_w, ((0, 0), (0, PROJ_N - nsa_w.shape[1]))).astype(BF16)
    dil_w = dil_w_in.astype(BF16)
    nsa_wo, dil_wo = nsa_w_out.astype(BF16), dil_w_out.astype(BF16)
    w1r = nsa_cmp_w1.reshape(2, 2, CMP_STRIDE, d, d).transpose(0, 2, 3, 1, 4).reshape(2, CMP_STRIDE * d, 2 * d)
    w1r = w1r.astype(BF16)
    cmp_b1 = nsa_cmp_b1.reshape(2, 1, d)
    cmp_w2 = nsa_cmp_w2.astype(BF16)
    kc_g = nsa_kc_g.reshape(1, d)

    n_pool = cache_nsa_kv.shape[0]
    cache_rows = cache_nsa_kv.transpose(0, 1, 3, 2, 4).reshape(n_pool, PAGE_SIZE * NSA_ROWS, LANE)
    win_rows = state_nsa_win.transpose(0, 1, 3, 2, 4).reshape(Bs, state_nsa_win.shape[1] * WIN_ROWS, LANE)
    dil_states = (state_dil_0, state_dil_1, state_dil_2)
    dil_rows = [s.reshape(Bs, s.shape[1] * KVH_ROWS, LANE) for s in dil_states]
    mem_rows_s = cache_mem_kv.reshape(2, Bs, N_MEM * KVH_ROWS, LANE)

    mem2d = mem_prompt.reshape(B * N_MEM, D)
    mem_rows_p = []
    for i in range(2):
        kv = _norm_matmul(mem2d, mem_norm[i], w_mem_kv[i].astype(BF16), tm=256, tn=1024)
        mem_rows_p.append(_memkv_post(kv, mem_k_g[i]).reshape(B, N_MEM * KVH_ROWS, LANE))

    cos_p, sin_p = _rope_tables(jnp.tile(jnp.arange(T, dtype=jnp.int32), B))
    cos_s, sin_s = _rope_tables(jnp.full((Bs,), past_len, jnp.int32))
    row3 = lambda a: a.astype(F32).reshape(Bs, 1, a.shape[-1])

    xs, *wb = _ffn_cast(x_sample.reshape(Bs, D), ffg, wg, wu, wd, 0, 0)
    xp = _ffn(x_prompt.reshape(B * T, D), ffg, *wb, 0, 0, tm=1024)
    nsa_gains = _pad_gains(nsa_q_g, nsa_ks_g, nsa_kw_g, mem_q_g[0])

    proj_p = _norm_matmul(xp, mix_norm[0], nsa_w, tm=1024, tn=1024)
    qn_p, qr_p, cmp_p, rows_p, win_p, kvb_p, mq_p = _nsa_post(proj_p, cos_p, sin_p, nsa_gains, tm=256)
    cmp_kv = _cmp_prompt(cmp_p.reshape(B, T, 768), w1r, cmp_b1, cmp_w2, kc_g)
    nc_p = T // CMP_STRIDE
    cover_p = _cover(nc_p - 1, T // SEL_BLOCK, nc_p, LANE)
    o_mix_p = _nsa_attn(qn_p.reshape(B, T, 1536), qr_p.reshape(B, T, 1536), proj_p.reshape(B, T, PROJ_N),
                        cmp_kv, kvb_p.reshape(B, T, 1536), cover_p)
    o_mem_p = _mem_attn(mq_p.reshape(B, T, 512), mem_rows_p[0], tq=512)
    xp = _out_proj(xp, o_mix_p.reshape(B * T, 1536), o_mem_p.reshape(B * T, 512), nsa_wo, tm=1024)

    proj_s = _norm_matmul(xs, mix_norm[0], nsa_w, tm=Bs, tn=1024)
    qn_s, qr_s, _, rows_s, win_s, kvb_s, mq_s = _nsa_post(proj_s, cos_s, sin_s, nsa_gains, tm=Bs)
    nc_s = past_len // CMP_STRIDE
    ns_s = -(-(past_len + 1) // SEL_BLOCK)
    cover_s = _cover(nc_s - 1, ns_s, nc_s, -(-ns_s // LANE) * LANE)
    sel_idx, ocmp_s = _dec_select(cache_rows, page_table, row3(qn_s), w1r, cmp_b1, cmp_w2, kc_g, cover_s)
    o_mix_s = _dec_attn(cache_rows, page_table, sel_idx[:, :G, :SEL_TOPK].reshape(-1), row3(qr_s), row3(kvb_s),
                        win_rows, proj_s.reshape(Bs, 1, PROJ_N), ocmp_s)
    o_mem_s = _mem_attn(row3(mq_s), mem_rows_s[0], tq=1)
    xs = _out_proj(xs, o_mix_s.reshape(Bs, 1536).astype(BF16), o_mem_s.reshape(Bs, 512).astype(BF16), nsa_wo,
                   tm=Bs)

    xs, *wb = _ffn_cast(xs, ffg, wg, wu, wd, 0, 1)
    xp = _ffn(xp, ffg, *wb, 0, 1, tm=1024)

    xs, *wb = _ffn_cast(xs, ffg, wg, wu, wd, 1, 0)
    xp = _ffn(xp, ffg, *wb, 1, 0, tm=1024)
    dil_gains = _pad_gains(dil_q_g, dil_k_g, mem_q_g[1])

    dproj_p = _norm_matmul(xp, mix_norm[1], dil_w, tm=1024, tn=1024)
    dmq_p, *rest = _dil_post(dproj_p, cos_p, sin_p, dil_gains, tm=256, seq_len=T)
    dnew_p, dqkv_p = rest[:3], rest[3:]
    band = [_dil_band(*dqkv_p[3 * g:3 * g + 3], g, tq, nr) for g, (tq, nr) in enumerate(((256, 1), (256, 2), (128, 4)))]
    o_dil_p = _dil_mix([o for o, _ in band], [s for _, s in band])
    o_dmem_p = _mem_attn(dmq_p.reshape(B, T, 512), mem_rows_p[1], tq=512)
    xp = _out_proj(xp, o_dil_p, o_dmem_p.reshape(B * T, 512), dil_wo, tm=1024)

    dproj_s = _norm_matmul(xs, mix_norm[1], dil_w, tm=Bs, tn=1024)
    dq_s, dmq_s, *dnew_s = _dil_post(dproj_s, cos_s, sin_s, dil_gains, tm=Bs)
    o_dil_s, *dil_rows_out = _dec_dil(row3(dq_s), [s.reshape(Bs, KVH_ROWS, LANE) for s in dnew_s], dil_rows)
    o_dmem_s = _mem_attn(row3(dmq_s), mem_rows_s[1], tq=1)
    xs = _out_proj(xs, o_dil_s.reshape(Bs, 512).astype(BF16), o_dmem_s.reshape(Bs, 512).astype(BF16), dil_wo,
                   tm=Bs)

    xs, *wb = _ffn_cast(xs, ffg, wg, wu, wd, 1, 1)
    xp = _ffn(xp, ffg, *wb, 1, 1, tm=1024)

    unrow = lambda a, n, outer, inner: a.reshape(n, -1, outer, inner, d).transpose(0, 1, 3, 2, 4)
    nsa_kv_p = unrow(rows_p, B, G, 4)
    nsa_kv_s = unrow(rows_s, Bs, G, 4)
    nsa_win_p = unrow(win_p, B, G, 2)[:, -min(NSA_WINDOW, T):]
    nsa_win_s = jnp.concatenate([state_nsa_win, unrow(win_s, Bs, G, 2)], axis=1)[:, -state_nsa_win.shape[1]:]
    outs_dil = []
    for g, (window, _) in enumerate(DIL_PAIRS):
        st = dil_states[g]
        outs_dil.append(dnew_p[g].reshape(B, T, 2, DIL_HEADS, d)[:, -min(window, T):])
        outs_dil.append(dil_rows_out[g].reshape(st.shape))
    mem_kv_out = jnp.stack([kv.reshape(B, N_MEM, 2, N_MEM_HEADS, d) for kv in mem_rows_p], axis=0)
    return (xp.reshape(B, T, D), xs.reshape(Bs, 1, D), nsa_kv_p, nsa_kv_s, nsa_win_p, nsa_win_s,
            *outs_dil, mem_kv_out)
```

```python
import functools

import jax
import jax.numpy as jnp
from jax import lax
from jax.experimental import pallas as pl
from jax.experimental.pallas import tpu as pltpu

F32 = jnp.float32
BF16 = jnp.bfloat16

D_MODEL = 2048
HEAD_DIM = 128
N_MIX_HEADS = 12
N_MEM_HEADS = 4
N_MEM = 256
NSA_KV_HEADS = 3
NSA_GROUP = 4
CMP_BLOCK = 32
CMP_STRIDE = 16
SEL_BLOCK = 64
SEL_TOPK = 16
NSA_WINDOW = 512
DIL_PAIRS = ((128, 1), (512, 4), (2048, 16))
DIL_HEADS = 4
PAGE_SIZE = 128
ROPE_THETA = 10000.0
EPS = 1e-6
SCALE = HEAD_DIM ** -0.5
NEG = -1e30
FORCE_SCORE = 1e6

PROJ_N = 5120
NSA_GATE_BLK = 34
LANE = 128
VMEM_LIMIT = 56 * 1024 * 1024


def _params(sem):
    return pltpu.CompilerParams(dimension_semantics=sem, vmem_limit_bytes=VMEM_LIMIT)


def _dot(a, b):
    return jnp.dot(a, b, preferred_element_type=F32)


def _dot_t(a, b):
    return lax.dot_general(a, b, (((1,), (1,)), ((), ())), preferred_element_type=F32)


def _dot3(a, b):
    a1 = a.astype(BF16)
    r1 = a - a1.astype(F32)
    a2 = r1.astype(BF16)
    a3 = (r1 - a2.astype(F32)).astype(BF16)
    return _dot(a1, b) + _dot(a2, b) + _dot(a3, b)


def _rms(x, g):
    return x * lax.rsqrt(jnp.mean(x * x, axis=-1, keepdims=True) + EPS) * g


def _rope(x, cos, sin):
    return x * cos + pltpu.roll(x, HEAD_DIM // 2, 1) * sin


def _rows16(row, nrep):
    rid = lax.broadcasted_iota(jnp.int32, (16, LANE), 0) & (nrep - 1)
    out = jnp.zeros((16, LANE), F32)
    for r in range(nrep):
        piece = jnp.broadcast_to(row[:, r * LANE:(r + 1) * LANE], (16, LANE))
        out = jnp.where(rid == r, piece, out)
    return out


def _softmax_masked(s, mask):
    s = jnp.where(mask, s, NEG)
    m = jnp.max(s, axis=-1, keepdims=True)
    e = jnp.where(mask, jnp.exp(s - m), 0.0)
    l = jnp.sum(e, axis=-1, keepdims=True)
    return e / jnp.maximum(l, 1e-30), m, l


def _ffn_step(f, nf, x_ref, g_ref, wg_ref, wu_ref, wd_ref, o_ref, h_ref, acc_ref):
    @pl.when(f == 0)
    def _():
        h_ref[...] = _rms(x_ref[...], g_ref[...]).astype(BF16)
        acc_ref[...] = jnp.zeros_like(acc_ref)

    h = h_ref[...]
    gate = _dot(h, wg_ref[...])
    up = _dot(h, wu_ref[...])
    a = (gate * jax.nn.sigmoid(gate) * up).astype(BF16)
    acc_ref[...] += _dot(a, wd_ref[...])

    @pl.when(f == nf - 1)
    def _():
        o_ref[...] = x_ref[...] + 0.5 * acc_ref[...]


def _ffn_kernel(x_ref, g_ref, wg_ref, wu_ref, wd_ref, o_ref, h_ref, *, nf):
    _ffn_step(pl.program_id(1), nf, x_ref, g_ref, wg_ref, wu_ref, wd_ref, o_ref, h_ref, o_ref)


def _ffn(x, g, wg, wu, wd, li, lj, tm, tf=512):
    M, D = x.shape
    F = wg.shape[-1]
    nf = F // tf
    return pl.pallas_call(
        functools.partial(_ffn_kernel, nf=nf),
        grid=(M // tm, nf),
        in_specs=[
            pl.BlockSpec((tm, D), lambda i, f: (i, 0)),
            pl.BlockSpec((None, None, 1, D), lambda i, f: (li, lj, 0, 0)),
            pl.BlockSpec((D, tf), lambda i, f: (0, f)),
            pl.BlockSpec((D, tf), lambda i, f: (0, f)),
            pl.BlockSpec((tf, D), lambda i, f: (f, 0)),
        ],
        out_specs=pl.BlockSpec((tm, D), lambda i, f: (i, 0)),
        out_shape=jax.ShapeDtypeStruct((M, D), F32),
        scratch_shapes=[pltpu.VMEM((tm, D), BF16)],
        compiler_params=_params(("parallel", "arbitrary")),
        name="ffn",
    )(x, g, wg, wu, wd)


def _ffn_cast_kernel(x_ref, g_ref, wg_ref, wu_ref, wd_ref, o_ref, wgb_ref, wub_ref, wdb_ref, h_ref, *, nf):
    wgb_ref[...] = wg_ref[...].astype(BF16)
    wub_ref[...] = wu_ref[...].astype(BF16)
    wdb_ref[...] = wd_ref[...].astype(BF16)
    _ffn_step(pl.program_id(0), nf, x_ref, g_ref, wgb_ref, wub_ref, wdb_ref, o_ref, h_ref, o_ref)


def _ffn_cast(x, g, wg, wu, wd, li, lj, tf=512):
    M, D = x.shape
    F = wg.shape[-1]
    nf = F // tf
    return pl.pallas_call(
        functools.partial(_ffn_cast_kernel, nf=nf),
        grid=(nf,),
        in_specs=[
            pl.BlockSpec((M, D), lambda f: (0, 0)),
            pl.BlockSpec((None, None, 1, D), lambda f: (li, lj, 0, 0)),
            pl.BlockSpec((None, None, D, tf), lambda f: (li, lj, 0, f)),
            pl.BlockSpec((None, None, D, tf), lambda f: (li, lj, 0, f)),
            pl.BlockSpec((None, None, tf, D), lambda f: (li, lj, f, 0)),
        ],
        out_specs=[pl.BlockSpec((M, D), lambda f: (0, 0)),
                   pl.BlockSpec((D, tf), lambda f: (0, f)),
                   pl.BlockSpec((D, tf), lambda f: (0, f)),
                   pl.BlockSpec((tf, D), lambda f: (f, 0))],
        out_shape=[jax.ShapeDtypeStruct((M, D), F32),
                   jax.ShapeDtypeStruct((D, F), BF16),
                   jax.ShapeDtypeStruct((D, F), BF16),
                   jax.ShapeDtypeStruct((F, D), BF16)],
        scratch_shapes=[pltpu.VMEM((M, D), BF16)],
        compiler_params=_params(("arbitrary",)),
        name="ffn_cast",
    )(x, g, wg, wu, wd)


def _nmm_kernel(x_ref, g_ref, w_ref, o_ref, h_ref):
    @pl.when(pl.program_id(1) == 0)
    def _():
        h_ref[...] = _rms(x_ref[...], g_ref[...]).astype(BF16)

    o_ref[...] = _dot(h_ref[...], w_ref[...])


def _norm_matmul(x, g, w, tm, tn):
    M, D = x.shape
    N = w.shape[1]
    return pl.pallas_call(
        _nmm_kernel,
        grid=(M // tm, N // tn),
        in_specs=[
            pl.BlockSpec((tm, D), lambda i, j: (i, 0)),
            pl.BlockSpec((1, D), lambda i, j: (0, 0)),
            pl.BlockSpec((D, tn), lambda i, j: (0, j)),
        ],
        out_specs=pl.BlockSpec((tm, tn), lambda i, j: (i, j)),
        out_shape=jax.ShapeDtypeStruct((M, N), F32),
        scratch_shapes=[pltpu.VMEM((tm, D), BF16)],
        compiler_params=_params(("parallel", "arbitrary")),
        name="norm_matmul",
    )(x, g.reshape(1, D), w)


def _oproj_kernel(x_ref, a_ref, b_ref, w_ref, o_ref, *, ka):
    o_ref[...] = x_ref[...] + _dot(a_ref[...], w_ref[:ka, :]) + _dot(b_ref[...], w_ref[ka:, :])


def _out_proj(x, a, b, w, tm, tn=1024):
    M, D = x.shape
    ka, kb = a.shape[1], b.shape[1]
    return pl.pallas_call(
        functools.partial(_oproj_kernel, ka=ka),
        grid=(M // tm, D // tn),
        in_specs=[
            pl.BlockSpec((tm, tn), lambda i, j: (i, j)),
            pl.BlockSpec((tm, ka), lambda i, j: (i, 0)),
            pl.BlockSpec((tm, kb), lambda i, j: (i, 0)),
            pl.BlockSpec((ka + kb, tn), lambda i, j: (0, j)),
        ],
        out_specs=pl.BlockSpec((tm, tn), lambda i, j: (i, j)),
        out_shape=jax.ShapeDtypeStruct((M, D), F32),
        compiler_params=_params(("parallel", "parallel")),
        name="out_proj",
    )(x, a, b, w)


def _put_rows(ref, row, rows_per_token, val):
    ref[pl.ds(row, val.shape[0], stride=rows_per_token), :] = val


def _get_rows(ref, row, rows_per_token, n):
    return ref[pl.ds(row, n, stride=rows_per_token), :]


NSA_ROWS = 4 * NSA_KV_HEADS
WIN_ROWS = 2 * NSA_KV_HEADS
KVH_ROWS = 2 * DIL_HEADS


def _nsa_post_kernel(p_ref, cos_ref, sin_ref, g_ref, qn_ref, qr_ref, cmp_ref, rows_ref, win_ref, kvb_ref, mq_ref):
    cos, sin = cos_ref[...], sin_ref[...]
    q_g, ks_g, kw_g, mq_g = g_ref[0:1, :], g_ref[1:2, :], g_ref[2:3, :], g_ref[3:4, :]

    def tile(i):
        return p_ref[:, i * LANE:(i + 1) * LANE]

    for h in range(N_MIX_HEADS):
        qn = _rms(tile(h), q_g)
        qn_ref[:, h * LANE:(h + 1) * LANE] = qn.astype(BF16)
        qr_ref[:, h * LANE:(h + 1) * LANE] = _rope(qn, cos, sin).astype(BF16)
    for g in range(NSA_KV_HEADS):
        kc, vc = tile(12 + g), tile(15 + g)
        ks = _rope(_rms(tile(18 + g), ks_g), cos, sin)
        vs = tile(21 + g)
        kw = _rope(_rms(tile(24 + g), kw_g), cos, sin)
        vw = tile(27 + g)
        for c, val in enumerate((kc, vc)):
            cmp_ref[:, (c * 3 + g) * LANE:(c * 3 + g + 1) * LANE] = val
        for c, val in enumerate((kc, vc, ks, vs)):
            _put_rows(rows_ref, g * 4 + c, NSA_ROWS, val)
        for c, val in enumerate((kw, vw)):
            _put_rows(win_ref, g * 2 + c, WIN_ROWS, val)
        for c, val in enumerate((ks, vs, kw, vw)):
            kvb_ref[:, (c * 3 + g) * LANE:(c * 3 + g + 1) * LANE] = val.astype(BF16)
    for h in range(N_MEM_HEADS):
        mq_ref[:, h * LANE:(h + 1) * LANE] = _rms(tile(30 + h), mq_g).astype(BF16)


def _nsa_post(p, cos, sin, gains, tm):
    M = p.shape[0]
    row = lambda n: pl.BlockSpec((tm, n), lambda i: (i, 0))
    flat = lambda r: pl.BlockSpec((tm * r, LANE), lambda i: (i, 0))
    return pl.pallas_call(
        _nsa_post_kernel,
        grid=(M // tm,),
        in_specs=[row(PROJ_N), row(LANE), row(LANE), pl.BlockSpec((8, LANE), lambda i: (0, 0))],
        out_specs=[row(1536), row(1536), row(768), flat(NSA_ROWS), flat(WIN_ROWS), row(1536), row(512)],
        out_shape=[
            jax.ShapeDtypeStruct((M, 1536), BF16),
            jax.ShapeDtypeStruct((M, 1536), BF16),
            jax.ShapeDtypeStruct((M, 768), F32),
            jax.ShapeDtypeStruct((M * NSA_ROWS, LANE), F32),
            jax.ShapeDtypeStruct((M * WIN_ROWS, LANE), F32),
            jax.ShapeDtypeStruct((M, 1536), BF16),
            jax.ShapeDtypeStruct((M, 512), BF16),
        ],
        compiler_params=_params(("parallel",)),
        name="nsa_post",
    )(p, cos, sin, gains)


def _put_residues(ref, hh, val, dil, tmp_ref):
    sl = slice(hh * LANE, (hh + 1) * LANE)
    if dil == 1:
        ref[0, :, sl] = val.astype(BF16)
        return
    tmp_ref[...] = val
    n = val.shape[0] // dil
    for r in range(dil):
        ref[r, :, sl] = tmp_ref[pl.ds(r, n, stride=dil), :].astype(BF16)


def _dil_post_kernel(p_ref, cos_ref, sin_ref, g_ref, *refs, by_residue):
    if by_residue:
        mq_ref, st0_ref, st1_ref, st2_ref = refs[:4]
        qkv_refs, tmp_ref = refs[4:13], refs[13]
    else:
        qr_ref, mq_ref, st0_ref, st1_ref, st2_ref = refs
    cos, sin = cos_ref[...], sin_ref[...]
    q_g, k_g, mq_g = g_ref[0:1, :], g_ref[1:2, :], g_ref[2:3, :]
    st_refs = (st0_ref, st1_ref, st2_ref)
    for h in range(N_MIX_HEADS):
        g, hh = divmod(h, DIL_HEADS)
        sl = slice(h * LANE, (h + 1) * LANE)
        q = _rope(_rms(p_ref[:, sl], q_g), cos, sin)
        k = _rope(_rms(p_ref[:, (12 + h) * LANE:(13 + h) * LANE], k_g), cos, sin)
        v = p_ref[:, (24 + h) * LANE:(25 + h) * LANE]
        _put_rows(st_refs[g], hh, KVH_ROWS, k)
        _put_rows(st_refs[g], DIL_HEADS + hh, KVH_ROWS, v)
        if by_residue:
            for c, val in enumerate((q, k, v)):
                _put_residues(qkv_refs[3 * g + c], hh, val, DIL_PAIRS[g][1], tmp_ref)
        else:
            qr_ref[:, sl] = q.astype(BF16)
    for h in range(N_MEM_HEADS):
        mq_ref[:, h * LANE:(h + 1) * LANE] = _rms(p_ref[:, (36 + h) * LANE:(37 + h) * LANE], mq_g).astype(BF16)


def _dil_post(p, cos, sin, gains, tm, seq_len=None):
    M = p.shape[0]
    row = lambda n: pl.BlockSpec((tm, n), lambda i: (i, 0))
    flat = pl.BlockSpec((tm * KVH_ROWS, LANE), lambda i: (i, 0))
    st_shape = jax.ShapeDtypeStruct((M * KVH_ROWS, LANE), F32)
    out_specs = [row(512), flat, flat, flat]
    out_shape = [jax.ShapeDtypeStruct((M, 512), BF16), st_shape, st_shape, st_shape]
    scratch = []
    if seq_len is None:
        out_specs = [row(1536)] + out_specs
        out_shape = [jax.ShapeDtypeStruct((M, 1536), BF16)] + out_shape
    else:
        nt = seq_len // tm
        for _, dil in DIL_PAIRS:
            assert tm % (16 * dil) == 0
            spec = pl.BlockSpec((None, dil, tm // dil, 4 * LANE), lambda i: (i // nt, 0, i % nt, 0))
            shape = jax.ShapeDtypeStruct((M // seq_len, dil, seq_len // dil, 4 * LANE), BF16)
            out_specs += [spec] * 3
            out_shape += [shape] * 3
        scratch = [pltpu.VMEM((tm, LANE), F32)]
    return pl.pallas_call(
        functools.partial(_dil_post_kernel, by_residue=seq_len is not None),
        grid=(M // tm,),
        in_specs=[row(PROJ_N), row(LANE), row(LANE), pl.BlockSpec((8, LANE), lambda i: (0, 0))],
        out_specs=out_specs,
        out_shape=out_shape,
        scratch_shapes=scratch,
        compiler_params=_params(("parallel",)),
        name="dil_post",
    )(p, cos, sin, gains)


def _memkv_post_kernel(x_ref, g_ref, o_ref):
    for h in range(N_MEM_HEADS):
        _put_rows(o_ref, h, KVH_ROWS, _rms(x_ref[:, h * LANE:(h + 1) * LANE], g_ref[...]))
        _put_rows(o_ref, N_MEM_HEADS + h, KVH_ROWS, x_ref[:, (N_MEM_HEADS + h) * LANE:(N_MEM_HEADS + h + 1) * LANE])


def _memkv_post(x, g, tm=256):
    M, N = x.shape
    return pl.pallas_call(
        _memkv_post_kernel,
        grid=(M // tm,),
        in_specs=[pl.BlockSpec((tm, N), lambda i: (i, 0)), pl.BlockSpec((1, LANE), lambda i: (0, 0))],
        out_specs=pl.BlockSpec((tm * KVH_ROWS, LANE), lambda i: (i, 0)),
        out_shape=jax.ShapeDtypeStruct((M * KVH_ROWS, LANE), F32),
        compiler_params=_params(("parallel",)),
        name="memkv_post",
    )(x, g.reshape(1, LANE))


def _mem_attn_kernel(q_ref, kv_ref, o_ref, *, tq):
    for h in range(N_MEM_HEADS):
        sl = slice(h * LANE, (h + 1) * LANE)
        if tq == 1:
            q = _rows16(q_ref[:, sl], 1).astype(BF16)
        else:
            q = q_ref[:, sl]
        k = _get_rows(kv_ref, h, KVH_ROWS, N_MEM).astype(BF16)
        v = _get_rows(kv_ref, N_MEM_HEADS + h, KVH_ROWS, N_MEM).astype(BF16)
        s = _dot_t(q, k) * SCALE
        m = jnp.max(s, axis=-1, keepdims=True)
        e = jnp.exp(s - m)
        p = e / jnp.sum(e, axis=-1, keepdims=True)
        o = _dot(p.astype(BF16), v)
        o_ref[:, sl] = o[0:tq, :].astype(o_ref.dtype)


def _mem_attn(q, kv, tq):
    B, T, _ = q.shape
    return pl.pallas_call(
        functools.partial(_mem_attn_kernel, tq=tq),
        grid=(B, T // tq),
        in_specs=[
            pl.BlockSpec((None, tq, 512), lambda b, i: (b, i, 0)),
            pl.BlockSpec((None, N_MEM * KVH_ROWS, LANE), lambda b, i: (b, 0, 0)),
        ],
        out_specs=pl.BlockSpec((None, tq, 512), lambda b, i: (b, i, 0)),
        out_shape=jax.ShapeDtypeStruct((B, T, 512), q.dtype),
        compiler_params=_params(("parallel", "parallel")),
        name="mem_attn",
    )(q, kv)


def _gelu_tanh(x):
    return 0.5 * x * (1.0 + jnp.tanh(0.7978845608028654 * (x + 0.044715 * (x * x * x))))


def _compress_finish(h, b1, w2):
    n = h.shape[0]
    hid = b1 + h[:, :LANE] + pltpu.roll(h[:, LANE:], n - 1, 0)
    return _dot(_gelu_tanh(hid).astype(BF16), w2)


def _compress(x_bf, w1, b1, w2):
    return _compress_finish(_dot(x_bf, w1), b1, w2)


def _cmp_prompt_kernel(x_ref, w1_ref, b1_ref, w2_ref, kcg_ref, o_ref, xs_ref, *, n):
    kv = pl.program_id(1)
    for c in range(CMP_STRIDE):
        xs_ref[:, c * LANE:(c + 1) * LANE] = x_ref[pl.ds(c, n, stride=CMP_STRIDE), :].astype(BF16)
    out = _compress(xs_ref[...], w1_ref[...], b1_ref[...], w2_ref[...])
    out = jnp.where(kv == 0, _rms(out, kcg_ref[...]), out)
    rid = lax.broadcasted_iota(jnp.int32, out.shape, 0)
    o_ref[...] = jnp.where(rid < n - 1, out, 0.0).astype(BF16)


def _cmp_prompt(rows, w1r, b1, w2, kc_g):
    B, T, _ = rows.shape
    n = T // CMP_STRIDE
    return pl.pallas_call(
        functools.partial(_cmp_prompt_kernel, n=n),
        grid=(B, 2, NSA_KV_HEADS),
        in_specs=[
            pl.BlockSpec((None, T, LANE), lambda b, kv, g: (b, 0, kv * 3 + g)),
            pl.BlockSpec((None, CMP_STRIDE * LANE, 2 * LANE), lambda b, kv, g: (kv, 0, 0)),
            pl.BlockSpec((None, 1, LANE), lambda b, kv, g: (kv, 0, 0)),
            pl.BlockSpec((None, LANE, LANE), lambda b, kv, g: (kv, 0, 0)),
            pl.BlockSpec((1, LANE), lambda b, kv, g: (0, 0)),
        ],
        out_specs=pl.BlockSpec((None, None, None, n, LANE), lambda b, kv, g: (b, kv, g, 0, 0)),
        out_shape=jax.ShapeDtypeStruct((B, 2, NSA_KV_HEADS, n, LANE), BF16),
        scratch_shapes=[pltpu.VMEM((n, CMP_STRIDE * LANE), BF16)],
        compiler_params=_params(("parallel", "parallel", "parallel")),
        name="cmp_prompt",
    )(rows, w1r, b1, w2, kc_g)


def _select_blocks(score, cur, n_blocks):
    tq = score.shape[0]
    blk = lax.broadcasted_iota(jnp.int32, score.shape, 1)
    forced = (blk == 0) | (blk == cur) | (blk == cur - 1)
    sc = jnp.where(blk <= cur, jnp.where(forced, FORCE_SCORE, score), NEG)
    sct = sc.T[0:n_blocks, :]
    bi = lax.broadcasted_iota(jnp.int32, sct.shape, 0)
    rank = jnp.zeros(sct.shape, F32)
    for i in range(n_blocks):
        si = sct[i:i + 1, :]
        ahead = (si > sct) | ((si == sct) & (bi > i))
        rank = rank + jnp.where(ahead, 1.0, 0.0)
    chosen = jnp.where((rank < SEL_TOPK) & (sct > 0.5 * NEG), 1.0, 0.0)
    return jnp.concatenate([chosen, jnp.zeros((LANE - n_blocks, tq), F32)], axis=0).T


def _score_tiles(q, k_ref, tiles, s_ref, m_ref, tk, first=False):
    ms = []
    for kt, slot, bias in tiles:
        s = _dot_t(q, k_ref[pl.ds(pl.multiple_of(kt * tk, tk), tk), :]) * SCALE
        if bias is not None:
            s = s + bias
        s_ref[slot] = s
        ms += [s[:, c * LANE:(c + 1) * LANE] for c in range(tk // LANE)]
    m = functools.reduce(jnp.maximum, ms)
    m_ref[...] = m if first else jnp.maximum(m_ref[...], m)


def _value_tiles(v_ref, tiles, s_ref, m_ref, l_ref, acc_ref, tk, first=False):
    m = m_ref[...]
    ls, pvs = [], []
    for kt, slot in tiles:
        es = [jnp.exp(s_ref[slot, :, c * LANE:(c + 1) * LANE] - m) for c in range(tk // LANE)]
        pvs.append(_dot(jnp.concatenate(es, axis=1).astype(BF16),
                        v_ref[pl.ds(pl.multiple_of(kt * tk, tk), tk), :]))
        ls += es
    l = functools.reduce(lambda a, b: a + b, ls)
    pv = functools.reduce(lambda a, b: a + b, pvs)
    if first:
        l_ref[...] = l
        acc_ref[...] = pv
    else:
        l_ref[...] += l
        acc_ref[...] += pv


def _nsa_attn_kernel(qn_ref, qr_ref, gl_ref, kc_ref, vc_ref, ks_ref, vs_ref, kw_ref, vw_ref, cover_ref,
                     o_ref, s_ref, m_ref, l_ref, acc_ref, sel_ref, *, tq, ns):
    qi = pl.program_id(2)
    R = NSA_GROUP
    t0 = qi * tq
    stack = lambda ref: jnp.concatenate([ref[:, r * LANE:(r + 1) * LANE] for r in range(R)], axis=0)
    rows4 = lambda x: jnp.concatenate([x] * R, axis=0)
    tpos_q = t0 + lax.broadcasted_iota(jnp.int32, (tq, 1), 0)
    row_in = lax.broadcasted_iota(jnp.int32, (R * tq, 1), 0) & (tq - 1)
    col = lax.broadcasted_iota(jnp.int32, (R * tq, tq), 1)

    s = _dot_t(stack(qn_ref), kc_ref[...]) * SCALE
    cblk = lax.broadcasted_iota(jnp.int32, (R * tq, LANE), 1)
    cmask = (CMP_STRIDE * cblk + (CMP_BLOCK - 1) <= t0 + row_in) & (cblk < kc_ref.shape[0] - 1)
    p, _, _ = _softmax_masked(s, cmask)
    o_cmp = _dot(p.astype(BF16), vc_ref[...])
    imp = p[0:tq] + p[tq:2 * tq] + p[2 * tq:3 * tq] + p[3 * tq:4 * tq]
    score = _dot3(imp, cover_ref[...])
    cur = lax.shift_right_arithmetic(tpos_q, SEL_BLOCK.bit_length() - 1)

    @pl.when(t0 + tq <= SEL_TOPK * SEL_BLOCK)
    def _():
        sel_ref[...] = jnp.where(lax.broadcasted_iota(jnp.int32, (tq, LANE), 1) <= cur, 1.0, 0.0).astype(BF16)

    @pl.when(t0 + tq > SEL_TOPK * SEL_BLOCK)
    def _():
        sel_ref[...] = _select_blocks(score, cur, ns).astype(BF16)

    sel = sel_ref[...]
    q_rot = stack(qr_ref)
    blocks_per_tile = tq // SEL_BLOCK
    causal = jnp.where(col <= row_in, 0.0, NEG)
    far = jnp.where(col >= row_in, 0.0, NEG)

    def row_max():
        m_ref[...] = jnp.broadcast_to(jnp.max(m_ref[...], axis=-1, keepdims=True), m_ref.shape)

    def result():
        return acc_ref[...] / jnp.maximum(jnp.sum(l_ref[...], axis=-1, keepdims=True), 1e-30)

    def member_bias(kt):
        key_blk = lax.shift_right_arithmetic(lax.broadcasted_iota(jnp.int32, (LANE, tq), 1),
                                             SEL_BLOCK.bit_length() - 1)
        expand = lax.broadcasted_iota(jnp.int32, (LANE, tq), 0) == kt * blocks_per_tile + key_blk
        member = _dot(sel, jnp.where(expand, 1.0, 0.0).astype(BF16))
        return rows4((member - 1.0) * (-NEG))

    sel_tile = lambda kt: (kt, kt, member_bias(kt))
    _score_tiles(q_rot, ks_ref, [(qi, qi, member_bias(qi) + causal)], s_ref, m_ref, tq, first=True)

    def sel_scores(i, carry):
        _score_tiles(q_rot, ks_ref, [sel_tile(2 * i), sel_tile(2 * i + 1)], s_ref, m_ref, tq)
        return carry

    lax.fori_loop(0, qi // 2, sel_scores, 0)

    @pl.when(qi % 2 == 1)
    def _():
        _score_tiles(q_rot, ks_ref, [sel_tile(qi - 1)], s_ref, m_ref, tq)

    row_max()
    _value_tiles(vs_ref, [(qi, qi)], s_ref, m_ref, l_ref, acc_ref, tq, first=True)

    def sel_values(i, carry):
        _value_tiles(vs_ref, [(2 * i, 2 * i), (2 * i + 1, 2 * i + 1)], s_ref, m_ref, l_ref, acc_ref, tq)
        return carry

    lax.fori_loop(0, qi // 2, sel_values, 0)

    @pl.when(qi % 2 == 1)
    def _():
        _value_tiles(vs_ref, [(qi - 1, qi - 1)], s_ref, m_ref, l_ref, acc_ref, tq)

    o_sel = result()

    n_back = NSA_WINDOW // tq
    win_tiles = []
    for back in range(n_back + 1):
        bias = jnp.where(qi >= back, 0.0, NEG)
        if back == 0:
            bias = causal
        elif back == n_back:
            bias = far + bias
        win_tiles.append((jnp.maximum(qi - back, 0), back, bias))
    _score_tiles(q_rot, kw_ref, win_tiles, s_ref, m_ref, tq, first=True)
    row_max()
    _value_tiles(vw_ref, [(kt, slot) for kt, slot, _ in win_tiles], s_ref, m_ref, l_ref, acc_ref, tq, first=True)
    o_win = result()

    gates = jax.nn.sigmoid(gl_ref[...])
    for r in range(R):
        rs = slice(r * tq, (r + 1) * tq)
        o = (gates[:, 3 * r:3 * r + 1] * o_cmp[rs] + gates[:, 3 * r + 1:3 * r + 2] * o_sel[rs]
             + gates[:, 3 * r + 2:3 * r + 3] * o_win[rs])
        o_ref[:, r * LANE:(r + 1) * LANE] = o.astype(BF16)


def _nsa_attn(qn, qr, proj, cmp_kv, kvb, cover, tq=256):
    B, T, _ = qn.shape
    nc = cmp_kv.shape[3]
    G = NSA_KV_HEADS
    assert tq % LANE == 0 and NSA_WINDOW % tq == 0 and nc == LANE and T // SEL_BLOCK <= LANE
    rows = NSA_GROUP * tq
    qspec = pl.BlockSpec((None, tq, 4 * LANE), lambda b, g, i: (b, i, g))
    kvspec = lambda c: pl.BlockSpec((None, T, LANE), lambda b, g, i: (b, 0, c * 3 + g))
    return pl.pallas_call(
        functools.partial(_nsa_attn_kernel, tq=tq, ns=T // SEL_BLOCK),
        grid=(B, G, T // tq),
        in_specs=[
            qspec, qspec,
            pl.BlockSpec((None, tq, LANE), lambda b, g, i: (b, i, NSA_GATE_BLK + g)),
            pl.BlockSpec((None, None, None, nc, LANE), lambda b, g, i: (b, 0, g, 0, 0)),
            pl.BlockSpec((None, None, None, nc, LANE), lambda b, g, i: (b, 1, g, 0, 0)),
            kvspec(0), kvspec(1), kvspec(2), kvspec(3),
            pl.BlockSpec((nc, LANE), lambda b, g, i: (0, 0)),
        ],
        out_specs=qspec,
        out_shape=jax.ShapeDtypeStruct((B, T, 1536), BF16),
        scratch_shapes=[pltpu.VMEM((T // tq, rows, tq), F32),
                        pltpu.VMEM((rows, LANE), F32),
                        pltpu.VMEM((rows, LANE), F32),
                        pltpu.VMEM((rows, LANE), F32),
                        pltpu.VMEM((tq, LANE), BF16)],
        compiler_params=_params(("parallel", "parallel", "arbitrary")),
        name="nsa_attn",
    )(qn, qr, proj, cmp_kv, cmp_kv, kvb, kvb, kvb, kvb, cover)


def _dil_band_kernel(q_ref, kp_ref, kc_ref, vp_ref, vc_ref, o_ref, st_ref, *, tq, window):
    i = pl.program_id(2)
    w = window
    row = lax.broadcasted_iota(jnp.int32, (w, 2 * w), 0)
    colk = lax.broadcasted_iota(jnp.int32, (w, 2 * w), 1)
    diff = w + row - colk
    band = (diff >= 0) & (diff <= window)
    lane = lax.broadcasted_iota(jnp.int32, (w, LANE), 1)
    for r in range(q_ref.shape[0]):
        for j in range(tq // w):
            mask = band & (i * tq + (j - 1) * w + colk >= 0)
            rows = slice(j * w, (j + 1) * w)
            stats = jnp.zeros((w, LANE), F32)
            for h in range(DIL_HEADS):
                sl = slice(h * LANE, (h + 1) * LANE)
                if j == 0:
                    k = jnp.concatenate([kp_ref[r, :, sl], kc_ref[r, 0:w, sl]], axis=0)
                    v = jnp.concatenate([vp_ref[r, :, sl], vc_ref[r, 0:w, sl]], axis=0)
                else:
                    k = kc_ref[r, (j - 1) * w:(j + 1) * w, sl]
                    v = vc_ref[r, (j - 1) * w:(j + 1) * w, sl]
                p, m, l = _softmax_masked(_dot_t(q_ref[r, rows, sl], k) * SCALE, mask)
                o_ref[r, rows, sl] = _dot(p.astype(BF16), v)
                stats = jnp.where(lane == h, m, stats)
                stats = jnp.where(lane == DIL_HEADS + h, l, stats)
            st_ref[r, rows, :] = stats


def _dil_band(q, k, v, g, tq, nr):
    B, dil, S, _ = q.shape
    window = DIL_PAIRS[g][0] // dil
    assert tq % window == 0 and S % tq == 0 and dil % nr == 0
    cur = lambda w: pl.BlockSpec((None, nr, tq, w), lambda b, r, i: (b, r, i, 0))
    prev = pl.BlockSpec((None, nr, window, 4 * LANE),
                        lambda b, r, i: (b, r, jnp.maximum(i * (tq // window) - 1, 0), 0))
    return pl.pallas_call(
        functools.partial(_dil_band_kernel, tq=tq, window=window),
        grid=(B, dil // nr, S // tq),
        in_specs=[cur(4 * LANE), prev, cur(4 * LANE), prev, cur(4 * LANE)],
        out_specs=[cur(4 * LANE), cur(LANE)],
        out_shape=[jax.ShapeDtypeStruct((B, dil, S, 4 * LANE), F32),
                   jax.ShapeDtypeStruct((B, dil, S, LANE), F32)],
        compiler_params=_params(("parallel", "parallel", "parallel")),
        name=f"dil_band{g}",
    )(q, k, k, v, v)


def _mix_groups(os_, ms, ls):
    m_all = jnp.maximum(jnp.maximum(ms[0], ms[1]), ms[2])
    ws = [jnp.exp(m - m_all) * l for m, l in zip(ms, ls)]
    tot = ws[0] + ws[1] + ws[2]
    return (ws[0] / tot) * os_[0] + (ws[1] / tot) * os_[1] + (ws[2] / tot) * os_[2]


def _dil_mix_kernel(o0_ref, o1_ref, o2_ref, s0_ref, s1_ref, s2_ref, o_ref, nat_ref):
    o_refs, s_refs = (o0_ref, o1_ref, o2_ref), (s0_ref, s1_ref, s2_ref)
    tm = o_ref.shape[0]

    def token_order(ref, sl):
        dil = ref.shape[0]
        if dil == 1:
            return ref[0, :, sl]
        for r in range(dil):
            nat_ref[pl.ds(r, tm // dil, stride=dil), :] = ref[r, :, sl]
        return nat_ref[...]

    stats = [token_order(s, slice(0, LANE)) for s in s_refs]
    for h in range(DIL_HEADS):
        sl = slice(h * LANE, (h + 1) * LANE)
        ms = [s[:, h:h + 1] for s in stats]
        ls = [s[:, DIL_HEADS + h:DIL_HEADS + h + 1] for s in stats]
        o_ref[:, sl] = _mix_groups([token_order(o, sl) for o in o_refs], ms, ls).astype(BF16)


def _dil_mix(os_, sts, tm=512):
    B, _, T, _ = os_[0].shape
    nt = T // tm
    in_specs = []
    for lanes, arrs in ((4 * LANE, os_), (LANE, sts)):
        for a in arrs:
            dil = a.shape[1]
            in_specs.append(pl.BlockSpec((None, dil, tm // dil, lanes), lambda b, i: (b, 0, i, 0)))
    return pl.pallas_call(
        _dil_mix_kernel,
        grid=(B, nt),
        in_specs=in_specs,
        out_specs=pl.BlockSpec((tm, 4 * LANE), lambda b, i: (b * nt + i, 0)),
        out_shape=jax.ShapeDtypeStruct((B * T, 4 * LANE), BF16),
        scratch_shapes=[pltpu.VMEM((tm, LANE), F32)],
        compiler_params=_params(("parallel", "parallel")),
        name="dil_mix",
    )(*os_, *sts)


DEC_PAGES_PER_STEP = 16


def _dec_select_kernel(tbl_ref, *refs, n_pp, n_steps):
    pages = refs[:n_pp]
    (qn_ref, w1_ref, b1_ref, w2_ref, kcg_ref, cover_ref, idx_ref, ocmp_ref, h_ref) = refs[n_pp:]
    j = pl.program_id(1)
    cpp = PAGE_SIZE // CMP_STRIDE
    rows = n_pp * cpp
    n = n_steps * rows
    ns = cover_ref.shape[1]

    for g in range(NSA_KV_HEADS):
        for kv in range(2):
            x = jnp.concatenate(
                [_get_rows(pg, g * 4 + kv, NSA_ROWS, PAGE_SIZE).reshape(cpp, CMP_STRIDE * LANE) for pg in pages],
                axis=0)
            h_ref[g * 2 + kv, pl.ds(pl.multiple_of(j * rows, rows), rows), :] = _dot(x.astype(BF16), w1_ref[kv])

    @pl.when(j == n_steps - 1)
    def _():
        idx_ref[...] = jnp.zeros_like(idx_ref)
        for g in range(NSA_KV_HEADS):
            kc = _rms(_compress_finish(h_ref[g * 2], b1_ref[0], w2_ref[0]), kcg_ref[...]).astype(BF16)
            vc = _compress_finish(h_ref[g * 2 + 1], b1_ref[1], w2_ref[1]).astype(BF16)
            q = _rows16(qn_ref[:, g * NSA_GROUP * LANE:(g + 1) * NSA_GROUP * LANE], NSA_GROUP).astype(BF16)
            s = _dot_t(q, kc) * SCALE
            valid = lax.broadcasted_iota(jnp.int32, s.shape, 1) < n - 1
            p, _, _ = _softmax_masked(s, valid)
            ocmp_ref[g] = _dot(p.astype(BF16), vc)[0:NSA_GROUP, :]
            rid = lax.broadcasted_iota(jnp.int32, p.shape, 0)
            imp = jnp.sum(jnp.where(rid < NSA_GROUP, p, 0.0), axis=0, keepdims=True)
            score = _dot3(jnp.broadcast_to(imp, (8, n)), cover_ref[...])

            a = jnp.broadcast_to(score[0:1, :], (ns, ns))
            lane = lax.broadcasted_iota(jnp.int32, (ns, ns), 1)
            sub = lax.broadcasted_iota(jnp.int32, (ns, ns), 0)
            cur = n * CMP_STRIDE // SEL_BLOCK
            forced = (lane == 0) | (lane == cur) | (lane == cur - 1)
            a = jnp.where(lane <= cur, jnp.where(forced, FORCE_SCORE, a), NEG)
            at = a.T
            ahead_r = (at > a) | ((at == a) & (sub < lane))
            chosen_r = ((jnp.sum(jnp.where(ahead_r, 1.0, 0.0), axis=0, keepdims=True) < SEL_TOPK)
                        & (a[0:1, :] > 0.5 * NEG))
            ahead_c = (a > at) | ((a == at) & (lane < sub))
            chosen_c = ((jnp.sum(jnp.where(ahead_c, 1.0, 0.0), axis=1, keepdims=True) < SEL_TOPK)
                        & (at[:, 0:1] > 0.5 * NEG))
            before = jnp.sum(jnp.where(chosen_r & (lane < sub), 1.0, 0.0), axis=1, keepdims=True)
            slot = lax.broadcasted_iota(jnp.int32, (ns, LANE), 1)
            onehot = chosen_c & (before == slot.astype(F32))
            blk = lax.broadcasted_iota(jnp.int32, (ns, LANE), 0)
            picked = jnp.sum(jnp.where(onehot, blk.astype(F32), 0.0), axis=0, keepdims=True)
            filled = jnp.sum(jnp.where(onehot, 1.0, 0.0), axis=0, keepdims=True)
            idx_ref[g:g + 1, :] = jnp.where(filled > 0.5, picked, float(cur)).astype(jnp.int32)


def _dec_select(cache, table, qn, w1r, b1, w2, kc_g, cover):
    B, n_pages = table.shape
    n_pp = DEC_PAGES_PER_STEP
    n_steps = n_pages // n_pp
    n = n_pages * (PAGE_SIZE // CMP_STRIDE)
    ns = cover.shape[1]
    G = NSA_KV_HEADS
    const = lambda *shape: pl.BlockSpec(shape, lambda b, j, tbl: (0,) * len(shape))
    page_spec = lambda p: pl.BlockSpec((None, PAGE_SIZE * NSA_ROWS, LANE), lambda b, j, tbl: (tbl[b, j * n_pp + p], 0, 0))
    grid_spec = pltpu.PrefetchScalarGridSpec(
        num_scalar_prefetch=1,
        grid=(B, n_steps),
        in_specs=[page_spec(p) for p in range(n_pp)] + [
            pl.BlockSpec((None, 1, 1536), lambda b, j, tbl: (b, 0, 0)),
            const(2, CMP_STRIDE * LANE, 2 * LANE), const(2, 1, LANE), const(2, LANE, LANE), const(1, LANE),
            const(n, ns),
        ],
        out_specs=[pl.BlockSpec((None, 8, LANE), lambda b, j, tbl: (b, 0, 0)),
                   pl.BlockSpec((None, G, NSA_GROUP, LANE), lambda b, j, tbl: (b, 0, 0, 0))],
        scratch_shapes=[pltpu.VMEM((2 * G, n, 2 * LANE), F32)],
    )
    return pl.pallas_call(
        functools.partial(_dec_select_kernel, n_pp=n_pp, n_steps=n_steps),
        grid_spec=grid_spec,
        out_shape=[jax.ShapeDtypeStruct((B, 8, LANE), jnp.int32),
                   jax.ShapeDtypeStruct((B, G, NSA_GROUP, LANE), F32)],
        compiler_params=_params(("arbitrary", "arbitrary")),
        name="dec_select",
    )(table, *([cache] * n_pp), qn, w1r, b1, w2, kc_g, cover)


def _dec_attn_kernel(tbl_ref, idx_ref, *refs, n_sel, cur, wb):
    blocks = refs[:n_sel]
    (qr_ref, kvn_ref, win_ref, gl_ref, ocmp_ref, o_ref) = refs[n_sel:]
    b, g = pl.program_id(0), pl.program_id(1)
    R = NSA_GROUP
    q = _rows16(qr_ref[...], R).astype(BF16)
    qf = q.astype(F32)
    new = kvn_ref[...]
    ks_n, vs_n, kw_n, vw_n = (new[:, c * LANE:(c + 1) * LANE] for c in range(4))

    k = jnp.concatenate([_get_rows(r, g * 4 + 2, NSA_ROWS, SEL_BLOCK) for r in blocks], axis=0).astype(BF16)
    v = jnp.concatenate([_get_rows(r, g * 4 + 3, NSA_ROWS, SEL_BLOCK) for r in blocks], axis=0).astype(BF16)
    s = _dot_t(q, k) * SCALE
    blk_of = lax.shift_right_arithmetic(lax.broadcasted_iota(jnp.int32, s.shape, 1), SEL_BLOCK.bit_length() - 1)
    valid = jnp.zeros(s.shape, jnp.int32)
    for n in range(n_sel):
        is_past = jnp.where(idx_ref[(b * NSA_KV_HEADS + g) * n_sel + n] != cur, 1, 0)
        valid = jnp.where(blk_of == n, is_past, valid)
    valid = valid > 0
    s_new = jnp.sum(qf * ks_n, axis=-1, keepdims=True) * SCALE
    s = jnp.where(valid, s, NEG)
    m = jnp.maximum(jnp.max(s, axis=-1, keepdims=True), s_new)
    e = jnp.where(valid, jnp.exp(s - m), 0.0)
    e_new = jnp.exp(s_new - m)
    l = jnp.sum(e, axis=-1, keepdims=True) + e_new
    o_sel = (_dot(e.astype(BF16), v) + e_new.astype(BF16).astype(F32) * vs_n) / l

    s = _dot_t(q, _get_rows(win_ref, g * 2, WIN_ROWS, wb).astype(BF16)) * SCALE
    s_new = jnp.sum(qf * kw_n, axis=-1, keepdims=True) * SCALE
    m = jnp.maximum(jnp.max(s, axis=-1, keepdims=True), s_new)
    e = jnp.exp(s - m)
    e_new = jnp.exp(s_new - m)
    l = jnp.sum(e, axis=-1, keepdims=True) + e_new
    o_win = (_dot(e.astype(BF16), _get_rows(win_ref, g * 2 + 1, WIN_ROWS, wb).astype(BF16))
             + e_new.astype(BF16).astype(F32) * vw_n) / l

    gates = jax.nn.sigmoid(gl_ref[...])
    o_cmp = ocmp_ref[...]
    for r in range(R):
        o = (gates[:, 3 * r:3 * r + 1] * o_cmp[r:r + 1] + gates[:, 3 * r + 1:3 * r + 2] * o_sel[r:r + 1]
             + gates[:, 3 * r + 2:3 * r + 3] * o_win[r:r + 1])
        o_ref[:, r * LANE:(r + 1) * LANE] = o


def _dec_attn(cache, table, idx, qr, kvb, win_state, proj, ocmp):
    B, n_pages = table.shape
    n_sel = SEL_TOPK
    cur = n_pages * PAGE_SIZE // SEL_BLOCK
    wb = win_state.shape[1] // WIN_ROWS
    assert wb <= NSA_WINDOW
    halves = PAGE_SIZE // SEL_BLOCK

    def blk_spec(n):
        def imap(b, g, tbl, idx):
            i = jnp.minimum(idx[(b * NSA_KV_HEADS + g) * n_sel + n], cur - 1)
            return (tbl[b, i // halves], i % halves, 0)
        return pl.BlockSpec((None, SEL_BLOCK * NSA_ROWS, LANE), imap)

    grid_spec = pltpu.PrefetchScalarGridSpec(
        num_scalar_prefetch=2,
        grid=(B, NSA_KV_HEADS),
        in_specs=[blk_spec(n) for n in range(n_sel)] + [
            pl.BlockSpec((None, 1, 4 * LANE), lambda b, g, tbl, idx: (b, 0, g)),
            pl.BlockSpec((None, 1, 4 * LANE), lambda b, g, tbl, idx: (b, 0, g)),
            pl.BlockSpec((None, wb * WIN_ROWS, LANE), lambda b, g, tbl, idx: (b, 0, 0)),
            pl.BlockSpec((None, 1, LANE), lambda b, g, tbl, idx: (b, 0, NSA_GATE_BLK + g)),
            pl.BlockSpec((None, None, NSA_GROUP, LANE), lambda b, g, tbl, idx: (b, g, 0, 0)),
        ],
        out_specs=pl.BlockSpec((None, 1, 4 * LANE), lambda b, g, tbl, idx: (b, 0, g)),
    )
    kvn = kvb.reshape(B, 1, 4, NSA_KV_HEADS, LANE).transpose(0, 1, 3, 2, 4).reshape(B, 1, 1536)
    return pl.pallas_call(
        functools.partial(_dec_attn_kernel, n_sel=n_sel, cur=cur, wb=wb),
        grid_spec=grid_spec,
        out_shape=jax.ShapeDtypeStruct((B, 1, 1536), F32),
        compiler_params=_params(("arbitrary", "arbitrary")),
        name="dec_attn",
    )(table, idx, *([cache] * n_sel), qr, kvn, win_state, proj, ocmp)


def _dec_dil_kernel(q_ref, n0_ref, n1_ref, n2_ref, s0_ref, s1_ref, s2_ref, o_ref, t0_ref, t1_ref, t2_ref):
    qall = q_ref[...]
    states, news, outs = (s0_ref, s1_ref, s2_ref), (n0_ref, n1_ref, n2_ref), (t0_ref, t1_ref, t2_ref)

    for st, new, out in zip(states, news, outs):
        keep = st.shape[0] - KVH_ROWS
        out[0:keep, :] = st[KVH_ROWS:, :]
        out[keep:, :] = new[...]

    for h in range(DIL_HEADS):
        os_, ms, ls = [], [], []
        for g, st in enumerate(states):
            hs = slice((g * DIL_HEADS + h) * LANE, (g * DIL_HEADS + h + 1) * LANE)
            window, dil = DIL_PAIRS[g]
            kn = news[g][h:h + 1, :].astype(BF16).astype(F32)
            vn = news[g][DIL_HEADS + h:DIL_HEADS + h + 1, :].astype(BF16).astype(F32)
            q = _rows16(qall[:, hs], 1).astype(BF16)
            k = _get_rows(st, h, KVH_ROWS * dil, window // dil).astype(BF16)
            v = _get_rows(st, DIL_HEADS + h, KVH_ROWS * dil, window // dil).astype(BF16)
            s = _dot_t(q, k) * SCALE
            s_new = jnp.sum(q.astype(F32) * kn, axis=-1, keepdims=True) * SCALE
            m = jnp.maximum(jnp.max(s, axis=-1, keepdims=True), s_new)
            e = jnp.exp(s - m)
            e_new = jnp.exp(s_new - m)
            l = jnp.sum(e, axis=-1, keepdims=True) + e_new
            ln = jnp.maximum(l, 1e-30)
            o = _dot((e / ln).astype(BF16), v) + (e_new / ln).astype(BF16).astype(F32) * vn
            os_.append(o)
            ms.append(m)
            ls.append(l)
        o_ref[:, h * LANE:(h + 1) * LANE] = _mix_groups(os_, ms, ls)[0:1, :]


def _dec_dil(qr, news, states):
    B = qr.shape[0]
    in_specs = [pl.BlockSpec((None, 1, 1536), lambda b: (b, 0, 0))]
    in_specs += [pl.BlockSpec((None, KVH_ROWS, LANE), lambda b: (b, 0, 0))] * len(DIL_PAIRS)
    st_specs = []
    for g, (window, dil) in enumerate(DIL_PAIRS):
        assert states[g].shape[1] == window * KVH_ROWS, "rolling buffer shorter than the window is not supported"
        st_specs.append(pl.BlockSpec((None, window * KVH_ROWS, LANE), lambda b: (b, 0, 0)))
    return pl.pallas_call(
        _dec_dil_kernel,
        grid=(B,),
        in_specs=in_specs + st_specs,
        out_specs=[pl.BlockSpec((None, 1, 4 * LANE), lambda b: (b, 0, 0))] + st_specs,
        out_shape=[jax.ShapeDtypeStruct((B, 1, 4 * LANE), F32)]
        + [jax.ShapeDtypeStruct(s.shape, F32) for s in states],
        compiler_params=_params(("parallel",)),
        name="dec_dil",
    )(qr, *news, *states)


def _rope_tables(pos):
    half = HEAD_DIM // 2
    inv = ROPE_THETA ** (-jnp.arange(half, dtype=F32) / half)
    ang = pos.astype(F32)[:, None] * inv
    cos, sin = jnp.cos(ang), jnp.sin(ang)
    return jnp.concatenate([cos, cos], axis=-1), jnp.concatenate([-sin, sin], axis=-1)


def _cover(nc, ns, rows, cols):
    c0 = jnp.arange(nc)[:, None] * CMP_STRIDE
    s0 = jnp.arange(ns)[None, :] * SEL_BLOCK
    cover = jnp.clip(jnp.minimum(c0 + CMP_BLOCK, s0 + SEL_BLOCK) - jnp.maximum(c0, s0), 0, CMP_BLOCK)
    cover = cover.astype(F32) / CMP_BLOCK
    return jnp.pad(cover, ((0, rows - nc), (0, cols - ns))).astype(BF16)


def _pad_gains(*gs):
    return jnp.pad(jnp.stack(gs, axis=0), ((0, 8 - len(gs)), (0, 0)))


def kernel(x_prompt, x_sample, mem_prompt, cache_nsa_kv, page_table, state_nsa_win, state_dil_0, state_dil_1,
           state_dil_2, cache_mem_kv, ff_norm, ff_w_gate, ff_w_up, ff_w_down, mix_norm, mem_norm, w_mem_kv,
           mem_q_g, mem_k_g, nsa_w_in, nsa_q_g, nsa_kc_g, nsa_ks_g, nsa_kw_g, nsa_cmp_w1, nsa_cmp_b1, nsa_cmp_w2,
           nsa_w_out, dil_w_in, dil_q_g, dil_k_g, dil_w_out):
    B, T, D = x_prompt.shape
    Bs = x_sample.shape[0]
    assert x_sample.shape[1] == 1, "the sample group is a single-token decode step"
    n_pages = page_table.shape[1]
    past_len = n_pages * PAGE_SIZE
    H, G, d = N_MIX_HEADS, NSA_KV_HEADS, HEAD_DIM

    wg, wu, wd = ff_w_gate, ff_w_up, ff_w_down
    ffg = ff_norm.reshape(ff_norm.shape[0], 2, 1, D)
    gate_w = jnp.pad(nsa_w_in[:, H * d:H * d + 3 * H].reshape(D, G, 3 * NSA_GROUP), ((0, 0), (0, 0), (0, LANE - 12)))
    nsa_w = jnp.concatenate([nsa_w_in[:, :H * d], nsa_w_in[:, H * d + 3 * H:], gate_w.reshape(D, G * LANE)], axis=1)
    nsa_w = jnp.pad(nsa_w, ((0, 0), (0, PROJ_N - nsa_w.shape[1]))).astype(BF16)
    dil_w = dil_w_in.astype(BF16)
    nsa_wo, dil_wo = nsa_w_out.astype(BF16), dil_w_out.astype(BF16)
    w1r = nsa_cmp_w1.reshape(2, 2, CMP_STRIDE, d, d).transpose(0, 2, 3, 1, 4).reshape(2, CMP_STRIDE * d, 2 * d)
    w1r = w1r.astype(BF16)
    cmp_b1 = nsa_cmp_b1.reshape(2, 1, d)
    cmp_w2 = nsa_cmp_w2.astype(BF16)
    kc_g = nsa_kc_g.reshape(1, d)

    n_pool = cache_nsa_kv.shape[0]
    cache_rows = cache_nsa_kv.transpose(0, 1, 3, 2, 4).reshape(n_pool, PAGE_SIZE * NSA_ROWS, LANE)
    win_rows = state_nsa_win.transpose(0, 1, 3, 2, 4).reshape(Bs, state_nsa_win.shape[1] * WIN_ROWS, LANE)
    dil_states = (state_dil_0, state_dil_1, state_dil_2)
    dil_rows = [s.reshape(Bs, s.shape[1] * KVH_ROWS, LANE) for s in dil_states]
    mem_rows_s = cache_mem_kv.reshape(2, Bs, N_MEM * KVH_ROWS, LANE)

    mem2d = mem_prompt.reshape(B * N_MEM, D)
    mem_rows_p = []
    for i in range(2):
        kv = _norm_matmul(mem2d, mem_norm[i], w_mem_kv[i].astype(BF16), tm=256, tn=1024)
        mem_rows_p.append(_memkv_post(kv, mem_k_g[i]).reshape(B, N_MEM * KVH_ROWS, LANE))

    cos_p, sin_p = _rope_tables(jnp.tile(jnp.arange(T, dtype=jnp.int32), B))
    cos_s, sin_s = _rope_tables(jnp.full((Bs,), past_len, jnp.int32))
    row3 = lambda a: a.astype(F32).reshape(Bs, 1, a.shape[-1])

    xs, *wb = _ffn_cast(x_sample.reshape(Bs, D), ffg, wg, wu, wd, 0, 0)
    xp = _ffn(x_prompt.reshape(B * T, D), ffg, *wb, 0, 0, tm=1024)
    nsa_gains = _pad_gains(nsa_q_g, nsa_ks_g, nsa_kw_g, mem_q_g[0])

    proj_p = _norm_matmul(xp, mix_norm[0], nsa_w, tm=1024, tn=1024)
    qn_p, qr_p, cmp_p, rows_p, win_p, kvb_p, mq_p = _nsa_post(proj_p, cos_p, sin_p, nsa_gains, tm=256)
    cmp_kv = _cmp_prompt(cmp_p.reshape(B, T, 768), w1r, cmp_b1, cmp_w2, kc_g)
    nc_p = T // CMP_STRIDE
    cover_p = _cover(nc_p - 1, T // SEL_BLOCK, nc_p, LANE)
    o_mix_p = _nsa_attn(qn_p.reshape(B, T, 1536), qr_p.reshape(B, T, 1536), proj_p.reshape(B, T, PROJ_N),
                        cmp_kv, kvb_p.reshape(B, T, 1536), cover_p)
    o_mem_p = _mem_attn(mq_p.reshape(B, T, 512), mem_rows_p[0], tq=512)
    xp = _out_proj(xp, o_mix_p.reshape(B * T, 1536), o_mem_p.reshape(B * T, 512), nsa_wo, tm=1024)

    proj_s = _norm_matmul(xs, mix_norm[0], nsa_w, tm=Bs, tn=1024)
    qn_s, qr_s, _, rows_s, win_s, kvb_s, mq_s = _nsa_post(proj_s, cos_s, sin_s, nsa_gains, tm=Bs)
    nc_s = past_len // CMP_STRIDE
    ns_s = -(-(past_len + 1) // SEL_BLOCK)
    cover_s = _cover(nc_s - 1, ns_s, nc_s, -(-ns_s // LANE) * LANE)
    sel_idx, ocmp_s = _dec_select(cache_rows, page_table, row3(qn_s), w1r, cmp_b1, cmp_w2, kc_g, cover_s)
    o_mix_s = _dec_attn(cache_rows, page_table, sel_idx[:, :G, :SEL_TOPK].reshape(-1), row3(qr_s), row3(kvb_s),
                        win_rows, proj_s.reshape(Bs, 1, PROJ_N), ocmp_s)
    o_mem_s = _mem_attn(row3(mq_s), mem_rows_s[0], tq=1)
    xs = _out_proj(xs, o_mix_s.reshape(Bs, 1536).astype(BF16), o_mem_s.reshape(Bs, 512).astype(BF16), nsa_wo,
                   tm=Bs)

    xs, *wb = _ffn_cast(xs, ffg, wg, wu, wd, 0, 1)
    xp = _ffn(xp, ffg, *wb, 0, 1, tm=1024)

    xs, *wb = _ffn_cast(xs, ffg, wg, wu, wd, 1, 0)
    xp = _ffn(xp, ffg, *wb, 1, 0, tm=1024)
    dil_gains = _pad_gains(dil_q_g, dil_k_g, mem_q_g[1])

    dproj_p = _norm_matmul(xp, mix_norm[1], dil_w, tm=1024, tn=1024)
    dmq_p, *rest = _dil_post(dproj_p, cos_p, sin_p, dil_gains, tm=256, seq_len=T)
    dnew_p, dqkv_p = rest[:3], rest[3:]
    band = [_dil_band(*dqkv_p[3 * g:3 * g + 3], g, tq, nr) for g, (tq, nr) in enumerate(((512, 1), (512, 2), (128, 4)))]
    o_dil_p = _dil_mix([o for o, _ in band], [s for _, s in band])
    o_dmem_p = _mem_attn(dmq_p.reshape(B, T, 512), mem_rows_p[1], tq=512)
    xp = _out_proj(xp, o_dil_p, o_dmem_p.reshape(B * T, 512), dil_wo, tm=1024)

    dproj_s = _norm_matmul(xs, mix_norm[1], dil_w, tm=Bs, tn=1024)
    dq_s, dmq_s, *dnew_s = _dil_post(dproj_s, cos_s, sin_s, dil_gains, tm=Bs)
    o_dil_s, *dil_rows_out = _dec_dil(row3(dq_s), [s.reshape(Bs, KVH_ROWS, LANE) for s in dnew_s], dil_rows)
    o_dmem_s = _mem_attn(row3(dmq_s), mem_rows_s[1], tq=1)
    xs = _out_proj(xs, o_dil_s.reshape(Bs, 512).astype(BF16), o_dmem_s.reshape(Bs, 512).astype(BF16), dil_wo,
                   tm=Bs)

    xs, *wb = _ffn_cast(xs, ffg, wg, wu, wd, 1, 1)
    xp = _ffn(xp, ffg, *wb, 1, 1, tm=1024)

    unrow = lambda a, n, outer, inner: a.reshape(n, -1, outer, inner, d).transpose(0, 1, 3, 2, 4)
    nsa_kv_p = unrow(rows_p, B, G, 4)
    nsa_kv_s = unrow(rows_s, Bs, G, 4)
    nsa_win_p = unrow(win_p, B, G, 2)[:, -min(NSA_WINDOW, T):]
    nsa_win_s = jnp.concatenate([state_nsa_win, unrow(win_s, Bs, G, 2)], axis=1)[:, -state_nsa_win.shape[1]:]
    outs_dil = []
    for g, (window, _) in enumerate(DIL_PAIRS):
        st = dil_states[g]
        outs_dil.append(dnew_p[g].reshape(B, T, 2, DIL_HEADS, d)[:, -min(window, T):])
        outs_dil.append(dil_rows_out[g].reshape(st.shape))
    mem_kv_out = jnp.stack([kv.reshape(B, N_MEM, 2, N_MEM_HEADS, d) for kv in mem_rows_p], axis=0)
    return (xp.reshape(B, T, D), xs.reshape(Bs, 1, D), nsa_kv_p, nsa_kv_s, nsa_win_p, nsa_win_s,
            *outs_dil, mem_kv_out)
```

```python
import functools

import jax
import jax.numpy as jnp
from jax import lax
from jax.experimental import pallas as pl
from jax.experimental.pallas import tpu as pltpu

F32 = jnp.float32
BF16 = jnp.bfloat16

D_MODEL = 2048
HEAD_DIM = 128
N_MIX_HEADS = 12
N_MEM_HEADS = 4
N_MEM = 256
NSA_KV_HEADS = 3
NSA_GROUP = 4
CMP_BLOCK = 32
CMP_STRIDE = 16
SEL_BLOCK = 64
SEL_TOPK = 16
NSA_WINDOW = 512
DIL_PAIRS = ((128, 1), (512, 4), (2048, 16))
DIL_HEADS = 4
PAGE_SIZE = 128
ROPE_THETA = 10000.0
EPS = 1e-6
SCALE = HEAD_DIM ** -0.5
NEG = -1e30
FORCE_SCORE = 1e6

PROJ_N = 5120
NSA_GATE_BLK = 34
LANE = 128
VMEM_LIMIT = 56 * 1024 * 1024


def _params(sem):
    return pltpu.CompilerParams(dimension_semantics=sem, vmem_limit_bytes=VMEM_LIMIT)


def _dot(a, b):
    return jnp.dot(a, b, preferred_element_type=F32)


def _dot_t(a, b):
    return lax.dot_general(a, b, (((1,), (1,)), ((), ())), preferred_element_type=F32)


def _dot3(a, b):
    a1 = a.astype(BF16)
    r1 = a - a1.astype(F32)
    a2 = r1.astype(BF16)
    a3 = (r1 - a2.astype(F32)).astype(BF16)
    return _dot(a1, b) + _dot(a2, b) + _dot(a3, b)


def _rms(x, g):
    return x * lax.rsqrt(jnp.mean(x * x, axis=-1, keepdims=True) + EPS) * g


def _rope(x, cos, sin):
    return x * cos + pltpu.roll(x, HEAD_DIM // 2, 1) * sin


def _rows16(row, nrep):
    rid = lax.broadcasted_iota(jnp.int32, (16, LANE), 0) & (nrep - 1)
    out = jnp.zeros((16, LANE), F32)
    for r in range(nrep):
        piece = jnp.broadcast_to(row[:, r * LANE:(r + 1) * LANE], (16, LANE))
        out = jnp.where(rid == r, piece, out)
    return out


def _softmax_masked(s, mask):
    s = jnp.where(mask, s, NEG)
    m = jnp.max(s, axis=-1, keepdims=True)
    e = jnp.where(mask, jnp.exp(s - m), 0.0)
    l = jnp.sum(e, axis=-1, keepdims=True)
    return e / jnp.maximum(l, 1e-30), m, l


def _ffn_step(f, nf, x_ref, g_ref, wg_ref, wu_ref, wd_ref, o_ref, h_ref, acc_ref):
    @pl.when(f == 0)
    def _():
        h_ref[...] = _rms(x_ref[...], g_ref[...]).astype(BF16)
        acc_ref[...] = jnp.zeros_like(acc_ref)

    h = h_ref[...]
    gate = _dot(h, wg_ref[...])
    up = _dot(h, wu_ref[...])
    a = (gate * jax.nn.sigmoid(gate) * up).astype(BF16)
    acc_ref[...] += _dot(a, wd_ref[...])

    @pl.when(f == nf - 1)
    def _():
        o_ref[...] = x_ref[...] + 0.5 * acc_ref[...]


def _ffn_kernel(x_ref, g_ref, wg_ref, wu_ref, wd_ref, o_ref, h_ref, *, nf):
    _ffn_step(pl.program_id(1), nf, x_ref, g_ref, wg_ref, wu_ref, wd_ref, o_ref, h_ref, o_ref)


def _ffn(x, g, wg, wu, wd, li, lj, tm, tf=512):
    M, D = x.shape
    F = wg.shape[-1]
    nf = F // tf
    return pl.pallas_call(
        functools.partial(_ffn_kernel, nf=nf),
        grid=(M // tm, nf),
        in_specs=[
            pl.BlockSpec((tm, D), lambda i, f: (i, 0)),
            pl.BlockSpec((None, None, 1, D), lambda i, f: (li, lj, 0, 0)),
            pl.BlockSpec((D, tf), lambda i, f: (0, f)),
            pl.BlockSpec((D, tf), lambda i, f: (0, f)),
            pl.BlockSpec((tf, D), lambda i, f: (f, 0)),
        ],
        out_specs=pl.BlockSpec((tm, D), lambda i, f: (i, 0)),
        out_shape=jax.ShapeDtypeStruct((M, D), F32),
        scratch_shapes=[pltpu.VMEM((tm, D), BF16)],
        compiler_params=_params(("parallel", "arbitrary")),
        name="ffn",
    )(x, g, wg, wu, wd)


def _ffn_cast_kernel(x_ref, g_ref, wg_ref, wu_ref, wd_ref, o_ref, wgb_ref, wub_ref, wdb_ref, h_ref, *, nf):
    wgb_ref[...] = wg_ref[...].astype(BF16)
    wub_ref[...] = wu_ref[...].astype(BF16)
    wdb_ref[...] = wd_ref[...].astype(BF16)
    _ffn_step(pl.program_id(0), nf, x_ref, g_ref, wgb_ref, wub_ref, wdb_ref, o_ref, h_ref, o_ref)


def _ffn_cast(x, g, wg, wu, wd, li, lj, tf=512):
    M, D = x.shape
    F = wg.shape[-1]
    nf = F // tf
    return pl.pallas_call(
        functools.partial(_ffn_cast_kernel, nf=nf),
        grid=(nf,),
        in_specs=[
            pl.BlockSpec((M, D), lambda f: (0, 0)),
            pl.BlockSpec((None, None, 1, D), lambda f: (li, lj, 0, 0)),
            pl.BlockSpec((None, None, D, tf), lambda f: (li, lj, 0, f)),
            pl.BlockSpec((None, None, D, tf), lambda f: (li, lj, 0, f)),
            pl.BlockSpec((None, None, tf, D), lambda f: (li, lj, f, 0)),
        ],
        out_specs=[pl.BlockSpec((M, D), lambda f: (0, 0)),
                   pl.BlockSpec((D, tf), lambda f: (0, f)),
                   pl.BlockSpec((D, tf), lambda f: (0, f)),
                   pl.BlockSpec((tf, D), lambda f: (f, 0))],
        out_shape=[jax.ShapeDtypeStruct((M, D), F32),
                   jax.ShapeDtypeStruct((D, F), BF16),
                   jax.ShapeDtypeStruct((D, F), BF16),
                   jax.ShapeDtypeStruct((F, D), BF16)],
        scratch_shapes=[pltpu.VMEM((M, D), BF16)],
        compiler_params=_params(("arbitrary",)),
        name="ffn_cast",
    )(x, g, wg, wu, wd)


def _nmm_kernel(x_ref, g_ref, w_ref, o_ref, h_ref):
    @pl.when(pl.program_id(1) == 0)
    def _():
        h_ref[...] = _rms(x_ref[...], g_ref[...]).astype(BF16)

    o_ref[...] = _dot(h_ref[...], w_ref[...])


def _norm_matmul(x, g, w, tm, tn):
    M, D = x.shape
    N = w.shape[1]
    return pl.pallas_call(
        _nmm_kernel,
        grid=(M // tm, N // tn),
        in_specs=[
            pl.BlockSpec((tm, D), lambda i, j: (i, 0)),
            pl.BlockSpec((1, D), lambda i, j: (0, 0)),
            pl.BlockSpec((D, tn), lambda i, j: (0, j)),
        ],
        out_specs=pl.BlockSpec((tm, tn), lambda i, j: (i, j)),
        out_shape=jax.ShapeDtypeStruct((M, N), F32),
        scratch_shapes=[pltpu.VMEM((tm, D), BF16)],
        compiler_params=_params(("parallel", "arbitrary")),
        name="norm_matmul",
    )(x, g.reshape(1, D), w)


def _oproj_kernel(x_ref, a_ref, b_ref, w_ref, o_ref, *, ka):
    o_ref[...] = x_ref[...] + _dot(a_ref[...], w_ref[:ka, :]) + _dot(b_ref[...], w_ref[ka:, :])


def _out_proj(x, a, b, w, tm, tn=1024):
    M, D = x.shape
    ka, kb = a.shape[1], b.shape[1]
    return pl.pallas_call(
        functools.partial(_oproj_kernel, ka=ka),
        grid=(M // tm, D // tn),
        in_specs=[
            pl.BlockSpec((tm, tn), lambda i, j: (i, j)),
            pl.BlockSpec((tm, ka), lambda i, j: (i, 0)),
            pl.BlockSpec((tm, kb), lambda i, j: (i, 0)),
            pl.BlockSpec((ka + kb, tn), lambda i, j: (0, j)),
        ],
        out_specs=pl.BlockSpec((tm, tn), lambda i, j: (i, j)),
        out_shape=jax.ShapeDtypeStruct((M, D), F32),
        compiler_params=_params(("parallel", "parallel")),
        name="out_proj",
    )(x, a, b, w)


def _put_rows(ref, row, rows_per_token, val):
    ref[pl.ds(row, val.shape[0], stride=rows_per_token), :] = val


def _get_rows(ref, row, rows_per_token, n):
    return ref[pl.ds(row, n, stride=rows_per_token), :]


NSA_ROWS = 4 * NSA_KV_HEADS
WIN_ROWS = 2 * NSA_KV_HEADS
KVH_ROWS = 2 * DIL_HEADS


def _nsa_post_kernel(p_ref, cos_ref, sin_ref, g_ref, qn_ref, qr_ref, cmp_ref, rows_ref, win_ref, kvb_ref, mq_ref):
    cos, sin = cos_ref[...], sin_ref[...]
    q_g, ks_g, kw_g, mq_g = g_ref[0:1, :], g_ref[1:2, :], g_ref[2:3, :], g_ref[3:4, :]

    def tile(i):
        return p_ref[:, i * LANE:(i + 1) * LANE]

    for h in range(N_MIX_HEADS):
        qn = _rms(tile(h), q_g)
        qn_ref[:, h * LANE:(h + 1) * LANE] = qn.astype(BF16)
        qr_ref[:, h * LANE:(h + 1) * LANE] = _rope(qn, cos, sin).astype(BF16)
    for g in range(NSA_KV_HEADS):
        kc, vc = tile(12 + g), tile(15 + g)
        ks = _rope(_rms(tile(18 + g), ks_g), cos, sin)
        vs = tile(21 + g)
        kw = _rope(_rms(tile(24 + g), kw_g), cos, sin)
        vw = tile(27 + g)
        for c, val in enumerate((kc, vc)):
            cmp_ref[:, (c * 3 + g) * LANE:(c * 3 + g + 1) * LANE] = val
        for c, val in enumerate((kc, vc, ks, vs)):
            _put_rows(rows_ref, g * 4 + c, NSA_ROWS, val)
        for c, val in enumerate((kw, vw)):
            _put_rows(win_ref, g * 2 + c, WIN_ROWS, val)
        for c, val in enumerate((ks, vs, kw, vw)):
            kvb_ref[:, (c * 3 + g) * LANE:(c * 3 + g + 1) * LANE] = val.astype(BF16)
    for h in range(N_MEM_HEADS):
        mq_ref[:, h * LANE:(h + 1) * LANE] = _rms(tile(30 + h), mq_g).astype(BF16)


def _nsa_post(p, cos, sin, gains, tm):
    M = p.shape[0]
    row = lambda n: pl.BlockSpec((tm, n), lambda i: (i, 0))
    flat = lambda r: pl.BlockSpec((tm * r, LANE), lambda i: (i, 0))
    return pl.pallas_call(
        _nsa_post_kernel,
        grid=(M // tm,),
        in_specs=[row(PROJ_N), row(LANE), row(LANE), pl.BlockSpec((8, LANE), lambda i: (0, 0))],
        out_specs=[row(1536), row(1536), row(768), flat(NSA_ROWS), flat(WIN_ROWS), row(1536), row(512)],
        out_shape=[
            jax.ShapeDtypeStruct((M, 1536), BF16),
            jax.ShapeDtypeStruct((M, 1536), BF16),
            jax.ShapeDtypeStruct((M, 768), F32),
            jax.ShapeDtypeStruct((M * NSA_ROWS, LANE), F32),
            jax.ShapeDtypeStruct((M * WIN_ROWS, LANE), F32),
            jax.ShapeDtypeStruct((M, 1536), BF16),
            jax.ShapeDtypeStruct((M, 512), BF16),
        ],
        compiler_params=_params(("parallel",)),
        name="nsa_post",
    )(p, cos, sin, gains)


def _put_residues(ref, hh, val, dil, tmp_ref):
    sl = slice(hh * LANE, (hh + 1) * LANE)
    if dil == 1:
        ref[0, :, sl] = val.astype(BF16)
        return
    tmp_ref[...] = val
    n = val.shape[0] // dil
    for r in range(dil):
        ref[r, :, sl] = tmp_ref[pl.ds(r, n, stride=dil), :].astype(BF16)


def _dil_post_kernel(p_ref, cos_ref, sin_ref, g_ref, *refs, by_residue):
    if by_residue:
        mq_ref, st0_ref, st1_ref, st2_ref = refs[:4]
        qkv_refs, tmp_ref = refs[4:13], refs[13]
    else:
        qr_ref, mq_ref, st0_ref, st1_ref, st2_ref = refs
    cos, sin = cos_ref[...], sin_ref[...]
    q_g, k_g, mq_g = g_ref[0:1, :], g_ref[1:2, :], g_ref[2:3, :]
    st_refs = (st0_ref, st1_ref, st2_ref)
    for h in range(N_MIX_HEADS):
        g, hh = divmod(h, DIL_HEADS)
        sl = slice(h * LANE, (h + 1) * LANE)
        q = _rope(_rms(p_ref[:, sl], q_g), cos, sin)
        k = _rope(_rms(p_ref[:, (12 + h) * LANE:(13 + h) * LANE], k_g), cos, sin)
        v = p_ref[:, (24 + h) * LANE:(25 + h) * LANE]
        _put_rows(st_refs[g], hh, KVH_ROWS, k)
        _put_rows(st_refs[g], DIL_HEADS + hh, KVH_ROWS, v)
        if by_residue:
            for c, val in enumerate((q, k, v)):
                _put_residues(qkv_refs[3 * g + c], hh, val, DIL_PAIRS[g][1], tmp_ref)
        else:
            qr_ref[:, sl] = q.astype(BF16)
    for h in range(N_MEM_HEADS):
        mq_ref[:, h * LANE:(h + 1) * LANE] = _rms(p_ref[:, (36 + h) * LANE:(37 + h) * LANE], mq_g).astype(BF16)


def _dil_post(p, cos, sin, gains, tm, seq_len=None):
    M = p.shape[0]
    row = lambda n: pl.BlockSpec((tm, n), lambda i: (i, 0))
    flat = pl.BlockSpec((tm * KVH_ROWS, LANE), lambda i: (i, 0))
    st_shape = jax.ShapeDtypeStruct((M * KVH_ROWS, LANE), F32)
    out_specs = [row(512), flat, flat, flat]
    out_shape = [jax.ShapeDtypeStruct((M, 512), BF16), st_shape, st_shape, st_shape]
    scratch = []
    if seq_len is None:
        out_specs = [row(1536)] + out_specs
        out_shape = [jax.ShapeDtypeStruct((M, 1536), BF16)] + out_shape
    else:
        nt = seq_len // tm
        for _, dil in DIL_PAIRS:
            assert tm % (16 * dil) == 0
            spec = pl.BlockSpec((None, dil, tm // dil, 4 * LANE), lambda i: (i // nt, 0, i % nt, 0))
            shape = jax.ShapeDtypeStruct((M // seq_len, dil, seq_len // dil, 4 * LANE), BF16)
            out_specs += [spec] * 3
            out_shape += [shape] * 3
        scratch = [pltpu.VMEM((tm, LANE), F32)]
    return pl.pallas_call(
        functools.partial(_dil_post_kernel, by_residue=seq_len is not None),
        grid=(M // tm,),
        in_specs=[row(PROJ_N), row(LANE), row(LANE), pl.BlockSpec((8, LANE), lambda i: (0, 0))],
        out_specs=out_specs,
        out_shape=out_shape,
        scratch_shapes=scratch,
        compiler_params=_params(("parallel",)),
        name="dil_post",
    )(p, cos, sin, gains)


def _memkv_post_kernel(x_ref, g_ref, o_ref):
    for h in range(N_MEM_HEADS):
        _put_rows(o_ref, h, KVH_ROWS, _rms(x_ref[:, h * LANE:(h + 1) * LANE], g_ref[...]))
        _put_rows(o_ref, N_MEM_HEADS + h, KVH_ROWS, x_ref[:, (N_MEM_HEADS + h) * LANE:(N_MEM_HEADS + h + 1) * LANE])


def _memkv_post(x, g, tm=256):
    M, N = x.shape
    return pl.pallas_call(
        _memkv_post_kernel,
        grid=(M // tm,),
        in_specs=[pl.BlockSpec((tm, N), lambda i: (i, 0)), pl.BlockSpec((1, LANE), lambda i: (0, 0))],
        out_specs=pl.BlockSpec((tm * KVH_ROWS, LANE), lambda i: (i, 0)),
        out_shape=jax.ShapeDtypeStruct((M * KVH_ROWS, LANE), F32),
        compiler_params=_params(("parallel",)),
        name="memkv_post",
    )(x, g.reshape(1, LANE))


def _mem_attn_kernel(q_ref, kv_ref, o_ref, *, tq):
    for h in range(N_MEM_HEADS):
        sl = slice(h * LANE, (h + 1) * LANE)
        if tq == 1:
            q = _rows16(q_ref[:, sl], 1).astype(BF16)
        else:
            q = q_ref[:, sl]
        k = _get_rows(kv_ref, h, KVH_ROWS, N_MEM).astype(BF16)
        v = _get_rows(kv_ref, N_MEM_HEADS + h, KVH_ROWS, N_MEM).astype(BF16)
        s = _dot_t(q, k) * SCALE
        m = jnp.max(s, axis=-1, keepdims=True)
        e = jnp.exp(s - m)
        p = e / jnp.sum(e, axis=-1, keepdims=True)
        o = _dot(p.astype(BF16), v)
        o_ref[:, sl] = o[0:tq, :].astype(o_ref.dtype)


def _mem_attn(q, kv, tq):
    B, T, _ = q.shape
    return pl.pallas_call(
        functools.partial(_mem_attn_kernel, tq=tq),
        grid=(B, T // tq),
        in_specs=[
            pl.BlockSpec((None, tq, 512), lambda b, i: (b, i, 0)),
            pl.BlockSpec((None, N_MEM * KVH_ROWS, LANE), lambda b, i: (b, 0, 0)),
        ],
        out_specs=pl.BlockSpec((None, tq, 512), lambda b, i: (b, i, 0)),
        out_shape=jax.ShapeDtypeStruct((B, T, 512), q.dtype),
        compiler_params=_params(("parallel", "parallel")),
        name="mem_attn",
    )(q, kv)


def _gelu_tanh(x):
    return 0.5 * x * (1.0 + jnp.tanh(0.7978845608028654 * (x + 0.044715 * (x * x * x))))


def _compress_finish(h, b1, w2):
    n = h.shape[0]
    hid = b1 + h[:, :LANE] + pltpu.roll(h[:, LANE:], n - 1, 0)
    return _dot(_gelu_tanh(hid).astype(BF16), w2)


def _compress(x_bf, w1, b1, w2):
    return _compress_finish(_dot(x_bf, w1), b1, w2)


def _cmp_prompt_kernel(x_ref, w1_ref, b1_ref, w2_ref, kcg_ref, o_ref, xs_ref, *, n):
    kv = pl.program_id(1)
    for c in range(CMP_STRIDE):
        xs_ref[:, c * LANE:(c + 1) * LANE] = x_ref[pl.ds(c, n, stride=CMP_STRIDE), :].astype(BF16)
    out = _compress(xs_ref[...], w1_ref[...], b1_ref[...], w2_ref[...])
    out = jnp.where(kv == 0, _rms(out, kcg_ref[...]), out)
    rid = lax.broadcasted_iota(jnp.int32, out.shape, 0)
    o_ref[...] = jnp.where(rid < n - 1, out, 0.0).astype(BF16)


def _cmp_prompt(rows, w1r, b1, w2, kc_g):
    B, T, _ = rows.shape
    n = T // CMP_STRIDE
    return pl.pallas_call(
        functools.partial(_cmp_prompt_kernel, n=n),
        grid=(B, 2, NSA_KV_HEADS),
        in_specs=[
            pl.BlockSpec((None, T, LANE), lambda b, kv, g: (b, 0, kv * 3 + g)),
            pl.BlockSpec((None, CMP_STRIDE * LANE, 2 * LANE), lambda b, kv, g: (kv, 0, 0)),
            pl.BlockSpec((None, 1, LANE), lambda b, kv, g: (kv, 0, 0)),
            pl.BlockSpec((None, LANE, LANE), lambda b, kv, g: (kv, 0, 0)),
            pl.BlockSpec((1, LANE), lambda b, kv, g: (0, 0)),
        ],
        out_specs=pl.BlockSpec((None, None, None, n, LANE), lambda b, kv, g: (b, kv, g, 0, 0)),
        out_shape=jax.ShapeDtypeStruct((B, 2, NSA_KV_HEADS, n, LANE), BF16),
        scratch_shapes=[pltpu.VMEM((n, CMP_STRIDE * LANE), BF16)],
        compiler_params=_params(("parallel", "parallel", "parallel")),
        name="cmp_prompt",
    )(rows, w1r, b1, w2, kc_g)


def _select_blocks(score, cur, n_blocks):
    tq = score.shape[0]
    blk = lax.broadcasted_iota(jnp.int32, score.shape, 1)
    forced = (blk == 0) | (blk == cur) | (blk == cur - 1)
    sc = jnp.where(blk <= cur, jnp.where(forced, FORCE_SCORE, score), NEG)
    sct = sc.T[0:n_blocks, :]
    bi = lax.broadcasted_iota(jnp.int32, sct.shape, 0)
    rank = jnp.zeros(sct.shape, F32)
    for i in range(n_blocks):
        si = sct[i:i + 1, :]
        ahead = (si > sct) | ((si == sct) & (bi > i))
        rank = rank + jnp.where(ahead, 1.0, 0.0)
    chosen = jnp.where((rank < SEL_TOPK) & (sct > 0.5 * NEG), 1.0, 0.0)
    return jnp.concatenate([chosen, jnp.zeros((LANE - n_blocks, tq), F32)], axis=0).T


def _score_tiles(q, k_ref, tiles, s_ref, m_ref, tk, first=False):
    ms = []
    for kt, slot, bias in tiles:
        s = _dot_t(q, k_ref[pl.ds(pl.multiple_of(kt * tk, tk), tk), :]) * SCALE
        if bias is not None:
            s = s + bias
        s_ref[slot] = s
        ms += [s[:, c * LANE:(c + 1) * LANE] for c in range(tk // LANE)]
    m = functools.reduce(jnp.maximum, ms)
    m_ref[...] = m if first else jnp.maximum(m_ref[...], m)


def _value_tiles(v_ref, tiles, s_ref, m_ref, l_ref, acc_ref, tk, first=False):
    m = m_ref[...]
    ls, pvs = [], []
    for kt, slot in tiles:
        es = [jnp.exp(s_ref[slot, :, c * LANE:(c + 1) * LANE] - m) for c in range(tk // LANE)]
        pvs.append(_dot(jnp.concatenate(es, axis=1).astype(BF16),
                        v_ref[pl.ds(pl.multiple_of(kt * tk, tk), tk), :]))
        ls += es
    l = functools.reduce(lambda a, b: a + b, ls)
    pv = functools.reduce(lambda a, b: a + b, pvs)
    if first:
        l_ref[...] = l
        acc_ref[...] = pv
    else:
        l_ref[...] += l
        acc_ref[...] += pv


def _nsa_attn_kernel(qn_ref, qr_ref, gl_ref, kc_ref, vc_ref, ks_ref, vs_ref, kw_ref, vw_ref, cover_ref,
                     o_ref, s_ref, m_ref, l_ref, acc_ref, sel_ref, *, tq, ns):
    qi = pl.program_id(2)
    R = NSA_GROUP
    t0 = qi * tq
    stack = lambda ref: jnp.concatenate([ref[:, r * LANE:(r + 1) * LANE] for r in range(R)], axis=0)
    rows4 = lambda x: jnp.concatenate([x] * R, axis=0)
    tpos_q = t0 + lax.broadcasted_iota(jnp.int32, (tq, 1), 0)
    row_in = lax.broadcasted_iota(jnp.int32, (R * tq, 1), 0) & (tq - 1)
    col = lax.broadcasted_iota(jnp.int32, (R * tq, tq), 1)

    s = _dot_t(stack(qn_ref), kc_ref[...]) * SCALE
    cblk = lax.broadcasted_iota(jnp.int32, (R * tq, LANE), 1)
    cmask = (CMP_STRIDE * cblk + (CMP_BLOCK - 1) <= t0 + row_in) & (cblk < kc_ref.shape[0] - 1)
    p, _, _ = _softmax_masked(s, cmask)
    o_cmp = _dot(p.astype(BF16), vc_ref[...])
    imp = p[0:tq] + p[tq:2 * tq] + p[2 * tq:3 * tq] + p[3 * tq:4 * tq]
    score = _dot3(imp, cover_ref[...])
    cur = lax.shift_right_arithmetic(tpos_q, SEL_BLOCK.bit_length() - 1)

    @pl.when(t0 + tq <= SEL_TOPK * SEL_BLOCK)
    def _():
        sel_ref[...] = jnp.where(lax.broadcasted_iota(jnp.int32, (tq, LANE), 1) <= cur, 1.0, 0.0).astype(BF16)

    @pl.when(t0 + tq > SEL_TOPK * SEL_BLOCK)
    def _():
        sel_ref[...] = _select_blocks(score, cur, ns).astype(BF16)

    sel = sel_ref[...]
    q_rot = stack(qr_ref)
    blocks_per_tile = tq // SEL_BLOCK
    causal = jnp.where(col <= row_in, 0.0, NEG)
    far = jnp.where(col >= row_in, 0.0, NEG)

    def row_max():
        m_ref[...] = jnp.broadcast_to(jnp.max(m_ref[...], axis=-1, keepdims=True), m_ref.shape)

    def result():
        return acc_ref[...] / jnp.maximum(jnp.sum(l_ref[...], axis=-1, keepdims=True), 1e-30)

    def member_bias(kt):
        key_blk = lax.shift_right_arithmetic(lax.broadcasted_iota(jnp.int32, (LANE, tq), 1),
                                             SEL_BLOCK.bit_length() - 1)
        expand = lax.broadcasted_iota(jnp.int32, (LANE, tq), 0) == kt * blocks_per_tile + key_blk
        member = _dot(sel, jnp.where(expand, 1.0, 0.0).astype(BF16))
        return rows4((member - 1.0) * (-NEG))

    sel_tile = lambda kt: (kt, kt, member_bias(kt))
    _score_tiles(q_rot, ks_ref, [(qi, qi, member_bias(qi) + causal)], s_ref, m_ref, tq, first=True)

    def sel_scores(i, carry):
        _score_tiles(q_rot, ks_ref, [sel_tile(2 * i), sel_tile(2 * i + 1)], s_ref, m_ref, tq)
        return carry

    lax.fori_loop(0, qi // 2, sel_scores, 0)

    @pl.when(qi % 2 == 1)
    def _():
        _score_tiles(q_rot, ks_ref, [sel_tile(qi - 1)], s_ref, m_ref, tq)

    row_max()
    _value_tiles(vs_ref, [(qi, qi)], s_ref, m_ref, l_ref, acc_ref, tq, first=True)

    def sel_values(i, carry):
        _value_tiles(vs_ref, [(2 * i, 2 * i), (2 * i + 1, 2 * i + 1)], s_ref, m_ref, l_ref, acc_ref, tq)
        return carry

    lax.fori_loop(0, qi // 2, sel_values, 0)

    @pl.when(qi % 2 == 1)
    def _():
        _value_tiles(vs_ref, [(qi - 1, qi - 1)], s_ref, m_ref, l_ref, acc_ref, tq)

    o_sel = result()

    n_back = NSA_WINDOW // tq
    win_tiles = []
    for back in range(n_back + 1):
        bias = jnp.where(qi >= back, 0.0, NEG)
        if back == 0:
            bias = causal
        elif back == n_back:
            bias = far + bias
        win_tiles.append((jnp.maximum(qi - back, 0), back, bias))
    _score_tiles(q_rot, kw_ref, win_tiles, s_ref, m_ref, tq, first=True)
    row_max()
    _value_tiles(vw_ref, [(kt, slot) for kt, slot, _ in win_tiles], s_ref, m_ref, l_ref, acc_ref, tq, first=True)
    o_win = result()

    gates = jax.nn.sigmoid(gl_ref[...])
    for r in range(R):
        rs = slice(r * tq, (r + 1) * tq)
        o = (gates[:, 3 * r:3 * r + 1] * o_cmp[rs] + gates[:, 3 * r + 1:3 * r + 2] * o_sel[rs]
             + gates[:, 3 * r + 2:3 * r + 3] * o_win[rs])
        o_ref[:, r * LANE:(r + 1) * LANE] = o.astype(BF16)


def _nsa_attn(qn, qr, proj, cmp_kv, kvb, cover, tq=256):
    B, T, _ = qn.shape
    nc = cmp_kv.shape[3]
    G = NSA_KV_HEADS
    assert tq % LANE == 0 and NSA_WINDOW % tq == 0 and nc == LANE and T // SEL_BLOCK <= LANE
    rows = NSA_GROUP * tq
    qspec = pl.BlockSpec((None, tq, 4 * LANE), lambda b, g, i: (b, i, g))
    kvspec = lambda c: pl.BlockSpec((None, T, LANE), lambda b, g, i: (b, 0, c * 3 + g))
    return pl.pallas_call(
        functools.partial(_nsa_attn_kernel, tq=tq, ns=T // SEL_BLOCK),
        grid=(B, G, T // tq),
        in_specs=[
            qspec, qspec,
            pl.BlockSpec((None, tq, LANE), lambda b, g, i: (b, i, NSA_GATE_BLK + g)),
            pl.BlockSpec((None, None, None, nc, LANE), lambda b, g, i: (b, 0, g, 0, 0)),
            pl.BlockSpec((None, None, None, nc, LANE), lambda b, g, i: (b, 1, g, 0, 0)),
            kvspec(0), kvspec(1), kvspec(2), kvspec(3),
            pl.BlockSpec((nc, LANE), lambda b, g, i: (0, 0)),
        ],
        out_specs=qspec,
        out_shape=jax.ShapeDtypeStruct((B, T, 1536), BF16),
        scratch_shapes=[pltpu.VMEM((T // tq, rows, tq), F32),
                        pltpu.VMEM((rows, LANE), F32),
                        pltpu.VMEM((rows, LANE), F32),
                        pltpu.VMEM((rows, LANE), F32),
                        pltpu.VMEM((tq, LANE), BF16)],
        compiler_params=_params(("parallel", "parallel", "arbitrary")),
        name="nsa_attn",
    )(qn, qr, proj, cmp_kv, cmp_kv, kvb, kvb, kvb, kvb, cover)


def _dil_band_kernel(q_ref, kp_ref, kc_ref, vp_ref, vc_ref, o_ref, st_ref, *, tq, window):
    i = pl.program_id(2)
    w = window
    row = lax.broadcasted_iota(jnp.int32, (w, 2 * w), 0)
    colk = lax.broadcasted_iota(jnp.int32, (w, 2 * w), 1)
    diff = w + row - colk
    band = (diff >= 0) & (diff <= window)
    lane = lax.broadcasted_iota(jnp.int32, (w, LANE), 1)
    for r in range(q_ref.shape[0]):
        for j in range(tq // w):
            mask = band & (i * tq + (j - 1) * w + colk >= 0)
            rows = slice(j * w, (j + 1) * w)
            stats = jnp.zeros((w, LANE), F32)
            for h in range(DIL_HEADS):
                sl = slice(h * LANE, (h + 1) * LANE)
                if j == 0:
                    k = jnp.concatenate([kp_ref[r, :, sl], kc_ref[r, 0:w, sl]], axis=0)
                    v = jnp.concatenate([vp_ref[r, :, sl], vc_ref[r, 0:w, sl]], axis=0)
                else:
                    k = kc_ref[r, (j - 1) * w:(j + 1) * w, sl]
                    v = vc_ref[r, (j - 1) * w:(j + 1) * w, sl]
                p, m, l = _softmax_masked(_dot_t(q_ref[r, rows, sl], k) * SCALE, mask)
                o_ref[r, rows, sl] = _dot(p.astype(BF16), v)
                stats = jnp.where(lane == h, m, stats)
                stats = jnp.where(lane == DIL_HEADS + h, l, stats)
            st_ref[r, rows, :] = stats


def _dil_band(q, k, v, g, tq, nr):
    B, dil, S, _ = q.shape
    window = DIL_PAIRS[g][0] // dil
    assert tq % window == 0 and S % tq == 0 and dil % nr == 0
    cur = lambda w: pl.BlockSpec((None, nr, tq, w), lambda b, r, i: (b, r, i, 0))
    prev = pl.BlockSpec((None, nr, window, 4 * LANE),
                        lambda b, r, i: (b, r, jnp.maximum(i * (tq // window) - 1, 0), 0))
    return pl.pallas_call(
        functools.partial(_dil_band_kernel, tq=tq, window=window),
        grid=(B, dil // nr, S // tq),
        in_specs=[cur(4 * LANE), prev, cur(4 * LANE), prev, cur(4 * LANE)],
        out_specs=[cur(4 * LANE), cur(LANE)],
        out_shape=[jax.ShapeDtypeStruct((B, dil, S, 4 * LANE), F32),
                   jax.ShapeDtypeStruct((B, dil, S, LANE), F32)],
        compiler_params=_params(("parallel", "parallel", "parallel")),
        name=f"dil_band{g}",
    )(q, k, k, v, v)


def _mix_groups(os_, ms, ls):
    m_all = jnp.maximum(jnp.maximum(ms[0], ms[1]), ms[2])
    ws = [jnp.exp(m - m_all) * l for m, l in zip(ms, ls)]
    tot = ws[0] + ws[1] + ws[2]
    return (ws[0] / tot) * os_[0] + (ws[1] / tot) * os_[1] + (ws[2] / tot) * os_[2]


def _dil_mix_kernel(o0_ref, o1_ref, o2_ref, s0_ref, s1_ref, s2_ref, o_ref, nat_ref):
    o_refs, s_refs = (o0_ref, o1_ref, o2_ref), (s0_ref, s1_ref, s2_ref)
    tm = o_ref.shape[0]

    def token_order(ref, sl):
        dil = ref.shape[0]
        if dil == 1:
            return ref[0, :, sl]
        for r in range(dil):
            nat_ref[pl.ds(r, tm // dil, stride=dil), :] = ref[r, :, sl]
        return nat_ref[...]

    stats = [token_order(s, slice(0, LANE)) for s in s_refs]
    for h in range(DIL_HEADS):
        sl = slice(h * LANE, (h + 1) * LANE)
        ms = [s[:, h:h + 1] for s in stats]
        ls = [s[:, DIL_HEADS + h:DIL_HEADS + h + 1] for s in stats]
        o_ref[:, sl] = _mix_groups([token_order(o, sl) for o in o_refs], ms, ls).astype(BF16)


def _dil_mix(os_, sts, tm=512):
    B, _, T, _ = os_[0].shape
    nt = T // tm
    in_specs = []
    for lanes, arrs in ((4 * LANE, os_), (LANE, sts)):
        for a in arrs:
            dil = a.shape[1]
            in_specs.append(pl.BlockSpec((None, dil, tm // dil, lanes), lambda b, i: (b, 0, i, 0)))
    return pl.pallas_call(
        _dil_mix_kernel,
        grid=(B, nt),
        in_specs=in_specs,
        out_specs=pl.BlockSpec((tm, 4 * LANE), lambda b, i: (b * nt + i, 0)),
        out_shape=jax.ShapeDtypeStruct((B * T, 4 * LANE), BF16),
        scratch_shapes=[pltpu.VMEM((tm, LANE), F32)],
        compiler_params=_params(("parallel", "parallel")),
        name="dil_mix",
    )(*os_, *sts)


DEC_PAGES_PER_STEP = 16


def _dec_select_kernel(tbl_ref, *refs, n_pp, n_steps):
    pages = refs[:n_pp]
    (qn_ref, w1_ref, b1_ref, w2_ref, kcg_ref, cover_ref, idx_ref, ocmp_ref, h_ref) = refs[n_pp:]
    j = pl.program_id(1)
    cpp = PAGE_SIZE // CMP_STRIDE
    rows = n_pp * cpp
    n = n_steps * rows
    ns = cover_ref.shape[1]

    for g in range(NSA_KV_HEADS):
        for kv in range(2):
            x = jnp.concatenate(
                [_get_rows(pg, g * 4 + kv, NSA_ROWS, PAGE_SIZE).reshape(cpp, CMP_STRIDE * LANE) for pg in pages],
                axis=0)
            h_ref[g * 2 + kv, pl.ds(pl.multiple_of(j * rows, rows), rows), :] = _dot(x.astype(BF16), w1_ref[kv])

    @pl.when(j == n_steps - 1)
    def _():
        idx_ref[...] = jnp.zeros_like(idx_ref)
        for g in range(NSA_KV_HEADS):
            kc = _rms(_compress_finish(h_ref[g * 2], b1_ref[0], w2_ref[0]), kcg_ref[...]).astype(BF16)
            vc = _compress_finish(h_ref[g * 2 + 1], b1_ref[1], w2_ref[1]).astype(BF16)
            q = _rows16(qn_ref[:, g * NSA_GROUP * LANE:(g + 1) * NSA_GROUP * LANE], NSA_GROUP).astype(BF16)
            s = _dot_t(q, kc) * SCALE
            valid = lax.broadcasted_iota(jnp.int32, s.shape, 1) < n - 1
            p, _, _ = _softmax_masked(s, valid)
            ocmp_ref[g] = _dot(p.astype(BF16), vc)[0:NSA_GROUP, :]
            rid = lax.broadcasted_iota(jnp.int32, p.shape, 0)
            imp = jnp.sum(jnp.where(rid < NSA_GROUP, p, 0.0), axis=0, keepdims=True)
            score = _dot3(jnp.broadcast_to(imp, (8, n)), cover_ref[...])

            a = jnp.broadcast_to(score[0:1, :], (ns, ns))
            lane = lax.broadcasted_iota(jnp.int32, (ns, ns), 1)
            sub = lax.broadcasted_iota(jnp.int32, (ns, ns), 0)
            cur = n * CMP_STRIDE // SEL_BLOCK
            forced = (lane == 0) | (lane == cur) | (lane == cur - 1)
            a = jnp.where(lane <= cur, jnp.where(forced, FORCE_SCORE, a), NEG)
            at = a.T
            ahead_r = (at > a) | ((at == a) & (sub < lane))
            chosen_r = ((jnp.sum(jnp.where(ahead_r, 1.0, 0.0), axis=0, keepdims=True) < SEL_TOPK)
                        & (a[0:1, :] > 0.5 * NEG))
            ahead_c = (a > at) | ((a == at) & (lane < sub))
            chosen_c = ((jnp.sum(jnp.where(ahead_c, 1.0, 0.0), axis=1, keepdims=True) < SEL_TOPK)
                        & (at[:, 0:1] > 0.5 * NEG))
            before = jnp.sum(jnp.where(chosen_r & (lane < sub), 1.0, 0.0), axis=1, keepdims=True)
            slot = lax.broadcasted_iota(jnp.int32, (ns, LANE), 1)
            onehot = chosen_c & (before == slot.astype(F32))
            blk = lax.broadcasted_iota(jnp.int32, (ns, LANE), 0)
            picked = jnp.sum(jnp.where(onehot, blk.astype(F32), 0.0), axis=0, keepdims=True)
            filled = jnp.sum(jnp.where(onehot, 1.0, 0.0), axis=0, keepdims=True)
            idx_ref[g:g + 1, :] = jnp.where(filled > 0.5, picked, float(cur)).astype(jnp.int32)


def _dec_select(cache, table, qn, w1r, b1, w2, kc_g, cover):
    B, n_pages = table.shape
    n_pp = DEC_PAGES_PER_STEP
    n_steps = n_pages // n_pp
    n = n_pages * (PAGE_SIZE // CMP_STRIDE)
    ns = cover.shape[1]
    G = NSA_KV_HEADS
    const = lambda *shape: pl.BlockSpec(shape, lambda b, j, tbl: (0,) * len(shape))
    page_spec = lambda p: pl.BlockSpec((None, PAGE_SIZE * NSA_ROWS, LANE), lambda b, j, tbl: (tbl[b, j * n_pp + p], 0, 0))
    grid_spec = pltpu.PrefetchScalarGridSpec(
        num_scalar_prefetch=1,
        grid=(B, n_steps),
        in_specs=[page_spec(p) for p in range(n_pp)] + [
            pl.BlockSpec((None, 1, 1536), lambda b, j, tbl: (b, 0, 0)),
            const(2, CMP_STRIDE * LANE, 2 * LANE), const(2, 1, LANE), const(2, LANE, LANE), const(1, LANE),
            const(n, ns),
        ],
        out_specs=[pl.BlockSpec((None, 8, LANE), lambda b, j, tbl: (b, 0, 0)),
                   pl.BlockSpec((None, G, NSA_GROUP, LANE), lambda b, j, tbl: (b, 0, 0, 0))],
        scratch_shapes=[pltpu.VMEM((2 * G, n, 2 * LANE), F32)],
    )
    return pl.pallas_call(
        functools.partial(_dec_select_kernel, n_pp=n_pp, n_steps=n_steps),
        grid_spec=grid_spec,
        out_shape=[jax.ShapeDtypeStruct((B, 8, LANE), jnp.int32),
                   jax.ShapeDtypeStruct((B, G, NSA_GROUP, LANE), F32)],
        compiler_params=_params(("arbitrary", "arbitrary")),
        name="dec_select",
    )(table, *([cache] * n_pp), qn, w1r, b1, w2, kc_g, cover)


def _dec_attn_kernel(tbl_ref, idx_ref, *refs, n_sel, cur, wb):
    blocks = refs[:n_sel]
    (qr_ref, kvn_ref, win_ref, gl_ref, ocmp_ref, o_ref) = refs[n_sel:]
    b, g = pl.program_id(0), pl.program_id(1)
    R = NSA_GROUP
    q = _rows16(qr_ref[...], R).astype(BF16)
    qf = q.astype(F32)
    new = kvn_ref[...]
    ks_n, vs_n, kw_n, vw_n = (new[:, c * LANE:(c + 1) * LANE] for c in range(4))

    rows = [r.reshape(SEL_BLOCK * 4, LANE) for r in blocks]
    k = jnp.concatenate([_get_rows(r, 2, 4, SEL_BLOCK) for r in rows], axis=0).astype(BF16)
    v = jnp.concatenate([_get_rows(r, 3, 4, SEL_BLOCK) for r in rows], axis=0).astype(BF16)
    s = _dot_t(q, k) * SCALE
    blk_of = lax.shift_right_arithmetic(lax.broadcasted_iota(jnp.int32, s.shape, 1), SEL_BLOCK.bit_length() - 1)
    valid = jnp.zeros(s.shape, jnp.int32)
    for n in range(n_sel):
        is_past = jnp.where(idx_ref[(b * NSA_KV_HEADS + g) * n_sel + n] != cur, 1, 0)
        valid = jnp.where(blk_of == n, is_past, valid)
    valid = valid > 0
    s_new = jnp.sum(qf * ks_n, axis=-1, keepdims=True) * SCALE
    s = jnp.where(valid, s, NEG)
    m = jnp.maximum(jnp.max(s, axis=-1, keepdims=True), s_new)
    e = jnp.where(valid, jnp.exp(s - m), 0.0)
    e_new = jnp.exp(s_new - m)
    l = jnp.sum(e, axis=-1, keepdims=True) + e_new
    o_sel = (_dot(e.astype(BF16), v) + e_new.astype(BF16).astype(F32) * vs_n) / l

    s = _dot_t(q, _get_rows(win_ref, g * 2, WIN_ROWS, wb).astype(BF16)) * SCALE
    s_new = jnp.sum(qf * kw_n, axis=-1, keepdims=True) * SCALE
    m = jnp.maximum(jnp.max(s, axis=-1, keepdims=True), s_new)
    e = jnp.exp(s - m)
    e_new = jnp.exp(s_new - m)
    l = jnp.sum(e, axis=-1, keepdims=True) + e_new
    o_win = (_dot(e.astype(BF16), _get_rows(win_ref, g * 2 + 1, WIN_ROWS, wb).astype(BF16))
             + e_new.astype(BF16).astype(F32) * vw_n) / l

    gates = jax.nn.sigmoid(gl_ref[...])
    o_cmp = ocmp_ref[...]
    for r in range(R):
        o = (gates[:, 3 * r:3 * r + 1] * o_cmp[r:r + 1] + gates[:, 3 * r + 1:3 * r + 2] * o_sel[r:r + 1]
             + gates[:, 3 * r + 2:3 * r + 3] * o_win[r:r + 1])
        o_ref[:, r * LANE:(r + 1) * LANE] = o


def _dec_attn(cache, table, idx, qr, kvb, win_state, proj, ocmp):
    B, n_pages = table.shape
    n_sel = SEL_TOPK
    cur = n_pages * PAGE_SIZE // SEL_BLOCK
    wb = win_state.shape[1] // WIN_ROWS
    assert wb <= NSA_WINDOW
    halves = PAGE_SIZE // SEL_BLOCK

    def blk_spec(n):
        def imap(b, g, tbl, idx):
            i = jnp.minimum(idx[(b * NSA_KV_HEADS + g) * n_sel + n], cur - 1)
            return (tbl[b, i // halves], i % halves, g, 0, 0)
        return pl.BlockSpec((None, SEL_BLOCK, None, 4, LANE), imap)

    grid_spec = pltpu.PrefetchScalarGridSpec(
        num_scalar_prefetch=2,
        grid=(B, NSA_KV_HEADS),
        in_specs=[blk_spec(n) for n in range(n_sel)] + [
            pl.BlockSpec((None, 1, 4 * LANE), lambda b, g, tbl, idx: (b, 0, g)),
            pl.BlockSpec((None, 1, 4 * LANE), lambda b, g, tbl, idx: (b, 0, g)),
            pl.BlockSpec((None, wb * WIN_ROWS, LANE), lambda b, g, tbl, idx: (b, 0, 0)),
            pl.BlockSpec((None, 1, LANE), lambda b, g, tbl, idx: (b, 0, NSA_GATE_BLK + g)),
            pl.BlockSpec((None, None, NSA_GROUP, LANE), lambda b, g, tbl, idx: (b, g, 0, 0)),
        ],
        out_specs=pl.BlockSpec((None, 1, 4 * LANE), lambda b, g, tbl, idx: (b, 0, g)),
    )
    kvn = kvb.reshape(B, 1, 4, NSA_KV_HEADS, LANE).transpose(0, 1, 3, 2, 4).reshape(B, 1, 1536)
    return pl.pallas_call(
        functools.partial(_dec_attn_kernel, n_sel=n_sel, cur=cur, wb=wb),
        grid_spec=grid_spec,
        out_shape=jax.ShapeDtypeStruct((B, 1, 1536), F32),
        compiler_params=_params(("arbitrary", "arbitrary")),
        name="dec_attn",
    )(table, idx, *([cache] * n_sel), qr, kvn, win_state, proj, ocmp)


def _dec_dil_kernel(q_ref, n0_ref, n1_ref, n2_ref, s0_ref, s1_ref, s2_ref, o_ref, t0_ref, t1_ref, t2_ref):
    qall = q_ref[...]
    states, news, outs = (s0_ref, s1_ref, s2_ref), (n0_ref, n1_ref, n2_ref), (t0_ref, t1_ref, t2_ref)

    for st, new, out in zip(states, news, outs):
        keep = st.shape[0] - KVH_ROWS
        out[0:keep, :] = st[KVH_ROWS:, :]
        out[keep:, :] = new[...]

    for h in range(DIL_HEADS):
        os_, ms, ls = [], [], []
        for g, st in enumerate(states):
            hs = slice((g * DIL_HEADS + h) * LANE, (g * DIL_HEADS + h + 1) * LANE)
            window, dil = DIL_PAIRS[g]
            kn = news[g][h:h + 1, :].astype(BF16).astype(F32)
            vn = news[g][DIL_HEADS + h:DIL_HEADS + h + 1, :].astype(BF16).astype(F32)
            q = _rows16(qall[:, hs], 1).astype(BF16)
            k = _get_rows(st, h, KVH_ROWS * dil, window // dil).astype(BF16)
            v = _get_rows(st, DIL_HEADS + h, KVH_ROWS * dil, window // dil).astype(BF16)
            s = _dot_t(q, k) * SCALE
            s_new = jnp.sum(q.astype(F32) * kn, axis=-1, keepdims=True) * SCALE
            m = jnp.maximum(jnp.max(s, axis=-1, keepdims=True), s_new)
            e = jnp.exp(s - m)
            e_new = jnp.exp(s_new - m)
            l = jnp.sum(e, axis=-1, keepdims=True) + e_new
            ln = jnp.maximum(l, 1e-30)
            o = _dot((e / ln).astype(BF16), v) + (e_new / ln).astype(BF16).astype(F32) * vn
            os_.append(o)
            ms.append(m)
            ls.append(l)
        o_ref[:, h * LANE:(h + 1) * LANE] = _mix_groups(os_, ms, ls)[0:1, :]


def _dec_dil(qr, news, states):
    B = qr.shape[0]
    in_specs = [pl.BlockSpec((None, 1, 1536), lambda b: (b, 0, 0))]
    in_specs += [pl.BlockSpec((None, KVH_ROWS, LANE), lambda b: (b, 0, 0))] * len(DIL_PAIRS)
    st_specs = []
    for g, (window, dil) in enumerate(DIL_PAIRS):
        assert states[g].shape[1] == window * KVH_ROWS, "rolling buffer shorter than the window is not supported"
        st_specs.append(pl.BlockSpec((None, window * KVH_ROWS, LANE), lambda b: (b, 0, 0)))
    return pl.pallas_call(
        _dec_dil_kernel,
        grid=(B,),
        in_specs=in_specs + st_specs,
        out_specs=[pl.BlockSpec((None, 1, 4 * LANE), lambda b: (b, 0, 0))] + st_specs,
        out_shape=[jax.ShapeDtypeStruct((B, 1, 4 * LANE), F32)]
        + [jax.ShapeDtypeStruct(s.shape, F32) for s in states],
        compiler_params=_params(("parallel",)),
        name="dec_dil",
    )(qr, *news, *states)


def _rope_tables(pos):
    half = HEAD_DIM // 2
    inv = ROPE_THETA ** (-jnp.arange(half, dtype=F32) / half)
    ang = pos.astype(F32)[:, None] * inv
    cos, sin = jnp.cos(ang), jnp.sin(ang)
    return jnp.concatenate([cos, cos], axis=-1), jnp.concatenate([-sin, sin], axis=-1)


def _cover(nc, ns, rows, cols):
    c0 = jnp.arange(nc)[:, None] * CMP_STRIDE
    s0 = jnp.arange(ns)[None, :] * SEL_BLOCK
    cover = jnp.clip(jnp.minimum(c0 + CMP_BLOCK, s0 + SEL_BLOCK) - jnp.maximum(c0, s0), 0, CMP_BLOCK)
    cover = cover.astype(F32) / CMP_BLOCK
    return jnp.pad(cover, ((0, rows - nc), (0, cols - ns))).astype(BF16)


def _pad_gains(*gs):
    return jnp.pad(jnp.stack(gs, axis=0), ((0, 8 - len(gs)), (0, 0)))


def kernel(x_prompt, x_sample, mem_prompt, cache_nsa_kv, page_table, state_nsa_win, state_dil_0, state_dil_1,
           state_dil_2, cache_mem_kv, ff_norm, ff_w_gate, ff_w_up, ff_w_down, mix_norm, mem_norm, w_mem_kv,
           mem_q_g, mem_k_g, nsa_w_in, nsa_q_g, nsa_kc_g, nsa_ks_g, nsa_kw_g, nsa_cmp_w1, nsa_cmp_b1, nsa_cmp_w2,
           nsa_w_out, dil_w_in, dil_q_g, dil_k_g, dil_w_out):
    B, T, D = x_prompt.shape
    Bs = x_sample.shape[0]
    assert x_sample.shape[1] == 1, "the sample group is a single-token decode step"
    n_pages = page_table.shape[1]
    past_len = n_pages * PAGE_SIZE
    H, G, d = N_MIX_HEADS, NSA_KV_HEADS, HEAD_DIM

    wg, wu, wd = ff_w_gate, ff_w_up, ff_w_down
    ffg = ff_norm.reshape(ff_norm.shape[0], 2, 1, D)
    gate_w = jnp.pad(nsa_w_in[:, H * d:H * d + 3 * H].reshape(D, G, 3 * NSA_GROUP), ((0, 0), (0, 0), (0, LANE - 12)))
    nsa_w = jnp.concatenate([nsa_w_in[:, :H * d], nsa_w_in[:, H * d + 3 * H:], gate_w.reshape(D, G * LANE)], axis=1)
    nsa_w = jnp.pad(nsa_w, ((0, 0), (0, PROJ_N - nsa_w.shape[1]))).astype(BF16)
    dil_w = dil_w_in.astype(BF16)
    nsa_wo, dil_wo = nsa_w_out.astype(BF16), dil_w_out.astype(BF16)
    w1r = nsa_cmp_w1.reshape(2, 2, CMP_STRIDE, d, d).transpose(0, 2, 3, 1, 4).reshape(2, CMP_STRIDE * d, 2 * d)
    w1r = w1r.astype(BF16)
    cmp_b1 = nsa_cmp_b1.reshape(2, 1, d)
    cmp_w2 = nsa_cmp_w2.astype(BF16)
    kc_g = nsa_kc_g.reshape(1, d)

    n_pool = cache_nsa_kv.shape[0]
    cache_rows = cache_nsa_kv.transpose(0, 1, 3, 2, 4).reshape(n_pool, PAGE_SIZE * NSA_ROWS, LANE)
    win_rows = state_nsa_win.transpose(0, 1, 3, 2, 4).reshape(Bs, state_nsa_win.shape[1] * WIN_ROWS, LANE)
    dil_states = (state_dil_0, state_dil_1, state_dil_2)
    dil_rows = [s.reshape(Bs, s.shape[1] * KVH_ROWS, LANE) for s in dil_states]
    mem_rows_s = cache_mem_kv.reshape(2, Bs, N_MEM * KVH_ROWS, LANE)

    mem2d = mem_prompt.reshape(B * N_MEM, D)
    mem_rows_p = []
    for i in range(2):
        kv = _norm_matmul(mem2d, mem_norm[i], w_mem_kv[i].astype(BF16), tm=256, tn=1024)
        mem_rows_p.append(_memkv_post(kv, mem_k_g[i]).reshape(B, N_MEM * KVH_ROWS, LANE))

    cos_p, sin_p = _rope_tables(jnp.tile(jnp.arange(T, dtype=jnp.int32), B))
    cos_s, sin_s = _rope_tables(jnp.full((Bs,), past_len, jnp.int32))
    row3 = lambda a: a.astype(F32).reshape(Bs, 1, a.shape[-1])

    xs, *wb = _ffn_cast(x_sample.reshape(Bs, D), ffg, wg, wu, wd, 0, 0)
    xp = _ffn(x_prompt.reshape(B * T, D), ffg, *wb, 0, 0, tm=1024)
    nsa_gains = _pad_gains(nsa_q_g, nsa_ks_g, nsa_kw_g, mem_q_g[0])

    proj_p = _norm_matmul(xp, mix_norm[0], nsa_w, tm=1024, tn=1280)
    qn_p, qr_p, cmp_p, rows_p, win_p, kvb_p, mq_p = _nsa_post(proj_p, cos_p, sin_p, nsa_gains, tm=256)
    cmp_kv = _cmp_prompt(cmp_p.reshape(B, T, 768), w1r, cmp_b1, cmp_w2, kc_g)
    nc_p = T // CMP_STRIDE
    cover_p = _cover(nc_p - 1, T // SEL_BLOCK, nc_p, LANE)
    o_mix_p = _nsa_attn(qn_p.reshape(B, T, 1536), qr_p.reshape(B, T, 1536), proj_p.reshape(B, T, PROJ_N),
                        cmp_kv, kvb_p.reshape(B, T, 1536), cover_p)
    o_mem_p = _mem_attn(mq_p.reshape(B, T, 512), mem_rows_p[0], tq=512)
    xp = _out_proj(xp, o_mix_p.reshape(B * T, 1536), o_mem_p.reshape(B * T, 512), nsa_wo, tm=1024)

    proj_s = _norm_matmul(xs, mix_norm[0], nsa_w, tm=Bs, tn=1024)
    qn_s, qr_s, _, rows_s, win_s, kvb_s, mq_s = _nsa_post(proj_s, cos_s, sin_s, nsa_gains, tm=Bs)
    nc_s = past_len // CMP_STRIDE
    ns_s = -(-(past_len + 1) // SEL_BLOCK)
    cover_s = _cover(nc_s - 1, ns_s, nc_s, -(-ns_s // LANE) * LANE)
    sel_idx, ocmp_s = _dec_select(cache_rows, page_table, row3(qn_s), w1r, cmp_b1, cmp_w2, kc_g, cover_s)
    o_mix_s = _dec_attn(cache_nsa_kv.transpose(0, 1, 3, 2, 4), page_table,
                        sel_idx[:, :G, :SEL_TOPK].reshape(-1), row3(qr_s), row3(kvb_s),
                        win_rows, proj_s.reshape(Bs, 1, PROJ_N), ocmp_s)
    o_mem_s = _mem_attn(row3(mq_s), mem_rows_s[0], tq=1)
    xs = _out_proj(xs, o_mix_s.reshape(Bs, 1536).astype(BF16), o_mem_s.reshape(Bs, 512).astype(BF16), nsa_wo,
                   tm=Bs)

    xs, *wb = _ffn_cast(xs, ffg, wg, wu, wd, 0, 1)
    xp = _ffn(xp, ffg, *wb, 0, 1, tm=1024)

    xs, *wb = _ffn_cast(xs, ffg, wg, wu, wd, 1, 0)
    xp = _ffn(xp, ffg, *wb, 1, 0, tm=1024)
    dil_gains = _pad_gains(dil_q_g, dil_k_g, mem_q_g[1])

    dproj_p = _norm_matmul(xp, mix_norm[1], dil_w, tm=1024, tn=1280)
    dmq_p, *rest = _dil_post(dproj_p, cos_p, sin_p, dil_gains, tm=256, seq_len=T)
    dnew_p, dqkv_p = rest[:3], rest[3:]
    band = [_dil_band(*dqkv_p[3 * g:3 * g + 3], g, tq, nr) for g, (tq, nr) in enumerate(((512, 1), (512, 2), (128, 4)))]
    o_dil_p = _dil_mix([o for o, _ in band], [s for _, s in band])
    o_dmem_p = _mem_attn(dmq_p.reshape(B, T, 512), mem_rows_p[1], tq=512)
    xp = _out_proj(xp, o_dil_p, o_dmem_p.reshape(B * T, 512), dil_wo, tm=1024)

    dproj_s = _norm_matmul(xs, mix_norm[1], dil_w, tm=Bs, tn=1024)
    dq_s, dmq_s, *dnew_s = _dil_post(dproj_s, cos_s, sin_s, dil_gains, tm=Bs)
    o_dil_s, *dil_rows_out = _dec_dil(row3(dq_s), [s.reshape(Bs, KVH_ROWS, LANE) for s in dnew_s], dil_rows)
    o_dmem_s = _mem_attn(row3(dmq_s), mem_rows_s[1], tq=1)
    xs = _out_proj(xs, o_dil_s.reshape(Bs, 512).astype(BF16), o_dmem_s.reshape(Bs, 512).astype(BF16), dil_wo,
                   tm=Bs)

    xs, *wb = _ffn_cast(xs, ffg, wg, wu, wd, 1, 1)
    xp = _ffn(xp, ffg, *wb, 1, 1, tm=1024)

    unrow = lambda a, n, outer, inner: a.reshape(n, -1, outer, inner, d).transpose(0, 1, 3, 2, 4)
    nsa_kv_p = unrow(rows_p, B, G, 4)
    nsa_kv_s = unrow(rows_s, Bs, G, 4)
    nsa_win_p = unrow(win_p, B, G, 2)[:, -min(NSA_WINDOW, T):]
    nsa_win_s = jnp.concatenate([state_nsa_win, unrow(win_s, Bs, G, 2)], axis=1)[:, -state_nsa_win.shape[1]:]
    outs_dil = []
    for g, (window, _) in enumerate(DIL_PAIRS):
        st = dil_states[g]
        outs_dil.append(dnew_p[g].reshape(B, T, 2, DIL_HEADS, d)[:, -min(window, T):])
        outs_dil.append(dil_rows_out[g].reshape(st.shape))
    mem_kv_out = jnp.stack([kv.reshape(B, N_MEM, 2, N_MEM_HEADS, d) for kv in mem_rows_p], axis=0)
    return (xp.reshape(B, T, D), xs.reshape(Bs, 1, D), nsa_kv_p, nsa_kv_s, nsa_win_p, nsa_win_s,
            *outs_dil, mem_kv_out)
```

```python
import functools

import jax
import jax.numpy as jnp
from jax import lax
from jax.experimental import pallas as pl
from jax.experimental.pallas import tpu as pltpu

F32 = jnp.float32
BF16 = jnp.bfloat16

D_MODEL = 2048
HEAD_DIM = 128
N_MIX_HEADS = 12
N_MEM_HEADS = 4
N_MEM = 256
NSA_KV_HEADS = 3
NSA_GROUP = 4
CMP_BLOCK = 32
CMP_STRIDE = 16
SEL_BLOCK = 64
SEL_TOPK = 16
NSA_WINDOW = 512
DIL_PAIRS = ((128, 1), (512, 4), (2048, 16))
DIL_HEADS = 4
PAGE_SIZE = 128
ROPE_THETA = 10000.0
EPS = 1e-6
SCALE = HEAD_DIM ** -0.5
NEG = -1e30
FORCE_SCORE = 1e6

PROJ_N = 5120
NSA_GATE_BLK = 34
LANE = 128
VMEM_LIMIT = 56 * 1024 * 1024


def _params(sem):
    return pltpu.CompilerParams(dimension_semantics=sem, vmem_limit_bytes=VMEM_LIMIT)


def _dot(a, b):
    return jnp.dot(a, b, preferred_element_type=F32)


def _dot_t(a, b):
    return lax.dot_general(a, b, (((1,), (1,)), ((), ())), preferred_element_type=F32)


def _dot3(a, b):
    a1 = a.astype(BF16)
    r1 = a - a1.astype(F32)
    a2 = r1.astype(BF16)
    a3 = (r1 - a2.astype(F32)).astype(BF16)
    return _dot(a1, b) + _dot(a2, b) + _dot(a3, b)


def _rms(x, g):
    return x * lax.rsqrt(jnp.mean(x * x, axis=-1, keepdims=True) + EPS) * g


def _rope(x, cos, sin):
    return x * cos + pltpu.roll(x, HEAD_DIM // 2, 1) * sin


def _rows16(row, nrep):
    rid = lax.broadcasted_iota(jnp.int32, (16, LANE), 0) & (nrep - 1)
    out = jnp.zeros((16, LANE), F32)
    for r in range(nrep):
        piece = jnp.broadcast_to(row[:, r * LANE:(r + 1) * LANE], (16, LANE))
        out = jnp.where(rid == r, piece, out)
    return out


def _softmax_masked(s, mask):
    s = jnp.where(mask, s, NEG)
    m = jnp.max(s, axis=-1, keepdims=True)
    e = jnp.where(mask, jnp.exp(s - m), 0.0)
    l = jnp.sum(e, axis=-1, keepdims=True)
    return e * (1.0 / jnp.maximum(l, 1e-30)), m, l


def _ffn_step(f, nf, x_ref, g_ref, wg_ref, wu_ref, wd_ref, o_ref, h_ref, acc_ref):
    @pl.when(f == 0)
    def _():
        h_ref[...] = _rms(x_ref[...], g_ref[...]).astype(BF16)
        acc_ref[...] = jnp.zeros_like(acc_ref)

    h = h_ref[...]
    gate = _dot(h, wg_ref[...])
    up = _dot(h, wu_ref[...])
    a = (gate * jax.nn.sigmoid(gate) * up).astype(BF16)
    acc_ref[...] += _dot(a, wd_ref[...])

    @pl.when(f == nf - 1)
    def _():
        o_ref[...] = x_ref[...] + 0.5 * acc_ref[...]


def _ffn_kernel(x_ref, g_ref, wg_ref, wu_ref, wd_ref, o_ref, h_ref, *, nf):
    _ffn_step(pl.program_id(1), nf, x_ref, g_ref, wg_ref, wu_ref, wd_ref, o_ref, h_ref, o_ref)


def _ffn(x, g, wg, wu, wd, li, lj, tm, tf=512):
    M, D = x.shape
    F = wg.shape[-1]
    nf = F // tf
    return pl.pallas_call(
        functools.partial(_ffn_kernel, nf=nf),
        grid=(M // tm, nf),
        in_specs=[
            pl.BlockSpec((tm, D), lambda i, f: (i, 0)),
            pl.BlockSpec((None, None, 1, D), lambda i, f: (li, lj, 0, 0)),
            pl.BlockSpec((D, tf), lambda i, f: (0, f)),
            pl.BlockSpec((D, tf), lambda i, f: (0, f)),
            pl.BlockSpec((tf, D), lambda i, f: (f, 0)),
        ],
        out_specs=pl.BlockSpec((tm, D), lambda i, f: (i, 0)),
        out_shape=jax.ShapeDtypeStruct((M, D), F32),
        scratch_shapes=[pltpu.VMEM((tm, D), BF16)],
        compiler_params=_params(("parallel", "arbitrary")),
        name="ffn",
    )(x, g, wg, wu, wd)


def _ffn_cast_kernel(x_ref, g_ref, wg_ref, wu_ref, wd_ref, o_ref, wgb_ref, wub_ref, wdb_ref, h_ref, *, nf):
    wgb_ref[...] = wg_ref[...].astype(BF16)
    wub_ref[...] = wu_ref[...].astype(BF16)
    wdb_ref[...] = wd_ref[...].astype(BF16)
    _ffn_step(pl.program_id(0), nf, x_ref, g_ref, wgb_ref, wub_ref, wdb_ref, o_ref, h_ref, o_ref)


def _ffn_cast(x, g, wg, wu, wd, li, lj, tf=512):
    M, D = x.shape
    F = wg.shape[-1]
    nf = F // tf
    return pl.pallas_call(
        functools.partial(_ffn_cast_kernel, nf=nf),
        grid=(nf,),
        in_specs=[
            pl.BlockSpec((M, D), lambda f: (0, 0)),
            pl.BlockSpec((None, None, 1, D), lambda f: (li, lj, 0, 0)),
            pl.BlockSpec((None, None, D, tf), lambda f: (li, lj, 0, f)),
            pl.BlockSpec((None, None, D, tf), lambda f: (li, lj, 0, f)),
            pl.BlockSpec((None, None, tf, D), lambda f: (li, lj, f, 0)),
        ],
        out_specs=[pl.BlockSpec((M, D), lambda f: (0, 0)),
                   pl.BlockSpec((D, tf), lambda f: (0, f)),
                   pl.BlockSpec((D, tf), lambda f: (0, f)),
                   pl.BlockSpec((tf, D), lambda f: (f, 0))],
        out_shape=[jax.ShapeDtypeStruct((M, D), F32),
                   jax.ShapeDtypeStruct((D, F), BF16),
                   jax.ShapeDtypeStruct((D, F), BF16),
                   jax.ShapeDtypeStruct((F, D), BF16)],
        scratch_shapes=[pltpu.VMEM((M, D), BF16)],
        compiler_params=_params(("arbitrary",)),
        name="ffn_cast",
    )(x, g, wg, wu, wd)


def _nmm_kernel(x_ref, g_ref, w_ref, o_ref, h_ref):
    @pl.when(pl.program_id(1) == 0)
    def _():
        h_ref[...] = _rms(x_ref[...], g_ref[...]).astype(BF16)

    o_ref[...] = _dot(h_ref[...], w_ref[...])


def _norm_matmul(x, g, w, tm, tn):
    M, D = x.shape
    N = w.shape[1]
    return pl.pallas_call(
        _nmm_kernel,
        grid=(M // tm, N // tn),
        in_specs=[
            pl.BlockSpec((tm, D), lambda i, j: (i, 0)),
            pl.BlockSpec((1, D), lambda i, j: (0, 0)),
            pl.BlockSpec((D, tn), lambda i, j: (0, j)),
        ],
        out_specs=pl.BlockSpec((tm, tn), lambda i, j: (i, j)),
        out_shape=jax.ShapeDtypeStruct((M, N), F32),
        scratch_shapes=[pltpu.VMEM((tm, D), BF16)],
        compiler_params=_params(("parallel", "arbitrary")),
        name="norm_matmul",
    )(x, g.reshape(1, D), w)


def _oproj_kernel(x_ref, a_ref, b_ref, w_ref, o_ref, *, ka):
    o_ref[...] = x_ref[...] + _dot(a_ref[...], w_ref[:ka, :]) + _dot(b_ref[...], w_ref[ka:, :])


def _out_proj(x, a, b, w, tm, tn=1024):
    M, D = x.shape
    ka, kb = a.shape[1], b.shape[1]
    return pl.pallas_call(
        functools.partial(_oproj_kernel, ka=ka),
        grid=(M // tm, D // tn),
        in_specs=[
            pl.BlockSpec((tm, tn), lambda i, j: (i, j)),
            pl.BlockSpec((tm, ka), lambda i, j: (i, 0)),
            pl.BlockSpec((tm, kb), lambda i, j: (i, 0)),
            pl.BlockSpec((ka + kb, tn), lambda i, j: (0, j)),
        ],
        out_specs=pl.BlockSpec((tm, tn), lambda i, j: (i, j)),
        out_shape=jax.ShapeDtypeStruct((M, D), F32),
        compiler_params=_params(("parallel", "parallel")),
        name="out_proj",
    )(x, a, b, w)


def _put_rows(ref, row, rows_per_token, val):
    ref[pl.ds(row, val.shape[0], stride=rows_per_token), :] = val


def _get_rows(ref, row, rows_per_token, n):
    return ref[pl.ds(row, n, stride=rows_per_token), :]


NSA_ROWS = 4 * NSA_KV_HEADS
WIN_ROWS = 2 * NSA_KV_HEADS
KVH_ROWS = 2 * DIL_HEADS


def _nsa_post_kernel(p_ref, cos_ref, sin_ref, g_ref, qn_ref, qr_ref, cmp_ref, rows_ref, win_ref, kvb_ref, mq_ref):
    cos, sin = cos_ref[...], sin_ref[...]
    q_g, ks_g, kw_g, mq_g = g_ref[0:1, :], g_ref[1:2, :], g_ref[2:3, :], g_ref[3:4, :]

    def tile(i):
        return p_ref[:, i * LANE:(i + 1) * LANE]

    for h in range(N_MIX_HEADS):
        qn = _rms(tile(h), q_g)
        qn_ref[:, h * LANE:(h + 1) * LANE] = qn.astype(BF16)
        qr_ref[:, h * LANE:(h + 1) * LANE] = _rope(qn, cos, sin).astype(BF16)
    for g in range(NSA_KV_HEADS):
        kc, vc = tile(12 + g), tile(15 + g)
        ks = _rope(_rms(tile(18 + g), ks_g), cos, sin)
        vs = tile(21 + g)
        kw = _rope(_rms(tile(24 + g), kw_g), cos, sin)
        vw = tile(27 + g)
        for c, val in enumerate((kc, vc)):
            cmp_ref[:, (c * 3 + g) * LANE:(c * 3 + g + 1) * LANE] = val
        for c, val in enumerate((kc, vc, ks, vs)):
            _put_rows(rows_ref, g * 4 + c, NSA_ROWS, val)
        for c, val in enumerate((kw, vw)):
            _put_rows(win_ref, g * 2 + c, WIN_ROWS, val)
        for c, val in enumerate((ks, vs, kw, vw)):
            kvb_ref[:, (c * 3 + g) * LANE:(c * 3 + g + 1) * LANE] = val.astype(BF16)
    for h in range(N_MEM_HEADS):
        mq_ref[:, h * LANE:(h + 1) * LANE] = _rms(tile(30 + h), mq_g).astype(BF16)


def _nsa_post(p, cos, sin, gains, tm):
    M = p.shape[0]
    row = lambda n: pl.BlockSpec((tm, n), lambda i: (i, 0))
    flat = lambda r: pl.BlockSpec((tm * r, LANE), lambda i: (i, 0))
    return pl.pallas_call(
        _nsa_post_kernel,
        grid=(M // tm,),
        in_specs=[row(PROJ_N), row(LANE), row(LANE), pl.BlockSpec((8, LANE), lambda i: (0, 0))],
        out_specs=[row(1536), row(1536), row(768), flat(NSA_ROWS), flat(WIN_ROWS), row(1536), row(512)],
        out_shape=[
            jax.ShapeDtypeStruct((M, 1536), BF16),
            jax.ShapeDtypeStruct((M, 1536), BF16),
            jax.ShapeDtypeStruct((M, 768), F32),
            jax.ShapeDtypeStruct((M * NSA_ROWS, LANE), F32),
            jax.ShapeDtypeStruct((M * WIN_ROWS, LANE), F32),
            jax.ShapeDtypeStruct((M, 1536), BF16),
            jax.ShapeDtypeStruct((M, 512), BF16),
        ],
        compiler_params=_params(("parallel",)),
        name="nsa_post",
    )(p, cos, sin, gains)


def _put_residues(ref, hh, val, dil, tmp_ref):
    sl = slice(hh * LANE, (hh + 1) * LANE)
    if dil == 1:
        ref[0, :, sl] = val.astype(BF16)
        return
    tmp_ref[...] = val
    n = val.shape[0] // dil
    for r in range(dil):
        ref[r, :, sl] = tmp_ref[pl.ds(r, n, stride=dil), :].astype(BF16)


def _dil_post_kernel(p_ref, cos_ref, sin_ref, g_ref, *refs, by_residue):
    if by_residue:
        mq_ref, st0_ref, st1_ref, st2_ref = refs[:4]
        qkv_refs, tmp_ref = refs[4:13], refs[13]
    else:
        qr_ref, mq_ref, st0_ref, st1_ref, st2_ref = refs
    cos, sin = cos_ref[...], sin_ref[...]
    q_g, k_g, mq_g = g_ref[0:1, :], g_ref[1:2, :], g_ref[2:3, :]
    st_refs = (st0_ref, st1_ref, st2_ref)
    for h in range(N_MIX_HEADS):
        g, hh = divmod(h, DIL_HEADS)
        sl = slice(h * LANE, (h + 1) * LANE)
        q = _rope(_rms(p_ref[:, sl], q_g), cos, sin)
        k = _rope(_rms(p_ref[:, (12 + h) * LANE:(13 + h) * LANE], k_g), cos, sin)
        v = p_ref[:, (24 + h) * LANE:(25 + h) * LANE]
        _put_rows(st_refs[g], hh, KVH_ROWS, k)
        _put_rows(st_refs[g], DIL_HEADS + hh, KVH_ROWS, v)
        if by_residue:
            for c, val in enumerate((q, k, v)):
                _put_residues(qkv_refs[3 * g + c], hh, val, DIL_PAIRS[g][1], tmp_ref)
        else:
            qr_ref[:, sl] = q.astype(BF16)
    for h in range(N_MEM_HEADS):
        mq_ref[:, h * LANE:(h + 1) * LANE] = _rms(p_ref[:, (36 + h) * LANE:(37 + h) * LANE], mq_g).astype(BF16)


def _dil_post(p, cos, sin, gains, tm, seq_len=None):
    M = p.shape[0]
    row = lambda n: pl.BlockSpec((tm, n), lambda i: (i, 0))
    flat = pl.BlockSpec((tm * KVH_ROWS, LANE), lambda i: (i, 0))
    st_shape = jax.ShapeDtypeStruct((M * KVH_ROWS, LANE), F32)
    out_specs = [row(512), flat, flat, flat]
    out_shape = [jax.ShapeDtypeStruct((M, 512), BF16), st_shape, st_shape, st_shape]
    scratch = []
    if seq_len is None:
        out_specs = [row(1536)] + out_specs
        out_shape = [jax.ShapeDtypeStruct((M, 1536), BF16)] + out_shape
    else:
        nt = seq_len // tm
        for _, dil in DIL_PAIRS:
            assert tm % (16 * dil) == 0
            spec = pl.BlockSpec((None, dil, tm // dil, 4 * LANE), lambda i: (i // nt, 0, i % nt, 0))
            shape = jax.ShapeDtypeStruct((M // seq_len, dil, seq_len // dil, 4 * LANE), BF16)
            out_specs += [spec] * 3
            out_shape += [shape] * 3
        scratch = [pltpu.VMEM((tm, LANE), F32)]
    return pl.pallas_call(
        functools.partial(_dil_post_kernel, by_residue=seq_len is not None),
        grid=(M // tm,),
        in_specs=[row(PROJ_N), row(LANE), row(LANE), pl.BlockSpec((8, LANE), lambda i: (0, 0))],
        out_specs=out_specs,
        out_shape=out_shape,
        scratch_shapes=scratch,
        compiler_params=_params(("parallel",)),
        name="dil_post",
    )(p, cos, sin, gains)


def _memkv_post_kernel(x_ref, g_ref, o_ref):
    for h in range(N_MEM_HEADS):
        _put_rows(o_ref, h, KVH_ROWS, _rms(x_ref[:, h * LANE:(h + 1) * LANE], g_ref[...]))
        _put_rows(o_ref, N_MEM_HEADS + h, KVH_ROWS, x_ref[:, (N_MEM_HEADS + h) * LANE:(N_MEM_HEADS + h + 1) * LANE])


def _memkv_post(x, g, tm=256):
    M, N = x.shape
    return pl.pallas_call(
        _memkv_post_kernel,
        grid=(M // tm,),
        in_specs=[pl.BlockSpec((tm, N), lambda i: (i, 0)), pl.BlockSpec((1, LANE), lambda i: (0, 0))],
        out_specs=pl.BlockSpec((tm * KVH_ROWS, LANE), lambda i: (i, 0)),
        out_shape=jax.ShapeDtypeStruct((M * KVH_ROWS, LANE), F32),
        compiler_params=_params(("parallel",)),
        name="memkv_post",
    )(x, g.reshape(1, LANE))


def _mem_attn_kernel(q_ref, kv_ref, o_ref, *, tq):
    for h in range(N_MEM_HEADS):
        sl = slice(h * LANE, (h + 1) * LANE)
        if tq == 1:
            q = _rows16(q_ref[:, sl], 1).astype(BF16)
        else:
            q = q_ref[:, sl]
        k = _get_rows(kv_ref, h, KVH_ROWS, N_MEM).astype(BF16)
        v = _get_rows(kv_ref, N_MEM_HEADS + h, KVH_ROWS, N_MEM).astype(BF16)
        s = _dot_t(q, k) * SCALE
        m = jnp.max(s, axis=-1, keepdims=True)
        e = jnp.exp(s - m)
        p = e / jnp.sum(e, axis=-1, keepdims=True)
        o = _dot(p.astype(BF16), v)
        o_ref[:, sl] = o[0:tq, :].astype(o_ref.dtype)


def _mem_attn(q, kv, tq):
    B, T, _ = q.shape
    return pl.pallas_call(
        functools.partial(_mem_attn_kernel, tq=tq),
        grid=(B, T // tq),
        in_specs=[
            pl.BlockSpec((None, tq, 512), lambda b, i: (b, i, 0)),
            pl.BlockSpec((None, N_MEM * KVH_ROWS, LANE), lambda b, i: (b, 0, 0)),
        ],
        out_specs=pl.BlockSpec((None, tq, 512), lambda b, i: (b, i, 0)),
        out_shape=jax.ShapeDtypeStruct((B, T, 512), q.dtype),
        compiler_params=_params(("parallel", "parallel")),
        name="mem_attn",
    )(q, kv)


def _gelu_tanh(x):
    return 0.5 * x * (1.0 + jnp.tanh(0.7978845608028654 * (x + 0.044715 * (x * x * x))))


def _compress_finish(h, b1, w2):
    n = h.shape[0]
    hid = b1 + h[:, :LANE] + pltpu.roll(h[:, LANE:], n - 1, 0)
    return _dot(_gelu_tanh(hid).astype(BF16), w2)


def _compress(x_bf, w1, b1, w2):
    return _compress_finish(_dot(x_bf, w1), b1, w2)


def _cmp_prompt_kernel(x_ref, w1_ref, b1_ref, w2_ref, kcg_ref, o_ref, xs_ref, *, n):
    kv = pl.program_id(1)
    for c in range(CMP_STRIDE):
        xs_ref[:, c * LANE:(c + 1) * LANE] = x_ref[pl.ds(c, n, stride=CMP_STRIDE), :].astype(BF16)
    out = _compress(xs_ref[...], w1_ref[...], b1_ref[...], w2_ref[...])
    out = jnp.where(kv == 0, _rms(out, kcg_ref[...]), out)
    rid = lax.broadcasted_iota(jnp.int32, out.shape, 0)
    o_ref[...] = jnp.where(rid < n - 1, out, 0.0).astype(BF16)


def _cmp_prompt(rows, w1r, b1, w2, kc_g):
    B, T, _ = rows.shape
    n = T // CMP_STRIDE
    return pl.pallas_call(
        functools.partial(_cmp_prompt_kernel, n=n),
        grid=(B, 2, NSA_KV_HEADS),
        in_specs=[
            pl.BlockSpec((None, T, LANE), lambda b, kv, g: (b, 0, kv * 3 + g)),
            pl.BlockSpec((None, CMP_STRIDE * LANE, 2 * LANE), lambda b, kv, g: (kv, 0, 0)),
            pl.BlockSpec((None, 1, LANE), lambda b, kv, g: (kv, 0, 0)),
            pl.BlockSpec((None, LANE, LANE), lambda b, kv, g: (kv, 0, 0)),
            pl.BlockSpec((1, LANE), lambda b, kv, g: (0, 0)),
        ],
        out_specs=pl.BlockSpec((None, None, None, n, LANE), lambda b, kv, g: (b, kv, g, 0, 0)),
        out_shape=jax.ShapeDtypeStruct((B, 2, NSA_KV_HEADS, n, LANE), BF16),
        scratch_shapes=[pltpu.VMEM((n, CMP_STRIDE * LANE), BF16)],
        compiler_params=_params(("parallel", "parallel", "parallel")),
        name="cmp_prompt",
    )(rows, w1r, b1, w2, kc_g)


def _select_blocks(score, cur, n_blocks):
    tq = score.shape[0]
    blk = lax.broadcasted_iota(jnp.int32, score.shape, 1)
    forced = (blk == 0) | (blk == cur) | (blk == cur - 1)
    sc = jnp.where(blk <= cur, jnp.where(forced, FORCE_SCORE, score), NEG)
    sct = sc.T[0:n_blocks, :]
    bi = lax.broadcasted_iota(jnp.int32, sct.shape, 0)
    rank = jnp.zeros(sct.shape, F32)
    for i in range(n_blocks):
        si = sct[i:i + 1, :]
        ahead = (si > sct) | ((si == sct) & (bi > i))
        rank = rank + jnp.where(ahead, 1.0, 0.0)
    chosen = jnp.where((rank < SEL_TOPK) & (sct > 0.5 * NEG), 1.0, 0.0)
    return jnp.concatenate([chosen, jnp.zeros((LANE - n_blocks, tq), F32)], axis=0).T


def _score_tiles(q, k_ref, tiles, s_ref, m_ref, tk, first=False):
    ms = []
    for kt, slot, bias in tiles:
        s = _dot_t(q, k_ref[pl.ds(pl.multiple_of(kt * tk, tk), tk), :]) * SCALE
        if bias is not None:
            s = s + bias
        s_ref[slot] = s
        ms += [s[:, c * LANE:(c + 1) * LANE] for c in range(tk // LANE)]
    m = functools.reduce(jnp.maximum, ms)
    m_ref[...] = m if first else jnp.maximum(m_ref[...], m)


def _value_tiles(v_ref, tiles, s_ref, m_ref, l_ref, acc_ref, tk, first=False):
    m = m_ref[...]
    ls, pvs = [], []
    for kt, slot in tiles:
        es = [jnp.exp(s_ref[slot, :, c * LANE:(c + 1) * LANE] - m) for c in range(tk // LANE)]
        pvs.append(_dot(jnp.concatenate(es, axis=1).astype(BF16),
                        v_ref[pl.ds(pl.multiple_of(kt * tk, tk), tk), :]))
        ls += es
    l = functools.reduce(lambda a, b: a + b, ls)
    pv = functools.reduce(lambda a, b: a + b, pvs)
    if first:
        l_ref[...] = l
        acc_ref[...] = pv
    else:
        l_ref[...] += l
        acc_ref[...] += pv


def _nsa_attn_kernel(qn_ref, qr_ref, gl_ref, kc_ref, vc_ref, ks_ref, vs_ref, kw_ref, vw_ref, cover_ref,
                     o_ref, s_ref, m_ref, l_ref, acc_ref, sel_ref, *, tq, ns):
    qi = pl.program_id(2)
    R = NSA_GROUP
    t0 = qi * tq
    stack = lambda ref: jnp.concatenate([ref[:, r * LANE:(r + 1) * LANE] for r in range(R)], axis=0)
    rows4 = lambda x: jnp.concatenate([x] * R, axis=0)
    tpos_q = t0 + lax.broadcasted_iota(jnp.int32, (tq, 1), 0)
    row_in = lax.broadcasted_iota(jnp.int32, (R * tq, 1), 0) & (tq - 1)
    col = lax.broadcasted_iota(jnp.int32, (R * tq, tq), 1)

    s = _dot_t(stack(qn_ref), kc_ref[...]) * SCALE
    cblk = lax.broadcasted_iota(jnp.int32, (R * tq, LANE), 1)
    cmask = (CMP_STRIDE * cblk + (CMP_BLOCK - 1) <= t0 + row_in) & (cblk < kc_ref.shape[0] - 1)
    p, _, _ = _softmax_masked(s, cmask)
    o_cmp = _dot(p.astype(BF16), vc_ref[...])
    imp = p[0:tq] + p[tq:2 * tq] + p[2 * tq:3 * tq] + p[3 * tq:4 * tq]
    score = _dot3(imp, cover_ref[...])
    cur = lax.shift_right_arithmetic(tpos_q, SEL_BLOCK.bit_length() - 1)

    @pl.when(t0 + tq <= SEL_TOPK * SEL_BLOCK)
    def _():
        sel_ref[...] = jnp.where(lax.broadcasted_iota(jnp.int32, (tq, LANE), 1) <= cur, 1.0, 0.0).astype(BF16)

    @pl.when(t0 + tq > SEL_TOPK * SEL_BLOCK)
    def _():
        sel_ref[...] = _select_blocks(score, cur, ns).astype(BF16)

    sel = sel_ref[...]
    q_rot = stack(qr_ref)
    blocks_per_tile = tq // SEL_BLOCK
    causal = jnp.where(col <= row_in, 0.0, NEG)
    far = jnp.where(col >= row_in, 0.0, NEG)

    def row_max():
        m_ref[...] = jnp.broadcast_to(jnp.max(m_ref[...], axis=-1, keepdims=True), m_ref.shape)

    def result():
        return acc_ref[...] * (1.0 / jnp.maximum(jnp.sum(l_ref[...], axis=-1, keepdims=True), 1e-30))

    def member_bias(kt):
        key_blk = lax.shift_right_arithmetic(lax.broadcasted_iota(jnp.int32, (LANE, tq), 1),
                                             SEL_BLOCK.bit_length() - 1)
        expand = lax.broadcasted_iota(jnp.int32, (LANE, tq), 0) == kt * blocks_per_tile + key_blk
        member = _dot(sel, jnp.where(expand, 1.0, 0.0).astype(BF16))
        return rows4((member - 1.0) * (-NEG))

    sel_tile = lambda kt: (kt, kt, member_bias(kt))
    _score_tiles(q_rot, ks_ref, [(qi, qi, member_bias(qi) + causal)], s_ref, m_ref, tq, first=True)

    def sel_scores(i, carry):
        _score_tiles(q_rot, ks_ref, [sel_tile(2 * i), sel_tile(2 * i + 1)], s_ref, m_ref, tq)
        return carry

    lax.fori_loop(0, qi // 2, sel_scores, 0)

    @pl.when(qi % 2 == 1)
    def _():
        _score_tiles(q_rot, ks_ref, [sel_tile(qi - 1)], s_ref, m_ref, tq)

    row_max()
    _value_tiles(vs_ref, [(qi, qi)], s_ref, m_ref, l_ref, acc_ref, tq, first=True)

    def sel_values(i, carry):
        _value_tiles(vs_ref, [(2 * i, 2 * i), (2 * i + 1, 2 * i + 1)], s_ref, m_ref, l_ref, acc_ref, tq)
        return carry

    lax.fori_loop(0, qi // 2, sel_values, 0)

    @pl.when(qi % 2 == 1)
    def _():
        _value_tiles(vs_ref, [(qi - 1, qi - 1)], s_ref, m_ref, l_ref, acc_ref, tq)

    o_sel = result()

    n_back = NSA_WINDOW // tq
    win_tiles = []
    for back in range(n_back + 1):
        bias = jnp.where(qi >= back, 0.0, NEG)
        if back == 0:
            bias = causal
        elif back == n_back:
            bias = far + bias
        win_tiles.append((jnp.maximum(qi - back, 0), back, bias))
    _score_tiles(q_rot, kw_ref, win_tiles, s_ref, m_ref, tq, first=True)
    row_max()
    _value_tiles(vw_ref, [(kt, slot) for kt, slot, _ in win_tiles], s_ref, m_ref, l_ref, acc_ref, tq, first=True)
    o_win = result()

    gates = jax.nn.sigmoid(gl_ref[...])
    for r in range(R):
        rs = slice(r * tq, (r + 1) * tq)
        o = (gates[:, 3 * r:3 * r + 1] * o_cmp[rs] + gates[:, 3 * r + 1:3 * r + 2] * o_sel[rs]
             + gates[:, 3 * r + 2:3 * r + 3] * o_win[rs])
        o_ref[:, r * LANE:(r + 1) * LANE] = o.astype(BF16)


def _nsa_attn(qn, qr, proj, cmp_kv, kvb, cover, tq=256):
    B, T, _ = qn.shape
    nc = cmp_kv.shape[3]
    G = NSA_KV_HEADS
    assert tq % LANE == 0 and NSA_WINDOW % tq == 0 and nc == LANE and T // SEL_BLOCK <= LANE
    rows = NSA_GROUP * tq
    qspec = pl.BlockSpec((None, tq, 4 * LANE), lambda b, g, i: (b, i, g))
    kvspec = lambda c: pl.BlockSpec((None, T, LANE), lambda b, g, i: (b, 0, c * 3 + g))
    return pl.pallas_call(
        functools.partial(_nsa_attn_kernel, tq=tq, ns=T // SEL_BLOCK),
        grid=(B, G, T // tq),
        in_specs=[
            qspec, qspec,
            pl.BlockSpec((None, tq, LANE), lambda b, g, i: (b, i, NSA_GATE_BLK + g)),
            pl.BlockSpec((None, None, None, nc, LANE), lambda b, g, i: (b, 0, g, 0, 0)),
            pl.BlockSpec((None, None, None, nc, LANE), lambda b, g, i: (b, 1, g, 0, 0)),
            kvspec(0), kvspec(1), kvspec(2), kvspec(3),
            pl.BlockSpec((nc, LANE), lambda b, g, i: (0, 0)),
        ],
        out_specs=qspec,
        out_shape=jax.ShapeDtypeStruct((B, T, 1536), BF16),
        scratch_shapes=[pltpu.VMEM((T // tq, rows, tq), F32),
                        pltpu.VMEM((rows, LANE), F32),
                        pltpu.VMEM((rows, LANE), F32),
                        pltpu.VMEM((rows, LANE), F32),
                        pltpu.VMEM((tq, LANE), BF16)],
        compiler_params=_params(("parallel", "parallel", "arbitrary")),
        name="nsa_attn",
    )(qn, qr, proj, cmp_kv, cmp_kv, kvb, kvb, kvb, kvb, cover)


def _dil_band_kernel(q_ref, kp_ref, kc_ref, vp_ref, vc_ref, o_ref, st_ref, *, tq, window):
    i = pl.program_id(2)
    w = window
    row = lax.broadcasted_iota(jnp.int32, (w, 2 * w), 0)
    colk = lax.broadcasted_iota(jnp.int32, (w, 2 * w), 1)
    diff = w + row - colk
    band = (diff >= 0) & (diff <= window)
    lane = lax.broadcasted_iota(jnp.int32, (w, LANE), 1)
    for r in range(q_ref.shape[0]):
        for j in range(tq // w):
            mask = band & (i * tq + (j - 1) * w + colk >= 0)
            rows = slice(j * w, (j + 1) * w)
            stats = jnp.zeros((w, LANE), F32)
            for h in range(DIL_HEADS):
                sl = slice(h * LANE, (h + 1) * LANE)
                if j == 0:
                    k = jnp.concatenate([kp_ref[r, :, sl], kc_ref[r, 0:w, sl]], axis=0)
                    v = jnp.concatenate([vp_ref[r, :, sl], vc_ref[r, 0:w, sl]], axis=0)
                else:
                    k = kc_ref[r, (j - 1) * w:(j + 1) * w, sl]
                    v = vc_ref[r, (j - 1) * w:(j + 1) * w, sl]
                p, m, l = _softmax_masked(_dot_t(q_ref[r, rows, sl], k) * SCALE, mask)
                o_ref[r, rows, sl] = _dot(p.astype(BF16), v)
                stats = jnp.where(lane == h, m, stats)
                stats = jnp.where(lane == DIL_HEADS + h, l, stats)
            st_ref[r, rows, :] = stats


def _dil_band(q, k, v, g, tq, nr):
    B, dil, S, _ = q.shape
    window = DIL_PAIRS[g][0] // dil
    assert tq % window == 0 and S % tq == 0 and dil % nr == 0
    cur = lambda w: pl.BlockSpec((None, nr, tq, w), lambda b, r, i: (b, r, i, 0))
    prev = pl.BlockSpec((None, nr, window, 4 * LANE),
                        lambda b, r, i: (b, r, jnp.maximum(i * (tq // window) - 1, 0), 0))
    return pl.pallas_call(
        functools.partial(_dil_band_kernel, tq=tq, window=window),
        grid=(B, dil // nr, S // tq),
        in_specs=[cur(4 * LANE), prev, cur(4 * LANE), prev, cur(4 * LANE)],
        out_specs=[cur(4 * LANE), cur(LANE)],
        out_shape=[jax.ShapeDtypeStruct((B, dil, S, 4 * LANE), F32),
                   jax.ShapeDtypeStruct((B, dil, S, LANE), F32)],
        compiler_params=_params(("parallel", "parallel", "parallel")),
        name=f"dil_band{g}",
    )(q, k, k, v, v)


def _mix_groups(os_, ms, ls):
    m_all = jnp.maximum(jnp.maximum(ms[0], ms[1]), ms[2])
    ws = [jnp.exp(m - m_all) * l for m, l in zip(ms, ls)]
    tot = ws[0] + ws[1] + ws[2]
    return (ws[0] / tot) * os_[0] + (ws[1] / tot) * os_[1] + (ws[2] / tot) * os_[2]


def _dil_mix_kernel(o0_ref, o1_ref, o2_ref, s0_ref, s1_ref, s2_ref, o_ref, nat_ref):
    o_refs, s_refs = (o0_ref, o1_ref, o2_ref), (s0_ref, s1_ref, s2_ref)
    tm = o_ref.shape[0]

    def token_order(ref, sl):
        dil = ref.shape[0]
        if dil == 1:
            return ref[0, :, sl]
        for r in range(dil):
            nat_ref[pl.ds(r, tm // dil, stride=dil), :] = ref[r, :, sl]
        return nat_ref[...]

    stats = [token_order(s, slice(0, LANE)) for s in s_refs]
    for h in range(DIL_HEADS):
        sl = slice(h * LANE, (h + 1) * LANE)
        ms = [s[:, h:h + 1] for s in stats]
        ls = [s[:, DIL_HEADS + h:DIL_HEADS + h + 1] for s in stats]
        o_ref[:, sl] = _mix_groups([token_order(o, sl) for o in o_refs], ms, ls).astype(BF16)


def _dil_mix(os_, sts, tm=512):
    B, _, T, _ = os_[0].shape
    nt = T // tm
    in_specs = []
    for lanes, arrs in ((4 * LANE, os_), (LANE, sts)):
        for a in arrs:
            dil = a.shape[1]
            in_specs.append(pl.BlockSpec((None, dil, tm // dil, lanes), lambda b, i: (b, 0, i, 0)))
    return pl.pallas_call(
        _dil_mix_kernel,
        grid=(B, nt),
        in_specs=in_specs,
        out_specs=pl.BlockSpec((tm, 4 * LANE), lambda b, i: (b * nt + i, 0)),
        out_shape=jax.ShapeDtypeStruct((B * T, 4 * LANE), BF16),
        scratch_shapes=[pltpu.VMEM((tm, LANE), F32)],
        compiler_params=_params(("parallel", "parallel")),
        name="dil_mix",
    )(*os_, *sts)


DEC_PAGES_PER_STEP = 16


def _dec_select_kernel(tbl_ref, *refs, n_pp, n_steps):
    pages = refs[:n_pp]
    (qn_ref, w1_ref, b1_ref, w2_ref, kcg_ref, cover_ref, idx_ref, ocmp_ref, h_ref) = refs[n_pp:]
    j = pl.program_id(1)
    cpp = PAGE_SIZE // CMP_STRIDE
    rows = n_pp * cpp
    n = n_steps * rows
    ns = cover_ref.shape[1]

    for g in range(NSA_KV_HEADS):
        for kv in range(2):
            x = jnp.concatenate(
                [_get_rows(pg, g * 4 + kv, NSA_ROWS, PAGE_SIZE).reshape(cpp, CMP_STRIDE * LANE) for pg in pages],
                axis=0)
            h_ref[g * 2 + kv, pl.ds(pl.multiple_of(j * rows, rows), rows), :] = _dot(x.astype(BF16), w1_ref[kv])

    @pl.when(j == n_steps - 1)
    def _():
        idx_ref[...] = jnp.zeros_like(idx_ref)
        for g in range(NSA_KV_HEADS):
            kc = _rms(_compress_finish(h_ref[g * 2], b1_ref[0], w2_ref[0]), kcg_ref[...]).astype(BF16)
            vc = _compress_finish(h_ref[g * 2 + 1], b1_ref[1], w2_ref[1]).astype(BF16)
            q = _rows16(qn_ref[:, g * NSA_GROUP * LANE:(g + 1) * NSA_GROUP * LANE], NSA_GROUP).astype(BF16)
            s = _dot_t(q, kc) * SCALE
            valid = lax.broadcasted_iota(jnp.int32, s.shape, 1) < n - 1
            p, _, _ = _softmax_masked(s, valid)
            ocmp_ref[g] = _dot(p.astype(BF16), vc)[0:NSA_GROUP, :]
            rid = lax.broadcasted_iota(jnp.int32, p.shape, 0)
            imp = jnp.sum(jnp.where(rid < NSA_GROUP, p, 0.0), axis=0, keepdims=True)
            score = _dot3(jnp.broadcast_to(imp, (8, n)), cover_ref[...])

            a = jnp.broadcast_to(score[0:1, :], (ns, ns))
            lane = lax.broadcasted_iota(jnp.int32, (ns, ns), 1)
            sub = lax.broadcasted_iota(jnp.int32, (ns, ns), 0)
            cur = n * CMP_STRIDE // SEL_BLOCK
            forced = (lane == 0) | (lane == cur) | (lane == cur - 1)
            a = jnp.where(lane <= cur, jnp.where(forced, FORCE_SCORE, a), NEG)
            at = a.T
            ahead_r = (at > a) | ((at == a) & (sub < lane))
            chosen_r = ((jnp.sum(jnp.where(ahead_r, 1.0, 0.0), axis=0, keepdims=True) < SEL_TOPK)
                        & (a[0:1, :] > 0.5 * NEG))
            ahead_c = (a > at) | ((a == at) & (lane < sub))
            chosen_c = ((jnp.sum(jnp.where(ahead_c, 1.0, 0.0), axis=1, keepdims=True) < SEL_TOPK)
                        & (at[:, 0:1] > 0.5 * NEG))
            before = jnp.sum(jnp.where(chosen_r & (lane < sub), 1.0, 0.0), axis=1, keepdims=True)
            slot = lax.broadcasted_iota(jnp.int32, (ns, LANE), 1)
            onehot = chosen_c & (before == slot.astype(F32))
            blk = lax.broadcasted_iota(jnp.int32, (ns, LANE), 0)
            picked = jnp.sum(jnp.where(onehot, blk.astype(F32), 0.0), axis=0, keepdims=True)
            filled = jnp.sum(jnp.where(onehot, 1.0, 0.0), axis=0, keepdims=True)
            idx_ref[g:g + 1, :] = jnp.where(filled > 0.5, picked, float(cur)).astype(jnp.int32)


def _dec_select(cache, table, qn, w1r, b1, w2, kc_g, cover):
    B, n_pages = table.shape
    n_pp = DEC_PAGES_PER_STEP
    n_steps = n_pages // n_pp
    n = n_pages * (PAGE_SIZE // CMP_STRIDE)
    ns = cover.shape[1]
    G = NSA_KV_HEADS
    const = lambda *shape: pl.BlockSpec(shape, lambda b, j, tbl: (0,) * len(shape))
    page_spec = lambda p: pl.BlockSpec((None, PAGE_SIZE * NSA_ROWS, LANE), lambda b, j, tbl: (tbl[b, j * n_pp + p], 0, 0))
    grid_spec = pltpu.PrefetchScalarGridSpec(
        num_scalar_prefetch=1,
        grid=(B, n_steps),
        in_specs=[page_spec(p) for p in range(n_pp)] + [
            pl.BlockSpec((None, 1, 1536), lambda b, j, tbl: (b, 0, 0)),
            const(2, CMP_STRIDE * LANE, 2 * LANE), const(2, 1, LANE), const(2, LANE, LANE), const(1, LANE),
            const(n, ns),
        ],
        out_specs=[pl.BlockSpec((None, 8, LANE), lambda b, j, tbl: (b, 0, 0)),
                   pl.BlockSpec((None, G, NSA_GROUP, LANE), lambda b, j, tbl: (b, 0, 0, 0))],
        scratch_shapes=[pltpu.VMEM((2 * G, n, 2 * LANE), F32)],
    )
    return pl.pallas_call(
        functools.partial(_dec_select_kernel, n_pp=n_pp, n_steps=n_steps),
        grid_spec=grid_spec,
        out_shape=[jax.ShapeDtypeStruct((B, 8, LANE), jnp.int32),
                   jax.ShapeDtypeStruct((B, G, NSA_GROUP, LANE), F32)],
        compiler_params=_params(("arbitrary", "arbitrary")),
        name="dec_select",
    )(table, *([cache] * n_pp), qn, w1r, b1, w2, kc_g, cover)


def _dec_attn_kernel(tbl_ref, idx_ref, *refs, n_sel, cur, wb):
    blocks = refs[:n_sel]
    (qr_ref, kvn_ref, win_ref, gl_ref, ocmp_ref, o_ref) = refs[n_sel:]
    b, g = pl.program_id(0), pl.program_id(1)
    R = NSA_GROUP
    q = _rows16(qr_ref[...], R).astype(BF16)
    qf = q.astype(F32)
    new = kvn_ref[...]
    ks_n, vs_n, kw_n, vw_n = (new[:, c * LANE:(c + 1) * LANE] for c in range(4))

    rows = [r.reshape(SEL_BLOCK * 4, LANE) for r in blocks]
    k = jnp.concatenate([_get_rows(r, 2, 4, SEL_BLOCK) for r in rows], axis=0).astype(BF16)
    v = jnp.concatenate([_get_rows(r, 3, 4, SEL_BLOCK) for r in rows], axis=0).astype(BF16)
    s = _dot_t(q, k) * SCALE
    blk_of = lax.shift_right_arithmetic(lax.broadcasted_iota(jnp.int32, s.shape, 1), SEL_BLOCK.bit_length() - 1)
    valid = jnp.zeros(s.shape, jnp.int32)
    for n in range(n_sel):
        is_past = jnp.where(idx_ref[(b * NSA_KV_HEADS + g) * n_sel + n] != cur, 1, 0)
        valid = jnp.where(blk_of == n, is_past, valid)
    valid = valid > 0
    s_new = jnp.sum(qf * ks_n, axis=-1, keepdims=True) * SCALE
    s = jnp.where(valid, s, NEG)
    m = jnp.maximum(jnp.max(s, axis=-1, keepdims=True), s_new)
    e = jnp.where(valid, jnp.exp(s - m), 0.0)
    e_new = jnp.exp(s_new - m)
    l = jnp.sum(e, axis=-1, keepdims=True) + e_new
    o_sel = (_dot(e.astype(BF16), v) + e_new.astype(BF16).astype(F32) * vs_n) / l

    s = _dot_t(q, _get_rows(win_ref, g * 2, WIN_ROWS, wb).astype(BF16)) * SCALE
    s_new = jnp.sum(qf * kw_n, axis=-1, keepdims=True) * SCALE
    m = jnp.maximum(jnp.max(s, axis=-1, keepdims=True), s_new)
    e = jnp.exp(s - m)
    e_new = jnp.exp(s_new - m)
    l = jnp.sum(e, axis=-1, keepdims=True) + e_new
    o_win = (_dot(e.astype(BF16), _get_rows(win_ref, g * 2 + 1, WIN_ROWS, wb).astype(BF16))
             + e_new.astype(BF16).astype(F32) * vw_n) / l

    gates = jax.nn.sigmoid(gl_ref[...])
    o_cmp = ocmp_ref[...]
    for r in range(R):
        o = (gates[:, 3 * r:3 * r + 1] * o_cmp[r:r + 1] + gates[:, 3 * r + 1:3 * r + 2] * o_sel[r:r + 1]
             + gates[:, 3 * r + 2:3 * r + 3] * o_win[r:r + 1])
        o_ref[:, r * LANE:(r + 1) * LANE] = o


def _dec_attn(cache, table, idx, qr, kvb, win_state, proj, ocmp):
    B, n_pages = table.shape
    n_sel = SEL_TOPK
    cur = n_pages * PAGE_SIZE // SEL_BLOCK
    wb = win_state.shape[1] // WIN_ROWS
    assert wb <= NSA_WINDOW
    halves = PAGE_SIZE // SEL_BLOCK

    def blk_spec(n):
        def imap(b, g, tbl, idx):
            i = jnp.minimum(idx[(b * NSA_KV_HEADS + g) * n_sel + n], cur - 1)
            return (tbl[b, i // halves], i % halves, g, 0, 0)
        return pl.BlockSpec((None, SEL_BLOCK, None, 4, LANE), imap)

    grid_spec = pltpu.PrefetchScalarGridSpec(
        num_scalar_prefetch=2,
        grid=(B, NSA_KV_HEADS),
        in_specs=[blk_spec(n) for n in range(n_sel)] + [
            pl.BlockSpec((None, 1, 4 * LANE), lambda b, g, tbl, idx: (b, 0, g)),
            pl.BlockSpec((None, 1, 4 * LANE), lambda b, g, tbl, idx: (b, 0, g)),
            pl.BlockSpec((None, wb * WIN_ROWS, LANE), lambda b, g, tbl, idx: (b, 0, 0)),
            pl.BlockSpec((None, 1, LANE), lambda b, g, tbl, idx: (b, 0, NSA_GATE_BLK + g)),
            pl.BlockSpec((None, None, NSA_GROUP, LANE), lambda b, g, tbl, idx: (b, g, 0, 0)),
        ],
        out_specs=pl.BlockSpec((None, 1, 4 * LANE), lambda b, g, tbl, idx: (b, 0, g)),
    )
    kvn = kvb.reshape(B, 1, 4, NSA_KV_HEADS, LANE).transpose(0, 1, 3, 2, 4).reshape(B, 1, 1536)
    return pl.pallas_call(
        functools.partial(_dec_attn_kernel, n_sel=n_sel, cur=cur, wb=wb),
        grid_spec=grid_spec,
        out_shape=jax.ShapeDtypeStruct((B, 1, 1536), F32),
        compiler_params=_params(("arbitrary", "arbitrary")),
        name="dec_attn",
    )(table, idx, *([cache] * n_sel), qr, kvn, win_state, proj, ocmp)


def _dec_dil_kernel(q_ref, n0_ref, n1_ref, n2_ref, s0_ref, s1_ref, s2_ref, o_ref, t0_ref, t1_ref, t2_ref):
    qall = q_ref[...]
    states, news, outs = (s0_ref, s1_ref, s2_ref), (n0_ref, n1_ref, n2_ref), (t0_ref, t1_ref, t2_ref)

    for st, new, out in zip(states, news, outs):
        keep = st.shape[0] - KVH_ROWS
        out[0:keep, :] = st[KVH_ROWS:, :]
        out[keep:, :] = new[...]

    for h in range(DIL_HEADS):
        os_, ms, ls = [], [], []
        for g, st in enumerate(states):
            hs = slice((g * DIL_HEADS + h) * LANE, (g * DIL_HEADS + h + 1) * LANE)
            window, dil = DIL_PAIRS[g]
            kn = news[g][h:h + 1, :].astype(BF16).astype(F32)
            vn = news[g][DIL_HEADS + h:DIL_HEADS + h + 1, :].astype(BF16).astype(F32)
            q = _rows16(qall[:, hs], 1).astype(BF16)
            k = _get_rows(st, h, KVH_ROWS * dil, window // dil).astype(BF16)
            v = _get_rows(st, DIL_HEADS + h, KVH_ROWS * dil, window // dil).astype(BF16)
            s = _dot_t(q, k) * SCALE
            s_new = jnp.sum(q.astype(F32) * kn, axis=-1, keepdims=True) * SCALE
            m = jnp.maximum(jnp.max(s, axis=-1, keepdims=True), s_new)
            e = jnp.exp(s - m)
            e_new = jnp.exp(s_new - m)
            l = jnp.sum(e, axis=-1, keepdims=True) + e_new
            ln = jnp.maximum(l, 1e-30)
            o = _dot((e / ln).astype(BF16), v) + (e_new / ln).astype(BF16).astype(F32) * vn
            os_.append(o)
            ms.append(m)
            ls.append(l)
        o_ref[:, h * LANE:(h + 1) * LANE] = _mix_groups(os_, ms, ls)[0:1, :]


def _dec_dil(qr, news, states):
    B = qr.shape[0]
    in_specs = [pl.BlockSpec((None, 1, 1536), lambda b: (b, 0, 0))]
    in_specs += [pl.BlockSpec((None, KVH_ROWS, LANE), lambda b: (b, 0, 0))] * len(DIL_PAIRS)
    st_specs = []
    for g, (window, dil) in enumerate(DIL_PAIRS):
        assert states[g].shape[1] == window * KVH_ROWS, "rolling buffer shorter than the window is not supported"
        st_specs.append(pl.BlockSpec((None, window * KVH_ROWS, LANE), lambda b: (b, 0, 0)))
    return pl.pallas_call(
        _dec_dil_kernel,
        grid=(B,),
        in_specs=in_specs + st_specs,
        out_specs=[pl.BlockSpec((None, 1, 4 * LANE), lambda b: (b, 0, 0))] + st_specs,
        out_shape=[jax.ShapeDtypeStruct((B, 1, 4 * LANE), F32)]
        + [jax.ShapeDtypeStruct(s.shape, F32) for s in states],
        compiler_params=_params(("parallel",)),
        name="dec_dil",
    )(qr, *news, *states)


def _rope_tables(pos):
    half = HEAD_DIM // 2
    inv = ROPE_THETA ** (-jnp.arange(half, dtype=F32) / half)
    ang = pos.astype(F32)[:, None] * inv
    cos, sin = jnp.cos(ang), jnp.sin(ang)
    return jnp.concatenate([cos, cos], axis=-1), jnp.concatenate([-sin, sin], axis=-1)


def _cover(nc, ns, rows, cols):
    c0 = jnp.arange(nc)[:, None] * CMP_STRIDE
    s0 = jnp.arange(ns)[None, :] * SEL_BLOCK
    cover = jnp.clip(jnp.minimum(c0 + CMP_BLOCK, s0 + SEL_BLOCK) - jnp.maximum(c0, s0), 0, CMP_BLOCK)
    cover = cover.astype(F32) / CMP_BLOCK
    return jnp.pad(cover, ((0, rows - nc), (0, cols - ns))).astype(BF16)


def _pad_gains(*gs):
    return jnp.pad(jnp.stack(gs, axis=0), ((0, 8 - len(gs)), (0, 0)))


def kernel(x_prompt, x_sample, mem_prompt, cache_nsa_kv, page_table, state_nsa_win, state_dil_0, state_dil_1,
           state_dil_2, cache_mem_kv, ff_norm, ff_w_gate, ff_w_up, ff_w_down, mix_norm, mem_norm, w_mem_kv,
           mem_q_g, mem_k_g, nsa_w_in, nsa_q_g, nsa_kc_g, nsa_ks_g, nsa_kw_g, nsa_cmp_w1, nsa_cmp_b1, nsa_cmp_w2,
           nsa_w_out, dil_w_in, dil_q_g, dil_k_g, dil_w_out):
    B, T, D = x_prompt.shape
    Bs = x_sample.shape[0]
    assert x_sample.shape[1] == 1, "the sample group is a single-token decode step"
    n_pages = page_table.shape[1]
    past_len = n_pages * PAGE_SIZE
    H, G, d = N_MIX_HEADS, NSA_KV_HEADS, HEAD_DIM

    wg, wu, wd = ff_w_gate, ff_w_up, ff_w_down
    ffg = ff_norm.reshape(ff_norm.shape[0], 2, 1, D)
    gate_w = jnp.pad(nsa_w_in[:, H * d:H * d + 3 * H].reshape(D, G, 3 * NSA_GROUP), ((0, 0), (0, 0), (0, LANE - 12)))
    nsa_w = jnp.concatenate([nsa_w_in[:, :H * d], nsa_w_in[:, H * d + 3 * H:], gate_w.reshape(D, G * LANE)], axis=1)
    nsa_w = jnp.pad(nsa_w, ((0, 0), (0, PROJ_N - nsa_w.shape[1]))).astype(BF16)
    dil_w = dil_w_in.astype(BF16)
    nsa_wo, dil_wo = nsa_w_out.astype(BF16), dil_w_out.astype(BF16)
    w1r = nsa_cmp_w1.reshape(2, 2, CMP_STRIDE, d, d).transpose(0, 2, 3, 1, 4).reshape(2, CMP_STRIDE * d, 2 * d)
    w1r = w1r.astype(BF16)
    cmp_b1 = nsa_cmp_b1.reshape(2, 1, d)
    cmp_w2 = nsa_cmp_w2.astype(BF16)
    kc_g = nsa_kc_g.reshape(1, d)

    n_pool = cache_nsa_kv.shape[0]
    cache_rows = cache_nsa_kv.transpose(0, 1, 3, 2, 4).reshape(n_pool, PAGE_SIZE * NSA_ROWS, LANE)
    win_rows = state_nsa_win.transpose(0, 1, 3, 2, 4).reshape(Bs, state_nsa_win.shape[1] * WIN_ROWS, LANE)
    dil_states = (state_dil_0, state_dil_1, state_dil_2)
    dil_rows = [s.reshape(Bs, s.shape[1] * KVH_ROWS, LANE) for s in dil_states]
    mem_rows_s = cache_mem_kv.reshape(2, Bs, N_MEM * KVH_ROWS, LANE)

    mem2d = mem_prompt.reshape(B * N_MEM, D)
    mem_rows_p = []
    for i in range(2):
        kv = _norm_matmul(mem2d, mem_norm[i], w_mem_kv[i].astype(BF16), tm=256, tn=1024)
        mem_rows_p.append(_memkv_post(kv, mem_k_g[i]).reshape(B, N_MEM * KVH_ROWS, LANE))

    cos_p, sin_p = _rope_tables(jnp.tile(jnp.arange(T, dtype=jnp.int32), B))
    cos_s, sin_s = _rope_tables(jnp.full((Bs,), past_len, jnp.int32))
    row3 = lambda a: a.astype(F32).reshape(Bs, 1, a.shape[-1])

    xs, *wb = _ffn_cast(x_sample.reshape(Bs, D), ffg, wg, wu, wd, 0, 0)
    xp = _ffn(x_prompt.reshape(B * T, D), ffg, *wb, 0, 0, tm=1024)
    nsa_gains = _pad_gains(nsa_q_g, nsa_ks_g, nsa_kw_g, mem_q_g[0])

    proj_p = _norm_matmul(xp, mix_norm[0], nsa_w, tm=1024, tn=1280)
    qn_p, qr_p, cmp_p, rows_p, win_p, kvb_p, mq_p = _nsa_post(proj_p, cos_p, sin_p, nsa_gains, tm=256)
    cmp_kv = _cmp_prompt(cmp_p.reshape(B, T, 768), w1r, cmp_b1, cmp_w2, kc_g)
    nc_p = T // CMP_STRIDE
    cover_p = _cover(nc_p - 1, T // SEL_BLOCK, nc_p, LANE)
    o_mix_p = _nsa_attn(qn_p.reshape(B, T, 1536), qr_p.reshape(B, T, 1536), proj_p.reshape(B, T, PROJ_N),
                        cmp_kv, kvb_p.reshape(B, T, 1536), cover_p)
    o_mem_p = _mem_attn(mq_p.reshape(B, T, 512), mem_rows_p[0], tq=1024)
    xp = _out_proj(xp, o_mix_p.reshape(B * T, 1536), o_mem_p.reshape(B * T, 512), nsa_wo, tm=512, tn=D)

    proj_s = _norm_matmul(xs, mix_norm[0], nsa_w, tm=Bs, tn=1024)
    qn_s, qr_s, _, rows_s, win_s, kvb_s, mq_s = _nsa_post(proj_s, cos_s, sin_s, nsa_gains, tm=Bs)
    nc_s = past_len // CMP_STRIDE
    ns_s = -(-(past_len + 1) // SEL_BLOCK)
    cover_s = _cover(nc_s - 1, ns_s, nc_s, -(-ns_s // LANE) * LANE)
    sel_idx, ocmp_s = _dec_select(cache_rows, page_table, row3(qn_s), w1r, cmp_b1, cmp_w2, kc_g, cover_s)
    o_mix_s = _dec_attn(cache_nsa_kv.transpose(0, 1, 3, 2, 4), page_table,
                        sel_idx[:, :G, :SEL_TOPK].reshape(-1), row3(qr_s), row3(kvb_s),
                        win_rows, proj_s.reshape(Bs, 1, PROJ_N), ocmp_s)
    o_mem_s = _mem_attn(row3(mq_s), mem_rows_s[0], tq=1)
    xs = _out_proj(xs, o_mix_s.reshape(Bs, 1536).astype(BF16), o_mem_s.reshape(Bs, 512).astype(BF16), nsa_wo,
                   tm=Bs)

    xs, *wb = _ffn_cast(xs, ffg, wg, wu, wd, 0, 1)
    xp = _ffn(xp, ffg, *wb, 0, 1, tm=1024)

    xs, *wb = _ffn_cast(xs, ffg, wg, wu, wd, 1, 0)
    xp = _ffn(xp, ffg, *wb, 1, 0, tm=1024)
    dil_gains = _pad_gains(dil_q_g, dil_k_g, mem_q_g[1])

    dproj_p = _norm_matmul(xp, mix_norm[1], dil_w, tm=1024, tn=1280)
    dmq_p, *rest = _dil_post(dproj_p, cos_p, sin_p, dil_gains, tm=256, seq_len=T)
    dnew_p, dqkv_p = rest[:3], rest[3:]
    band = [_dil_band(*dqkv_p[3 * g:3 * g + 3], g, tq, nr) for g, (tq, nr) in enumerate(((512, 1), (512, 2), (128, 4)))]
    o_dil_p = _dil_mix([o for o, _ in band], [s for _, s in band])
    o_dmem_p = _mem_attn(dmq_p.reshape(B, T, 512), mem_rows_p[1], tq=1024)
    xp = _out_proj(xp, o_dil_p, o_dmem_p.reshape(B * T, 512), dil_wo, tm=512, tn=D)

    dproj_s = _norm_matmul(xs, mix_norm[1], dil_w, tm=Bs, tn=1024)
    dq_s, dmq_s, *dnew_s = _dil_post(dproj_s, cos_s, sin_s, dil_gains, tm=Bs)
    o_dil_s, *dil_rows_out = _dec_dil(row3(dq_s), [s.reshape(Bs, KVH_ROWS, LANE) for s in dnew_s], dil_rows)
    o_dmem_s = _mem_attn(row3(dmq_s), mem_rows_s[1], tq=1)
    xs = _out_proj(xs, o_dil_s.reshape(Bs, 512).astype(BF16), o_dmem_s.reshape(Bs, 512).astype(BF16), dil_wo,
                   tm=Bs)

    xs, *wb = _ffn_cast(xs, ffg, wg, wu, wd, 1, 1)
    xp = _ffn(xp, ffg, *wb, 1, 1, tm=1024)

    unrow = lambda a, n, outer, inner: a.reshape(n, -1, outer, inner, d).transpose(0, 1, 3, 2, 4)
    nsa_kv_p = unrow(rows_p, B, G, 4)
    nsa_kv_s = unrow(rows_s, Bs, G, 4)
    nsa_win_p = unrow(win_p, B, G, 2)[:, -min(NSA_WINDOW, T):]
    nsa_win_s = jnp.concatenate([state_nsa_win, unrow(win_s, Bs, G, 2)], axis=1)[:, -state_nsa_win.shape[1]:]
    outs_dil = []
    for g, (window, _) in enumerate(DIL_PAIRS):
        st = dil_states[g]
        outs_dil.append(dnew_p[g].reshape(B, T, 2, DIL_HEADS, d)[:, -min(window, T):])
        outs_dil.append(dil_rows_out[g].reshape(st.shape))
    mem_kv_out = jnp.stack([kv.reshape(B, N_MEM, 2, N_MEM_HEADS, d) for kv in mem_rows_p], axis=0)
    return (xp.reshape(B, T, D), xs.reshape(Bs, 1, D), nsa_kv_p, nsa_kv_s, nsa_win_p, nsa_win_s,
            *outs_dil, mem_kv_out)
```

```python
import functools

import jax
import jax.numpy as jnp
from jax import lax
from jax.experimental import pallas as pl
from jax.experimental.pallas import tpu as pltpu

F32 = jnp.float32
BF16 = jnp.bfloat16

D_MODEL = 2048
HEAD_DIM = 128
N_MIX_HEADS = 12
N_MEM_HEADS = 4
N_MEM = 256
NSA_KV_HEADS = 3
NSA_GROUP = 4
CMP_BLOCK = 32
CMP_STRIDE = 16
SEL_BLOCK = 64
SEL_TOPK = 16
NSA_WINDOW = 512
DIL_PAIRS = ((128, 1), (512, 4), (2048, 16))
DIL_HEADS = 4
PAGE_SIZE = 128
ROPE_THETA = 10000.0
EPS = 1e-6
SCALE = HEAD_DIM ** -0.5
NEG = -1e30
FORCE_SCORE = 1e6

PROJ_N = 5120
NSA_GATE_BLK = 34
LANE = 128
VMEM_LIMIT = 56 * 1024 * 1024


def _params(sem):
    return pltpu.CompilerParams(dimension_semantics=sem, vmem_limit_bytes=VMEM_LIMIT)


def _dot(a, b):
    return jnp.dot(a, b, preferred_element_type=F32)


def _dot_t(a, b):
    return lax.dot_general(a, b, (((1,), (1,)), ((), ())), preferred_element_type=F32)


def _dot3(a, b):
    a1 = a.astype(BF16)
    r1 = a - a1.astype(F32)
    a2 = r1.astype(BF16)
    a3 = (r1 - a2.astype(F32)).astype(BF16)
    return _dot(a1, b) + _dot(a2, b) + _dot(a3, b)


def _rms(x, g):
    return x * lax.rsqrt(jnp.mean(x * x, axis=-1, keepdims=True) + EPS) * g


def _rope(x, cos, sin):
    return x * cos + pltpu.roll(x, HEAD_DIM // 2, 1) * sin


def _rows16(row, nrep):
    rid = lax.broadcasted_iota(jnp.int32, (16, LANE), 0) & (nrep - 1)
    out = jnp.zeros((16, LANE), F32)
    for r in range(nrep):
        piece = jnp.broadcast_to(row[:, r * LANE:(r + 1) * LANE], (16, LANE))
        out = jnp.where(rid == r, piece, out)
    return out


def _softmax_masked(s, mask):
    s = jnp.where(mask, s, NEG)
    m = jnp.max(s, axis=-1, keepdims=True)
    e = jnp.where(mask, jnp.exp(s - m), 0.0)
    l = jnp.sum(e, axis=-1, keepdims=True)
    return e * (1.0 / jnp.maximum(l, 1e-30)), m, l


def _ffn_step(f, nf, x_ref, g_ref, wg_ref, wu_ref, wd_ref, o_ref, h_ref, acc_ref):
    @pl.when(f == 0)
    def _():
        h_ref[...] = _rms(x_ref[...], g_ref[...]).astype(BF16)
        acc_ref[...] = jnp.zeros_like(acc_ref)

    h = h_ref[...]
    gate = _dot(h, wg_ref[...])
    up = _dot(h, wu_ref[...])
    a = (gate * jax.nn.sigmoid(gate) * up).astype(BF16)
    acc_ref[...] += _dot(a, wd_ref[...])

    @pl.when(f == nf - 1)
    def _():
        o_ref[...] = x_ref[...] + 0.5 * acc_ref[...]


def _ffn_kernel(x_ref, g_ref, wg_ref, wu_ref, wd_ref, o_ref, h_ref, *, nf):
    _ffn_step(pl.program_id(1), nf, x_ref, g_ref, wg_ref, wu_ref, wd_ref, o_ref, h_ref, o_ref)


def _ffn(x, g, wg, wu, wd, li, lj, tm, tf=512):
    M, D = x.shape
    F = wg.shape[-1]
    nf = F // tf
    return pl.pallas_call(
        functools.partial(_ffn_kernel, nf=nf),
        grid=(M // tm, nf),
        in_specs=[
            pl.BlockSpec((tm, D), lambda i, f: (i, 0)),
            pl.BlockSpec((None, None, 1, D), lambda i, f: (li, lj, 0, 0)),
            pl.BlockSpec((D, tf), lambda i, f: (0, f)),
            pl.BlockSpec((D, tf), lambda i, f: (0, f)),
            pl.BlockSpec((tf, D), lambda i, f: (f, 0)),
        ],
        out_specs=pl.BlockSpec((tm, D), lambda i, f: (i, 0)),
        out_shape=jax.ShapeDtypeStruct((M, D), F32),
        scratch_shapes=[pltpu.VMEM((tm, D), BF16)],
        compiler_params=_params(("parallel", "arbitrary")),
        name="ffn",
    )(x, g, wg, wu, wd)


def _ffn_cast_kernel(x_ref, g_ref, wg_ref, wu_ref, wd_ref, o_ref, wgb_ref, wub_ref, wdb_ref, h_ref, *, nf):
    wgb_ref[...] = wg_ref[...].astype(BF16)
    wub_ref[...] = wu_ref[...].astype(BF16)
    wdb_ref[...] = wd_ref[...].astype(BF16)
    _ffn_step(pl.program_id(0), nf, x_ref, g_ref, wgb_ref, wub_ref, wdb_ref, o_ref, h_ref, o_ref)


def _ffn_cast(x, g, wg, wu, wd, li, lj, tf=512):
    M, D = x.shape
    F = wg.shape[-1]
    nf = F // tf
    return pl.pallas_call(
        functools.partial(_ffn_cast_kernel, nf=nf),
        grid=(nf,),
        in_specs=[
            pl.BlockSpec((M, D), lambda f: (0, 0)),
            pl.BlockSpec((None, None, 1, D), lambda f: (li, lj, 0, 0)),
            pl.BlockSpec((None, None, D, tf), lambda f: (li, lj, 0, f)),
            pl.BlockSpec((None, None, D, tf), lambda f: (li, lj, 0, f)),
            pl.BlockSpec((None, None, tf, D), lambda f: (li, lj, f, 0)),
        ],
        out_specs=[pl.BlockSpec((M, D), lambda f: (0, 0)),
                   pl.BlockSpec((D, tf), lambda f: (0, f)),
                   pl.BlockSpec((D, tf), lambda f: (0, f)),
                   pl.BlockSpec((tf, D), lambda f: (f, 0))],
        out_shape=[jax.ShapeDtypeStruct((M, D), F32),
                   jax.ShapeDtypeStruct((D, F), BF16),
                   jax.ShapeDtypeStruct((D, F), BF16),
                   jax.ShapeDtypeStruct((F, D), BF16)],
        scratch_shapes=[pltpu.VMEM((M, D), BF16)],
        compiler_params=_params(("arbitrary",)),
        name="ffn_cast",
    )(x, g, wg, wu, wd)


def _nmm_kernel(x_ref, g_ref, w_ref, o_ref, h_ref):
    @pl.when(pl.program_id(1) == 0)
    def _():
        h_ref[...] = _rms(x_ref[...], g_ref[...]).astype(BF16)

    o_ref[...] = _dot(h_ref[...], w_ref[...])


def _norm_matmul(x, g, w, tm, tn):
    M, D = x.shape
    N = w.shape[1]
    return pl.pallas_call(
        _nmm_kernel,
        grid=(M // tm, N // tn),
        in_specs=[
            pl.BlockSpec((tm, D), lambda i, j: (i, 0)),
            pl.BlockSpec((1, D), lambda i, j: (0, 0)),
            pl.BlockSpec((D, tn), lambda i, j: (0, j)),
        ],
        out_specs=pl.BlockSpec((tm, tn), lambda i, j: (i, j)),
        out_shape=jax.ShapeDtypeStruct((M, N), F32),
        scratch_shapes=[pltpu.VMEM((tm, D), BF16)],
        compiler_params=_params(("parallel", "arbitrary")),
        name="norm_matmul",
    )(x, g.reshape(1, D), w)


def _oproj_kernel(x_ref, a_ref, b_ref, w_ref, o_ref, *, ka):
    o_ref[...] = x_ref[...] + _dot(a_ref[...], w_ref[:ka, :]) + _dot(b_ref[...], w_ref[ka:, :])


def _out_proj(x, a, b, w, tm, tn=1024):
    M, D = x.shape
    ka, kb = a.shape[1], b.shape[1]
    return pl.pallas_call(
        functools.partial(_oproj_kernel, ka=ka),
        grid=(M // tm, D // tn),
        in_specs=[
            pl.BlockSpec((tm, tn), lambda i, j: (i, j)),
            pl.BlockSpec((tm, ka), lambda i, j: (i, 0)),
            pl.BlockSpec((tm, kb), lambda i, j: (i, 0)),
            pl.BlockSpec((ka + kb, tn), lambda i, j: (0, j)),
        ],
        out_specs=pl.BlockSpec((tm, tn), lambda i, j: (i, j)),
        out_shape=jax.ShapeDtypeStruct((M, D), F32),
        compiler_params=_params(("parallel", "parallel")),
        name="out_proj",
    )(x, a, b, w)


def _put_rows(ref, row, rows_per_token, val):
    ref[pl.ds(row, val.shape[0], stride=rows_per_token), :] = val


def _get_rows(ref, row, rows_per_token, n):
    return ref[pl.ds(row, n, stride=rows_per_token), :]


NSA_ROWS = 4 * NSA_KV_HEADS
WIN_ROWS = 2 * NSA_KV_HEADS
KVH_ROWS = 2 * DIL_HEADS


def _nsa_post_kernel(p_ref, cos_ref, sin_ref, g_ref, qn_ref, qr_ref, cmp_ref, rows_ref, win_ref, kvb_ref, mq_ref):
    cos, sin = cos_ref[...], sin_ref[...]
    q_g, ks_g, kw_g, mq_g = g_ref[0:1, :], g_ref[1:2, :], g_ref[2:3, :], g_ref[3:4, :]

    def tile(i):
        return p_ref[:, i * LANE:(i + 1) * LANE]

    for h in range(N_MIX_HEADS):
        qn = _rms(tile(h), q_g)
        qn_ref[:, h * LANE:(h + 1) * LANE] = qn.astype(BF16)
        qr_ref[:, h * LANE:(h + 1) * LANE] = _rope(qn, cos, sin).astype(BF16)
    for g in range(NSA_KV_HEADS):
        kc, vc = tile(12 + g), tile(15 + g)
        ks = _rope(_rms(tile(18 + g), ks_g), cos, sin)
        vs = tile(21 + g)
        kw = _rope(_rms(tile(24 + g), kw_g), cos, sin)
        vw = tile(27 + g)
        for c, val in enumerate((kc, vc)):
            cmp_ref[:, (c * 3 + g) * LANE:(c * 3 + g + 1) * LANE] = val
        for c, val in enumerate((kc, vc, ks, vs)):
            _put_rows(rows_ref, g * 4 + c, NSA_ROWS, val)
        for c, val in enumerate((kw, vw)):
            _put_rows(win_ref, g * 2 + c, WIN_ROWS, val)
        for c, val in enumerate((ks, vs, kw, vw)):
            kvb_ref[:, (c * 3 + g) * LANE:(c * 3 + g + 1) * LANE] = val.astype(BF16)
    for h in range(N_MEM_HEADS):
        mq_ref[:, h * LANE:(h + 1) * LANE] = _rms(tile(30 + h), mq_g).astype(BF16)


def _nsa_post(p, cos, sin, gains, tm):
    M = p.shape[0]
    row = lambda n: pl.BlockSpec((tm, n), lambda i: (i, 0))
    flat = lambda r: pl.BlockSpec((tm * r, LANE), lambda i: (i, 0))
    return pl.pallas_call(
        _nsa_post_kernel,
        grid=(M // tm,),
        in_specs=[row(PROJ_N), row(LANE), row(LANE), pl.BlockSpec((8, LANE), lambda i: (0, 0))],
        out_specs=[row(1536), row(1536), row(768), flat(NSA_ROWS), flat(WIN_ROWS), row(1536), row(512)],
        out_shape=[
            jax.ShapeDtypeStruct((M, 1536), BF16),
            jax.ShapeDtypeStruct((M, 1536), BF16),
            jax.ShapeDtypeStruct((M, 768), F32),
            jax.ShapeDtypeStruct((M * NSA_ROWS, LANE), F32),
            jax.ShapeDtypeStruct((M * WIN_ROWS, LANE), F32),
            jax.ShapeDtypeStruct((M, 1536), BF16),
            jax.ShapeDtypeStruct((M, 512), BF16),
        ],
        compiler_params=_params(("parallel",)),
        name="nsa_post",
    )(p, cos, sin, gains)


def _put_residues(ref, hh, val, dil, tmp_ref):
    sl = slice(hh * LANE, (hh + 1) * LANE)
    if dil == 1:
        ref[0, :, sl] = val.astype(BF16)
        return
    tmp_ref[...] = val
    n = val.shape[0] // dil
    for r in range(dil):
        ref[r, :, sl] = tmp_ref[pl.ds(r, n, stride=dil), :].astype(BF16)


def _dil_post_kernel(p_ref, cos_ref, sin_ref, g_ref, *refs, by_residue):
    if by_residue:
        mq_ref, st0_ref, st1_ref, st2_ref = refs[:4]
        qkv_refs, tmp_ref = refs[4:13], refs[13]
    else:
        qr_ref, mq_ref, st0_ref, st1_ref, st2_ref = refs
    cos, sin = cos_ref[...], sin_ref[...]
    q_g, k_g, mq_g = g_ref[0:1, :], g_ref[1:2, :], g_ref[2:3, :]
    st_refs = (st0_ref, st1_ref, st2_ref)
    for h in range(N_MIX_HEADS):
        g, hh = divmod(h, DIL_HEADS)
        sl = slice(h * LANE, (h + 1) * LANE)
        q = _rope(_rms(p_ref[:, sl], q_g), cos, sin)
        k = _rope(_rms(p_ref[:, (12 + h) * LANE:(13 + h) * LANE], k_g), cos, sin)
        v = p_ref[:, (24 + h) * LANE:(25 + h) * LANE]
        _put_rows(st_refs[g], hh, KVH_ROWS, k)
        _put_rows(st_refs[g], DIL_HEADS + hh, KVH_ROWS, v)
        if by_residue:
            for c, val in enumerate((q, k, v)):
                _put_residues(qkv_refs[3 * g + c], hh, val, DIL_PAIRS[g][1], tmp_ref)
        else:
            qr_ref[:, sl] = q.astype(BF16)
    for h in range(N_MEM_HEADS):
        mq_ref[:, h * LANE:(h + 1) * LANE] = _rms(p_ref[:, (36 + h) * LANE:(37 + h) * LANE], mq_g).astype(BF16)


def _dil_post(p, cos, sin, gains, tm, seq_len=None):
    M = p.shape[0]
    row = lambda n: pl.BlockSpec((tm, n), lambda i: (i, 0))
    flat = pl.BlockSpec((tm * KVH_ROWS, LANE), lambda i: (i, 0))
    st_shape = jax.ShapeDtypeStruct((M * KVH_ROWS, LANE), F32)
    out_specs = [row(512), flat, flat, flat]
    out_shape = [jax.ShapeDtypeStruct((M, 512), BF16), st_shape, st_shape, st_shape]
    scratch = []
    if seq_len is None:
        out_specs = [row(1536)] + out_specs
        out_shape = [jax.ShapeDtypeStruct((M, 1536), BF16)] + out_shape
    else:
        nt = seq_len // tm
        for _, dil in DIL_PAIRS:
            assert tm % (16 * dil) == 0
            spec = pl.BlockSpec((None, dil, tm // dil, 4 * LANE), lambda i: (i // nt, 0, i % nt, 0))
            shape = jax.ShapeDtypeStruct((M // seq_len, dil, seq_len // dil, 4 * LANE), BF16)
            out_specs += [spec] * 3
            out_shape += [shape] * 3
        scratch = [pltpu.VMEM((tm, LANE), F32)]
    return pl.pallas_call(
        functools.partial(_dil_post_kernel, by_residue=seq_len is not None),
        grid=(M // tm,),
        in_specs=[row(PROJ_N), row(LANE), row(LANE), pl.BlockSpec((8, LANE), lambda i: (0, 0))],
        out_specs=out_specs,
        out_shape=out_shape,
        scratch_shapes=scratch,
        compiler_params=_params(("parallel",)),
        name="dil_post",
    )(p, cos, sin, gains)


def _memkv_post_kernel(x_ref, g_ref, o_ref):
    for h in range(N_MEM_HEADS):
        _put_rows(o_ref, h, KVH_ROWS, _rms(x_ref[:, h * LANE:(h + 1) * LANE], g_ref[...]))
        _put_rows(o_ref, N_MEM_HEADS + h, KVH_ROWS, x_ref[:, (N_MEM_HEADS + h) * LANE:(N_MEM_HEADS + h + 1) * LANE])


def _memkv_post(x, g, tm=256):
    M, N = x.shape
    return pl.pallas_call(
        _memkv_post_kernel,
        grid=(M // tm,),
        in_specs=[pl.BlockSpec((tm, N), lambda i: (i, 0)), pl.BlockSpec((1, LANE), lambda i: (0, 0))],
        out_specs=pl.BlockSpec((tm * KVH_ROWS, LANE), lambda i: (i, 0)),
        out_shape=jax.ShapeDtypeStruct((M * KVH_ROWS, LANE), F32),
        compiler_params=_params(("parallel",)),
        name="memkv_post",
    )(x, g.reshape(1, LANE))


def _mem_attn_kernel(q_ref, kv_ref, o_ref, *, tq):
    for h in range(N_MEM_HEADS):
        sl = slice(h * LANE, (h + 1) * LANE)
        if tq == 1:
            q = _rows16(q_ref[:, sl], 1).astype(BF16)
        else:
            q = q_ref[:, sl]
        k = _get_rows(kv_ref, h, KVH_ROWS, N_MEM).astype(BF16)
        v = _get_rows(kv_ref, N_MEM_HEADS + h, KVH_ROWS, N_MEM).astype(BF16)
        s = _dot_t(q, k) * SCALE
        m = jnp.max(s, axis=-1, keepdims=True)
        e = jnp.exp(s - m)
        p = e * (1.0 / jnp.sum(e, axis=-1, keepdims=True))
        o = _dot(p.astype(BF16), v)
        o_ref[:, sl] = o[0:tq, :].astype(o_ref.dtype)


def _mem_attn(q, kv, tq):
    B, T, _ = q.shape
    return pl.pallas_call(
        functools.partial(_mem_attn_kernel, tq=tq),
        grid=(B, T // tq),
        in_specs=[
            pl.BlockSpec((None, tq, 512), lambda b, i: (b, i, 0)),
            pl.BlockSpec((None, N_MEM * KVH_ROWS, LANE), lambda b, i: (b, 0, 0)),
        ],
        out_specs=pl.BlockSpec((None, tq, 512), lambda b, i: (b, i, 0)),
        out_shape=jax.ShapeDtypeStruct((B, T, 512), q.dtype),
        compiler_params=_params(("parallel", "parallel")),
        name="mem_attn",
    )(q, kv)


def _gelu_tanh(x):
    return 0.5 * x * (1.0 + jnp.tanh(0.7978845608028654 * (x + 0.044715 * (x * x * x))))


def _compress_finish(h, b1, w2):
    n = h.shape[0]
    hid = b1 + h[:, :LANE] + pltpu.roll(h[:, LANE:], n - 1, 0)
    return _dot(_gelu_tanh(hid).astype(BF16), w2)


def _compress(x_bf, w1, b1, w2):
    return _compress_finish(_dot(x_bf, w1), b1, w2)


def _cmp_prompt_kernel(x_ref, w1_ref, b1_ref, w2_ref, kcg_ref, o_ref, xs_ref, *, n):
    kv = pl.program_id(1)
    for c in range(CMP_STRIDE):
        xs_ref[:, c * LANE:(c + 1) * LANE] = x_ref[pl.ds(c, n, stride=CMP_STRIDE), :].astype(BF16)
    out = _compress(xs_ref[...], w1_ref[...], b1_ref[...], w2_ref[...])
    out = jnp.where(kv == 0, _rms(out, kcg_ref[...]), out)
    rid = lax.broadcasted_iota(jnp.int32, out.shape, 0)
    o_ref[...] = jnp.where(rid < n - 1, out, 0.0).astype(BF16)


def _cmp_prompt(rows, w1r, b1, w2, kc_g):
    B, T, _ = rows.shape
    n = T // CMP_STRIDE
    return pl.pallas_call(
        functools.partial(_cmp_prompt_kernel, n=n),
        grid=(B, 2, NSA_KV_HEADS),
        in_specs=[
            pl.BlockSpec((None, T, LANE), lambda b, kv, g: (b, 0, kv * 3 + g)),
            pl.BlockSpec((None, CMP_STRIDE * LANE, 2 * LANE), lambda b, kv, g: (kv, 0, 0)),
            pl.BlockSpec((None, 1, LANE), lambda b, kv, g: (kv, 0, 0)),
            pl.BlockSpec((None, LANE, LANE), lambda b, kv, g: (kv, 0, 0)),
            pl.BlockSpec((1, LANE), lambda b, kv, g: (0, 0)),
        ],
        out_specs=pl.BlockSpec((None, None, None, n, LANE), lambda b, kv, g: (b, kv, g, 0, 0)),
        out_shape=jax.ShapeDtypeStruct((B, 2, NSA_KV_HEADS, n, LANE), BF16),
        scratch_shapes=[pltpu.VMEM((n, CMP_STRIDE * LANE), BF16)],
        compiler_params=_params(("parallel", "parallel", "parallel")),
        name="cmp_prompt",
    )(rows, w1r, b1, w2, kc_g)


def _select_blocks(score, cur, n_blocks):
    tq = score.shape[0]
    blk = lax.broadcasted_iota(jnp.int32, score.shape, 1)
    forced = (blk == 0) | (blk == cur) | (blk == cur - 1)
    sc = jnp.where(blk <= cur, jnp.where(forced, FORCE_SCORE, score), NEG)
    sct = sc.T[0:n_blocks, :]
    bi = lax.broadcasted_iota(jnp.int32, sct.shape, 0)
    rank = jnp.zeros(sct.shape, F32)
    for i in range(n_blocks):
        si = sct[i:i + 1, :]
        ahead = (si > sct) | ((si == sct) & (bi > i))
        rank = rank + jnp.where(ahead, 1.0, 0.0)
    chosen = jnp.where((rank < SEL_TOPK) & (sct > 0.5 * NEG), 1.0, 0.0)
    return jnp.concatenate([chosen, jnp.zeros((LANE - n_blocks, tq), F32)], axis=0).T


def _score_tiles(q, k_ref, tiles, s_ref, m_ref, tk, first=False):
    ms = []
    for kt, slot, bias in tiles:
        s = _dot_t(q, k_ref[pl.ds(pl.multiple_of(kt * tk, tk), tk), :]) * SCALE
        if bias is not None:
            s = s + bias
        s_ref[slot] = s
        ms += [s[:, c * LANE:(c + 1) * LANE] for c in range(tk // LANE)]
    m = functools.reduce(jnp.maximum, ms)
    m_ref[...] = m if first else jnp.maximum(m_ref[...], m)


def _value_tiles(v_ref, tiles, s_ref, m_ref, l_ref, acc_ref, tk, first=False):
    m = m_ref[...]
    ls, pvs = [], []
    for kt, slot in tiles:
        es = [jnp.exp(s_ref[slot, :, c * LANE:(c + 1) * LANE] - m) for c in range(tk // LANE)]
        pvs.append(_dot(jnp.concatenate(es, axis=1).astype(BF16),
                        v_ref[pl.ds(pl.multiple_of(kt * tk, tk), tk), :]))
        ls += es
    l = functools.reduce(lambda a, b: a + b, ls)
    pv = functools.reduce(lambda a, b: a + b, pvs)
    if first:
        l_ref[...] = l
        acc_ref[...] = pv
    else:
        l_ref[...] += l
        acc_ref[...] += pv


def _nsa_attn_kernel(qn_ref, qr_ref, gl_ref, kc_ref, vc_ref, ks_ref, vs_ref, kw_ref, vw_ref, cover_ref,
                     o_ref, s_ref, m_ref, l_ref, acc_ref, sel_ref, *, tq, ns):
    qi = pl.program_id(2)
    R = NSA_GROUP
    t0 = qi * tq
    stack = lambda ref: jnp.concatenate([ref[:, r * LANE:(r + 1) * LANE] for r in range(R)], axis=0)
    rows4 = lambda x: jnp.concatenate([x] * R, axis=0)
    tpos_q = t0 + lax.broadcasted_iota(jnp.int32, (tq, 1), 0)
    row_in = lax.broadcasted_iota(jnp.int32, (R * tq, 1), 0) & (tq - 1)
    col = lax.broadcasted_iota(jnp.int32, (R * tq, tq), 1)

    s = _dot_t(stack(qn_ref), kc_ref[...]) * SCALE
    cblk = lax.broadcasted_iota(jnp.int32, (R * tq, LANE), 1)
    cmask = (CMP_STRIDE * cblk + (CMP_BLOCK - 1) <= t0 + row_in) & (cblk < kc_ref.shape[0] - 1)
    p, _, _ = _softmax_masked(s, cmask)
    o_cmp = _dot(p.astype(BF16), vc_ref[...])
    imp = p[0:tq] + p[tq:2 * tq] + p[2 * tq:3 * tq] + p[3 * tq:4 * tq]
    score = _dot3(imp, cover_ref[...])
    cur = lax.shift_right_arithmetic(tpos_q, SEL_BLOCK.bit_length() - 1)

    @pl.when(t0 + tq <= SEL_TOPK * SEL_BLOCK)
    def _():
        sel_ref[...] = jnp.where(lax.broadcasted_iota(jnp.int32, (tq, LANE), 1) <= cur, 1.0, 0.0).astype(BF16)

    @pl.when(t0 + tq > SEL_TOPK * SEL_BLOCK)
    def _():
        sel_ref[...] = _select_blocks(score, cur, ns).astype(BF16)

    sel = sel_ref[...]
    q_rot = stack(qr_ref)
    blocks_per_tile = tq // SEL_BLOCK
    causal = jnp.where(col <= row_in, 0.0, NEG)
    far = jnp.where(col >= row_in, 0.0, NEG)

    def row_max():
        m_ref[...] = jnp.broadcast_to(jnp.max(m_ref[...], axis=-1, keepdims=True), m_ref.shape)

    def result():
        return acc_ref[...] * (1.0 / jnp.maximum(jnp.sum(l_ref[...], axis=-1, keepdims=True), 1e-30))

    def member_bias(kt):
        key_blk = lax.shift_right_arithmetic(lax.broadcasted_iota(jnp.int32, (LANE, tq), 1),
                                             SEL_BLOCK.bit_length() - 1)
        expand = lax.broadcasted_iota(jnp.int32, (LANE, tq), 0) == kt * blocks_per_tile + key_blk
        member = _dot(sel, jnp.where(expand, 1.0, 0.0).astype(BF16))
        return rows4((member - 1.0) * (-NEG))

    sel_tile = lambda kt: (kt, kt, member_bias(kt))
    _score_tiles(q_rot, ks_ref, [(qi, qi, member_bias(qi) + causal)], s_ref, m_ref, tq, first=True)

    def sel_scores(i, carry):
        _score_tiles(q_rot, ks_ref, [sel_tile(2 * i), sel_tile(2 * i + 1)], s_ref, m_ref, tq)
        return carry

    lax.fori_loop(0, qi // 2, sel_scores, 0)

    @pl.when(qi % 2 == 1)
    def _():
        _score_tiles(q_rot, ks_ref, [sel_tile(qi - 1)], s_ref, m_ref, tq)

    row_max()
    _value_tiles(vs_ref, [(qi, qi)], s_ref, m_ref, l_ref, acc_ref, tq, first=True)

    def sel_values(i, carry):
        _value_tiles(vs_ref, [(2 * i, 2 * i), (2 * i + 1, 2 * i + 1)], s_ref, m_ref, l_ref, acc_ref, tq)
        return carry

    lax.fori_loop(0, qi // 2, sel_values, 0)

    @pl.when(qi % 2 == 1)
    def _():
        _value_tiles(vs_ref, [(qi - 1, qi - 1)], s_ref, m_ref, l_ref, acc_ref, tq)

    o_sel = result()

    n_back = NSA_WINDOW // tq
    win_tiles = []
    for back in range(n_back + 1):
        bias = jnp.where(qi >= back, 0.0, NEG)
        if back == 0:
            bias = causal
        elif back == n_back:
            bias = far + bias
        win_tiles.append((jnp.maximum(qi - back, 0), back, bias))
    _score_tiles(q_rot, kw_ref, win_tiles, s_ref, m_ref, tq, first=True)
    row_max()
    _value_tiles(vw_ref, [(kt, slot) for kt, slot, _ in win_tiles], s_ref, m_ref, l_ref, acc_ref, tq, first=True)
    o_win = result()

    gates = jax.nn.sigmoid(gl_ref[...])
    for r in range(R):
        rs = slice(r * tq, (r + 1) * tq)
        o = (gates[:, 3 * r:3 * r + 1] * o_cmp[rs] + gates[:, 3 * r + 1:3 * r + 2] * o_sel[rs]
             + gates[:, 3 * r + 2:3 * r + 3] * o_win[rs])
        o_ref[:, r * LANE:(r + 1) * LANE] = o.astype(BF16)


def _nsa_attn(qn, qr, proj, cmp_kv, kvb, cover, tq=256):
    B, T, _ = qn.shape
    nc = cmp_kv.shape[3]
    G = NSA_KV_HEADS
    assert tq % LANE == 0 and NSA_WINDOW % tq == 0 and nc == LANE and T // SEL_BLOCK <= LANE
    rows = NSA_GROUP * tq
    qspec = pl.BlockSpec((None, tq, 4 * LANE), lambda b, g, i: (b, i, g))
    kvspec = lambda c: pl.BlockSpec((None, T, LANE), lambda b, g, i: (b, 0, c * 3 + g))
    return pl.pallas_call(
        functools.partial(_nsa_attn_kernel, tq=tq, ns=T // SEL_BLOCK),
        grid=(B, G, T // tq),
        in_specs=[
            qspec, qspec,
            pl.BlockSpec((None, tq, LANE), lambda b, g, i: (b, i, NSA_GATE_BLK + g)),
            pl.BlockSpec((None, None, None, nc, LANE), lambda b, g, i: (b, 0, g, 0, 0)),
            pl.BlockSpec((None, None, None, nc, LANE), lambda b, g, i: (b, 1, g, 0, 0)),
            kvspec(0), kvspec(1), kvspec(2), kvspec(3),
            pl.BlockSpec((nc, LANE), lambda b, g, i: (0, 0)),
        ],
        out_specs=qspec,
        out_shape=jax.ShapeDtypeStruct((B, T, 1536), BF16),
        scratch_shapes=[pltpu.VMEM((T // tq, rows, tq), F32),
                        pltpu.VMEM((rows, LANE), F32),
                        pltpu.VMEM((rows, LANE), F32),
                        pltpu.VMEM((rows, LANE), F32),
                        pltpu.VMEM((tq, LANE), BF16)],
        compiler_params=_params(("parallel", "parallel", "arbitrary")),
        name="nsa_attn",
    )(qn, qr, proj, cmp_kv, cmp_kv, kvb, kvb, kvb, kvb, cover)


def _dil_band_kernel(q_ref, kp_ref, kc_ref, vp_ref, vc_ref, o_ref, st_ref, *, tq, window):
    i = pl.program_id(2)
    w = window
    row = lax.broadcasted_iota(jnp.int32, (w, 2 * w), 0)
    colk = lax.broadcasted_iota(jnp.int32, (w, 2 * w), 1)
    diff = w + row - colk
    band = (diff >= 0) & (diff <= window)
    lane = lax.broadcasted_iota(jnp.int32, (w, LANE), 1)
    for r in range(q_ref.shape[0]):
        for j in range(tq // w):
            mask = band & (i * tq + (j - 1) * w + colk >= 0)
            rows = slice(j * w, (j + 1) * w)
            stats = jnp.zeros((w, LANE), F32)
            for h in range(DIL_HEADS):
                sl = slice(h * LANE, (h + 1) * LANE)
                if j == 0:
                    k = jnp.concatenate([kp_ref[r, :, sl], kc_ref[r, 0:w, sl]], axis=0)
                    v = jnp.concatenate([vp_ref[r, :, sl], vc_ref[r, 0:w, sl]], axis=0)
                else:
                    k = kc_ref[r, (j - 1) * w:(j + 1) * w, sl]
                    v = vc_ref[r, (j - 1) * w:(j + 1) * w, sl]
                p, m, l = _softmax_masked(_dot_t(q_ref[r, rows, sl], k) * SCALE, mask)
                o_ref[r, rows, sl] = _dot(p.astype(BF16), v)
                stats = jnp.where(lane == h, m, stats)
                stats = jnp.where(lane == DIL_HEADS + h, l, stats)
            st_ref[r, rows, :] = stats


def _dil_band(q, k, v, g, tq, nr):
    B, dil, S, _ = q.shape
    window = DIL_PAIRS[g][0] // dil
    assert tq % window == 0 and S % tq == 0 and dil % nr == 0
    cur = lambda w: pl.BlockSpec((None, nr, tq, w), lambda b, r, i: (b, r, i, 0))
    prev = pl.BlockSpec((None, nr, window, 4 * LANE),
                        lambda b, r, i: (b, r, jnp.maximum(i * (tq // window) - 1, 0), 0))
    return pl.pallas_call(
        functools.partial(_dil_band_kernel, tq=tq, window=window),
        grid=(B, dil // nr, S // tq),
        in_specs=[cur(4 * LANE), prev, cur(4 * LANE), prev, cur(4 * LANE)],
        out_specs=[cur(4 * LANE), cur(LANE)],
        out_shape=[jax.ShapeDtypeStruct((B, dil, S, 4 * LANE), F32),
                   jax.ShapeDtypeStruct((B, dil, S, LANE), F32)],
        compiler_params=_params(("parallel", "parallel", "parallel")),
        name=f"dil_band{g}",
    )(q, k, k, v, v)


def _mix_groups(os_, ms, ls):
    m_all = jnp.maximum(jnp.maximum(ms[0], ms[1]), ms[2])
    ws = [jnp.exp(m - m_all) * l for m, l in zip(ms, ls)]
    tot = ws[0] + ws[1] + ws[2]
    return (ws[0] / tot) * os_[0] + (ws[1] / tot) * os_[1] + (ws[2] / tot) * os_[2]


def _dil_mix_kernel(o0_ref, o1_ref, o2_ref, s0_ref, s1_ref, s2_ref, o_ref, nat_ref):
    o_refs, s_refs = (o0_ref, o1_ref, o2_ref), (s0_ref, s1_ref, s2_ref)
    tm = o_ref.shape[0]

    def token_order(ref, sl):
        dil = ref.shape[0]
        if dil == 1:
            return ref[0, :, sl]
        for r in range(dil):
            nat_ref[pl.ds(r, tm // dil, stride=dil), :] = ref[r, :, sl]
        return nat_ref[...]

    stats = [token_order(s, slice(0, LANE)) for s in s_refs]
    for h in range(DIL_HEADS):
        sl = slice(h * LANE, (h + 1) * LANE)
        ms = [s[:, h:h + 1] for s in stats]
        ls = [s[:, DIL_HEADS + h:DIL_HEADS + h + 1] for s in stats]
        o_ref[:, sl] = _mix_groups([token_order(o, sl) for o in o_refs], ms, ls).astype(BF16)


def _dil_mix(os_, sts, tm=512):
    B, _, T, _ = os_[0].shape
    nt = T // tm
    in_specs = []
    for lanes, arrs in ((4 * LANE, os_), (LANE, sts)):
        for a in arrs:
            dil = a.shape[1]
            in_specs.append(pl.BlockSpec((None, dil, tm // dil, lanes), lambda b, i: (b, 0, i, 0)))
    return pl.pallas_call(
        _dil_mix_kernel,
        grid=(B, nt),
        in_specs=in_specs,
        out_specs=pl.BlockSpec((tm, 4 * LANE), lambda b, i: (b * nt + i, 0)),
        out_shape=jax.ShapeDtypeStruct((B * T, 4 * LANE), BF16),
        scratch_shapes=[pltpu.VMEM((tm, LANE), F32)],
        compiler_params=_params(("parallel", "parallel")),
        name="dil_mix",
    )(*os_, *sts)


DEC_PAGES_PER_STEP = 16


def _dec_select_kernel(tbl_ref, *refs, n_pp, n_steps):
    pages = refs[:n_pp]
    (qn_ref, w1_ref, b1_ref, w2_ref, kcg_ref, cover_ref, idx_ref, ocmp_ref, h_ref) = refs[n_pp:]
    j = pl.program_id(1)
    cpp = PAGE_SIZE // CMP_STRIDE
    rows = n_pp * cpp
    n = n_steps * rows
    ns = cover_ref.shape[1]

    for g in range(NSA_KV_HEADS):
        for kv in range(2):
            x = jnp.concatenate(
                [_get_rows(pg, g * 4 + kv, NSA_ROWS, PAGE_SIZE).reshape(cpp, CMP_STRIDE * LANE) for pg in pages],
                axis=0)
            h_ref[g * 2 + kv, pl.ds(pl.multiple_of(j * rows, rows), rows), :] = _dot(x.astype(BF16), w1_ref[kv])

    @pl.when(j == n_steps - 1)
    def _():
        idx_ref[...] = jnp.zeros_like(idx_ref)
        for g in range(NSA_KV_HEADS):
            kc = _rms(_compress_finish(h_ref[g * 2], b1_ref[0], w2_ref[0]), kcg_ref[...]).astype(BF16)
            vc = _compress_finish(h_ref[g * 2 + 1], b1_ref[1], w2_ref[1]).astype(BF16)
            q = _rows16(qn_ref[:, g * NSA_GROUP * LANE:(g + 1) * NSA_GROUP * LANE], NSA_GROUP).astype(BF16)
            s = _dot_t(q, kc) * SCALE
            valid = lax.broadcasted_iota(jnp.int32, s.shape, 1) < n - 1
            p, _, _ = _softmax_masked(s, valid)
            ocmp_ref[g] = _dot(p.astype(BF16), vc)[0:NSA_GROUP, :]
            rid = lax.broadcasted_iota(jnp.int32, p.shape, 0)
            imp = jnp.sum(jnp.where(rid < NSA_GROUP, p, 0.0), axis=0, keepdims=True)
            score = _dot3(jnp.broadcast_to(imp, (8, n)), cover_ref[...])

            a = jnp.broadcast_to(score[0:1, :], (ns, ns))
            lane = lax.broadcasted_iota(jnp.int32, (ns, ns), 1)
            sub = lax.broadcasted_iota(jnp.int32, (ns, ns), 0)
            cur = n * CMP_STRIDE // SEL_BLOCK
            forced = (lane == 0) | (lane == cur) | (lane == cur - 1)
            a = jnp.where(lane <= cur, jnp.where(forced, FORCE_SCORE, a), NEG)
            at = a.T
            ahead_r = (at > a) | ((at == a) & (sub < lane))
            chosen_r = ((jnp.sum(jnp.where(ahead_r, 1.0, 0.0), axis=0, keepdims=True) < SEL_TOPK)
                        & (a[0:1, :] > 0.5 * NEG))
            ahead_c = (a > at) | ((a == at) & (lane < sub))
            chosen_c = ((jnp.sum(jnp.where(ahead_c, 1.0, 0.0), axis=1, keepdims=True) < SEL_TOPK)
                        & (at[:, 0:1] > 0.5 * NEG))
            before = jnp.sum(jnp.where(chosen_r & (lane < sub), 1.0, 0.0), axis=1, keepdims=True)
            slot = lax.broadcasted_iota(jnp.int32, (ns, LANE), 1)
            onehot = chosen_c & (before == slot.astype(F32))
            blk = lax.broadcasted_iota(jnp.int32, (ns, LANE), 0)
            picked = jnp.sum(jnp.where(onehot, blk.astype(F32), 0.0), axis=0, keepdims=True)
            filled = jnp.sum(jnp.where(onehot, 1.0, 0.0), axis=0, keepdims=True)
            idx_ref[g:g + 1, :] = jnp.where(filled > 0.5, picked, float(cur)).astype(jnp.int32)


def _dec_select(cache, table, qn, w1r, b1, w2, kc_g, cover):
    B, n_pages = table.shape
    n_pp = DEC_PAGES_PER_STEP
    n_steps = n_pages // n_pp
    n = n_pages * (PAGE_SIZE // CMP_STRIDE)
    ns = cover.shape[1]
    G = NSA_KV_HEADS
    const = lambda *shape: pl.BlockSpec(shape, lambda b, j, tbl: (0,) * len(shape))
    page_spec = lambda p: pl.BlockSpec((None, PAGE_SIZE * NSA_ROWS, LANE), lambda b, j, tbl: (tbl[b, j * n_pp + p], 0, 0))
    grid_spec = pltpu.PrefetchScalarGridSpec(
        num_scalar_prefetch=1,
        grid=(B, n_steps),
        in_specs=[page_spec(p) for p in range(n_pp)] + [
            pl.BlockSpec((None, 1, 1536), lambda b, j, tbl: (b, 0, 0)),
            const(2, CMP_STRIDE * LANE, 2 * LANE), const(2, 1, LANE), const(2, LANE, LANE), const(1, LANE),
            const(n, ns),
        ],
        out_specs=[pl.BlockSpec((None, 8, LANE), lambda b, j, tbl: (b, 0, 0)),
                   pl.BlockSpec((None, G, NSA_GROUP, LANE), lambda b, j, tbl: (b, 0, 0, 0))],
        scratch_shapes=[pltpu.VMEM((2 * G, n, 2 * LANE), F32)],
    )
    return pl.pallas_call(
        functools.partial(_dec_select_kernel, n_pp=n_pp, n_steps=n_steps),
        grid_spec=grid_spec,
        out_shape=[jax.ShapeDtypeStruct((B, 8, LANE), jnp.int32),
                   jax.ShapeDtypeStruct((B, G, NSA_GROUP, LANE), F32)],
        compiler_params=_params(("arbitrary", "arbitrary")),
        name="dec_select",
    )(table, *([cache] * n_pp), qn, w1r, b1, w2, kc_g, cover)


def _dec_attn_kernel(tbl_ref, idx_ref, *refs, n_sel, cur, wb):
    blocks = refs[:n_sel]
    (qr_ref, kvn_ref, win_ref, gl_ref, ocmp_ref, o_ref) = refs[n_sel:]
    b, g = pl.program_id(0), pl.program_id(1)
    R = NSA_GROUP
    q = _rows16(qr_ref[...], R).astype(BF16)
    qf = q.astype(F32)
    new = kvn_ref[...]
    ks_n, vs_n, kw_n, vw_n = (new[:, c * LANE:(c + 1) * LANE] for c in range(4))

    rows = [r.reshape(SEL_BLOCK * 4, LANE) for r in blocks]
    k = jnp.concatenate([_get_rows(r, 2, 4, SEL_BLOCK) for r in rows], axis=0).astype(BF16)
    v = jnp.concatenate([_get_rows(r, 3, 4, SEL_BLOCK) for r in rows], axis=0).astype(BF16)
    s = _dot_t(q, k) * SCALE
    blk_of = lax.shift_right_arithmetic(lax.broadcasted_iota(jnp.int32, s.shape, 1), SEL_BLOCK.bit_length() - 1)
    valid = jnp.zeros(s.shape, jnp.int32)
    for n in range(n_sel):
        is_past = jnp.where(idx_ref[(b * NSA_KV_HEADS + g) * n_sel + n] != cur, 1, 0)
        valid = jnp.where(blk_of == n, is_past, valid)
    valid = valid > 0
    s_new = jnp.sum(qf * ks_n, axis=-1, keepdims=True) * SCALE
    s = jnp.where(valid, s, NEG)
    m = jnp.maximum(jnp.max(s, axis=-1, keepdims=True), s_new)
    e = jnp.where(valid, jnp.exp(s - m), 0.0)
    e_new = jnp.exp(s_new - m)
    l = jnp.sum(e, axis=-1, keepdims=True) + e_new
    o_sel = (_dot(e.astype(BF16), v) + e_new.astype(BF16).astype(F32) * vs_n) / l

    s = _dot_t(q, _get_rows(win_ref, g * 2, WIN_ROWS, wb).astype(BF16)) * SCALE
    s_new = jnp.sum(qf * kw_n, axis=-1, keepdims=True) * SCALE
    m = jnp.maximum(jnp.max(s, axis=-1, keepdims=True), s_new)
    e = jnp.exp(s - m)
    e_new = jnp.exp(s_new - m)
    l = jnp.sum(e, axis=-1, keepdims=True) + e_new
    o_win = (_dot(e.astype(BF16), _get_rows(win_ref, g * 2 + 1, WIN_ROWS, wb).astype(BF16))
             + e_new.astype(BF16).astype(F32) * vw_n) / l

    gates = jax.nn.sigmoid(gl_ref[...])
    o_cmp = ocmp_ref[...]
    for r in range(R):
        o = (gates[:, 3 * r:3 * r + 1] * o_cmp[r:r + 1] + gates[:, 3 * r + 1:3 * r + 2] * o_sel[r:r + 1]
             + gates[:, 3 * r + 2:3 * r + 3] * o_win[r:r + 1])
        o_ref[:, r * LANE:(r + 1) * LANE] = o


def _dec_attn(cache, table, idx, qr, kvb, win_state, proj, ocmp):
    B, n_pages = table.shape
    n_sel = SEL_TOPK
    cur = n_pages * PAGE_SIZE // SEL_BLOCK
    wb = win_state.shape[1] // WIN_ROWS
    assert wb <= NSA_WINDOW
    halves = PAGE_SIZE // SEL_BLOCK

    def blk_spec(n):
        def imap(b, g, tbl, idx):
            i = jnp.minimum(idx[(b * NSA_KV_HEADS + g) * n_sel + n], cur - 1)
            return (tbl[b, i // halves], i % halves, g, 0, 0)
        return pl.BlockSpec((None, SEL_BLOCK, None, 4, LANE), imap)

    grid_spec = pltpu.PrefetchScalarGridSpec(
        num_scalar_prefetch=2,
        grid=(B, NSA_KV_HEADS),
        in_specs=[blk_spec(n) for n in range(n_sel)] + [
            pl.BlockSpec((None, 1, 4 * LANE), lambda b, g, tbl, idx: (b, 0, g)),
            pl.BlockSpec((None, 1, 4 * LANE), lambda b, g, tbl, idx: (b, 0, g)),
            pl.BlockSpec((None, wb * WIN_ROWS, LANE), lambda b, g, tbl, idx: (b, 0, 0)),
            pl.BlockSpec((None, 1, LANE), lambda b, g, tbl, idx: (b, 0, NSA_GATE_BLK + g)),
            pl.BlockSpec((None, None, NSA_GROUP, LANE), lambda b, g, tbl, idx: (b, g, 0, 0)),
        ],
        out_specs=pl.BlockSpec((None, 1, 4 * LANE), lambda b, g, tbl, idx: (b, 0, g)),
    )
    kvn = kvb.reshape(B, 1, 4, NSA_KV_HEADS, LANE).transpose(0, 1, 3, 2, 4).reshape(B, 1, 1536)
    return pl.pallas_call(
        functools.partial(_dec_attn_kernel, n_sel=n_sel, cur=cur, wb=wb),
        grid_spec=grid_spec,
        out_shape=jax.ShapeDtypeStruct((B, 1, 1536), F32),
        compiler_params=_params(("arbitrary", "arbitrary")),
        name="dec_attn",
    )(table, idx, *([cache] * n_sel), qr, kvn, win_state, proj, ocmp)


def _dec_dil_kernel(q_ref, n0_ref, n1_ref, n2_ref, s0_ref, s1_ref, s2_ref, o_ref, t0_ref, t1_ref, t2_ref):
    qall = q_ref[...]
    states, news, outs = (s0_ref, s1_ref, s2_ref), (n0_ref, n1_ref, n2_ref), (t0_ref, t1_ref, t2_ref)

    for st, new, out in zip(states, news, outs):
        keep = st.shape[0] - KVH_ROWS
        out[0:keep, :] = st[KVH_ROWS:, :]
        out[keep:, :] = new[...]

    for h in range(DIL_HEADS):
        os_, ms, ls = [], [], []
        for g, st in enumerate(states):
            hs = slice((g * DIL_HEADS + h) * LANE, (g * DIL_HEADS + h + 1) * LANE)
            window, dil = DIL_PAIRS[g]
            kn = news[g][h:h + 1, :].astype(BF16).astype(F32)
            vn = news[g][DIL_HEADS + h:DIL_HEADS + h + 1, :].astype(BF16).astype(F32)
            q = _rows16(qall[:, hs], 1).astype(BF16)
            k = _get_rows(st, h, KVH_ROWS * dil, window // dil).astype(BF16)
            v = _get_rows(st, DIL_HEADS + h, KVH_ROWS * dil, window // dil).astype(BF16)
            s = _dot_t(q, k) * SCALE
            s_new = jnp.sum(q.astype(F32) * kn, axis=-1, keepdims=True) * SCALE
            m = jnp.maximum(jnp.max(s, axis=-1, keepdims=True), s_new)
            e = jnp.exp(s - m)
            e_new = jnp.exp(s_new - m)
            l = jnp.sum(e, axis=-1, keepdims=True) + e_new
            ln = jnp.maximum(l, 1e-30)
            o = _dot((e / ln).astype(BF16), v) + (e_new / ln).astype(BF16).astype(F32) * vn
            os_.append(o)
            ms.append(m)
            ls.append(l)
        o_ref[:, h * LANE:(h + 1) * LANE] = _mix_groups(os_, ms, ls)[0:1, :]


def _dec_dil(qr, news, states):
    B = qr.shape[0]
    in_specs = [pl.BlockSpec((None, 1, 1536), lambda b: (b, 0, 0))]
    in_specs += [pl.BlockSpec((None, KVH_ROWS, LANE), lambda b: (b, 0, 0))] * len(DIL_PAIRS)
    st_specs = []
    for g, (window, dil) in enumerate(DIL_PAIRS):
        assert states[g].shape[1] == window * KVH_ROWS, "rolling buffer shorter than the window is not supported"
        st_specs.append(pl.BlockSpec((None, window * KVH_ROWS, LANE), lambda b: (b, 0, 0)))
    return pl.pallas_call(
        _dec_dil_kernel,
        grid=(B,),
        in_specs=in_specs + st_specs,
        out_specs=[pl.BlockSpec((None, 1, 4 * LANE), lambda b: (b, 0, 0))] + st_specs,
        out_shape=[jax.ShapeDtypeStruct((B, 1, 4 * LANE), F32)]
        + [jax.ShapeDtypeStruct(s.shape, F32) for s in states],
        compiler_params=_params(("parallel",)),
        name="dec_dil",
    )(qr, *news, *states)


def _rope_tables(pos):
    half = HEAD_DIM // 2
    inv = ROPE_THETA ** (-jnp.arange(half, dtype=F32) / half)
    ang = pos.astype(F32)[:, None] * inv
    cos, sin = jnp.cos(ang), jnp.sin(ang)
    return jnp.concatenate([cos, cos], axis=-1), jnp.concatenate([-sin, sin], axis=-1)


def _cover(nc, ns, rows, cols):
    c0 = jnp.arange(nc)[:, None] * CMP_STRIDE
    s0 = jnp.arange(ns)[None, :] * SEL_BLOCK
    cover = jnp.clip(jnp.minimum(c0 + CMP_BLOCK, s0 + SEL_BLOCK) - jnp.maximum(c0, s0), 0, CMP_BLOCK)
    cover = cover.astype(F32) / CMP_BLOCK
    return jnp.pad(cover, ((0, rows - nc), (0, cols - ns))).astype(BF16)


def _pad_gains(*gs):
    return jnp.pad(jnp.stack(gs, axis=0), ((0, 8 - len(gs)), (0, 0)))


def kernel(x_prompt, x_sample, mem_prompt, cache_nsa_kv, page_table, state_nsa_win, state_dil_0, state_dil_1,
           state_dil_2, cache_mem_kv, ff_norm, ff_w_gate, ff_w_up, ff_w_down, mix_norm, mem_norm, w_mem_kv,
           mem_q_g, mem_k_g, nsa_w_in, nsa_q_g, nsa_kc_g, nsa_ks_g, nsa_kw_g, nsa_cmp_w1, nsa_cmp_b1, nsa_cmp_w2,
           nsa_w_out, dil_w_in, dil_q_g, dil_k_g, dil_w_out):
    B, T, D = x_prompt.shape
    Bs = x_sample.shape[0]
    assert x_sample.shape[1] == 1, "the sample group is a single-token decode step"
    n_pages = page_table.shape[1]
    past_len = n_pages * PAGE_SIZE
    H, G, d = N_MIX_HEADS, NSA_KV_HEADS, HEAD_DIM

    wg, wu, wd = ff_w_gate, ff_w_up, ff_w_down
    ffg = ff_norm.reshape(ff_norm.shape[0], 2, 1, D)
    gate_w = jnp.pad(nsa_w_in[:, H * d:H * d + 3 * H].reshape(D, G, 3 * NSA_GROUP), ((0, 0), (0, 0), (0, LANE - 12)))
    nsa_w = jnp.concatenate([nsa_w_in[:, :H * d], nsa_w_in[:, H * d + 3 * H:], gate_w.reshape(D, G * LANE)], axis=1)
    nsa_w = jnp.pad(nsa_w, ((0, 0), (0, PROJ_N - nsa_w.shape[1]))).astype(BF16)
    dil_w = dil_w_in.astype(BF16)
    nsa_wo, dil_wo = nsa_w_out.astype(BF16), dil_w_out.astype(BF16)
    w1r = nsa_cmp_w1.reshape(2, 2, CMP_STRIDE, d, d).transpose(0, 2, 3, 1, 4).reshape(2, CMP_STRIDE * d, 2 * d)
    w1r = w1r.astype(BF16)
    cmp_b1 = nsa_cmp_b1.reshape(2, 1, d)
    cmp_w2 = nsa_cmp_w2.astype(BF16)
    kc_g = nsa_kc_g.reshape(1, d)

    n_pool = cache_nsa_kv.shape[0]
    cache_rows = cache_nsa_kv.transpose(0, 1, 3, 2, 4).reshape(n_pool, PAGE_SIZE * NSA_ROWS, LANE)
    win_rows = state_nsa_win.transpose(0, 1, 3, 2, 4).reshape(Bs, state_nsa_win.shape[1] * WIN_ROWS, LANE)
    dil_states = (state_dil_0, state_dil_1, state_dil_2)
    dil_rows = [s.reshape(Bs, s.shape[1] * KVH_ROWS, LANE) for s in dil_states]
    mem_rows_s = cache_mem_kv.reshape(2, Bs, N_MEM * KVH_ROWS, LANE)

    mem2d = mem_prompt.reshape(B * N_MEM, D)
    mem_rows_p = []
    for i in range(2):
        kv = _norm_matmul(mem2d, mem_norm[i], w_mem_kv[i].astype(BF16), tm=256, tn=1024)
        mem_rows_p.append(_memkv_post(kv, mem_k_g[i]).reshape(B, N_MEM * KVH_ROWS, LANE))

    cos_p, sin_p = _rope_tables(jnp.tile(jnp.arange(T, dtype=jnp.int32), B))
    cos_s, sin_s = _rope_tables(jnp.full((Bs,), past_len, jnp.int32))
    row3 = lambda a: a.astype(F32).reshape(Bs, 1, a.shape[-1])

    xs, *wb = _ffn_cast(x_sample.reshape(Bs, D), ffg, wg, wu, wd, 0, 0)
    xp = _ffn(x_prompt.reshape(B * T, D), ffg, *wb, 0, 0, tm=1024)
    nsa_gains = _pad_gains(nsa_q_g, nsa_ks_g, nsa_kw_g, mem_q_g[0])

    proj_p = _norm_matmul(xp, mix_norm[0], nsa_w, tm=1024, tn=1280)
    qn_p, qr_p, cmp_p, rows_p, win_p, kvb_p, mq_p = _nsa_post(proj_p, cos_p, sin_p, nsa_gains, tm=512)
    cmp_kv = _cmp_prompt(cmp_p.reshape(B, T, 768), w1r, cmp_b1, cmp_w2, kc_g)
    nc_p = T // CMP_STRIDE
    cover_p = _cover(nc_p - 1, T // SEL_BLOCK, nc_p, LANE)
    o_mix_p = _nsa_attn(qn_p.reshape(B, T, 1536), qr_p.reshape(B, T, 1536), proj_p.reshape(B, T, PROJ_N),
                        cmp_kv, kvb_p.reshape(B, T, 1536), cover_p)
    o_mem_p = _mem_attn(mq_p.reshape(B, T, 512), mem_rows_p[0], tq=1024)
    xp = _out_proj(xp, o_mix_p.reshape(B * T, 1536), o_mem_p.reshape(B * T, 512), nsa_wo, tm=512, tn=D)

    proj_s = _norm_matmul(xs, mix_norm[0], nsa_w, tm=Bs, tn=1024)
    qn_s, qr_s, _, rows_s, win_s, kvb_s, mq_s = _nsa_post(proj_s, cos_s, sin_s, nsa_gains, tm=Bs)
    nc_s = past_len // CMP_STRIDE
    ns_s = -(-(past_len + 1) // SEL_BLOCK)
    cover_s = _cover(nc_s - 1, ns_s, nc_s, -(-ns_s // LANE) * LANE)
    sel_idx, ocmp_s = _dec_select(cache_rows, page_table, row3(qn_s), w1r, cmp_b1, cmp_w2, kc_g, cover_s)
    o_mix_s = _dec_attn(cache_nsa_kv.transpose(0, 1, 3, 2, 4), page_table,
                        sel_idx[:, :G, :SEL_TOPK].reshape(-1), row3(qr_s), row3(kvb_s),
                        win_rows, proj_s.reshape(Bs, 1, PROJ_N), ocmp_s)
    o_mem_s = _mem_attn(row3(mq_s), mem_rows_s[0], tq=1)
    xs = _out_proj(xs, o_mix_s.reshape(Bs, 1536).astype(BF16), o_mem_s.reshape(Bs, 512).astype(BF16), nsa_wo,
                   tm=Bs)

    xs, *wb = _ffn_cast(xs, ffg, wg, wu, wd, 0, 1)
    xp = _ffn(xp, ffg, *wb, 0, 1, tm=1024)

    xs, *wb = _ffn_cast(xs, ffg, wg, wu, wd, 1, 0)
    xp = _ffn(xp, ffg, *wb, 1, 0, tm=1024)
    dil_gains = _pad_gains(dil_q_g, dil_k_g, mem_q_g[1])

    dproj_p = _norm_matmul(xp, mix_norm[1], dil_w, tm=1024, tn=1280)
    dmq_p, *rest = _dil_post(dproj_p, cos_p, sin_p, dil_gains, tm=512, seq_len=T)
    dnew_p, dqkv_p = rest[:3], rest[3:]
    band = [_dil_band(*dqkv_p[3 * g:3 * g + 3], g, tq, nr) for g, (tq, nr) in enumerate(((512, 1), (512, 2), (128, 4)))]
    o_dil_p = _dil_mix([o for o, _ in band], [s for _, s in band])
    o_dmem_p = _mem_attn(dmq_p.reshape(B, T, 512), mem_rows_p[1], tq=1024)
    xp = _out_proj(xp, o_dil_p, o_dmem_p.reshape(B * T, 512), dil_wo, tm=512, tn=D)

    dproj_s = _norm_matmul(xs, mix_norm[1], dil_w, tm=Bs, tn=1024)
    dq_s, dmq_s, *dnew_s = _dil_post(dproj_s, cos_s, sin_s, dil_gains, tm=Bs)
    o_dil_s, *dil_rows_out = _dec_dil(row3(dq_s), [s.reshape(Bs, KVH_ROWS, LANE) for s in dnew_s], dil_rows)
    o_dmem_s = _mem_attn(row3(dmq_s), mem_rows_s[1], tq=1)
    xs = _out_proj(xs, o_dil_s.reshape(Bs, 512).astype(BF16), o_dmem_s.reshape(Bs, 512).astype(BF16), dil_wo,
                   tm=Bs)

    xs, *wb = _ffn_cast(xs, ffg, wg, wu, wd, 1, 1)
    xp = _ffn(xp, ffg, *wb, 1, 1, tm=1024)

    unrow = lambda a, n, outer, inner: a.reshape(n, -1, outer, inner, d).transpose(0, 1, 3, 2, 4)
    nsa_kv_p = unrow(rows_p, B, G, 4)
    nsa_kv_s = unrow(rows_s, Bs, G, 4)
    nsa_win_p = unrow(win_p, B, G, 2)[:, -min(NSA_WINDOW, T):]
    nsa_win_s = jnp.concatenate([state_nsa_win, unrow(win_s, Bs, G, 2)], axis=1)[:, -state_nsa_win.shape[1]:]
    outs_dil = []
    for g, (window, _) in enumerate(DIL_PAIRS):
        st = dil_states[g]
        outs_dil.append(dnew_p[g].reshape(B, T, 2, DIL_HEADS, d)[:, -min(window, T):])
        outs_dil.append(dil_rows_out[g].reshape(st.shape))
    mem_kv_out = jnp.stack([kv.reshape(B, N_MEM, 2, N_MEM_HEADS, d) for kv in mem_rows_p], axis=0)
    return (xp.reshape(B, T, D), xs.reshape(Bs, 1, D), nsa_kv_p, nsa_kv_s, nsa_win_p, nsa_win_s,
            *outs_dil, mem_kv_out)
```
